```python
import math
import jax, jax.numpy as jnp
from jax import lax
import numpy as np

D_MODEL = 1024
BATCH = 4
SEQ = 8192
DEPTH = 1

MLA_HEADS = 8
MLA_Q_RANK = 384
MLA_KV_RANK = 256
MLA_NOPE_DIM = 128
MLA_ROPE_DIM = 64
MLA_V_DIM = D_MODEL // MLA_HEADS
ROPE_THETA = 10000.0
Q_BLOCK = 128
DIL_GROUPS = ((128, 1), (512, 4), (2048, 16))
DIL_HEADS = 8
DIL_HEAD_DIM = D_MODEL // DIL_HEADS
N_DIL_HEADS = len(DIL_GROUPS) * DIL_HEADS
REL_BUCKETS = 32
REL_MAX_DIST = 1024
N_EXPERTS = 16
EC_CAPACITY_FACTOR = 2
D_FF_EXPERT = 2 * D_MODEL
N_BRANCHES = 2
NORM_EPS = 1e-6
NEG_INF = -1e30
COLS_DIL = len(DIL_GROUPS) * 3 * DIL_HEADS * DIL_HEAD_DIM
SPLIT_SIZES = (MLA_Q_RANK, MLA_KV_RANK, MLA_ROPE_DIM, COLS_DIL, N_BRANCHES * D_MODEL)
D_IN = sum(SPLIT_SIZES)

kernel_name = 'hybrid_mla_dilated_ec_moe_block'


def rmsnorm(t, g):
    tf = t.astype(jnp.float32)
    y = tf * lax.rsqrt(jnp.mean(tf * tf, axis=-1, keepdims=True) + NORM_EPS)
    return (y * g.astype(jnp.float32)).astype(t.dtype)


def rope(t, cos, sin):
    t1, t2 = jnp.split(t, 2, axis=-1)
    return jnp.concatenate([t1 * cos - t2 * sin, t1 * sin + t2 * cos], axis=-1).astype(t.dtype)


def t5_bucket(rel):
    nb = REL_BUCKETS // 2
    ret = jnp.where(rel > 0, nb, 0)
    n = jnp.abs(rel)
    max_exact = nb // 2
    nf = jnp.maximum(n, 1).astype(jnp.float32)
    large = max_exact + (jnp.log(nf / max_exact) / math.log(REL_MAX_DIST / max_exact)
                         * (nb - max_exact)).astype(jnp.int32)
    large = jnp.minimum(large, nb - 1)
    return ret + jnp.where(n < max_exact, n, large)


def mla_attention(c_q, c_kv, k_pe, g_q_lat, g_kv_lat, w_uq, w_ukv, cos, sin):
    B, S, _ = c_q.shape
    q = (rmsnorm(c_q, g_q_lat) @ w_uq).reshape(B, S, MLA_HEADS, MLA_NOPE_DIM + MLA_ROPE_DIM)
    q_nope = q[..., :MLA_NOPE_DIM]
    q_pe = rope(q[..., MLA_NOPE_DIM:], cos[:, :, None], sin[:, :, None])
    kv = (rmsnorm(c_kv, g_kv_lat) @ w_ukv).reshape(B, S, MLA_HEADS, MLA_NOPE_DIM + MLA_V_DIM)
    k_nope, v = kv[..., :MLA_NOPE_DIM], kv[..., MLA_NOPE_DIM:]
    k_pe = rope(k_pe, cos, sin)
    scale = (MLA_NOPE_DIM + MLA_ROPE_DIM) ** -0.5
    nqb = S // Q_BLOCK

    def to_blocks(t):
        return jnp.moveaxis(t.reshape((B, nqb, Q_BLOCK) + t.shape[2:]), 1, 0)

    def attend(blk):
        qn, qp = blk
        s = (jnp.einsum('bqhc,bkhc->bhqk', qn, k_nope)
             + jnp.einsum('bqhr,bkr->bhqk', qp, k_pe))
        p = jax.nn.softmax(s.astype(jnp.float32) * scale, axis=-1).astype(v.dtype)
        return jnp.einsum('bhqk,bkhc->bqhc', p, v)

    o = lax.map(attend, (to_blocks(q_nope), to_blocks(q_pe)))
    return jnp.moveaxis(o, 0, 1).reshape(B, S, MLA_HEADS * MLA_V_DIM)


def dilated_band_attention(q, k, v, pos, bias_table, window, dilation):
    B, S, H, Dh = q.shape
    d = dilation
    r = window // (2 * d)
    L = S // d
    nb = -(-L // r)
    Lp = nb * r

    def to_classes(t):
        t = jnp.moveaxis(t.reshape((B, L, d) + t.shape[2:]), 2, 1)
        return jnp.pad(t, [(0, 0), (0, 0), (0, Lp - L)] + [(0, 0)] * (t.ndim - 3))

    def band(t):
        t = jnp.pad(t, [(0, 0), (0, 0), (r, r)] + [(0, 0)] * (t.ndim - 3))
        t = t.reshape((B, d, nb + 2, r) + t.shape[3:])
        return jnp.concatenate([t[:, :, :-2], t[:, :, 1:-1], t[:, :, 2:]], axis=3)

    qc, kc, vc, pc = to_classes(q), to_classes(k), to_classes(v), to_classes(pos)
    kb, vb, pkb = band(kc), band(vc), band(pc)
    qb = qc.reshape(B, d, nb, r, H, Dh)
    pqb = pc.reshape(B, d, nb, r)
    s = jnp.einsum('bgnqhc,bgnkhc->bgnhqk', qb, kb).astype(jnp.float32) * (Dh ** -0.5)
    rel = pkb[:, :, :, None, :] - pqb[..., None]
    bias = bias_table[t5_bucket(rel)]
    s = s + jnp.moveaxis(bias, -1, 3).astype(jnp.float32)
    qi = jnp.arange(nb)[:, None, None] * r + jnp.arange(r)[None, :, None]
    kj = (jnp.arange(nb)[:, None, None] - 1) * r + jnp.arange(3 * r)[None, None, :]
    valid = (jnp.abs(kj - qi) <= r) & (kj >= 0) & (kj < L)
    s = jnp.where(valid[:, None], s, NEG_INF)
    m = jnp.max(s, axis=-1, keepdims=True)
    p = jnp.exp(s - m)
    den = jnp.sum(p, axis=-1, keepdims=True)
    o = jnp.einsum('bgnhqk,bgnkhc->bgnqhc', (p / den).astype(v.dtype), vb)
    lse = (m + jnp.log(den))[..., 0]
    o = o.reshape(B, d, Lp, H, Dh)[:, :, :L]
    o = jnp.moveaxis(o, 1, 2).reshape(B, S, H, Dh)
    lse = jnp.moveaxis(lse, 3, 4).reshape(B, d, Lp, H)[:, :, :L]
    lse = jnp.moveaxis(lse, 1, 2).reshape(B, S, H)
    return o, lse


def expert_choice_ffn(h, w_router, w_gate, w_up, w_down):
    B, S, D = h.shape
    cap = EC_CAPACITY_FACTOR * S // N_EXPERTS
    aff = jax.nn.softmax((h @ w_router).astype(jnp.float32), axis=-1)
    gate, idx = lax.top_k(jnp.swapaxes(aff, 1, 2), cap)
    bidx = jnp.arange(B)[:, None, None]
    xe = h[bidx, idx]
    hid = (jax.nn.silu(jnp.einsum('becd,edf->becf', xe, w_gate))
           * jnp.einsum('becd,edf->becf', xe, w_up))
    ye = jnp.einsum('becf,efd->becd', hid, w_down) * gate[..., None].astype(h.dtype)
    return jnp.zeros_like(h).at[bidx, idx].add(ye)


def setup_inputs(seed: int = 0) -> dict:
    key = jax.random.key(seed)
    ks = jax.random.split(key, 20)
    f32 = jnp.float32

    def nrm(k, shape, s):
        return jax.random.normal(k, shape, f32) * s

    def gain(k, shape):
        return 1.0 + 0.02 * jax.random.normal(k, shape, f32)

    x = nrm(ks[0], (BATCH, SEQ, D_MODEL), 1.0)
    c = nrm(ks[1], (BATCH, D_MODEL), 1.0)
    offs = jax.random.randint(ks[2], (BATCH, 1), 0, 4096, dtype=jnp.int32)
    positions = offs + jnp.arange(SEQ, dtype=jnp.int32)[None, :]
    w_ada = nrm(ks[3], (DEPTH, D_MODEL, 6 * D_MODEL), 0.02)
    b_ada = nrm(ks[4], (DEPTH, 6 * D_MODEL), 0.02)
    g_norm_mix = gain(ks[5], (DEPTH, D_MODEL))
    w_in = nrm(ks[6], (DEPTH, D_MODEL, D_IN), D_MODEL ** -0.5)
    g_q_lat = gain(ks[7], (DEPTH, MLA_Q_RANK))
    g_kv_lat = gain(ks[8], (DEPTH, MLA_KV_RANK))
    w_uq = nrm(ks[9], (DEPTH, MLA_Q_RANK, MLA_HEADS * (MLA_NOPE_DIM + MLA_ROPE_DIM)), MLA_Q_RANK ** -0.5)
    w_ukv = nrm(ks[10], (DEPTH, MLA_KV_RANK, MLA_HEADS * (MLA_NOPE_DIM + MLA_V_DIM)), MLA_KV_RANK ** -0.5)
    rel_bias = nrm(ks[11], (REL_BUCKETS, N_DIL_HEADS), 0.5)
    w_out = nrm(ks[12], (DEPTH, D_MODEL, D_MODEL), D_MODEL ** -0.5)
    g_norm_ffn = gain(ks[13], (DEPTH, D_MODEL))
    w_router = nrm(ks[14], (DEPTH, D_MODEL, N_EXPERTS), D_MODEL ** -0.5)
    w_gate = nrm(ks[15], (DEPTH, N_EXPERTS, D_MODEL, D_FF_EXPERT), D_MODEL ** -0.5)
    w_up = nrm(ks[16], (DEPTH, N_EXPERTS, D_MODEL, D_FF_EXPERT), D_MODEL ** -0.5)
    w_down = nrm(ks[17], (DEPTH, N_EXPERTS, D_FF_EXPERT, D_MODEL), D_FF_EXPERT ** -0.5)
    g_final = gain(ks[18], (D_MODEL,))
    return {'x': x, 'c': c, 'positions': positions, 'w_ada': w_ada, 'b_ada': b_ada,
            'g_norm_mix': g_norm_mix, 'w_in': w_in, 'g_q_lat': g_q_lat, 'g_kv_lat': g_kv_lat,
            'w_uq': w_uq, 'w_ukv': w_ukv, 'rel_bias': rel_bias, 'w_out': w_out,
            'g_norm_ffn': g_norm_ffn, 'w_router': w_router, 'w_gate': w_gate, 'w_up': w_up,
            'w_down': w_down, 'g_final': g_final}


def reference(x, c, positions, w_ada, b_ada, g_norm_mix, w_in, g_q_lat, g_kv_lat, w_uq, w_ukv,
              rel_bias, w_out, g_norm_ffn, w_router, w_gate, w_up, w_down, g_final):
    B, S, D = x.shape
    inv_freq = ROPE_THETA ** (-jnp.arange(0, MLA_ROPE_DIM, 2, dtype=jnp.float32) / MLA_ROPE_DIM)
    ang = positions.astype(jnp.float32)[..., None] * inv_freq
    cos, sin = jnp.cos(ang), jnp.sin(ang)
    cond = jax.nn.silu(c)
    split_at = [int(v) for v in np.cumsum(SPLIT_SIZES)[:-1]]
    for l in range(DEPTH):
        mod = (cond @ w_ada[l] + b_ada[l])[:, None, :]
        sh1, sc1, gt1, sh2, sc2, gt2 = jnp.split(mod, 6, axis=-1)
        h = rmsnorm(x, g_norm_mix[l]) * (1.0 + sc1) + sh1
        c_q, c_kv, k_pe, dil, gates = jnp.split(h @ w_in[l], split_at, axis=-1)
        o_a = mla_attention(c_q, c_kv, k_pe, g_q_lat[l], g_kv_lat[l], w_uq[l], w_ukv[l], cos, sin)
        dil = dil.reshape(B, S, len(DIL_GROUPS), 3, DIL_HEADS, DIL_HEAD_DIM)
        outs, lses = [], []
        for gi, (win, dd) in enumerate(DIL_GROUPS):
            o_g, lse_g = dilated_band_attention(
                dil[:, :, gi, 0], dil[:, :, gi, 1], dil[:, :, gi, 2], positions,
                rel_bias[:, gi * DIL_HEADS:(gi + 1) * DIL_HEADS], win, dd)
            outs.append(o_g)
            lses.append(lse_g)
        wts = jax.nn.softmax(jnp.stack(lses), axis=0)
        o_b = jnp.einsum('gbsh,gbshc->bshc', wts.astype(x.dtype), jnp.stack(outs)).reshape(B, S, D)
        g_a, g_b = jnp.split(jax.nn.sigmoid(gates), 2, axis=-1)
        x = x + gt1 * ((g_a * o_a + g_b * o_b) @ w_out[l])
        h2 = rmsnorm(x, g_norm_ffn[l]) * (1.0 + sc2) + sh2
        x = x + gt2 * expert_choice_ffn(h2, w_router[l], w_gate[l], w_up[l], w_down[l])
    return rmsnorm(x, g_final)
```

```python
import functools
import math

import jax
import jax.numpy as jnp
from jax import lax
from jax.experimental import pallas as pl
from jax.experimental.pallas import tpu as pltpu

F32 = jnp.float32
BF16 = jnp.bfloat16
I32 = jnp.int32

MLA_HEADS = 8
MLA_Q_RANK = 384
MLA_KV_RANK = 256
MLA_NOPE = 128
MLA_ROPE = 64
MLA_V = 128
ROPE_THETA = 10000.0
DIL_GROUPS = ((128, 1), (512, 4), (2048, 16))
DIL_HEADS = 8
DIL_HEAD_DIM = 128
REL_BUCKETS = 32
REL_MAX_DIST = 1024
N_EXPERTS = 16
EC_CAPACITY_FACTOR = 2
NORM_EPS = 1e-6
NEG_INF = -1e30

LANES = 128
QK_PAD = 256
LAT_PAD = 768
VMEM_LIMIT = 56 * 1024 * 1024


def _cparams(sem):
    return pltpu.CompilerParams(dimension_semantics=sem, vmem_limit_bytes=VMEM_LIMIT)


def _nt_dot(a, b):
    return lax.dot_general(a, b, (((1,), (1,)), ((), ())), preferred_element_type=F32)


def _mod_kernel(c_ref, w_ref, b_ref, o_ref):
    c = c_ref[...]
    cond = c * jax.nn.sigmoid(c)
    o_ref[...] = jnp.dot(cond, w_ref[...], preferred_element_type=F32,
                         precision=lax.Precision.HIGHEST) + b_ref[...]


def _modulation(c, w_ada, b_ada):
    B, D = c.shape
    rows = 8
    c8 = jnp.zeros((rows, D), F32).at[:B].set(c)
    n6 = w_ada.shape[1]
    out = pl.pallas_call(
        _mod_kernel,
        grid=(n6 // D,),
        in_specs=[pl.BlockSpec((rows, D), lambda j: (0, 0)),
                  pl.BlockSpec((D, D), lambda j: (0, j)),
                  pl.BlockSpec((1, D), lambda j: (0, j))],
        out_specs=pl.BlockSpec((rows, D), lambda j: (0, j)),
        out_shape=jax.ShapeDtypeStruct((rows, n6), F32),
        compiler_params=_cparams(("arbitrary",)),
        name="modulation",
    )(c8, w_ada, b_ada.reshape(1, n6))
    return out[:B].reshape(B, 6, D)


def _inproj_kernel(x_ref, mod_ref, g_ref, wl_ref, wm_ref, lat_ref, out_ref, h_ref, *, n_plain):
    n = pl.program_id(2)

    @pl.when(n == 0)
    def _():
        x = x_ref[0]
        y = x * lax.rsqrt(jnp.mean(x * x, axis=-1, keepdims=True) + NORM_EPS) * g_ref[...]
        h = y * (1.0 + mod_ref[0, 1:2, :]) + mod_ref[0, 0:1, :]
        hb = h.astype(BF16)
        h_ref[...] = hb
        lat_ref[0] = jnp.dot(hb, wl_ref[...], preferred_element_type=F32)

    acc = jnp.dot(h_ref[...], wm_ref[...], preferred_element_type=F32)

    @pl.when(n < n_plain)
    def _():
        out_ref[0] = acc.astype(BF16)

    @pl.when(n >= n_plain)
    def _():
        out_ref[0] = jax.nn.sigmoid(acc).astype(BF16)


def _in_projection(x, mod, g, w_lat, w_main, n_plain, tm=1024, tn=1024):
    B, S, D = x.shape
    nmain = w_main.shape[1]
    return pl.pallas_call(
        functools.partial(_inproj_kernel, n_plain=n_plain),
        grid=(B, S // tm, nmain // tn),
        in_specs=[pl.BlockSpec((1, tm, D), lambda b, i, n: (b, i, 0)),
                  pl.BlockSpec((1, 6, D), lambda b, i, n: (b, 0, 0)),
                  pl.BlockSpec((1, D), lambda b, i, n: (0, 0)),
                  pl.BlockSpec((D, LAT_PAD), lambda b, i, n: (0, 0)),
                  pl.BlockSpec((D, tn), lambda b, i, n: (0, n))],
        out_specs=[pl.BlockSpec((1, tm, LAT_PAD), lambda b, i, n: (b, i, 0)),
                   pl.BlockSpec((1, tm, tn), lambda b, i, n: (b, i, n))],
        out_shape=[jax.ShapeDtypeStruct((B, S, LAT_PAD), F32),
                   jax.ShapeDtypeStruct((B, S, nmain), BF16)],
        scratch_shapes=[pltpu.VMEM((tm, D), BF16)],
        compiler_params=_cparams(("parallel", "parallel", "arbitrary")),
        name="in_projection",
    )(x, mod, g, w_lat, w_main)


def _mla_prep_kernel(lat_ref, pos_ref, gq_ref, gkv_ref, wq_ref, wqs_ref, wk_ref, wv_ref,
                     freq_ref, sgn_ref, sel_ref, sels_ref, q_ref, k_ref, v_ref, *, scale):
    lat = lat_ref[0]
    cq = lat[:, :MLA_Q_RANK]
    ckv = lat[:, MLA_Q_RANK:MLA_Q_RANK + MLA_KV_RANK]
    kpe = lat[:, MLA_Q_RANK + MLA_KV_RANK:]
    cqn = (cq * lax.rsqrt(jnp.mean(cq * cq, axis=-1, keepdims=True) + NORM_EPS)
           * gq_ref[...]).astype(BF16)
    ckvn = (ckv * lax.rsqrt(jnp.mean(ckv * ckv, axis=-1, keepdims=True) + NORM_EPS)
            * gkv_ref[...]).astype(BF16)
    ang = pos_ref[0].astype(F32) * freq_ref[...]
    cos = jnp.cos(ang)
    sin = jnp.sin(ang) * sgn_ref[...]
    qa = jnp.dot(cqn, wq_ref[...], preferred_element_type=F32)
    qs = jnp.dot(cqn, wqs_ref[...], preferred_element_type=F32)
    kn = jnp.dot(ckvn, wk_ref[...], preferred_element_type=F32)
    v_ref[0] = jnp.dot(ckvn, wv_ref[...], preferred_element_type=F32).astype(BF16)
    hp = lax.Precision.HIGHEST
    rk = (jnp.dot(kpe, sel_ref[...], preferred_element_type=F32, precision=hp) * cos
          + jnp.dot(kpe, sels_ref[...], preferred_element_type=F32, precision=hp) * sin)
    cos_q = cos * scale
    sin_q = sin * scale
    for h in range(MLA_HEADS):
        sl = slice(h * QK_PAD, (h + 1) * QK_PAD)
        q_ref[0, :, sl] = (qa[:, sl] * cos_q + qs[:, sl] * sin_q).astype(BF16)
        k_ref[0, :, sl] = (kn[:, sl] + rk).astype(BF16)


def _mla_prep(lat, positions, g_q, g_kv, wq, wqs, wk, wv, tm=512):
    B, S, _ = lat.shape
    H = MLA_HEADS
    half = MLA_ROPE // 2
    inv_freq = ROPE_THETA ** (-jnp.arange(0, MLA_ROPE, 2, dtype=F32) / MLA_ROPE)
    freq = jnp.zeros((1, QK_PAD), F32).at[0, MLA_NOPE:MLA_NOPE + MLA_ROPE].set(
        jnp.concatenate([inv_freq, inv_freq]))
    sgn = jnp.zeros((1, QK_PAD), F32).at[0, MLA_NOPE:MLA_NOPE + half].set(-1.0)
    sgn = sgn.at[0, MLA_NOPE + half:MLA_NOPE + MLA_ROPE].set(1.0)
    r = jnp.arange(MLA_ROPE)
    sel = jnp.zeros((LANES, QK_PAD), F32).at[r, MLA_NOPE + r].set(1.0)
    sels = jnp.zeros((LANES, QK_PAD), F32).at[r, MLA_NOPE + (r + half) % MLA_ROPE].set(1.0)
    scale = (MLA_NOPE + MLA_ROPE) ** -0.5
    const = lambda b, i: (0, 0)
    return pl.pallas_call(
        functools.partial(_mla_prep_kernel, scale=scale),
        grid=(B, S // tm),
        in_specs=[pl.BlockSpec((1, tm, LAT_PAD), lambda b, i: (b, i, 0)),
                  pl.BlockSpec((1, tm, 1), lambda b, i: (b, i, 0)),
                  pl.BlockSpec((1, MLA_Q_RANK), const),
                  pl.BlockSpec((1, MLA_KV_RANK), const),
                  pl.BlockSpec((MLA_Q_RANK, H * QK_PAD), const),
                  pl.BlockSpec((MLA_Q_RANK, H * QK_PAD), const),
                  pl.BlockSpec((MLA_KV_RANK, H * QK_PAD), const),
                  pl.BlockSpec((MLA_KV_RANK, H * MLA_V), const),
                  pl.BlockSpec((1, QK_PAD), const),
                  pl.BlockSpec((1, QK_PAD), const),
                  pl.BlockSpec((LANES, QK_PAD), const),
                  pl.BlockSpec((LANES, QK_PAD), const)],
        out_specs=[pl.BlockSpec((1, tm, H * QK_PAD), lambda b, i: (b, i, 0)),
                   pl.BlockSpec((1, tm, H * QK_PAD), lambda b, i: (b, i, 0)),
                   pl.BlockSpec((1, tm, H * MLA_V), lambda b, i: (b, i, 0))],
        out_shape=[jax.ShapeDtypeStruct((B, S, H * QK_PAD), BF16),
                   jax.ShapeDtypeStruct((B, S, H * QK_PAD), BF16),
                   jax.ShapeDtypeStruct((B, S, H * MLA_V), BF16)],
        compiler_params=_cparams(("parallel", "parallel")),
        name="mla_prep",
    )(lat, positions.reshape(B, S, 1), g_q, g_kv, wq, wqs, wk, wv, freq, sgn, sel, sels)


def _flash_kernel(q_ref, k_ref, v_ref, o_ref, m_ref, l_ref, acc_ref):
    ki = pl.program_id(3)

    @pl.when(ki == 0)
    def _():
        m_ref[...] = jnp.full(m_ref.shape, -jnp.inf, F32)
        l_ref[...] = jnp.zeros(l_ref.shape, F32)
        acc_ref[...] = jnp.zeros(acc_ref.shape, F32)

    s = _nt_dot(q_ref[0], k_ref[0])
    m_prev = m_ref[...]
    m_new = jnp.maximum(m_prev, jnp.max(s, axis=-1, keepdims=True))
    alpha = jnp.exp(m_prev - m_new)
    p = jnp.exp(s - m_new)
    l_ref[...] = alpha * l_ref[...] + jnp.sum(p, axis=-1, keepdims=True)
    acc_ref[...] = alpha * acc_ref[...] + jnp.dot(p.astype(BF16), v_ref[0],
                                                  preferred_element_type=F32)
    m_ref[...] = m_new

    @pl.when(ki == pl.num_programs(3) - 1)
    def _():
        o_ref[0] = (acc_ref[...] / l_ref[...]).astype(BF16)


def _flash_attention(q, k, v, tq=1024, tk=1024):
    B, S, _ = q.shape
    H = MLA_HEADS
    return pl.pallas_call(
        _flash_kernel,
        grid=(B, H, S // tq, S // tk),
        in_specs=[pl.BlockSpec((1, tq, QK_PAD), lambda b, h, i, j: (b, i, h)),
                  pl.BlockSpec((1, tk, QK_PAD), lambda b, h, i, j: (b, j, h)),
                  pl.BlockSpec((1, tk, MLA_V), lambda b, h, i, j: (b, j, h))],
        out_specs=pl.BlockSpec((1, tq, MLA_V), lambda b, h, i, j: (b, i, h)),
        out_shape=jax.ShapeDtypeStruct((B, S, H * MLA_V), BF16),
        scratch_shapes=[pltpu.VMEM((tq, 1), F32), pltpu.VMEM((tq, 1), F32),
                        pltpu.VMEM((tq, MLA_V), F32)],
        compiler_params=_cparams(("parallel", "parallel", "parallel", "arbitrary")),
        name="mla_flash",
    )(q, k, v)


def _t5_bucket(rel):
    nb = REL_BUCKETS // 2
    max_exact = nb // 2
    ret = jnp.where(rel > 0, nb, 0)
    n = jnp.abs(rel)
    nf = jnp.maximum(n, 1).astype(F32)
    large = max_exact + (jnp.log(nf / max_exact) / math.log(REL_MAX_DIST / max_exact)
                         * (nb - max_exact)).astype(I32)
    large = jnp.minimum(large, nb - 1)
    return ret + jnp.where(n < max_exact, n, large)


def _dilated_kernel(*refs, r, tq, n_rows, has_prev):
    if has_prev:
        (q_ref, kp_ref, kc_ref, kn_ref, vp_ref, vc_ref, vn_ref, pq_ref, pk_ref, tab_ref,
         op_ref, lp_ref, o_ref, l_ref, kcat_ref, vcat_ref) = refs
    else:
        (q_ref, kp_ref, kc_ref, kn_ref, vp_ref, vc_ref, vn_ref, pq_ref, pk_ref, tab_ref,
         o_ref, l_ref, kcat_ref, vcat_ref) = refs
    n = pl.program_id(2)
    nk = tq + 2 * r
    kcat_ref[0:r, :] = kp_ref[0]
    kcat_ref[r:r + tq, :] = kc_ref[0]
    kcat_ref[r + tq:nk, :] = kn_ref[0]
    vcat_ref[0:r, :] = vp_ref[0]
    vcat_ref[r:r + tq, :] = vc_ref[0]
    vcat_ref[r + tq:nk, :] = vn_ref[0]

    qi = n * tq + lax.broadcasted_iota(I32, (tq, nk), 0)
    kj = n * tq - r + lax.broadcasted_iota(I32, (tq, nk), 1)
    valid = (jnp.abs(kj - qi) <= r) & (kj >= 0) & (kj < n_rows)
    rel = pk_ref[0, 0, 0] - pq_ref[0, 0]
    bucket = _t5_bucket(rel)
    lane_grp = lax.broadcasted_iota(I32, (tq, LANES), 1) // (LANES // DIL_HEADS)
    scale = DIL_HEAD_DIM ** -0.5
    lse_tile = jnp.zeros((tq, LANES), F32)
    for h in range(DIL_HEADS):
        sl = slice(h * DIL_HEAD_DIM, (h + 1) * DIL_HEAD_DIM)
        s = _nt_dot(q_ref[0, :, sl], kcat_ref[:, sl]) * scale
        tab = jnp.broadcast_to(tab_ref[h:h + 1, :], (tq, LANES))
        bias = jnp.concatenate(
            [jnp.take_along_axis(tab, bucket[:, c * LANES:(c + 1) * LANES], axis=1)
             for c in range(nk // LANES)], axis=1)
        s = jnp.where(valid, s + bias, NEG_INF)
        m = jnp.max(s, axis=-1, keepdims=True)
        p = jnp.exp(s - m)
        den = jnp.sum(p, axis=-1, keepdims=True)
        o = jnp.dot(p.astype(BF16), vcat_ref[:, sl], preferred_element_type=F32) / den
        lse = m + jnp.log(den)
        if has_prev:
            lse_p = lp_ref[0, :, h * (LANES // DIL_HEADS):h * (LANES // DIL_HEADS) + 1]
            mx = jnp.maximum(lse_p, lse)
            w_p = jnp.exp(lse_p - mx)
            w_g = jnp.exp(lse - mx)
            tot = w_p + w_g
            o = (w_p * op_ref[0, :, sl].astype(F32) + w_g * o) / tot
            lse = mx + jnp.log(tot)
        o_ref[0, :, sl] = o.astype(BF16)
        lse_tile = jnp.where(lane_grp == h, lse, lse_tile)
    l_ref[0] = lse_tile


def _dilated_group(main, positions, table, gi, window, d, ncol, prev, tq=128):
    B, S, _ = main.shape
    HD = DIL_HEADS * DIL_HEAD_DIM
    r = window // (2 * d)
    L = S // d
    nt = L // tq
    rb = tq // r
    nk = tq + 2 * r
    mainv = main.reshape(B, L, d * ncol * HD)
    cq, ck, cv = 3 * gi, 3 * gi + 1, 3 * gi + 2
    pcls = positions.reshape(B, L, d).transpose(0, 2, 1)
    pq = pcls.reshape(B, d, L, 1)
    kidx = jnp.clip(jnp.arange(nt)[:, None] * tq - r + jnp.arange(nk)[None, :], 0, L - 1)
    pk = pcls[:, :, kidx].reshape(B, d, nt, 1, nk)
    tab = jnp.zeros((DIL_HEADS, LANES), F32).at[:, :REL_BUCKETS].set(
        table[:, gi * DIL_HEADS:(gi + 1) * DIL_HEADS].T)

    def cur(c):
        return pl.BlockSpec((1, tq, HD), lambda b, g, n: (b, n, g * ncol + c))

    def prv(c):
        return pl.BlockSpec((1, r, HD), lambda b, g, n: (b, jnp.maximum(n * rb - 1, 0), g * ncol + c))

    def nxt(c):
        return pl.BlockSpec((1, r, HD),
                            lambda b, g, n: (b, jnp.minimum((n + 1) * rb, L // r - 1), g * ncol + c))

    in_specs = [cur(cq), prv(ck), cur(ck), nxt(ck), prv(cv), cur(cv), nxt(cv),
                pl.BlockSpec((1, 1, tq, 1), lambda b, g, n: (b, g, n, 0)),
                pl.BlockSpec((1, 1, 1, 1, nk), lambda b, g, n: (b, g, n, 0, 0)),
                pl.BlockSpec((DIL_HEADS, LANES), lambda b, g, n: (0, 0))]
    args = [mainv] * 7 + [pq, pk, tab]
    o_spec = pl.BlockSpec((1, tq, HD), lambda b, g, n: (b, n, g))
    l_spec = pl.BlockSpec((1, tq, LANES), lambda b, g, n: (b, n, g))
    if prev is not None:
        o_prev, l_prev = prev
        in_specs += [o_spec, l_spec]
        args += [o_prev.reshape(B, L, d * HD), l_prev.reshape(B, L, d * LANES)]
    o, l = pl.pallas_call(
        functools.partial(_dilated_kernel, r=r, tq=tq, n_rows=L, has_prev=prev is not None),
        grid=(B, d, nt),
        in_specs=in_specs,
        out_specs=[o_spec, l_spec],
        out_shape=[jax.ShapeDtypeStruct((B, L, d * HD), BF16),
                   jax.ShapeDtypeStruct((B, L, d * LANES), F32)],
        scratch_shapes=[pltpu.VMEM((nk, HD), BF16), pltpu.VMEM((nk, HD), BF16)],
        compiler_params=_cparams(("parallel", "parallel", "parallel")),
        name=f"dilated_g{gi}",
    )(*args)
    return o.reshape(B, S, HD), l.reshape(B, S, LANES)


def _outproj_kernel(oa_ref, ob_ref, ga_ref, gb_ref, x_ref, mod_ref, g_ref, wo_ref, wr_ref,
                    x1_ref, h2_ref, aff_ref, afft_ref):
    comb = (ga_ref[0].astype(F32) * oa_ref[0].astype(F32)
            + gb_ref[0].astype(F32) * ob_ref[0].astype(F32)).astype(BF16)
    x1 = x_ref[0] + mod_ref[0, 2:3, :] * jnp.dot(comb, wo_ref[...], preferred_element_type=F32)
    x1_ref[0] = x1
    y = x1 * lax.rsqrt(jnp.mean(x1 * x1, axis=-1, keepdims=True) + NORM_EPS) * g_ref[...]
    h2 = y * (1.0 + mod_ref[0, 4:5, :]) + mod_ref[0, 3:4, :]
    h2_ref[0] = h2.astype(BF16)
    logits = jnp.dot(h2, wr_ref[...], preferred_element_type=F32,
                     precision=lax.Precision.HIGHEST)
    lane = lax.broadcasted_iota(I32, logits.shape, 1)
    logits = jnp.where(lane < N_EXPERTS, logits, -jnp.inf)
    e = jnp.exp(logits - jnp.max(logits, axis=-1, keepdims=True))
    aff = e / jnp.sum(e, axis=-1, keepdims=True)
    aff_ref[0] = aff
    afft_ref[0] = aff.T[:N_EXPERTS, :]


def _out_projection(o_a, o_b, main, ncol, x, mod, g, w_out, w_router, tm=512):
    B, S, D = x.shape
    const = lambda b, i: (0, 0)
    row = lambda b, i: (b, i, 0)
    wr = jnp.zeros((D, LANES), F32).at[:, :N_EXPERTS].set(w_router)
    return pl.pallas_call(
        _outproj_kernel,
        grid=(B, S // tm),
        in_specs=[pl.BlockSpec((1, tm, D), row),
                  pl.BlockSpec((1, tm, D), row),
                  pl.BlockSpec((1, tm, D), lambda b, i: (b, i, ncol - 2)),
                  pl.BlockSpec((1, tm, D), lambda b, i: (b, i, ncol - 1)),
                  pl.BlockSpec((1, tm, D), row),
                  pl.BlockSpec((1, 6, D), lambda b, i: (b, 0, 0)),
                  pl.BlockSpec((1, D), const),
                  pl.BlockSpec((D, D), const),
                  pl.BlockSpec((D, LANES), const)],
        out_specs=[pl.BlockSpec((1, tm, D), row),
                   pl.BlockSpec((1, tm, D), row),
                   pl.BlockSpec((1, tm, LANES), row),
                   pl.BlockSpec((1, N_EXPERTS, tm), lambda b, i: (b, 0, i))],
        out_shape=[jax.ShapeDtypeStruct((B, S, D), F32),
                   jax.ShapeDtypeStruct((B, S, D), BF16),
                   jax.ShapeDtypeStruct((B, S, LANES), F32),
                   jax.ShapeDtypeStruct((B, N_EXPERTS, S), F32)],
        compiler_params=_cparams(("parallel", "parallel")),
        name="out_projection",
    )(o_a, o_b, main, main, x, mod, g, w_out, wr)


def _select_kernel(afft_ref, thr_ref, tie_ref, slot_ref, *, cap, chunk):
    aff = afft_ref[0]
    E, S = aff.shape
    bits = lax.bitcast_convert_type(aff, I32)

    def count(mask):
        return jnp.sum(mask.astype(I32), axis=1, keepdims=True)

    def thr_step(i, v):
        cand = v | jnp.left_shift(jnp.int32(1), 30 - i)
        return jnp.where(count(bits >= cand) >= cap, cand, v)

    thr = lax.fori_loop(0, 31, thr_step, jnp.zeros((E, 1), I32))
    need = cap - count(bits > thr)
    eq = bits == thr
    idx = lax.broadcasted_iota(I32, (E, S), 1)
    nbits = max(1, (S - 1).bit_length())

    def tie_step(i, j):
        cand = j | jnp.left_shift(jnp.int32(1), nbits - 1 - i)
        return jnp.where(count(eq & (idx < cand)) < need, cand, j)

    tie = lax.fori_loop(0, nbits, tie_step, jnp.zeros((E, 1), I32))
    sel = (bits > thr) | (eq & (idx <= tie))
    thr_ref[0] = jnp.broadcast_to(thr, (E, LANES))
    tie_ref[0] = jnp.broadcast_to(tie, (E, LANES))

    upper = (lax.broadcasted_iota(I32, (chunk, chunk), 0)
             <= lax.broadcasted_iota(I32, (chunk, chunk), 1)).astype(BF16)
    carry = jnp.zeros((E, 1), F32)
    self_ = sel.astype(F32)
    for c in range(S // chunk):
        scf = self_[:, c * chunk:(c + 1) * chunk]
        incl = jnp.dot(scf.astype(BF16), upper, preferred_element_type=F32)
        pos = (incl - scf + carry).astype(I32)
        slot_ref[0, :, c * chunk:(c + 1) * chunk] = jnp.where(scf > 0.0, pos, -1)
        carry = carry + incl[:, chunk - 1:chunk]


def _select(aff_t, cap, chunk=512):
    B, E, S = aff_t.shape
    return pl.pallas_call(
        functools.partial(_select_kernel, cap=cap, chunk=chunk),
        grid=(B,),
        in_specs=[pl.BlockSpec((1, E, S), lambda b: (b, 0, 0))],
        out_specs=[pl.BlockSpec((1, E, LANES), lambda b: (b, 0, 0)),
                   pl.BlockSpec((1, E, LANES), lambda b: (b, 0, 0)),
                   pl.BlockSpec((1, E, S), lambda b: (b, 0, 0))],
        out_shape=[jax.ShapeDtypeStruct((B, E, LANES), I32),
                   jax.ShapeDtypeStruct((B, E, LANES), I32),
                   jax.ShapeDtypeStruct((B, E, S), I32)],
        compiler_params=_cparams(("parallel",)),
        name="ec_select",
    )(aff_t)


def _gather_kernel(slot_ref, h_ref, xe_ref):
    t = pl.program_id(2)
    cap = xe_ref.shape[2]
    tb = h_ref.shape[1]
    onehot = (lax.broadcasted_iota(I32, (cap, tb), 0) == slot_ref[0, 0]).astype(BF16)
    part = jnp.dot(onehot, h_ref[0], preferred_element_type=F32).astype(BF16)

    @pl.when(t == 0)
    def _():
        xe_ref[0, 0] = part

    @pl.when(t > 0)
    def _():
        xe_ref[0, 0] = xe_ref[0, 0] + part


def _gather(slot, h2, cap, tb=1024):
    B, E, S = slot.shape
    D = h2.shape[-1]
    return pl.pallas_call(
        _gather_kernel,
        grid=(B, E, S // tb),
        in_specs=[pl.BlockSpec((1, 1, 1, tb), lambda b, e, t: (b, e, 0, t)),
                  pl.BlockSpec((1, tb, D), lambda b, e, t: (b, t, 0))],
        out_specs=pl.BlockSpec((1, 1, cap, D), lambda b, e, t: (b, e, 0, 0)),
        out_shape=jax.ShapeDtypeStruct((B, E, cap, D), BF16),
        compiler_params=_cparams(("parallel", "parallel", "arbitrary")),
        name="ec_gather",
    )(slot.reshape(B, E, 1, S), h2)


def _ffn_kernel(xe_ref, wg_ref, wu_ref, wd_ref, ye_ref, acc_ref):
    f = pl.program_id(2)
    nb, _, cap, D = xe_ref.shape
    x = xe_ref[...].reshape(nb * cap, D)
    g = jnp.dot(x, wg_ref[0].astype(BF16), preferred_element_type=F32)
    u = jnp.dot(x, wu_ref[0].astype(BF16), preferred_element_type=F32)
    hid = (g * jax.nn.sigmoid(g) * u).astype(BF16)
    part = jnp.dot(hid, wd_ref[0].astype(BF16), preferred_element_type=F32)

    @pl.when(f == 0)
    def _():
        acc_ref[...] = part

    @pl.when(f > 0)
    def _():
        acc_ref[...] += part

    @pl.when(f == pl.num_programs(2) - 1)
    def _():
        ye_ref[...] = acc_ref[...].astype(BF16).reshape(ye_ref.shape)


def _expert_ffn(xe, w_gate, w_up, w_down, nb=2, tf=512):
    B, E, cap, D = xe.shape
    F = w_gate.shape[-1]
    return pl.pallas_call(
        _ffn_kernel,
        grid=(E, B // nb, F // tf),
        in_specs=[pl.BlockSpec((nb, 1, cap, D), lambda e, b, f: (b, e, 0, 0)),
                  pl.BlockSpec((1, D, tf), lambda e, b, f: (e, 0, f)),
                  pl.BlockSpec((1, D, tf), lambda e, b, f: (e, 0, f)),
                  pl.BlockSpec((1, tf, D), lambda e, b, f: (e, f, 0))],
        out_specs=pl.BlockSpec((nb, 1, cap, D), lambda e, b, f: (b, e, 0, 0)),
        out_shape=jax.ShapeDtypeStruct((B, E, cap, D), BF16),
        scratch_shapes=[pltpu.VMEM((nb * cap, D), F32)],
        compiler_params=_cparams(("parallel", "parallel", "arbitrary")),
        name="ec_ffn",
    )(xe, w_gate, w_up, w_down)


def _combine_kernel(aff_ref, thr_ref, tie_ref, ye_ref, x1_ref, mod_ref, g_ref, out_ref,
                    slot_ref, acc_ref, carry_ref):
    t = pl.program_id(1)
    e = pl.program_id(2)
    tb = aff_ref.shape[1]
    cap = ye_ref.shape[2]
    lane = lax.broadcasted_iota(I32, (tb, LANES), 1)

    @pl.when(e == 0)
    def _():
        @pl.when(t == 0)
        def _():
            carry_ref[...] = jnp.zeros(carry_ref.shape, F32)

        bits = lax.bitcast_convert_type(aff_ref[0], I32)
        tok = t * tb + lax.broadcasted_iota(I32, (tb, LANES), 0)
        thr = thr_ref[0]
        sel = ((bits > thr) | ((bits == thr) & (tok <= tie_ref[0]))) & (lane < N_EXPERTS)
        self_ = sel.astype(F32)
        lower = (lax.broadcasted_iota(I32, (tb, tb), 0)
                 >= lax.broadcasted_iota(I32, (tb, tb), 1)).astype(BF16)
        incl = jnp.dot(lower, self_.astype(BF16), preferred_element_type=F32)
        pos = (incl - self_ + carry_ref[...]).astype(I32)
        slot_ref[...] = jnp.where(sel, pos, -1)
        carry_ref[...] = carry_ref[...] + incl[tb - 1:tb, :]
        acc_ref[...] = jnp.zeros(acc_ref.shape, F32)

    pick = lane == e
    slot_e = jnp.sum(jnp.where(pick, slot_ref[...], 0), axis=1, keepdims=True)
    gate_e = jnp.sum(jnp.where(pick, aff_ref[0], 0.0), axis=1, keepdims=True)
    onehot = (lax.broadcasted_iota(I32, (tb, cap), 1) == slot_e).astype(BF16)
    acc_ref[...] += gate_e * jnp.dot(onehot, ye_ref[0, 0], preferred_element_type=F32)

    @pl.when(e == pl.num_programs(2) - 1)
    def _():
        x2 = x1_ref[0] + mod_ref[0, 5:6, :] * acc_ref[...]
        out_ref[0] = (x2 * lax.rsqrt(jnp.mean(x2 * x2, axis=-1, keepdims=True) + NORM_EPS)
                      * g_ref[...])


def _combine(aff, thr, tie, ye, x1, mod, g_final, tb=512):
    B, S, D = x1.shape
    E, cap = ye.shape[1], ye.shape[2]
    return pl.pallas_call(
        _combine_kernel,
        grid=(B, S // tb, E),
        in_specs=[pl.BlockSpec((1, tb, LANES), lambda b, t, e: (b, t, 0)),
                  pl.BlockSpec((1, 1, LANES), lambda b, t, e: (b, 0, 0)),
                  pl.BlockSpec((1, 1, LANES), lambda b, t, e: (b, 0, 0)),
                  pl.BlockSpec((1, 1, cap, D), lambda b, t, e: (b, e, 0, 0)),
                  pl.BlockSpec((1, tb, D), lambda b, t, e: (b, t, 0)),
                  pl.BlockSpec((1, 6, D), lambda b, t, e: (b, 0, 0)),
                  pl.BlockSpec((1, D), lambda b, t, e: (0, 0))],
        out_specs=pl.BlockSpec((1, tb, D), lambda b, t, e: (b, t, 0)),
        out_shape=jax.ShapeDtypeStruct((B, S, D), F32),
        scratch_shapes=[pltpu.VMEM((tb, LANES), I32), pltpu.VMEM((tb, D), F32),
                        pltpu.VMEM((1, LANES), F32)],
        compiler_params=_cparams(("parallel", "arbitrary", "arbitrary")),
        name="ec_combine",
    )(aff, thr, tie, ye, x1, mod, g_final)


def _prep_weights(w_in, w_uq, w_ukv):
    D = w_in.shape[0]
    H = MLA_HEADS
    n_lat = MLA_Q_RANK + MLA_KV_RANK + MLA_ROPE
    w_lat = jnp.zeros((D, LAT_PAD), F32).at[:, :n_lat].set(w_in[:, :n_lat]).astype(BF16)
    w_main = w_in[:, n_lat:].astype(BF16)
    half = MLA_ROPE // 2
    uq = w_uq.reshape(MLA_Q_RANK, H, MLA_NOPE + MLA_ROPE)
    pe = uq[:, :, MLA_NOPE:]
    zq = jnp.zeros((MLA_Q_RANK, H, QK_PAD - MLA_NOPE - MLA_ROPE), F32)
    wq = jnp.concatenate([uq[:, :, :MLA_NOPE], pe, zq], axis=2)
    wqs = jnp.concatenate([jnp.zeros_like(uq[:, :, :MLA_NOPE]), pe[:, :, half:], pe[:, :, :half], zq],
                          axis=2)
    ukv = w_ukv.reshape(MLA_KV_RANK, H, MLA_NOPE + MLA_V)
    wk = jnp.concatenate([ukv[:, :, :MLA_NOPE],
                          jnp.zeros((MLA_KV_RANK, H, QK_PAD - MLA_NOPE), F32)], axis=2)
    wv = ukv[:, :, MLA_NOPE:]
    return (w_lat, w_main, wq.reshape(MLA_Q_RANK, H * QK_PAD).astype(BF16),
            wqs.reshape(MLA_Q_RANK, H * QK_PAD).astype(BF16),
            wk.reshape(MLA_KV_RANK, H * QK_PAD).astype(BF16),
            wv.reshape(MLA_KV_RANK, H * MLA_V).astype(BF16))


def kernel(x, c, positions, w_ada, b_ada, g_norm_mix, w_in, g_q_lat, g_kv_lat, w_uq, w_ukv,
           rel_bias, w_out, g_norm_ffn, w_router, w_gate, w_up, w_down, g_final):
    B, S, D = x.shape
    assert w_ada.shape[0] == 1, "the final norm is fused into the (single) layer's last kernel"
    for l in range(w_ada.shape[0]):
        mod = _modulation(c, w_ada[l], b_ada[l])
        w_lat, w_main, wq, wqs, wk, wv = _prep_weights(w_in[l], w_uq[l], w_ukv[l])
        ncol = w_main.shape[1] // D
        lat, main = _in_projection(x, mod, g_norm_mix[l].reshape(1, D), w_lat, w_main,
                                   n_plain=ncol - 2)
        q, k, v = _mla_prep(lat, positions, g_q_lat[l].reshape(1, -1), g_kv_lat[l].reshape(1, -1),
                            wq, wqs, wk, wv)
        o_a = _flash_attention(q, k, v)
        prev = None
        for gi, (window, d) in enumerate(DIL_GROUPS):
            prev = _dilated_group(main, positions, rel_bias, gi, window, d, ncol, prev)
        o_b = prev[0]
        x1, h2, aff, aff_t = _out_projection(o_a, o_b, main, ncol, x, mod,
                                             g_norm_ffn[l].reshape(1, D),
                                             w_out[l].astype(BF16), w_router[l])
        cap = EC_CAPACITY_FACTOR * S // N_EXPERTS
        thr, tie, slot = _select(aff_t, cap)
        xe = _gather(slot, h2, cap)
        ye = _expert_ffn(xe, w_gate[l], w_up[l], w_down[l])
        pad = jnp.zeros((B, 1, LANES - N_EXPERTS), I32)
        thr_l = jnp.concatenate([thr[:, :, 0].reshape(B, 1, N_EXPERTS), pad], axis=2)
        tie_l = jnp.concatenate([tie[:, :, 0].reshape(B, 1, N_EXPERTS), pad], axis=2)
        x = _combine(aff, thr_l, tie_l, ye, x1, mod, g_final.reshape(1, D))
    return x
```

```python
import functools
import math

import jax
import jax.numpy as jnp
from jax import lax
from jax.experimental import pallas as pl
from jax.experimental.pallas import tpu as pltpu

F32 = jnp.float32
BF16 = jnp.bfloat16
I32 = jnp.int32

MLA_HEADS = 8
MLA_Q_RANK = 384
MLA_KV_RANK = 256
MLA_NOPE = 128
MLA_ROPE = 64
MLA_V = 128
ROPE_THETA = 10000.0
DIL_GROUPS = ((128, 1), (512, 4), (2048, 16))
DIL_HEADS = 8
DIL_HEAD_DIM = 128
REL_BUCKETS = 32
REL_MAX_DIST = 1024
N_EXPERTS = 16
EC_CAPACITY_FACTOR = 2
NORM_EPS = 1e-6
NEG_INF = -1e30

LANES = 128
QK_PAD = 256
LAT_PAD = 768
VMEM_LIMIT = 56 * 1024 * 1024


def _cparams(sem):
    return pltpu.CompilerParams(dimension_semantics=sem, vmem_limit_bytes=VMEM_LIMIT)


def _nt_dot(a, b):
    return lax.dot_general(a, b, (((1,), (1,)), ((), ())), preferred_element_type=F32)


def _mod_kernel(c_ref, w_ref, b_ref, o_ref):
    c = c_ref[...]
    cond = c * jax.nn.sigmoid(c)
    o_ref[...] = jnp.dot(cond, w_ref[...], preferred_element_type=F32,
                         precision=lax.Precision.HIGHEST) + b_ref[...]


def _modulation(c, w_ada, b_ada):
    B, D = c.shape
    rows = 8
    c8 = jnp.zeros((rows, D), F32).at[:B].set(c)
    n6 = w_ada.shape[1]
    out = pl.pallas_call(
        _mod_kernel,
        grid=(n6 // D,),
        in_specs=[pl.BlockSpec((rows, D), lambda j: (0, 0)),
                  pl.BlockSpec((D, D), lambda j: (0, j)),
                  pl.BlockSpec((1, D), lambda j: (0, j))],
        out_specs=pl.BlockSpec((rows, D), lambda j: (0, j)),
        out_shape=jax.ShapeDtypeStruct((rows, n6), F32),
        compiler_params=_cparams(("arbitrary",)),
        name="modulation",
    )(c8, w_ada, b_ada.reshape(1, n6))
    return out[:B].reshape(B, 6, D)


def _inproj_kernel(*refs, d, n_plain, n_tiles, with_lat):
    nx = len(refs) - (7 if with_lat else 5)
    x_refs = refs[:nx]
    if with_lat:
        mod_ref, g_ref, wl_ref, wm_ref, lat_ref, out_ref, h_ref = refs[nx:]
    else:
        mod_ref, g_ref, wm_ref, out_ref, h_ref = refs[nx:]
    n = pl.program_id(2)
    tm = x_refs[0].shape[1]
    rows = tm // d

    @pl.when(n == 0)
    def _():
        for g in range(d):
            if d == 1:
                x = x_refs[0][0]
            else:
                x = jnp.concatenate([xr[0, pl.ds(g, rows, stride=d), :] for xr in x_refs], axis=1)
            y = x * lax.rsqrt(jnp.mean(x * x, axis=-1, keepdims=True) + NORM_EPS) * g_ref[...]
            h = y * (1.0 + mod_ref[0, 1:2, :]) + mod_ref[0, 0:1, :]
            h_ref[g * rows:(g + 1) * rows, :] = h.astype(BF16)
        if with_lat:
            lat_ref[0] = jnp.dot(h_ref[...], wl_ref[...], preferred_element_type=F32)

    acc = jnp.dot(h_ref[...], wm_ref[...], preferred_element_type=F32)
    tn = acc.shape[1]
    if n_plain == n_tiles:
        out_ref[0] = acc.astype(BF16).reshape(d, rows, tn)
    else:
        @pl.when(n < n_plain)
        def _():
            out_ref[0] = acc.astype(BF16).reshape(d, rows, tn)

        @pl.when(n >= n_plain)
        def _():
            out_ref[0] = jax.nn.sigmoid(acc).astype(BF16).reshape(d, rows, tn)


def _in_projection(x, mod, g, w_main, d, n_plain, w_lat=None, tm=1024, tn=1024):
    B, S, D = x.shape
    nmain = w_main.shape[1]
    n_tiles = nmain // tn
    with_lat = w_lat is not None
    if d == 1:
        in_specs = [pl.BlockSpec((1, tm, D), lambda b, i, n: (b, i, 0))]
    else:
        in_specs = [pl.BlockSpec((1, tm, LANES), functools.partial(lambda b, i, n, j: (b, i, j), j=j))
                    for j in range(D // LANES)]
    args = [x] * len(in_specs) + [mod, g]
    in_specs += [pl.BlockSpec((1, 6, D), lambda b, i, n: (b, 0, 0)),
                 pl.BlockSpec((1, D), lambda b, i, n: (0, 0))]
    out_specs = [pl.BlockSpec((1, d, tm // d, tn), lambda b, i, n: (b, 0, i, n))]
    out_shape = [jax.ShapeDtypeStruct((B, d, S // d, nmain), BF16)]
    if with_lat:
        in_specs.append(pl.BlockSpec((D, LAT_PAD), lambda b, i, n: (0, 0)))
        args.append(w_lat)
        out_specs.insert(0, pl.BlockSpec((1, tm, LAT_PAD), lambda b, i, n: (b, i, 0)))
        out_shape.insert(0, jax.ShapeDtypeStruct((B, S, LAT_PAD), F32))
    in_specs.append(pl.BlockSpec((D, tn), lambda b, i, n: (0, n)))
    args.append(w_main)
    return pl.pallas_call(
        functools.partial(_inproj_kernel, d=d, n_plain=n_plain, n_tiles=n_tiles, with_lat=with_lat),
        grid=(B, S // tm, n_tiles),
        in_specs=in_specs,
        out_specs=out_specs,
        out_shape=out_shape,
        scratch_shapes=[pltpu.VMEM((tm, D), BF16)],
        compiler_params=_cparams(("parallel", "parallel", "arbitrary")),
        name=f"in_projection_d{d}",
    )(*args)


def _mla_prep_kernel(lat_ref, pos_ref, gq_ref, gkv_ref, wq_ref, wqs_ref, wk_ref, wv_ref,
                     freq_ref, sgn_ref, sel_ref, sels_ref, one_ref, q_ref, k_ref, v_ref, *, scale):
    lat = lat_ref[0]
    cq = lat[:, :MLA_Q_RANK]
    ckv = lat[:, MLA_Q_RANK:MLA_Q_RANK + MLA_KV_RANK]
    kpe = lat[:, MLA_Q_RANK + MLA_KV_RANK:]
    cqn = (cq * lax.rsqrt(jnp.mean(cq * cq, axis=-1, keepdims=True) + NORM_EPS)
           * gq_ref[...]).astype(BF16)
    ckvn = (ckv * lax.rsqrt(jnp.mean(ckv * ckv, axis=-1, keepdims=True) + NORM_EPS)
            * gkv_ref[...]).astype(BF16)
    ang = pos_ref[0].astype(F32) * freq_ref[...]
    cos = jnp.cos(ang)
    sin = jnp.sin(ang) * sgn_ref[...]
    qa = jnp.dot(cqn, wq_ref[...], preferred_element_type=F32)
    qs = jnp.dot(cqn, wqs_ref[...], preferred_element_type=F32)
    kn = jnp.dot(ckvn, wk_ref[...], preferred_element_type=F32)
    v_ref[0] = (jnp.dot(ckvn, wv_ref[...], preferred_element_type=F32) + one_ref[...]).astype(BF16)
    hp = lax.Precision.HIGHEST
    rk = (jnp.dot(kpe, sel_ref[...], preferred_element_type=F32, precision=hp) * cos
          + jnp.dot(kpe, sels_ref[...], preferred_element_type=F32, precision=hp) * sin)
    cos_q = cos * scale
    sin_q = sin * scale
    for h in range(MLA_HEADS):
        sl = slice(h * QK_PAD, (h + 1) * QK_PAD)
        q_ref[0, :, sl] = (qa[:, sl] * cos_q + qs[:, sl] * sin_q).astype(BF16)
        k_ref[0, :, sl] = (kn[:, sl] + rk).astype(BF16)


def _mla_prep(lat, positions, g_q, g_kv, wq, wqs, wk, wv, tm=512):
    B, S, _ = lat.shape
    H = MLA_HEADS
    half = MLA_ROPE // 2
    inv_freq = ROPE_THETA ** (-jnp.arange(0, MLA_ROPE, 2, dtype=F32) / MLA_ROPE)
    freq = jnp.zeros((1, QK_PAD), F32).at[0, MLA_NOPE:MLA_NOPE + MLA_ROPE].set(
        jnp.concatenate([inv_freq, inv_freq]))
    sgn = jnp.zeros((1, QK_PAD), F32).at[0, MLA_NOPE:MLA_NOPE + half].set(-1.0)
    sgn = sgn.at[0, MLA_NOPE + half:MLA_NOPE + MLA_ROPE].set(1.0)
    r = jnp.arange(MLA_ROPE)
    sel = jnp.zeros((LANES, QK_PAD), F32).at[r, MLA_NOPE + r].set(1.0)
    sels = jnp.zeros((LANES, QK_PAD), F32).at[r, MLA_NOPE + (r + half) % MLA_ROPE].set(1.0)
    scale = (MLA_NOPE + MLA_ROPE) ** -0.5 * math.log2(math.e)
    ones_col = jnp.zeros((H, QK_PAD), F32).at[:, MLA_V].set(1.0).reshape(1, H * QK_PAD)
    const = lambda b, i: (0, 0)
    return pl.pallas_call(
        functools.partial(_mla_prep_kernel, scale=scale),
        grid=(B, S // tm),
        in_specs=[pl.BlockSpec((1, tm, LAT_PAD), lambda b, i: (b, i, 0)),
                  pl.BlockSpec((1, tm, 1), lambda b, i: (b, i, 0)),
                  pl.BlockSpec((1, MLA_Q_RANK), const),
                  pl.BlockSpec((1, MLA_KV_RANK), const),
                  pl.BlockSpec((MLA_Q_RANK, H * QK_PAD), const),
                  pl.BlockSpec((MLA_Q_RANK, H * QK_PAD), const),
                  pl.BlockSpec((MLA_KV_RANK, H * QK_PAD), const),
                  pl.BlockSpec((MLA_KV_RANK, H * QK_PAD), const),
                  pl.BlockSpec((1, QK_PAD), const),
                  pl.BlockSpec((1, QK_PAD), const),
                  pl.BlockSpec((LANES, QK_PAD), const),
                  pl.BlockSpec((LANES, QK_PAD), const),
                  pl.BlockSpec((1, H * QK_PAD), const)],
        out_specs=[pl.BlockSpec((1, tm, H * QK_PAD), lambda b, i: (b, i, 0)),
                   pl.BlockSpec((1, tm, H * QK_PAD), lambda b, i: (b, i, 0)),
                   pl.BlockSpec((1, tm, H * QK_PAD), lambda b, i: (b, i, 0))],
        out_shape=[jax.ShapeDtypeStruct((B, S, H * QK_PAD), BF16),
                   jax.ShapeDtypeStruct((B, S, H * QK_PAD), BF16),
                   jax.ShapeDtypeStruct((B, S, H * QK_PAD), BF16)],
        compiler_params=_cparams(("parallel", "parallel")),
        name="mla_prep",
    )(lat, positions.reshape(B, S, 1), g_q, g_kv, wq, wqs, wk, wv, freq, sgn, sel, sels, ones_col)


def _flash_kernel(q_ref, k_ref, v_ref, o_ref, m_ref, acc_ref, s_ref, *, tkc, n_chunks):
    q = q_ref[0]
    m_ref[...] = jnp.full(m_ref.shape, -jnp.inf, F32)
    acc_ref[...] = jnp.zeros(acc_ref.shape, F32)

    def scores(j):
        off = pl.multiple_of(j * tkc, tkc)
        return _nt_dot(q, k_ref[0, pl.ds(off, tkc), :])

    def consume(j, slot):
        off = pl.multiple_of(j * tkc, tkc)
        s = s_ref[slot]
        m_prev = m_ref[...]
        m_new = jnp.maximum(m_prev, jnp.max(s, axis=-1, keepdims=True))
        alpha = jnp.exp2(m_prev - m_new)
        p = jnp.exp2(s - m_new).astype(BF16)
        acc_ref[...] = alpha * acc_ref[...] + jnp.dot(p, v_ref[0, pl.ds(off, tkc), :],
                                                      preferred_element_type=F32)
        m_ref[...] = m_new

    s_ref[0] = scores(0)

    def body(jj, carry):
        j = 2 * jj
        s_ref[1] = scores(j + 1)
        consume(j, 0)
        s_ref[0] = scores(jnp.minimum(j + 2, n_chunks - 1))
        consume(j + 1, 1)
        return carry

    lax.fori_loop(0, n_chunks // 2, body, 0)
    acc = acc_ref[...]
    o_ref[0] = (acc[:, :MLA_V] / acc[:, MLA_V:MLA_V + 1]).astype(BF16)


def _flash_attention(q, k, v, tq=1024, tkc=1024):
    B, S, _ = q.shape
    H = MLA_HEADS
    n_chunks = S // tkc
    assert n_chunks % 2 == 0
    return pl.pallas_call(
        functools.partial(_flash_kernel, tkc=tkc, n_chunks=n_chunks),
        grid=(B, H, S // tq),
        in_specs=[pl.BlockSpec((1, tq, QK_PAD), lambda b, h, i: (b, i, h)),
                  pl.BlockSpec((1, S, QK_PAD), lambda b, h, i: (b, 0, h)),
                  pl.BlockSpec((1, S, QK_PAD), lambda b, h, i: (b, 0, h))],
        out_specs=pl.BlockSpec((1, tq, MLA_V), lambda b, h, i: (b, i, h)),
        out_shape=jax.ShapeDtypeStruct((B, S, H * MLA_V), BF16),
        scratch_shapes=[pltpu.VMEM((tq, 1), F32), pltpu.VMEM((tq, QK_PAD), F32),
                        pltpu.VMEM((2, tq, tkc), F32)],
        compiler_params=_cparams(("parallel", "parallel", "parallel")),
        name="mla_flash",
    )(q, k, v)


def _t5_bucket(rel):
    nb = REL_BUCKETS // 2
    max_exact = nb // 2
    ret = jnp.where(rel > 0, nb, 0)
    n = jnp.abs(rel)
    nf = jnp.maximum(n, 1).astype(F32)
    large = max_exact + (jnp.log(nf / max_exact) / math.log(REL_MAX_DIST / max_exact)
                         * (nb - max_exact)).astype(I32)
    large = jnp.minimum(large, nb - 1)
    return ret + jnp.where(n < max_exact, n, large)


def _dilated_kernel(q_ref, kp_ref, kc_ref, kn_ref, vp_ref, vc_ref, vn_ref, pq_ref, pk_ref, tab_ref,
                    o_ref, l_ref, kcat_ref, vcat_ref, *, r, tq, n_rows):
    n = pl.program_id(2)
    nk = tq + 2 * r
    kcat_ref[0:r, :] = kp_ref[0, 0]
    kcat_ref[r:r + tq, :] = kc_ref[0, 0]
    kcat_ref[r + tq:nk, :] = kn_ref[0, 0]
    vcat_ref[0:r, :] = vp_ref[0, 0]
    vcat_ref[r:r + tq, :] = vc_ref[0, 0]
    vcat_ref[r + tq:nk, :] = vn_ref[0, 0]

    qi = n * tq + lax.broadcasted_iota(I32, (tq, nk), 0)
    kj = n * tq - r + lax.broadcasted_iota(I32, (tq, nk), 1)
    valid = (jnp.abs(kj - qi) <= r) & (kj >= 0) & (kj < n_rows)
    rel = pk_ref[0, 0, 0] - pq_ref[0, 0]
    bucket = _t5_bucket(rel)
    lane_grp = lax.broadcasted_iota(I32, (tq, LANES), 1) // (LANES // DIL_HEADS)
    scale = DIL_HEAD_DIM ** -0.5
    lse_tile = jnp.zeros((tq, LANES), F32)
    for h in range(DIL_HEADS):
        sl = slice(h * DIL_HEAD_DIM, (h + 1) * DIL_HEAD_DIM)
        s = _nt_dot(q_ref[0, 0, :, sl], kcat_ref[:, sl]) * scale
        tab = jnp.broadcast_to(tab_ref[h:h + 1, :], (tq, LANES))
        bias = jnp.concatenate(
            [jnp.take_along_axis(tab, bucket[:, c * LANES:(c + 1) * LANES], axis=1)
             for c in range(nk // LANES)], axis=1)
        s = jnp.where(valid, s + bias, NEG_INF)
        m = jnp.max(s, axis=-1, keepdims=True)
        p = jnp.exp(s - m)
        den = jnp.sum(p, axis=-1, keepdims=True)
        o = jnp.dot(p.astype(BF16), vcat_ref[:, sl], preferred_element_type=F32) / den
        o_ref[0, 0, :, sl] = o.astype(BF16)
        lse_tile = jnp.where(lane_grp == h, m + jnp.log(den), lse_tile)
    l_ref[0, 0] = lse_tile


def _dilated_group(qkv, positions, table, gi, window, d, tq=128):
    B, _, L, _ = qkv.shape
    HD = DIL_HEADS * DIL_HEAD_DIM
    r = window // (2 * d)
    nt = L // tq
    rb = tq // r
    nk = tq + 2 * r
    pcls = positions.reshape(B, L, d).transpose(0, 2, 1)
    pq = pcls.reshape(B, d, L, 1)
    kidx = jnp.clip(jnp.arange(nt)[:, None] * tq - r + jnp.arange(nk)[None, :], 0, L - 1)
    pk = pcls[:, :, kidx].reshape(B, d, nt, 1, nk)
    tab = jnp.zeros((DIL_HEADS, LANES), F32).at[:, :REL_BUCKETS].set(
        table[:, gi * DIL_HEADS:(gi + 1) * DIL_HEADS].T)

    def cur(c):
        return pl.BlockSpec((1, 1, tq, HD), lambda b, g, n: (b, g, n, c))

    def prv(c):
        return pl.BlockSpec((1, 1, r, HD), lambda b, g, n: (b, g, jnp.maximum(n * rb - 1, 0), c))

    def nxt(c):
        return pl.BlockSpec((1, 1, r, HD),
                            lambda b, g, n: (b, g, jnp.minimum((n + 1) * rb, L // r - 1), c))

    return pl.pallas_call(
        functools.partial(_dilated_kernel, r=r, tq=tq, n_rows=L),
        grid=(B, d, nt),
        in_specs=[cur(0), prv(1), cur(1), nxt(1), prv(2), cur(2), nxt(2),
                  pl.BlockSpec((1, 1, tq, 1), lambda b, g, n: (b, g, n, 0)),
                  pl.BlockSpec((1, 1, 1, 1, nk), lambda b, g, n: (b, g, n, 0, 0)),
                  pl.BlockSpec((DIL_HEADS, LANES), lambda b, g, n: (0, 0))],
        out_specs=[pl.BlockSpec((1, 1, tq, HD), lambda b, g, n: (b, g, n, 0)),
                   pl.BlockSpec((1, 1, tq, LANES), lambda b, g, n: (b, g, n, 0))],
        out_shape=[jax.ShapeDtypeStruct((B, d, L, HD), BF16),
                   jax.ShapeDtypeStruct((B, d, L, LANES), F32)],
        scratch_shapes=[pltpu.VMEM((nk, HD), BF16), pltpu.VMEM((nk, HD), BF16)],
        compiler_params=_cparams(("parallel", "parallel", "parallel")),
        name=f"dilated_g{gi}",
    )(*([qkv] * 7 + [pq, pk, tab]))


def _outproj_kernel(*refs, dils):
    ng = len(dils)
    oa_ref = refs[0]
    og_refs = refs[1:1 + ng]
    lg_refs = refs[1 + ng:1 + 2 * ng]
    (ga_ref, gb_ref, x_ref, mod_ref, g_ref, wo_ref, wr_ref,
     x1_ref, h2_ref, aff_ref, afft_ref, o_scr, l_scr) = refs[1 + 2 * ng:]
    tm = x_ref.shape[1]
    for i, d in enumerate(dils):
        rows = tm // d
        for g in range(d):
            dst = pl.ds(g, rows, stride=d) if d > 1 else slice(None)
            og = og_refs[i][0, g].astype(F32)
            for h in range(DIL_HEADS):
                o_scr[i, h, dst, :] = og[:, h * DIL_HEAD_DIM:(h + 1) * DIL_HEAD_DIM]
            l_scr[i, dst, :] = lg_refs[i][0, g]
    lses = [l_scr[i] for i in range(ng)]
    mx = functools.reduce(jnp.maximum, lses)
    es = [jnp.exp(l - mx) for l in lses]
    tot = functools.reduce(lambda a, b: a + b, es)
    wts = [e / tot for e in es]
    cols = []
    for h in range(DIL_HEADS):
        c = h * (LANES // DIL_HEADS)
        cols.append(functools.reduce(
            lambda a, b: a + b, [wts[i][:, c:c + 1] * o_scr[i, h] for i in range(ng)]))
    o_b = jnp.concatenate(cols, axis=1)
    comb = (ga_ref[0, 0].astype(F32) * oa_ref[0].astype(F32)
            + gb_ref[0, 0].astype(F32) * o_b).astype(BF16)
    x1 = x_ref[0] + mod_ref[0, 2:3, :] * jnp.dot(comb, wo_ref[...], preferred_element_type=F32)
    x1_ref[0] = x1
    y = x1 * lax.rsqrt(jnp.mean(x1 * x1, axis=-1, keepdims=True) + NORM_EPS) * g_ref[...]
    h2 = y * (1.0 + mod_ref[0, 4:5, :]) + mod_ref[0, 3:4, :]
    h2_ref[0] = h2.astype(BF16)
    logits = jnp.dot(h2, wr_ref[...], preferred_element_type=F32,
                     precision=lax.Precision.HIGHEST)
    lane = lax.broadcasted_iota(I32, logits.shape, 1)
    logits = jnp.where(lane < N_EXPERTS, logits, -jnp.inf)
    e = jnp.exp(logits - jnp.max(logits, axis=-1, keepdims=True))
    aff = e / jnp.sum(e, axis=-1, keepdims=True)
    aff_ref[0] = aff
    afft_ref[0] = aff.T[:N_EXPERTS, :]


def _out_projection(o_a, dil_outs, main, gate_col, x, mod, g, w_out, w_router, tm=512):
    B, S, D = x.shape
    const = lambda b, i: (0, 0)
    row = lambda b, i: (b, i, 0)
    wr = jnp.zeros((D, LANES), F32).at[:, :N_EXPERTS].set(w_router)
    dils = tuple(o.shape[1] for o, _ in dil_outs)
    og_specs = [pl.BlockSpec((1, d, tm // d, D), lambda b, i: (b, 0, i, 0)) for d in dils]
    lg_specs = [pl.BlockSpec((1, d, tm // d, LANES), lambda b, i: (b, 0, i, 0)) for d in dils]
    return pl.pallas_call(
        functools.partial(_outproj_kernel, dils=dils),
        grid=(B, S // tm),
        in_specs=[pl.BlockSpec((1, tm, D), row)] + og_specs + lg_specs + [
                  pl.BlockSpec((1, 1, tm, D), lambda b, i: (b, 0, i, gate_col)),
                  pl.BlockSpec((1, 1, tm, D), lambda b, i: (b, 0, i, gate_col + 1)),
                  pl.BlockSpec((1, tm, D), row),
                  pl.BlockSpec((1, 6, D), lambda b, i: (b, 0, 0)),
                  pl.BlockSpec((1, D), const),
                  pl.BlockSpec((D, D), const),
                  pl.BlockSpec((D, LANES), const)],
        out_specs=[pl.BlockSpec((1, tm, D), row),
                   pl.BlockSpec((1, tm, D), row),
                   pl.BlockSpec((1, tm, LANES), row),
                   pl.BlockSpec((1, N_EXPERTS, tm), lambda b, i: (b, 0, i))],
        out_shape=[jax.ShapeDtypeStruct((B, S, D), F32),
                   jax.ShapeDtypeStruct((B, S, D), BF16),
                   jax.ShapeDtypeStruct((B, S, LANES), F32),
                   jax.ShapeDtypeStruct((B, N_EXPERTS, S), F32)],
        scratch_shapes=[pltpu.VMEM((len(dils), DIL_HEADS, tm, DIL_HEAD_DIM), F32),
                        pltpu.VMEM((len(dils), tm, LANES), F32)],
        compiler_params=_cparams(("parallel", "parallel")),
        name="out_projection",
    )(o_a, *[o for o, _ in dil_outs], *[l for _, l in dil_outs], main, main, x, mod, g, w_out, wr)


def _select_kernel(afft_ref, thr_ref, tie_ref, slot_ref, *, cap, chunk):
    aff = afft_ref[0]
    E, S = aff.shape
    bits = lax.bitcast_convert_type(aff, I32)

    def count(mask):
        return jnp.sum(mask.astype(I32), axis=1, keepdims=True)

    def thr_step(i, v):
        cand = v | jnp.left_shift(jnp.int32(1), 30 - i)
        return jnp.where(count(bits >= cand) >= cap, cand, v)

    thr = lax.fori_loop(0, 31, thr_step, jnp.zeros((E, 1), I32))
    need = cap - count(bits > thr)
    eq = bits == thr
    idx = lax.broadcasted_iota(I32, (E, S), 1)
    nbits = max(1, (S - 1).bit_length())

    def tie_step(i, j):
        cand = j | jnp.left_shift(jnp.int32(1), nbits - 1 - i)
        return jnp.where(count(eq & (idx < cand)) < need, cand, j)

    tie = lax.fori_loop(0, nbits, tie_step, jnp.zeros((E, 1), I32))
    sel = (bits > thr) | (eq & (idx <= tie))
    thr_ref[0] = jnp.broadcast_to(thr, (E, LANES))
    tie_ref[0] = jnp.broadcast_to(tie, (E, LANES))

    upper = (lax.broadcasted_iota(I32, (chunk, chunk), 0)
             <= lax.broadcasted_iota(I32, (chunk, chunk), 1)).astype(BF16)
    carry = jnp.zeros((E, 1), F32)
    self_ = sel.astype(F32)
    for c in range(S // chunk):
        scf = self_[:, c * chunk:(c + 1) * chunk]
        incl = jnp.dot(scf.astype(BF16), upper, preferred_element_type=F32)
        pos = (incl - scf + carry).astype(I32)
        slot_ref[0, :, c * chunk:(c + 1) * chunk] = jnp.where(scf > 0.0, pos, -1)
        carry = carry + incl[:, chunk - 1:chunk]


def _select(aff_t, cap, chunk=512):
    B, E, S = aff_t.shape
    return pl.pallas_call(
        functools.partial(_select_kernel, cap=cap, chunk=chunk),
        grid=(B,),
        in_specs=[pl.BlockSpec((1, E, S), lambda b: (b, 0, 0))],
        out_specs=[pl.BlockSpec((1, E, LANES), lambda b: (b, 0, 0)),
                   pl.BlockSpec((1, E, LANES), lambda b: (b, 0, 0)),
                   pl.BlockSpec((1, E, S), lambda b: (b, 0, 0))],
        out_shape=[jax.ShapeDtypeStruct((B, E, LANES), I32),
                   jax.ShapeDtypeStruct((B, E, LANES), I32),
                   jax.ShapeDtypeStruct((B, E, S), I32)],
        compiler_params=_cparams(("parallel",)),
        name="ec_select",
    )(aff_t)


def _gather_kernel(slot_ref, h_ref, xe_ref):
    t = pl.program_id(2)
    cap = xe_ref.shape[2]
    tb = h_ref.shape[1]
    onehot = (lax.broadcasted_iota(I32, (cap, tb), 0) == slot_ref[0, 0]).astype(BF16)
    part = jnp.dot(onehot, h_ref[0], preferred_element_type=F32).astype(BF16)

    @pl.when(t == 0)
    def _():
        xe_ref[0, 0] = part

    @pl.when(t > 0)
    def _():
        xe_ref[0, 0] = xe_ref[0, 0] + part


def _gather(slot, h2, cap, tb=1024):
    B, E, S = slot.shape
    D = h2.shape[-1]
    return pl.pallas_call(
        _gather_kernel,
        grid=(B, E, S // tb),
        in_specs=[pl.BlockSpec((1, 1, 1, tb), lambda b, e, t: (b, e, 0, t)),
                  pl.BlockSpec((1, tb, D), lambda b, e, t: (b, t, 0))],
        out_specs=pl.BlockSpec((1, 1, cap, D), lambda b, e, t: (b, e, 0, 0)),
        out_shape=jax.ShapeDtypeStruct((B, E, cap, D), BF16),
        compiler_params=_cparams(("parallel", "parallel", "arbitrary")),
        name="ec_gather",
    )(slot.reshape(B, E, 1, S), h2)


def _ffn_kernel(xe_ref, wg_ref, wu_ref, wd_ref, ye_ref, acc_ref):
    f = pl.program_id(2)
    nb, _, cap, D = xe_ref.shape
    x = xe_ref[...].reshape(nb * cap, D)
    g = jnp.dot(x, wg_ref[0].astype(BF16), preferred_element_type=F32)
    u = jnp.dot(x, wu_ref[0].astype(BF16), preferred_element_type=F32)
    hid = (g * jax.nn.sigmoid(g) * u).astype(BF16)
    part = jnp.dot(hid, wd_ref[0].astype(BF16), preferred_element_type=F32)

    @pl.when(f == 0)
    def _():
        acc_ref[...] = part

    @pl.when(f > 0)
    def _():
        acc_ref[...] += part

    @pl.when(f == pl.num_programs(2) - 1)
    def _():
        ye_ref[...] = acc_ref[...].astype(BF16).reshape(ye_ref.shape)


def _expert_ffn(xe, w_gate, w_up, w_down, nb=2, tf=512):
    B, E, cap, D = xe.shape
    F = w_gate.shape[-1]
    return pl.pallas_call(
        _ffn_kernel,
        grid=(E, B // nb, F // tf),
        in_specs=[pl.BlockSpec((nb, 1, cap, D), lambda e, b, f: (b, e, 0, 0)),
                  pl.BlockSpec((1, D, tf), lambda e, b, f: (e, 0, f)),
                  pl.BlockSpec((1, D, tf), lambda e, b, f: (e, 0, f)),
                  pl.BlockSpec((1, tf, D), lambda e, b, f: (e, f, 0))],
        out_specs=pl.BlockSpec((nb, 1, cap, D), lambda e, b, f: (b, e, 0, 0)),
        out_shape=jax.ShapeDtypeStruct((B, E, cap, D), BF16),
        scratch_shapes=[pltpu.VMEM((nb * cap, D), F32)],
        compiler_params=_cparams(("parallel", "parallel", "arbitrary")),
        name="ec_ffn",
    )(xe, w_gate, w_up, w_down)


def _combine_kernel(aff_ref, thr_ref, tie_ref, ye_ref, x1_ref, mod_ref, g_ref, out_ref,
                    slot_ref, acc_ref, carry_ref):
    t = pl.program_id(1)
    e = pl.program_id(2)
    tb = aff_ref.shape[1]
    cap = ye_ref.shape[2]
    lane = lax.broadcasted_iota(I32, (tb, LANES), 1)

    @pl.when(e == 0)
    def _():
        @pl.when(t == 0)
        def _():
            carry_ref[...] = jnp.zeros(carry_ref.shape, F32)

        bits = lax.bitcast_convert_type(aff_ref[0], I32)
        tok = t * tb + lax.broadcasted_iota(I32, (tb, LANES), 0)
        thr = thr_ref[0]
        sel = ((bits > thr) | ((bits == thr) & (tok <= tie_ref[0]))) & (lane < N_EXPERTS)
        self_ = sel.astype(F32)
        lower = (lax.broadcasted_iota(I32, (tb, tb), 0)
                 >= lax.broadcasted_iota(I32, (tb, tb), 1)).astype(BF16)
        incl = jnp.dot(lower, self_.astype(BF16), preferred_element_type=F32)
        pos = (incl - self_ + carry_ref[...]).astype(I32)
        slot_ref[...] = jnp.where(sel, pos, -1)
        carry_ref[...] = carry_ref[...] + incl[tb - 1:tb, :]
        acc_ref[...] = jnp.zeros(acc_ref.shape, F32)

    pick = lane == e
    slot_e = jnp.sum(jnp.where(pick, slot_ref[...], 0), axis=1, keepdims=True)
    gate_e = jnp.sum(jnp.where(pick, aff_ref[0], 0.0), axis=1, keepdims=True)
    onehot = (lax.broadcasted_iota(I32, (tb, cap), 1) == slot_e).astype(BF16)
    acc_ref[...] += gate_e * jnp.dot(onehot, ye_ref[0, 0], preferred_element_type=F32)

    @pl.when(e == pl.num_programs(2) - 1)
    def _():
        x2 = x1_ref[0] + mod_ref[0, 5:6, :] * acc_ref[...]
        out_ref[0] = (x2 * lax.rsqrt(jnp.mean(x2 * x2, axis=-1, keepdims=True) + NORM_EPS)
                      * g_ref[...])


def _combine(aff, thr, tie, ye, x1, mod, g_final, tb=512):
    B, S, D = x1.shape
    E, cap = ye.shape[1], ye.shape[2]
    return pl.pallas_call(
        _combine_kernel,
        grid=(B, S // tb, E),
        in_specs=[pl.BlockSpec((1, tb, LANES), lambda b, t, e: (b, t, 0)),
                  pl.BlockSpec((1, 1, LANES), lambda b, t, e: (b, 0, 0)),
                  pl.BlockSpec((1, 1, LANES), lambda b, t, e: (b, 0, 0)),
                  pl.BlockSpec((1, 1, cap, D), lambda b, t, e: (b, e, 0, 0)),
                  pl.BlockSpec((1, tb, D), lambda b, t, e: (b, t, 0)),
                  pl.BlockSpec((1, 6, D), lambda b, t, e: (b, 0, 0)),
                  pl.BlockSpec((1, D), lambda b, t, e: (0, 0))],
        out_specs=pl.BlockSpec((1, tb, D), lambda b, t, e: (b, t, 0)),
        out_shape=jax.ShapeDtypeStruct((B, S, D), F32),
        scratch_shapes=[pltpu.VMEM((tb, LANES), I32), pltpu.VMEM((tb, D), F32),
                        pltpu.VMEM((1, LANES), F32)],
        compiler_params=_cparams(("parallel", "arbitrary", "arbitrary")),
        name="ec_combine",
    )(aff, thr, tie, ye, x1, mod, g_final)


def _prep_weights(w_in, w_uq, w_ukv):
    D = w_in.shape[0]
    H = MLA_HEADS
    n_lat = MLA_Q_RANK + MLA_KV_RANK + MLA_ROPE
    n_grp = 3 * DIL_HEADS * DIL_HEAD_DIM
    n_dil = len(DIL_GROUPS) * n_grp
    w_lat = jnp.zeros((D, LAT_PAD), F32).at[:, :n_lat].set(w_in[:, :n_lat]).astype(BF16)
    w_grp = [w_in[:, n_lat + i * n_grp:n_lat + (i + 1) * n_grp].astype(BF16)
             for i in range(len(DIL_GROUPS))]
    w_gates = w_in[:, n_lat + n_dil:].astype(BF16)
    half = MLA_ROPE // 2
    uq = w_uq.reshape(MLA_Q_RANK, H, MLA_NOPE + MLA_ROPE)
    pe = uq[:, :, MLA_NOPE:]
    zq = jnp.zeros((MLA_Q_RANK, H, QK_PAD - MLA_NOPE - MLA_ROPE), F32)
    wq = jnp.concatenate([uq[:, :, :MLA_NOPE], pe, zq], axis=2)
    wqs = jnp.concatenate([jnp.zeros_like(uq[:, :, :MLA_NOPE]), pe[:, :, half:], pe[:, :, :half], zq],
                          axis=2)
    ukv = w_ukv.reshape(MLA_KV_RANK, H, MLA_NOPE + MLA_V)
    wk = jnp.concatenate([ukv[:, :, :MLA_NOPE],
                          jnp.zeros((MLA_KV_RANK, H, QK_PAD - MLA_NOPE), F32)], axis=2)
    wv = jnp.concatenate([ukv[:, :, MLA_NOPE:],
                          jnp.zeros((MLA_KV_RANK, H, QK_PAD - MLA_V), F32)], axis=2)
    return (w_lat, w_grp, w_gates, wq.reshape(MLA_Q_RANK, H * QK_PAD).astype(BF16),
            wqs.reshape(MLA_Q_RANK, H * QK_PAD).astype(BF16),
            wk.reshape(MLA_KV_RANK, H * QK_PAD).astype(BF16),
            wv.reshape(MLA_KV_RANK, H * QK_PAD).astype(BF16))


def kernel(x, c, positions, w_ada, b_ada, g_norm_mix, w_in, g_q_lat, g_kv_lat, w_uq, w_ukv,
           rel_bias, w_out, g_norm_ffn, w_router, w_gate, w_up, w_down, g_final):
    B, S, D = x.shape
    assert w_ada.shape[0] == 1, "the final norm is fused into the (single) layer's last kernel"
    assert DIL_GROUPS[0][1] == 1, "the gates ride along with the undilated group's projection"
    for l in range(w_ada.shape[0]):
        mod = _modulation(c, w_ada[l], b_ada[l])
        w_lat, w_grp, w_gates, wq, wqs, wk, wv = _prep_weights(w_in[l], w_uq[l], w_ukv[l])
        g_mix = g_norm_mix[l].reshape(1, D)
        n_qkv = w_grp[0].shape[1] // D
        lat, main = _in_projection(x, mod, g_mix, jnp.concatenate([w_grp[0], w_gates], axis=1),
                                   d=1, n_plain=n_qkv, w_lat=w_lat)
        qkvs = [main] + [_in_projection(x, mod, g_mix, w_grp[gi], d=d, n_plain=n_qkv)[0]
                         for gi, (_, d) in enumerate(DIL_GROUPS) if gi > 0]
        q, k, v = _mla_prep(lat, positions, g_q_lat[l].reshape(1, -1), g_kv_lat[l].reshape(1, -1),
                            wq, wqs, wk, wv)
        o_a = _flash_attention(q, k, v)
        dil_outs = [_dilated_group(qkvs[gi], positions, rel_bias, gi, window, d)
                    for gi, (window, d) in enumerate(DIL_GROUPS)]
        x1, h2, aff, aff_t = _out_projection(o_a, dil_outs, main, n_qkv, x, mod,
                                             g_norm_ffn[l].reshape(1, D),
                                             w_out[l].astype(BF16), w_router[l])
        cap = EC_CAPACITY_FACTOR * S // N_EXPERTS
        thr, tie, slot = _select(aff_t, cap)
        xe = _gather(slot, h2, cap)
        ye = _expert_ffn(xe, w_gate[l], w_up[l], w_down[l])
        pad = jnp.zeros((B, 1, LANES - N_EXPERTS), I32)
        thr_l = jnp.concatenate([thr[:, :, 0].reshape(B, 1, N_EXPERTS), pad], axis=2)
        tie_l = jnp.concatenate([tie[:, :, 0].reshape(B, 1, N_EXPERTS), pad], axis=2)
        x = _combine(aff, thr_l, tie_l, ye, x1, mod, g_final.reshape(1, D))
    return x
```

```python
import functools
import math

import jax
import jax.numpy as jnp
from jax import lax
from jax.experimental import pallas as pl
from jax.experimental.pallas import tpu as pltpu

F32 = jnp.float32
BF16 = jnp.bfloat16
I32 = jnp.int32

MLA_HEADS = 8
MLA_Q_RANK = 384
MLA_KV_RANK = 256
MLA_NOPE = 128
MLA_ROPE = 64
MLA_V = 128
ROPE_THETA = 10000.0
DIL_GROUPS = ((128, 1), (512, 4), (2048, 16))
DIL_HEADS = 8
DIL_HEAD_DIM = 128
REL_BUCKETS = 32
REL_MAX_DIST = 1024
N_EXPERTS = 16
EC_CAPACITY_FACTOR = 2
NORM_EPS = 1e-6
NEG_INF = -1e30

LANES = 128
QK_PAD = 256
LAT_PAD = 768
VMEM_LIMIT = 56 * 1024 * 1024


def _cparams(sem):
    return pltpu.CompilerParams(dimension_semantics=sem, vmem_limit_bytes=VMEM_LIMIT)


def _nt_dot(a, b):
    return lax.dot_general(a, b, (((1,), (1,)), ((), ())), preferred_element_type=F32)


def _mod_kernel(c_ref, w_ref, b_ref, o_ref):
    c = c_ref[...]
    cond = c * jax.nn.sigmoid(c)
    o_ref[...] = jnp.dot(cond, w_ref[...], preferred_element_type=F32,
                         precision=lax.Precision.HIGHEST) + b_ref[...]


def _modulation(c, w_ada, b_ada):
    B, D = c.shape
    rows = 8
    c8 = jnp.zeros((rows, D), F32).at[:B].set(c)
    n6 = w_ada.shape[1]
    out = pl.pallas_call(
        _mod_kernel,
        grid=(n6 // D,),
        in_specs=[pl.BlockSpec((rows, D), lambda j: (0, 0)),
                  pl.BlockSpec((D, D), lambda j: (0, j)),
                  pl.BlockSpec((1, D), lambda j: (0, j))],
        out_specs=pl.BlockSpec((rows, D), lambda j: (0, j)),
        out_shape=jax.ShapeDtypeStruct((rows, n6), F32),
        compiler_params=_cparams(("arbitrary",)),
        name="modulation",
    )(c8, w_ada, b_ada.reshape(1, n6))
    return out[:B].reshape(B, 6, D)


def _inproj_kernel(*refs, d, n_plain, n_tiles, with_lat):
    nx = len(refs) - (7 if with_lat else 5)
    x_refs = refs[:nx]
    if with_lat:
        mod_ref, g_ref, wl_ref, wm_ref, lat_ref, out_ref, h_ref = refs[nx:]
    else:
        mod_ref, g_ref, wm_ref, out_ref, h_ref = refs[nx:]
    n = pl.program_id(2)
    tm = x_refs[0].shape[1]
    rows = tm // d

    @pl.when(n == 0)
    def _():
        for g in range(d):
            if d == 1:
                x = x_refs[0][0]
            else:
                x = jnp.concatenate([xr[0, pl.ds(g, rows, stride=d), :] for xr in x_refs], axis=1)
            y = x * lax.rsqrt(jnp.mean(x * x, axis=-1, keepdims=True) + NORM_EPS) * g_ref[...]
            h = y * (1.0 + mod_ref[0, 1:2, :]) + mod_ref[0, 0:1, :]
            h_ref[g * rows:(g + 1) * rows, :] = h.astype(BF16)
        if with_lat:
            lat_ref[0] = jnp.dot(h_ref[...], wl_ref[...], preferred_element_type=F32)

    acc = jnp.dot(h_ref[...], wm_ref[...], preferred_element_type=F32)
    tn = acc.shape[1]
    if n_plain == n_tiles:
        out_ref[0] = acc.astype(BF16).reshape(d, rows, tn)
    else:
        @pl.when(n < n_plain)
        def _():
            out_ref[0] = acc.astype(BF16).reshape(d, rows, tn)

        @pl.when(n >= n_plain)
        def _():
            out_ref[0] = jax.nn.sigmoid(acc).astype(BF16).reshape(d, rows, tn)


def _in_projection(x, mod, g, w_main, d, n_plain, w_lat=None, tm=1024, tn=1024):
    B, S, D = x.shape
    nmain = w_main.shape[1]
    n_tiles = nmain // tn
    with_lat = w_lat is not None
    if d == 1:
        in_specs = [pl.BlockSpec((1, tm, D), lambda b, i, n: (b, i, 0))]
    else:
        in_specs = [pl.BlockSpec((1, tm, LANES), functools.partial(lambda b, i, n, j: (b, i, j), j=j))
                    for j in range(D // LANES)]
    args = [x] * len(in_specs) + [mod, g]
    in_specs += [pl.BlockSpec((1, 6, D), lambda b, i, n: (b, 0, 0)),
                 pl.BlockSpec((1, D), lambda b, i, n: (0, 0))]
    out_specs = [pl.BlockSpec((1, d, tm // d, tn), lambda b, i, n: (b, 0, i, n))]
    out_shape = [jax.ShapeDtypeStruct((B, d, S // d, nmain), BF16)]
    if with_lat:
        in_specs.append(pl.BlockSpec((D, LAT_PAD), lambda b, i, n: (0, 0)))
        args.append(w_lat)
        out_specs.insert(0, pl.BlockSpec((1, tm, LAT_PAD), lambda b, i, n: (b, i, 0)))
        out_shape.insert(0, jax.ShapeDtypeStruct((B, S, LAT_PAD), F32))
    in_specs.append(pl.BlockSpec((D, tn), lambda b, i, n: (0, n)))
    args.append(w_main)
    return pl.pallas_call(
        functools.partial(_inproj_kernel, d=d, n_plain=n_plain, n_tiles=n_tiles, with_lat=with_lat),
        grid=(B, S // tm, n_tiles),
        in_specs=in_specs,
        out_specs=out_specs,
        out_shape=out_shape,
        scratch_shapes=[pltpu.VMEM((tm, D), BF16)],
        compiler_params=_cparams(("parallel", "parallel", "arbitrary")),
        name=f"in_projection_d{d}",
    )(*args)


def _mla_prep_kernel(lat_ref, pos_ref, gq_ref, gkv_ref, wq_ref, wqs_ref, wk_ref, wv_ref,
                     freq_ref, sgn_ref, sel_ref, sels_ref, one_ref, q_ref, k_ref, v_ref, *, scale):
    lat = lat_ref[0]
    cq = lat[:, :MLA_Q_RANK]
    ckv = lat[:, MLA_Q_RANK:MLA_Q_RANK + MLA_KV_RANK]
    kpe = lat[:, MLA_Q_RANK + MLA_KV_RANK:]
    cqn = (cq * lax.rsqrt(jnp.mean(cq * cq, axis=-1, keepdims=True) + NORM_EPS)
           * gq_ref[...]).astype(BF16)
    ckvn = (ckv * lax.rsqrt(jnp.mean(ckv * ckv, axis=-1, keepdims=True) + NORM_EPS)
            * gkv_ref[...]).astype(BF16)
    ang = pos_ref[0].astype(F32) * freq_ref[...]
    cos = jnp.cos(ang)
    sin = jnp.sin(ang) * sgn_ref[...]
    qa = jnp.dot(cqn, wq_ref[...], preferred_element_type=F32)
    qs = jnp.dot(cqn, wqs_ref[...], preferred_element_type=F32)
    kn = jnp.dot(ckvn, wk_ref[...], preferred_element_type=F32)
    v_ref[0] = (jnp.dot(ckvn, wv_ref[...], preferred_element_type=F32) + one_ref[...]).astype(BF16)
    hp = lax.Precision.HIGHEST
    rk = (jnp.dot(kpe, sel_ref[...], preferred_element_type=F32, precision=hp) * cos
          + jnp.dot(kpe, sels_ref[...], preferred_element_type=F32, precision=hp) * sin)
    cos_q = cos * scale
    sin_q = sin * scale
    for h in range(MLA_HEADS):
        sl = slice(h * QK_PAD, (h + 1) * QK_PAD)
        q_ref[0, :, sl] = (qa[:, sl] * cos_q + qs[:, sl] * sin_q).astype(BF16)
        k_ref[0, :, sl] = (kn[:, sl] + rk).astype(BF16)


def _mla_prep(lat, positions, g_q, g_kv, wq, wqs, wk, wv, tm=512):
    B, S, _ = lat.shape
    H = MLA_HEADS
    half = MLA_ROPE // 2
    inv_freq = ROPE_THETA ** (-jnp.arange(0, MLA_ROPE, 2, dtype=F32) / MLA_ROPE)
    freq = jnp.zeros((1, QK_PAD), F32).at[0, MLA_NOPE:MLA_NOPE + MLA_ROPE].set(
        jnp.concatenate([inv_freq, inv_freq]))
    sgn = jnp.zeros((1, QK_PAD), F32).at[0, MLA_NOPE:MLA_NOPE + half].set(-1.0)
    sgn = sgn.at[0, MLA_NOPE + half:MLA_NOPE + MLA_ROPE].set(1.0)
    r = jnp.arange(MLA_ROPE)
    sel = jnp.zeros((LANES, QK_PAD), F32).at[r, MLA_NOPE + r].set(1.0)
    sels = jnp.zeros((LANES, QK_PAD), F32).at[r, MLA_NOPE + (r + half) % MLA_ROPE].set(1.0)
    scale = (MLA_NOPE + MLA_ROPE) ** -0.5 * math.log2(math.e)
    ones_col = jnp.zeros((H, QK_PAD), F32).at[:, MLA_V].set(1.0).reshape(1, H * QK_PAD)
    const = lambda b, i: (0, 0)
    return pl.pallas_call(
        functools.partial(_mla_prep_kernel, scale=scale),
        grid=(B, S // tm),
        in_specs=[pl.BlockSpec((1, tm, LAT_PAD), lambda b, i: (b, i, 0)),
                  pl.BlockSpec((1, tm, 1), lambda b, i: (b, i, 0)),
                  pl.BlockSpec((1, MLA_Q_RANK), const),
                  pl.BlockSpec((1, MLA_KV_RANK), const),
                  pl.BlockSpec((MLA_Q_RANK, H * QK_PAD), const),
                  pl.BlockSpec((MLA_Q_RANK, H * QK_PAD), const),
                  pl.BlockSpec((MLA_KV_RANK, H * QK_PAD), const),
                  pl.BlockSpec((MLA_KV_RANK, H * QK_PAD), const),
                  pl.BlockSpec((1, QK_PAD), const),
                  pl.BlockSpec((1, QK_PAD), const),
                  pl.BlockSpec((LANES, QK_PAD), const),
                  pl.BlockSpec((LANES, QK_PAD), const),
                  pl.BlockSpec((1, H * QK_PAD), const)],
        out_specs=[pl.BlockSpec((1, tm, H * QK_PAD), lambda b, i: (b, i, 0)),
                   pl.BlockSpec((1, tm, H * QK_PAD), lambda b, i: (b, i, 0)),
                   pl.BlockSpec((1, tm, H * QK_PAD), lambda b, i: (b, i, 0))],
        out_shape=[jax.ShapeDtypeStruct((B, S, H * QK_PAD), BF16),
                   jax.ShapeDtypeStruct((B, S, H * QK_PAD), BF16),
                   jax.ShapeDtypeStruct((B, S, H * QK_PAD), BF16)],
        compiler_params=_cparams(("parallel", "parallel")),
        name="mla_prep",
    )(lat, positions.reshape(B, S, 1), g_q, g_kv, wq, wqs, wk, wv, freq, sgn, sel, sels, ones_col)


def _flash_kernel(q_ref, k_ref, v_ref, o_ref, m_ref, acc_ref, s_ref, *, tkc, n_chunks):
    q = q_ref[0]
    m_ref[...] = jnp.full(m_ref.shape, -jnp.inf, F32)
    acc_ref[...] = jnp.zeros(acc_ref.shape, F32)

    def scores(j):
        off = pl.multiple_of(j * tkc, tkc)
        return _nt_dot(q, k_ref[0, pl.ds(off, tkc), :])

    def consume(j, slot):
        off = pl.multiple_of(j * tkc, tkc)
        s = s_ref[slot]
        m_prev = m_ref[...]
        m_new = jnp.maximum(m_prev, jnp.max(s, axis=-1, keepdims=True))
        alpha = jnp.exp2(m_prev - m_new)
        p = jnp.exp2(s - m_new).astype(BF16)
        acc_ref[...] = alpha * acc_ref[...] + jnp.dot(p, v_ref[0, pl.ds(off, tkc), :],
                                                      preferred_element_type=F32)
        m_ref[...] = m_new

    s_ref[0] = scores(0)

    def body(jj, carry):
        j = 2 * jj
        s_ref[1] = scores(j + 1)
        consume(j, 0)
        s_ref[0] = scores(jnp.minimum(j + 2, n_chunks - 1))
        consume(j + 1, 1)
        return carry

    lax.fori_loop(0, n_chunks // 2, body, 0)
    acc = acc_ref[...]
    o_ref[0] = (acc[:, :MLA_V] / acc[:, MLA_V:MLA_V + 1]).astype(BF16)


def _flash_attention(q, k, v, tq=1024, tkc=1024):
    B, S, _ = q.shape
    H = MLA_HEADS
    n_chunks = S // tkc
    assert n_chunks % 2 == 0
    return pl.pallas_call(
        functools.partial(_flash_kernel, tkc=tkc, n_chunks=n_chunks),
        grid=(B, H, S // tq),
        in_specs=[pl.BlockSpec((1, tq, QK_PAD), lambda b, h, i: (b, i, h)),
                  pl.BlockSpec((1, S, QK_PAD), lambda b, h, i: (b, 0, h)),
                  pl.BlockSpec((1, S, QK_PAD), lambda b, h, i: (b, 0, h))],
        out_specs=pl.BlockSpec((1, tq, MLA_V), lambda b, h, i: (b, i, h)),
        out_shape=jax.ShapeDtypeStruct((B, S, H * MLA_V), BF16),
        scratch_shapes=[pltpu.VMEM((tq, 1), F32), pltpu.VMEM((tq, QK_PAD), F32),
                        pltpu.VMEM((2, tq, tkc), F32)],
        compiler_params=_cparams(("parallel", "parallel", "parallel")),
        name="mla_flash",
    )(q, k, v)


def _t5_bucket(rel):
    nb = REL_BUCKETS // 2
    max_exact = nb // 2
    ret = jnp.where(rel > 0, nb, 0)
    n = jnp.abs(rel)
    nf = jnp.maximum(n, 1).astype(F32)
    large = max_exact + (jnp.log(nf / max_exact) / math.log(REL_MAX_DIST / max_exact)
                         * (nb - max_exact)).astype(I32)
    large = jnp.minimum(large, nb - 1)
    return ret + jnp.where(n < max_exact, n, large)


def _dilated_kernel(q_ref, kp_ref, kc_ref, kn_ref, vp_ref, vc_ref, vn_ref, pq_ref, pk_ref, tab_ref,
                    o_ref, l_ref, kcat_ref, vcat_ref, s_scr, *, r, tq, n_rows):
    n = pl.program_id(2)
    nk = tq + 2 * r
    kcat_ref[0:r, :] = kp_ref[0, 0]
    kcat_ref[r:r + tq, :] = kc_ref[0, 0]
    kcat_ref[r + tq:nk, :] = kn_ref[0, 0]
    vcat_ref[0:r, :] = vp_ref[0, 0]
    vcat_ref[r:r + tq, :] = vc_ref[0, 0]
    vcat_ref[r + tq:nk, :] = vn_ref[0, 0]

    qi = n * tq + lax.broadcasted_iota(I32, (tq, nk), 0)
    kj = n * tq - r + lax.broadcasted_iota(I32, (tq, nk), 1)
    valid = (jnp.abs(kj - qi) <= r) & (kj >= 0) & (kj < n_rows)
    rel = pk_ref[0, 0, 0] - pq_ref[0, 0]
    bucket = _t5_bucket(rel)
    lane_grp = lax.broadcasted_iota(I32, (tq, LANES), 1) // (LANES // DIL_HEADS)
    scale = DIL_HEAD_DIM ** -0.5
    for h in range(DIL_HEADS):
        sl = slice(h * DIL_HEAD_DIM, (h + 1) * DIL_HEAD_DIM)
        s = _nt_dot(q_ref[0, 0, :, sl], kcat_ref[:, sl]) * scale
        tab = jnp.broadcast_to(tab_ref[h:h + 1, :], (tq, LANES))
        bias = jnp.concatenate(
            [jnp.take_along_axis(tab, bucket[:, c * LANES:(c + 1) * LANES], axis=1)
             for c in range(nk // LANES)], axis=1)
        s_scr[h] = jnp.where(valid, s + bias, NEG_INF)
    lse_tile = jnp.zeros((tq, LANES), F32)
    for h in range(DIL_HEADS):
        sl = slice(h * DIL_HEAD_DIM, (h + 1) * DIL_HEAD_DIM)
        s = s_scr[h]
        m = jnp.max(s, axis=-1, keepdims=True)
        p = jnp.exp(s - m)
        den = jnp.sum(p, axis=-1, keepdims=True)
        o = jnp.dot(p.astype(BF16), vcat_ref[:, sl], preferred_element_type=F32) / den
        o_ref[0, 0, :, sl] = o.astype(BF16)
        lse_tile = jnp.where(lane_grp == h, m + jnp.log(den), lse_tile)
    l_ref[0, 0] = lse_tile


def _dilated_group(qkv, positions, table, gi, window, d, tq=128):
    B, _, L, _ = qkv.shape
    HD = DIL_HEADS * DIL_HEAD_DIM
    r = window // (2 * d)
    nt = L // tq
    rb = tq // r
    nk = tq + 2 * r
    pcls = positions.reshape(B, L, d).transpose(0, 2, 1)
    pq = pcls.reshape(B, d, L, 1)
    kidx = jnp.clip(jnp.arange(nt)[:, None] * tq - r + jnp.arange(nk)[None, :], 0, L - 1)
    pk = pcls[:, :, kidx].reshape(B, d, nt, 1, nk)
    tab = jnp.zeros((DIL_HEADS, LANES), F32).at[:, :REL_BUCKETS].set(
        table[:, gi * DIL_HEADS:(gi + 1) * DIL_HEADS].T)

    def cur(c):
        return pl.BlockSpec((1, 1, tq, HD), lambda b, g, n: (b, g, n, c))

    def prv(c):
        return pl.BlockSpec((1, 1, r, HD), lambda b, g, n: (b, g, jnp.maximum(n * rb - 1, 0), c))

    def nxt(c):
        return pl.BlockSpec((1, 1, r, HD),
                            lambda b, g, n: (b, g, jnp.minimum((n + 1) * rb, L // r - 1), c))

    return pl.pallas_call(
        functools.partial(_dilated_kernel, r=r, tq=tq, n_rows=L),
        grid=(B, d, nt),
        in_specs=[cur(0), prv(1), cur(1), nxt(1), prv(2), cur(2), nxt(2),
                  pl.BlockSpec((1, 1, tq, 1), lambda b, g, n: (b, g, n, 0)),
                  pl.BlockSpec((1, 1, 1, 1, nk), lambda b, g, n: (b, g, n, 0, 0)),
                  pl.BlockSpec((DIL_HEADS, LANES), lambda b, g, n: (0, 0))],
        out_specs=[pl.BlockSpec((1, 1, tq, HD), lambda b, g, n: (b, g, n, 0)),
                   pl.BlockSpec((1, 1, tq, LANES), lambda b, g, n: (b, g, n, 0))],
        out_shape=[jax.ShapeDtypeStruct((B, d, L, HD), BF16),
                   jax.ShapeDtypeStruct((B, d, L, LANES), F32)],
        scratch_shapes=[pltpu.VMEM((nk, HD), BF16), pltpu.VMEM((nk, HD), BF16),
                        pltpu.VMEM((DIL_HEADS, tq, nk), F32)],
        compiler_params=_cparams(("parallel", "parallel", "parallel")),
        name=f"dilated_g{gi}",
    )(*([qkv] * 7 + [pq, pk, tab]))


def _outproj_kernel(*refs, dils):
    ng = len(dils)
    oa_ref = refs[0]
    og_refs = refs[1:1 + ng]
    lg_refs = refs[1 + ng:1 + 2 * ng]
    (ga_ref, gb_ref, x_ref, mod_ref, g_ref, wo_ref, wr_ref,
     x1_ref, h2_ref, aff_ref, afft_ref, o_scr, l_scr) = refs[1 + 2 * ng:]
    tm = x_ref.shape[1]
    for i, d in enumerate(dils):
        rows = tm // d
        for g in range(d):
            dst = pl.ds(g, rows, stride=d) if d > 1 else slice(None)
            og = og_refs[i][0, g].astype(F32)
            for h in range(DIL_HEADS):
                o_scr[i, h, dst, :] = og[:, h * DIL_HEAD_DIM:(h + 1) * DIL_HEAD_DIM]
            l_scr[i, dst, :] = lg_refs[i][0, g]
    lses = [l_scr[i] for i in range(ng)]
    mx = functools.reduce(jnp.maximum, lses)
    es = [jnp.exp(l - mx) for l in lses]
    tot = functools.reduce(lambda a, b: a + b, es)
    wts = [e / tot for e in es]
    cols = []
    for h in range(DIL_HEADS):
        c = h * (LANES // DIL_HEADS)
        cols.append(functools.reduce(
            lambda a, b: a + b, [wts[i][:, c:c + 1] * o_scr[i, h] for i in range(ng)]))
    o_b = jnp.concatenate(cols, axis=1)
    comb = (ga_ref[0, 0].astype(F32) * oa_ref[0].astype(F32)
            + gb_ref[0, 0].astype(F32) * o_b).astype(BF16)
    x1 = x_ref[0] + mod_ref[0, 2:3, :] * jnp.dot(comb, wo_ref[...], preferred_element_type=F32)
    x1_ref[0] = x1
    y = x1 * lax.rsqrt(jnp.mean(x1 * x1, axis=-1, keepdims=True) + NORM_EPS) * g_ref[...]
    h2 = y * (1.0 + mod_ref[0, 4:5, :]) + mod_ref[0, 3:4, :]
    h2_ref[0] = h2.astype(BF16)
    logits = jnp.dot(h2, wr_ref[...], preferred_element_type=F32,
                     precision=lax.Precision.HIGHEST)
    lane = lax.broadcasted_iota(I32, logits.shape, 1)
    logits = jnp.where(lane < N_EXPERTS, logits, -jnp.inf)
    e = jnp.exp(logits - jnp.max(logits, axis=-1, keepdims=True))
    aff = e / jnp.sum(e, axis=-1, keepdims=True)
    aff_ref[0] = aff
    afft_ref[0] = aff.T[:N_EXPERTS, :]


def _out_projection(o_a, dil_outs, main, gate_col, x, mod, g, w_out, w_router, tm=512):
    B, S, D = x.shape
    const = lambda b, i: (0, 0)
    row = lambda b, i: (b, i, 0)
    wr = jnp.zeros((D, LANES), F32).at[:, :N_EXPERTS].set(w_router)
    dils = tuple(o.shape[1] for o, _ in dil_outs)
    og_specs = [pl.BlockSpec((1, d, tm // d, D), lambda b, i: (b, 0, i, 0)) for d in dils]
    lg_specs = [pl.BlockSpec((1, d, tm // d, LANES), lambda b, i: (b, 0, i, 0)) for d in dils]
    return pl.pallas_call(
        functools.partial(_outproj_kernel, dils=dils),
        grid=(B, S // tm),
        in_specs=[pl.BlockSpec((1, tm, D), row)] + og_specs + lg_specs + [
                  pl.BlockSpec((1, 1, tm, D), lambda b, i: (b, 0, i, gate_col)),
                  pl.BlockSpec((1, 1, tm, D), lambda b, i: (b, 0, i, gate_col + 1)),
                  pl.BlockSpec((1, tm, D), row),
                  pl.BlockSpec((1, 6, D), lambda b, i: (b, 0, 0)),
                  pl.BlockSpec((1, D), const),
                  pl.BlockSpec((D, D), const),
                  pl.BlockSpec((D, LANES), const)],
        out_specs=[pl.BlockSpec((1, tm, D), row),
                   pl.BlockSpec((1, tm, D), row),
                   pl.BlockSpec((1, tm, LANES), row),
                   pl.BlockSpec((1, N_EXPERTS, tm), lambda b, i: (b, 0, i))],
        out_shape=[jax.ShapeDtypeStruct((B, S, D), F32),
                   jax.ShapeDtypeStruct((B, S, D), BF16),
                   jax.ShapeDtypeStruct((B, S, LANES), F32),
                   jax.ShapeDtypeStruct((B, N_EXPERTS, S), F32)],
        scratch_shapes=[pltpu.VMEM((len(dils), DIL_HEADS, tm, DIL_HEAD_DIM), F32),
                        pltpu.VMEM((len(dils), tm, LANES), F32)],
        compiler_params=_cparams(("parallel", "parallel")),
        name="out_projection",
    )(o_a, *[o for o, _ in dil_outs], *[l for _, l in dil_outs], main, main, x, mod, g, w_out, wr)


def _select_kernel(afft_ref, thr_ref, tie_ref, slot_ref, boff_ref, *, cap, chunk):
    aff = afft_ref[0]
    E, S = aff.shape
    bits = lax.bitcast_convert_type(aff, I32)

    def count(mask):
        return jnp.sum(mask.astype(I32), axis=1, keepdims=True)

    def thr_step(i, v):
        cand = v | jnp.left_shift(jnp.int32(1), 30 - i)
        return jnp.where(count(bits >= cand) >= cap, cand, v)

    thr = lax.fori_loop(0, 31, thr_step, jnp.zeros((E, 1), I32))
    need = cap - count(bits > thr)
    eq = bits == thr
    idx = lax.broadcasted_iota(I32, (E, S), 1)
    nbits = max(1, (S - 1).bit_length())

    def tie_step(i, j):
        cand = j | jnp.left_shift(jnp.int32(1), nbits - 1 - i)
        return jnp.where(count(eq & (idx < cand)) < need, cand, j)

    tie = lax.fori_loop(0, nbits, tie_step, jnp.zeros((E, 1), I32))
    sel = (bits > thr) | (eq & (idx <= tie))
    thr_ref[0] = jnp.broadcast_to(thr, (E, LANES))
    tie_ref[0] = jnp.broadcast_to(tie, (E, LANES))

    upper = (lax.broadcasted_iota(I32, (chunk, chunk), 0)
             <= lax.broadcasted_iota(I32, (chunk, chunk), 1)).astype(BF16)
    carry = jnp.zeros((E, 1), F32)
    self_ = sel.astype(F32)
    lane = lax.broadcasted_iota(I32, (E, LANES), 1)
    boff = jnp.zeros((E, LANES), I32)
    for c in range(S // chunk):
        boff = jnp.where(lane == c, carry.astype(I32), boff)
        scf = self_[:, c * chunk:(c + 1) * chunk]
        incl = jnp.dot(scf.astype(BF16), upper, preferred_element_type=F32)
        pos = (incl - scf + carry).astype(I32)
        slot_ref[0, :, c * chunk:(c + 1) * chunk] = jnp.where(scf > 0.0, pos, -1)
        carry = carry + incl[:, chunk - 1:chunk]
    boff_ref[0] = jnp.where(lane == S // chunk, carry.astype(I32), boff)


def _select(aff_t, cap, chunk):
    B, E, S = aff_t.shape
    assert S // chunk < LANES
    return pl.pallas_call(
        functools.partial(_select_kernel, cap=cap, chunk=chunk),
        grid=(B,),
        in_specs=[pl.BlockSpec((1, E, S), lambda b: (b, 0, 0))],
        out_specs=[pl.BlockSpec((1, E, LANES), lambda b: (b, 0, 0)),
                   pl.BlockSpec((1, E, LANES), lambda b: (b, 0, 0)),
                   pl.BlockSpec((1, E, S), lambda b: (b, 0, 0)),
                   pl.BlockSpec((1, E, LANES), lambda b: (b, 0, 0))],
        out_shape=[jax.ShapeDtypeStruct((B, E, LANES), I32),
                   jax.ShapeDtypeStruct((B, E, LANES), I32),
                   jax.ShapeDtypeStruct((B, E, S), I32),
                   jax.ShapeDtypeStruct((B, E, LANES), I32)],
        compiler_params=_cparams(("parallel",)),
        name="ec_select",
    )(aff_t)


def _gather_kernel(boff_ref, slot_ref, h_ref, xe_ref, acc_ref, *, sb):
    _, _, nblk, tbk = slot_ref.shape
    cap = xe_ref.shape[2]
    base = (pl.program_id(0) * pl.num_programs(1) + pl.program_id(1)) * (nblk + 1)
    for s0 in range(0, cap, sb):
        def count(i, c):
            o = boff_ref[base + i]
            return (c[0] + (o <= s0).astype(I32), c[1] + (o < s0 + sb).astype(I32))

        n_le, n_lt = lax.fori_loop(0, nblk, count, (jnp.int32(0), jnp.int32(0)))
        acc_ref[...] = jnp.zeros(acc_ref.shape, F32)

        def body(tb, carry):
            onehot = (lax.broadcasted_iota(I32, (sb, tbk), 0) + s0
                      == slot_ref[0, 0, pl.ds(tb, 1), :]).astype(BF16)
            off = pl.multiple_of(tb * tbk, tbk)
            acc_ref[...] += jnp.dot(onehot, h_ref[0, pl.ds(off, tbk), :],
                                    preferred_element_type=F32)
            return carry

        lax.fori_loop(n_le - 1, n_lt, body, 0)
        xe_ref[0, 0, s0:s0 + sb, :] = acc_ref[...].astype(BF16)


def _gather(boff_flat, slot, h2, cap, chunk, sb=256):
    B, E, S = slot.shape
    D = h2.shape[-1]
    return pl.pallas_call(
        functools.partial(_gather_kernel, sb=sb),
        grid_spec=pltpu.PrefetchScalarGridSpec(
            num_scalar_prefetch=1,
            grid=(B, E),
            in_specs=[pl.BlockSpec((1, 1, S // chunk, chunk), lambda b, e, off: (b, e, 0, 0)),
                      pl.BlockSpec((1, S, D), lambda b, e, off: (b, 0, 0))],
            out_specs=pl.BlockSpec((1, 1, cap, D), lambda b, e, off: (b, e, 0, 0)),
            scratch_shapes=[pltpu.VMEM((sb, D), F32)]),
        out_shape=jax.ShapeDtypeStruct((B, E, cap, D), BF16),
        compiler_params=_cparams(("parallel", "arbitrary")),
        name="ec_gather",
    )(boff_flat, slot.reshape(B, E, S // chunk, chunk), h2)


def _ffn_kernel(xe_ref, wg_ref, wu_ref, wd_ref, ye_ref, acc_ref):
    f = pl.program_id(2)
    nb, _, cap, D = xe_ref.shape
    x = xe_ref[...].reshape(nb * cap, D)
    g = jnp.dot(x, wg_ref[0].astype(BF16), preferred_element_type=F32)
    u = jnp.dot(x, wu_ref[0].astype(BF16), preferred_element_type=F32)
    hid = (g * jax.nn.sigmoid(g) * u).astype(BF16)
    part = jnp.dot(hid, wd_ref[0].astype(BF16), preferred_element_type=F32)

    @pl.when(f == 0)
    def _():
        acc_ref[...] = part

    @pl.when(f > 0)
    def _():
        acc_ref[...] += part

    @pl.when(f == pl.num_programs(2) - 1)
    def _():
        ye_ref[...] = acc_ref[...].astype(BF16).reshape(ye_ref.shape)


def _expert_ffn(xe, w_gate, w_up, w_down, nb=2, tf=512):
    B, E, cap, D = xe.shape
    F = w_gate.shape[-1]
    return pl.pallas_call(
        _ffn_kernel,
        grid=(E, B // nb, F // tf),
        in_specs=[pl.BlockSpec((nb, 1, cap, D), lambda e, b, f: (b, e, 0, 0)),
                  pl.BlockSpec((1, D, tf), lambda e, b, f: (e, 0, f)),
                  pl.BlockSpec((1, D, tf), lambda e, b, f: (e, 0, f)),
                  pl.BlockSpec((1, tf, D), lambda e, b, f: (e, f, 0))],
        out_specs=pl.BlockSpec((nb, 1, cap, D), lambda e, b, f: (b, e, 0, 0)),
        out_shape=jax.ShapeDtypeStruct((B, E, cap, D), BF16),
        scratch_shapes=[pltpu.VMEM((nb * cap, D), F32)],
        compiler_params=_cparams(("parallel", "parallel", "arbitrary")),
        name="ec_ffn",
    )(xe, w_gate, w_up, w_down)


def _combine_kernel(boff_ref, aff_ref, thr_ref, tie_ref, ye_ref, x1_ref, mod_ref, g_ref, out_ref,
                    slot_ref, carry_ref, *, tb, wslot, nblk):
    b = pl.program_id(0)
    t2 = pl.program_id(1)
    e = pl.program_id(2)
    nsub = aff_ref.shape[1] // tb
    lane = lax.broadcasted_iota(I32, (tb, LANES), 1)

    @pl.when(e == 0)
    def _():
        @pl.when(t2 == 0)
        def _():
            carry_ref[...] = jnp.zeros(carry_ref.shape, F32)

        lower = (lax.broadcasted_iota(I32, (tb, tb), 0)
                 >= lax.broadcasted_iota(I32, (tb, tb), 1)).astype(BF16)
        for u in range(nsub):
            rows = slice(u * tb, (u + 1) * tb)
            bits = lax.bitcast_convert_type(aff_ref[0, rows, :], I32)
            tok = (t2 * nsub + u) * tb + lax.broadcasted_iota(I32, (tb, LANES), 0)
            thr = thr_ref[0]
            sel = ((bits > thr) | ((bits == thr) & (tok <= tie_ref[0]))) & (lane < N_EXPERTS)
            self_ = sel.astype(F32)
            incl = jnp.dot(lower, self_.astype(BF16), preferred_element_type=F32)
            pos = (incl - self_ + carry_ref[...]).astype(I32)
            slot_ref[rows, :] = jnp.where(sel, pos, -1)
            carry_ref[...] = carry_ref[...] + incl[tb - 1:tb, :]
        out_ref[...] = jnp.zeros(out_ref.shape, F32)

    pick = lane == e
    base = (b * pl.num_programs(2) + e) * (nblk + 1) + t2 * nsub
    for u in range(nsub):
        rows = slice(u * tb, (u + 1) * tb)
        lo = boff_ref[base + u]
        hi = boff_ref[base + u + 1]
        slot_e = jnp.sum(jnp.where(pick, slot_ref[rows, :], 0), axis=1, keepdims=True)
        gate_e = jnp.sum(jnp.where(pick, aff_ref[0, rows, :], 0.0), axis=1, keepdims=True)

        def body(w, carry):
            woff = pl.multiple_of(w * wslot, wslot)
            onehot = (lax.broadcasted_iota(I32, (tb, wslot), 1) + woff == slot_e).astype(BF16)
            out_ref[0, rows, :] += gate_e * jnp.dot(onehot, ye_ref[0, 0, pl.ds(woff, wslot), :],
                                                    preferred_element_type=F32)
            return carry

        lax.fori_loop(lo // wslot, (hi + wslot - 1) // wslot, body, 0)

    @pl.when(e == pl.num_programs(2) - 1)
    def _():
        for u in range(nsub):
            rows = slice(u * tb, (u + 1) * tb)
            x2 = x1_ref[0, rows, :] + mod_ref[0, 5:6, :] * out_ref[0, rows, :]
            out_ref[0, rows, :] = (x2 * lax.rsqrt(jnp.mean(x2 * x2, axis=-1, keepdims=True)
                                                  + NORM_EPS) * g_ref[...])


def _combine(boff_flat, aff, thr, tie, ye, x1, mod, g_final, tb, tsup=2048, wslot=256):
    B, S, D = x1.shape
    E, cap = ye.shape[1], ye.shape[2]
    tsup = min(tsup, S)
    return pl.pallas_call(
        functools.partial(_combine_kernel, tb=tb, wslot=wslot, nblk=S // tb),
        grid_spec=pltpu.PrefetchScalarGridSpec(
            num_scalar_prefetch=1,
            grid=(B, S // tsup, E),
            in_specs=[pl.BlockSpec((1, tsup, LANES), lambda b, t, e, off: (b, t, 0)),
                      pl.BlockSpec((1, 1, LANES), lambda b, t, e, off: (b, 0, 0)),
                      pl.BlockSpec((1, 1, LANES), lambda b, t, e, off: (b, 0, 0)),
                      pl.BlockSpec((1, 1, cap, D), lambda b, t, e, off: (b, e, 0, 0)),
                      pl.BlockSpec((1, tsup, D), lambda b, t, e, off: (b, t, 0)),
                      pl.BlockSpec((1, 6, D), lambda b, t, e, off: (b, 0, 0)),
                      pl.BlockSpec((1, D), lambda b, t, e, off: (0, 0))],
            out_specs=pl.BlockSpec((1, tsup, D), lambda b, t, e, off: (b, t, 0)),
            scratch_shapes=[pltpu.VMEM((tsup, LANES), I32), pltpu.VMEM((1, LANES), F32)]),
        out_shape=jax.ShapeDtypeStruct((B, S, D), F32),
        compiler_params=_cparams(("parallel", "arbitrary", "arbitrary")),
        name="ec_combine",
    )(boff_flat, aff, thr, tie, ye, x1, mod, g_final)


def _prep_weights(w_in, w_uq, w_ukv):
    D = w_in.shape[0]
    H = MLA_HEADS
    n_lat = MLA_Q_RANK + MLA_KV_RANK + MLA_ROPE
    n_grp = 3 * DIL_HEADS * DIL_HEAD_DIM
    n_dil = len(DIL_GROUPS) * n_grp
    w_lat = jnp.zeros((D, LAT_PAD), F32).at[:, :n_lat].set(w_in[:, :n_lat]).astype(BF16)
    w_grp = [w_in[:, n_lat + i * n_grp:n_lat + (i + 1) * n_grp].astype(BF16)
             for i in range(len(DIL_GROUPS))]
    w_gates = w_in[:, n_lat + n_dil:].astype(BF16)
    half = MLA_ROPE // 2
    uq = w_uq.reshape(MLA_Q_RANK, H, MLA_NOPE + MLA_ROPE)
    pe = uq[:, :, MLA_NOPE:]
    zq = jnp.zeros((MLA_Q_RANK, H, QK_PAD - MLA_NOPE - MLA_ROPE), F32)
    wq = jnp.concatenate([uq[:, :, :MLA_NOPE], pe, zq], axis=2)
    wqs = jnp.concatenate([jnp.zeros_like(uq[:, :, :MLA_NOPE]), pe[:, :, half:], pe[:, :, :half], zq],
                          axis=2)
    ukv = w_ukv.reshape(MLA_KV_RANK, H, MLA_NOPE + MLA_V)
    wk = jnp.concatenate([ukv[:, :, :MLA_NOPE],
                          jnp.zeros((MLA_KV_RANK, H, QK_PAD - MLA_NOPE), F32)], axis=2)
    wv = jnp.concatenate([ukv[:, :, MLA_NOPE:],
                          jnp.zeros((MLA_KV_RANK, H, QK_PAD - MLA_V), F32)], axis=2)
    return (w_lat, w_grp, w_gates, wq.reshape(MLA_Q_RANK, H * QK_PAD).astype(BF16),
            wqs.reshape(MLA_Q_RANK, H * QK_PAD).astype(BF16),
            wk.reshape(MLA_KV_RANK, H * QK_PAD).astype(BF16),
            wv.reshape(MLA_KV_RANK, H * QK_PAD).astype(BF16))


def kernel(x, c, positions, w_ada, b_ada, g_norm_mix, w_in, g_q_lat, g_kv_lat, w_uq, w_ukv,
           rel_bias, w_out, g_norm_ffn, w_router, w_gate, w_up, w_down, g_final):
    B, S, D = x.shape
    assert w_ada.shape[0] == 1, "the final norm is fused into the (single) layer's last kernel"
    assert DIL_GROUPS[0][1] == 1, "the gates ride along with the undilated group's projection"
    for l in range(w_ada.shape[0]):
        mod = _modulation(c, w_ada[l], b_ada[l])
        w_lat, w_grp, w_gates, wq, wqs, wk, wv = _prep_weights(w_in[l], w_uq[l], w_ukv[l])
        g_mix = g_norm_mix[l].reshape(1, D)
        n_qkv = w_grp[0].shape[1] // D
        lat, main = _in_projection(x, mod, g_mix, jnp.concatenate([w_grp[0], w_gates], axis=1),
                                   d=1, n_plain=n_qkv, w_lat=w_lat)
        qkvs = [main] + [_in_projection(x, mod, g_mix, w_grp[gi], d=d, n_plain=n_qkv)[0]
                         for gi, (_, d) in enumerate(DIL_GROUPS) if gi > 0]
        q, k, v = _mla_prep(lat, positions, g_q_lat[l].reshape(1, -1), g_kv_lat[l].reshape(1, -1),
                            wq, wqs, wk, wv)
        o_a = _flash_attention(q, k, v)
        dil_outs = [_dilated_group(qkvs[gi], positions, rel_bias, gi, window, d)
                    for gi, (window, d) in enumerate(DIL_GROUPS)]
        x1, h2, aff, aff_t = _out_projection(o_a, dil_outs, main, n_qkv, x, mod,
                                             g_norm_ffn[l].reshape(1, D),
                                             w_out[l].astype(BF16), w_router[l])
        cap = EC_CAPACITY_FACTOR * S // N_EXPERTS
        chunk = 512
        thr, tie, slot, boff = _select(aff_t, cap, chunk)
        boff_flat = boff[:, :, :S // chunk + 1].reshape(-1)
        xe = _gather(boff_flat, slot, h2, cap, chunk)
        ye = _expert_ffn(xe, w_gate[l], w_up[l], w_down[l])
        pad = jnp.zeros((B, 1, LANES - N_EXPERTS), I32)
        thr_l = jnp.concatenate([thr[:, :, 0].reshape(B, 1, N_EXPERTS), pad], axis=2)
        tie_l = jnp.concatenate([tie[:, :, 0].reshape(B, 1, N_EXPERTS), pad], axis=2)
        x = _combine(boff_flat, aff, thr_l, tie_l, ye, x1, mod, g_final.reshape(1, D), tb=chunk)
    return x
```

```python
import functools
import math

import jax
import jax.numpy as jnp
from jax import lax
from jax.experimental import pallas as pl
from jax.experimental.pallas import tpu as pltpu

F32 = jnp.float32
BF16 = jnp.bfloat16
I32 = jnp.int32

MLA_HEADS = 8
MLA_Q_RANK = 384
MLA_KV_RANK = 256
MLA_NOPE = 128
MLA_ROPE = 64
MLA_V = 128
ROPE_THETA = 10000.0
DIL_GROUPS = ((128, 1), (512, 4), (2048, 16))
DIL_HEADS = 8
DIL_HEAD_DIM = 128
REL_BUCKETS = 32
REL_MAX_DIST = 1024
N_EXPERTS = 16
EC_CAPACITY_FACTOR = 2
NORM_EPS = 1e-6
NEG_INF = -1e30

LANES = 128
QK_PAD = 256
LAT_PAD = 768
VMEM_LIMIT = 56 * 1024 * 1024


def _cparams(sem):
    return pltpu.CompilerParams(dimension_semantics=sem, vmem_limit_bytes=VMEM_LIMIT)


def _nt_dot(a, b):
    return lax.dot_general(a, b, (((1,), (1,)), ((), ())), preferred_element_type=F32)


def _mod_kernel(c_ref, w_ref, b_ref, o_ref):
    c = c_ref[...]
    cond = c * jax.nn.sigmoid(c)
    o_ref[...] = jnp.dot(cond, w_ref[...], preferred_element_type=F32,
                         precision=lax.Precision.HIGHEST) + b_ref[...]


def _modulation(c, w_ada, b_ada):
    B, D = c.shape
    rows = 8
    c8 = jnp.zeros((rows, D), F32).at[:B].set(c)
    n6 = w_ada.shape[1]
    out = pl.pallas_call(
        _mod_kernel,
        grid=(n6 // D,),
        in_specs=[pl.BlockSpec((rows, D), lambda j: (0, 0)),
                  pl.BlockSpec((D, D), lambda j: (0, j)),
                  pl.BlockSpec((1, D), lambda j: (0, j))],
        out_specs=pl.BlockSpec((rows, D), lambda j: (0, j)),
        out_shape=jax.ShapeDtypeStruct((rows, n6), F32),
        compiler_params=_cparams(("arbitrary",)),
        name="modulation",
    )(c8, w_ada, b_ada.reshape(1, n6))
    return out[:B].reshape(B, 6, D)


def _inproj_kernel(*refs, d, n_plain, n_tiles, with_lat):
    nx = len(refs) - (7 if with_lat else 5)
    x_refs = refs[:nx]
    if with_lat:
        mod_ref, g_ref, wl_ref, wm_ref, lat_ref, out_ref, h_ref = refs[nx:]
    else:
        mod_ref, g_ref, wm_ref, out_ref, h_ref = refs[nx:]
    n = pl.program_id(2)
    tm = x_refs[0].shape[1]
    rows = tm // d

    @pl.when(n == 0)
    def _():
        for g in range(d):
            if d == 1:
                x = x_refs[0][0]
            else:
                x = jnp.concatenate([xr[0, pl.ds(g, rows, stride=d), :] for xr in x_refs], axis=1)
            y = x * lax.rsqrt(jnp.mean(x * x, axis=-1, keepdims=True) + NORM_EPS) * g_ref[...]
            h = y * (1.0 + mod_ref[0, 1:2, :]) + mod_ref[0, 0:1, :]
            h_ref[g * rows:(g + 1) * rows, :] = h.astype(BF16)
        if with_lat:
            lat_ref[0] = jnp.dot(h_ref[...], wl_ref[...], preferred_element_type=F32)

    acc = jnp.dot(h_ref[...], wm_ref[...], preferred_element_type=F32)
    tn = acc.shape[1]
    if n_plain == n_tiles:
        out_ref[0] = acc.astype(BF16).reshape(d, rows, tn)
    else:
        @pl.when(n < n_plain)
        def _():
            out_ref[0] = acc.astype(BF16).reshape(d, rows, tn)

        @pl.when(n >= n_plain)
        def _():
            out_ref[0] = jax.nn.sigmoid(acc).astype(BF16).reshape(d, rows, tn)


def _in_projection(x, mod, g, w_main, d, n_plain, w_lat=None, tm=1024, tn=1024):
    B, S, D = x.shape
    nmain = w_main.shape[1]
    n_tiles = nmain // tn
    with_lat = w_lat is not None
    if d == 1:
        in_specs = [pl.BlockSpec((1, tm, D), lambda b, i, n: (b, i, 0))]
    else:
        in_specs = [pl.BlockSpec((1, tm, LANES), functools.partial(lambda b, i, n, j: (b, i, j), j=j))
                    for j in range(D // LANES)]
    args = [x] * len(in_specs) + [mod, g]
    in_specs += [pl.BlockSpec((1, 6, D), lambda b, i, n: (b, 0, 0)),
                 pl.BlockSpec((1, D), lambda b, i, n: (0, 0))]
    out_specs = [pl.BlockSpec((1, d, tm // d, tn), lambda b, i, n: (b, 0, i, n))]
    out_shape = [jax.ShapeDtypeStruct((B, d, S // d, nmain), BF16)]
    if with_lat:
        in_specs.append(pl.BlockSpec((D, LAT_PAD), lambda b, i, n: (0, 0)))
        args.append(w_lat)
        out_specs.insert(0, pl.BlockSpec((1, tm, LAT_PAD), lambda b, i, n: (b, i, 0)))
        out_shape.insert(0, jax.ShapeDtypeStruct((B, S, LAT_PAD), F32))
    in_specs.append(pl.BlockSpec((D, tn), lambda b, i, n: (0, n)))
    args.append(w_main)
    return pl.pallas_call(
        functools.partial(_inproj_kernel, d=d, n_plain=n_plain, n_tiles=n_tiles, with_lat=with_lat),
        grid=(B, S // tm, n_tiles),
        in_specs=in_specs,
        out_specs=out_specs,
        out_shape=out_shape,
        scratch_shapes=[pltpu.VMEM((tm, D), BF16)],
        compiler_params=_cparams(("parallel", "parallel", "arbitrary")),
        name=f"in_projection_d{d}",
    )(*args)


def _mla_prep_kernel(lat_ref, pos_ref, gq_ref, gkv_ref, wq_ref, wqs_ref, wk_ref, wv_ref,
                     freq_ref, sgn_ref, sel_ref, sels_ref, one_ref, q_ref, k_ref, v_ref, *, scale):
    lat = lat_ref[0]
    cq = lat[:, :MLA_Q_RANK]
    ckv = lat[:, MLA_Q_RANK:MLA_Q_RANK + MLA_KV_RANK]
    kpe = lat[:, MLA_Q_RANK + MLA_KV_RANK:]
    cqn = (cq * lax.rsqrt(jnp.mean(cq * cq, axis=-1, keepdims=True) + NORM_EPS)
           * gq_ref[...]).astype(BF16)
    ckvn = (ckv * lax.rsqrt(jnp.mean(ckv * ckv, axis=-1, keepdims=True) + NORM_EPS)
            * gkv_ref[...]).astype(BF16)
    ang = pos_ref[0].astype(F32) * freq_ref[...]
    cos = jnp.cos(ang)
    sin = jnp.sin(ang) * sgn_ref[...]
    qa = jnp.dot(cqn, wq_ref[...], preferred_element_type=F32)
    qs = jnp.dot(cqn, wqs_ref[...], preferred_element_type=F32)
    kn = jnp.dot(ckvn, wk_ref[...], preferred_element_type=F32)
    v_ref[0] = (jnp.dot(ckvn, wv_ref[...], preferred_element_type=F32) + one_ref[...]).astype(BF16)
    hp = lax.Precision.HIGHEST
    rk = (jnp.dot(kpe, sel_ref[...], preferred_element_type=F32, precision=hp) * cos
          + jnp.dot(kpe, sels_ref[...], preferred_element_type=F32, precision=hp) * sin)
    cos_q = cos * scale
    sin_q = sin * scale
    for h in range(MLA_HEADS):
        sl = slice(h * QK_PAD, (h + 1) * QK_PAD)
        q_ref[0, :, sl] = (qa[:, sl] * cos_q + qs[:, sl] * sin_q).astype(BF16)
        k_ref[0, :, sl] = (kn[:, sl] + rk).astype(BF16)


def _mla_prep(lat, positions, g_q, g_kv, wq, wqs, wk, wv, tm=512):
    B, S, _ = lat.shape
    H = MLA_HEADS
    half = MLA_ROPE // 2
    inv_freq = ROPE_THETA ** (-jnp.arange(0, MLA_ROPE, 2, dtype=F32) / MLA_ROPE)
    freq = jnp.zeros((1, QK_PAD), F32).at[0, MLA_NOPE:MLA_NOPE + MLA_ROPE].set(
        jnp.concatenate([inv_freq, inv_freq]))
    sgn = jnp.zeros((1, QK_PAD), F32).at[0, MLA_NOPE:MLA_NOPE + half].set(-1.0)
    sgn = sgn.at[0, MLA_NOPE + half:MLA_NOPE + MLA_ROPE].set(1.0)
    r = jnp.arange(MLA_ROPE)
    sel = jnp.zeros((LANES, QK_PAD), F32).at[r, MLA_NOPE + r].set(1.0)
    sels = jnp.zeros((LANES, QK_PAD), F32).at[r, MLA_NOPE + (r + half) % MLA_ROPE].set(1.0)
    scale = (MLA_NOPE + MLA_ROPE) ** -0.5 * math.log2(math.e)
    ones_col = jnp.zeros((H, QK_PAD), F32).at[:, MLA_V].set(1.0).reshape(1, H * QK_PAD)
    const = lambda b, i: (0, 0)
    return pl.pallas_call(
        functools.partial(_mla_prep_kernel, scale=scale),
        grid=(B, S // tm),
        in_specs=[pl.BlockSpec((1, tm, LAT_PAD), lambda b, i: (b, i, 0)),
                  pl.BlockSpec((1, tm, 1), lambda b, i: (b, i, 0)),
                  pl.BlockSpec((1, MLA_Q_RANK), const),
                  pl.BlockSpec((1, MLA_KV_RANK), const),
                  pl.BlockSpec((MLA_Q_RANK, H * QK_PAD), const),
                  pl.BlockSpec((MLA_Q_RANK, H * QK_PAD), const),
                  pl.BlockSpec((MLA_KV_RANK, H * QK_PAD), const),
                  pl.BlockSpec((MLA_KV_RANK, H * QK_PAD), const),
                  pl.BlockSpec((1, QK_PAD), const),
                  pl.BlockSpec((1, QK_PAD), const),
                  pl.BlockSpec((LANES, QK_PAD), const),
                  pl.BlockSpec((LANES, QK_PAD), const),
                  pl.BlockSpec((1, H * QK_PAD), const)],
        out_specs=[pl.BlockSpec((1, tm, H * QK_PAD), lambda b, i: (b, i, 0)),
                   pl.BlockSpec((1, tm, H * QK_PAD), lambda b, i: (b, i, 0)),
                   pl.BlockSpec((1, tm, H * QK_PAD), lambda b, i: (b, i, 0))],
        out_shape=[jax.ShapeDtypeStruct((B, S, H * QK_PAD), BF16),
                   jax.ShapeDtypeStruct((B, S, H * QK_PAD), BF16),
                   jax.ShapeDtypeStruct((B, S, H * QK_PAD), BF16)],
        compiler_params=_cparams(("parallel", "parallel")),
        name="mla_prep",
    )(lat, positions.reshape(B, S, 1), g_q, g_kv, wq, wqs, wk, wv, freq, sgn, sel, sels, ones_col)


def _flash_kernel(q_ref, k_ref, v_ref, o_ref, m_ref, acc_ref, s_ref, *, tq, tkc):
    S = k_ref.shape[1]
    nk = S // tkc
    nq = S // tq
    half = nk // 2
    acc_ref[...] = jnp.zeros(acc_ref.shape, F32)
    m_ref[...] = jnp.full(m_ref.shape, -jnp.inf, F32)

    def scores(qi, kj):
        qoff = pl.multiple_of(qi * tq, tq)
        koff = pl.multiple_of(kj * tkc, tkc)
        return _nt_dot(q_ref[0, pl.ds(qoff, tq), :], k_ref[0, pl.ds(koff, tkc), :])

    def consume(kj, slot, first):
        koff = pl.multiple_of(kj * tkc, tkc)
        s = s_ref[slot]
        m_prev = m_ref[...]
        if first is not None:
            m_prev = jnp.where(first, -jnp.inf, m_prev)
        m_new = jnp.maximum(m_prev, jnp.max(s, axis=-1, keepdims=True))
        alpha = jnp.exp2(m_prev - m_new)
        p = jnp.exp2(s - m_new).astype(BF16)
        acc_ref[...] = alpha * acc_ref[...] + jnp.dot(p, v_ref[0, pl.ds(koff, tkc), :],
                                                      preferred_element_type=F32)
        m_ref[...] = m_new

    s_ref[0] = scores(0, 0)

    def body(t, carry):
        qi = t // half
        kj = 2 * (t % half)
        s_ref[1] = scores(qi, kj + 1)
        consume(kj, 0, kj == 0)
        t1 = jnp.minimum(t + 1, nq * half - 1)
        s_ref[0] = scores(t1 // half, 2 * (t1 % half))
        consume(kj + 1, 1, None)

        @pl.when(kj + 2 == nk)
        def _():
            acc = acc_ref[...]
            qoff = pl.multiple_of(qi * tq, tq)
            o_ref[0, pl.ds(qoff, tq), :] = (acc[:, :MLA_V] / acc[:, MLA_V:MLA_V + 1]).astype(BF16)
        return carry

    lax.fori_loop(0, nq * half, body, 0)


def _flash_attention(q, k, v, tq=1024, tkc=1024):
    B, S, _ = q.shape
    H = MLA_HEADS
    assert (S // tkc) % 2 == 0 and S % tq == 0
    whole = pl.BlockSpec((1, S, QK_PAD), lambda b, h: (b, 0, h))
    return pl.pallas_call(
        functools.partial(_flash_kernel, tq=tq, tkc=tkc),
        grid=(B, H),
        in_specs=[whole, whole, whole],
        out_specs=pl.BlockSpec((1, S, MLA_V), lambda b, h: (b, 0, h)),
        out_shape=jax.ShapeDtypeStruct((B, S, H * MLA_V), BF16),
        scratch_shapes=[pltpu.VMEM((tq, 1), F32), pltpu.VMEM((tq, QK_PAD), F32),
                        pltpu.VMEM((2, tq, tkc), F32)],
        compiler_params=_cparams(("parallel", "parallel")),
        name="mla_flash",
    )(q, k, v)


def _t5_bucket(rel):
    nb = REL_BUCKETS // 2
    max_exact = nb // 2
    ret = jnp.where(rel > 0, nb, 0)
    n = jnp.abs(rel)
    nf = jnp.maximum(n, 1).astype(F32)
    large = max_exact + (jnp.log(nf / max_exact) / math.log(REL_MAX_DIST / max_exact)
                         * (nb - max_exact)).astype(I32)
    large = jnp.minimum(large, nb - 1)
    return ret + jnp.where(n < max_exact, n, large)


def _dilated_kernel(q_ref, kp_ref, kc_ref, kn_ref, vp_ref, vc_ref, vn_ref, pq_ref, pk_ref, tab_ref,
                    o_ref, l_ref, kcat_ref, vcat_ref, s_scr, *, r, tq, ts, n_rows):
    n = pl.program_id(2)
    nsub = tq // ts
    nk = ts + 2 * r
    kcat_ref[0:r, :] = kp_ref[0, 0]
    kcat_ref[r:r + tq, :] = kc_ref[0, 0]
    kcat_ref[r + tq:tq + 2 * r, :] = kn_ref[0, 0]
    vcat_ref[0:r, :] = vp_ref[0, 0]
    vcat_ref[r:r + tq, :] = vc_ref[0, 0]
    vcat_ref[r + tq:tq + 2 * r, :] = vn_ref[0, 0]
    lane_grp = lax.broadcasted_iota(I32, (ts, LANES), 1) // (LANES // DIL_HEADS)
    scale = DIL_HEAD_DIM ** -0.5 * math.log2(math.e)
    for j in range(nsub):
        qi = n * tq + j * ts + lax.broadcasted_iota(I32, (ts, nk), 0)
        kj = n * tq + j * ts - r + lax.broadcasted_iota(I32, (ts, nk), 1)
        valid = (jnp.abs(kj - qi) <= r) & (kj >= 0) & (kj < n_rows)
        rel = pk_ref[0, 0, j] - pq_ref[0, 0, j * ts:(j + 1) * ts, :]
        bucket = jnp.where(valid, _t5_bucket(rel), REL_BUCKETS)
        for h in range(DIL_HEADS):
            sl = slice(h * DIL_HEAD_DIM, (h + 1) * DIL_HEAD_DIM)
            s = _nt_dot(q_ref[0, 0, j * ts:(j + 1) * ts, sl], kcat_ref[j * ts:j * ts + nk, sl]) * scale
            tab = jnp.broadcast_to(tab_ref[h:h + 1, :], (ts, LANES))
            bias = jnp.concatenate(
                [jnp.take_along_axis(tab, bucket[:, c * LANES:(c + 1) * LANES], axis=1)
                 for c in range(nk // LANES)], axis=1)
            s_scr[j * DIL_HEADS + h] = s + bias
    for j in range(nsub):
        lse_tile = jnp.zeros((ts, LANES), F32)
        for h in range(DIL_HEADS):
            sl = slice(h * DIL_HEAD_DIM, (h + 1) * DIL_HEAD_DIM)
            s = s_scr[j * DIL_HEADS + h]
            m = jnp.max(s, axis=-1, keepdims=True)
            p = jnp.exp2(s - m)
            den = jnp.sum(p, axis=-1, keepdims=True)
            o = jnp.dot(p.astype(BF16), vcat_ref[j * ts:j * ts + nk, sl],
                        preferred_element_type=F32) / den
            o_ref[0, 0, j * ts:(j + 1) * ts, sl] = o.astype(BF16)
            lse_tile = jnp.where(lane_grp == h, m * math.log(2.0) + jnp.log(den), lse_tile)
        l_ref[0, 0, j * ts:(j + 1) * ts, :] = lse_tile


def _dilated_group(qkv, positions, table, gi, window, d, tq=512, ts=128):
    B, _, L, _ = qkv.shape
    HD = DIL_HEADS * DIL_HEAD_DIM
    r = window // (2 * d)
    tq = min(tq, L)
    nt = L // tq
    nsub = tq // ts
    rb = tq // r
    nk = ts + 2 * r
    pcls = positions.reshape(B, L, d).transpose(0, 2, 1)
    pq = pcls.reshape(B, d, L, 1)
    ppad = jnp.pad(pcls, ((0, 0), (0, 0), (r, ts + r)))
    pk = jnp.concatenate([ppad[:, :, :L].reshape(B, d, L // ts, ts),
                          ppad[:, :, ts:ts + L].reshape(B, d, L // ts, ts)[..., :2 * r]], axis=-1)
    pk = pk.reshape(B, d, L // ts, 1, nk)
    tab = jnp.zeros((DIL_HEADS, LANES), F32).at[:, :REL_BUCKETS].set(
        table[:, gi * DIL_HEADS:(gi + 1) * DIL_HEADS].T * math.log2(math.e))
    tab = tab.at[:, REL_BUCKETS].set(NEG_INF)

    def cur(c):
        return pl.BlockSpec((1, 1, tq, HD), lambda b, g, n: (b, g, n, c))

    def prv(c):
        return pl.BlockSpec((1, 1, r, HD), lambda b, g, n: (b, g, jnp.maximum(n * rb - 1, 0), c))

    def nxt(c):
        return pl.BlockSpec((1, 1, r, HD),
                            lambda b, g, n: (b, g, jnp.minimum((n + 1) * rb, L // r - 1), c))

    return pl.pallas_call(
        functools.partial(_dilated_kernel, r=r, tq=tq, ts=ts, n_rows=L),
        grid=(B, d, nt),
        in_specs=[cur(0), prv(1), cur(1), nxt(1), prv(2), cur(2), nxt(2),
                  pl.BlockSpec((1, 1, tq, 1), lambda b, g, n: (b, g, n, 0)),
                  pl.BlockSpec((1, 1, nsub, 1, nk), lambda b, g, n: (b, g, n, 0, 0)),
                  pl.BlockSpec((DIL_HEADS, LANES), lambda b, g, n: (0, 0))],
        out_specs=[pl.BlockSpec((1, 1, tq, HD), lambda b, g, n: (b, g, n, 0)),
                   pl.BlockSpec((1, 1, tq, LANES), lambda b, g, n: (b, g, n, 0))],
        out_shape=[jax.ShapeDtypeStruct((B, d, L, HD), BF16),
                   jax.ShapeDtypeStruct((B, d, L, LANES), F32)],
        scratch_shapes=[pltpu.VMEM((tq + 2 * r, HD), BF16), pltpu.VMEM((tq + 2 * r, HD), BF16),
                        pltpu.VMEM((nsub * DIL_HEADS, ts, nk), F32)],
        compiler_params=_cparams(("parallel", "parallel", "parallel")),
        name=f"dilated_g{gi}",
    )(*([qkv] * 7 + [pq, pk, tab]))


def _outproj_kernel(*refs, dils):
    ng = len(dils)
    oa_ref = refs[0]
    og_refs = refs[1:1 + ng]
    lg_refs = refs[1 + ng:1 + 2 * ng]
    (ga_ref, gb_ref, x_ref, mod_ref, g_ref, wo_ref, wr_ref,
     x1_ref, h2_ref, aff_ref, afft_ref, o_scr, l_scr) = refs[1 + 2 * ng:]
    tm = x_ref.shape[1]
    for i, d in enumerate(dils):
        rows = tm // d
        for g in range(d):
            dst = pl.ds(g, rows, stride=d) if d > 1 else slice(None)
            og = og_refs[i][0, g].astype(F32)
            for h in range(DIL_HEADS):
                o_scr[i, h, dst, :] = og[:, h * DIL_HEAD_DIM:(h + 1) * DIL_HEAD_DIM]
            l_scr[i, dst, :] = lg_refs[i][0, g]
    lses = [l_scr[i] for i in range(ng)]
    mx = functools.reduce(jnp.maximum, lses)
    es = [jnp.exp(l - mx) for l in lses]
    tot = functools.reduce(lambda a, b: a + b, es)
    wts = [e / tot for e in es]
    cols = []
    for h in range(DIL_HEADS):
        c = h * (LANES // DIL_HEADS)
        cols.append(functools.reduce(
            lambda a, b: a + b, [wts[i][:, c:c + 1] * o_scr[i, h] for i in range(ng)]))
    o_b = jnp.concatenate(cols, axis=1)
    comb = (ga_ref[0, 0].astype(F32) * oa_ref[0].astype(F32)
            + gb_ref[0, 0].astype(F32) * o_b).astype(BF16)
    x1 = x_ref[0] + mod_ref[0, 2:3, :] * jnp.dot(comb, wo_ref[...], preferred_element_type=F32)
    x1_ref[0] = x1
    y = x1 * lax.rsqrt(jnp.mean(x1 * x1, axis=-1, keepdims=True) + NORM_EPS) * g_ref[...]
    h2 = y * (1.0 + mod_ref[0, 4:5, :]) + mod_ref[0, 3:4, :]
    h2_ref[0] = h2.astype(BF16)
    logits = jnp.dot(h2, wr_ref[...], preferred_element_type=F32,
                     precision=lax.Precision.HIGHEST)
    lane = lax.broadcasted_iota(I32, logits.shape, 1)
    logits = jnp.where(lane < N_EXPERTS, logits, -jnp.inf)
    e = jnp.exp(logits - jnp.max(logits, axis=-1, keepdims=True))
    aff = e / jnp.sum(e, axis=-1, keepdims=True)
    aff_ref[0] = aff
    afft_ref[0] = aff.T[:N_EXPERTS, :]


def _out_projection(o_a, dil_outs, main, gate_col, x, mod, g, w_out, w_router, tm=512):
    B, S, D = x.shape
    const = lambda b, i: (0, 0)
    row = lambda b, i: (b, i, 0)
    wr = jnp.zeros((D, LANES), F32).at[:, :N_EXPERTS].set(w_router)
    dils = tuple(o.shape[1] for o, _ in dil_outs)
    og_specs = [pl.BlockSpec((1, d, tm // d, D), lambda b, i: (b, 0, i, 0)) for d in dils]
    lg_specs = [pl.BlockSpec((1, d, tm // d, LANES), lambda b, i: (b, 0, i, 0)) for d in dils]
    return pl.pallas_call(
        functools.partial(_outproj_kernel, dils=dils),
        grid=(B, S // tm),
        in_specs=[pl.BlockSpec((1, tm, D), row)] + og_specs + lg_specs + [
                  pl.BlockSpec((1, 1, tm, D), lambda b, i: (b, 0, i, gate_col)),
                  pl.BlockSpec((1, 1, tm, D), lambda b, i: (b, 0, i, gate_col + 1)),
                  pl.BlockSpec((1, tm, D), row),
                  pl.BlockSpec((1, 6, D), lambda b, i: (b, 0, 0)),
                  pl.BlockSpec((1, D), const),
                  pl.BlockSpec((D, D), const),
                  pl.BlockSpec((D, LANES), const)],
        out_specs=[pl.BlockSpec((1, tm, D), row),
                   pl.BlockSpec((1, tm, D), row),
                   pl.BlockSpec((1, tm, LANES), row),
                   pl.BlockSpec((1, N_EXPERTS, tm), lambda b, i: (b, 0, i))],
        out_shape=[jax.ShapeDtypeStruct((B, S, D), F32),
                   jax.ShapeDtypeStruct((B, S, D), BF16),
                   jax.ShapeDtypeStruct((B, S, LANES), F32),
                   jax.ShapeDtypeStruct((B, N_EXPERTS, S), F32)],
        scratch_shapes=[pltpu.VMEM((len(dils), DIL_HEADS, tm, DIL_HEAD_DIM), F32),
                        pltpu.VMEM((len(dils), tm, LANES), F32)],
        compiler_params=_cparams(("parallel", "parallel")),
        name="out_projection",
    )(o_a, *[o for o, _ in dil_outs], *[l for _, l in dil_outs], main, main, x, mod, g, w_out, wr)


def _select_kernel(afft_ref, thr_ref, tie_ref, slot_ref, boff_ref, *, cap, chunk):
    aff = afft_ref[0]
    E, S = aff.shape
    bits = lax.bitcast_convert_type(aff, I32)

    def count(mask):
        return jnp.sum(mask.astype(I32), axis=1, keepdims=True)

    def thr_step(i, v):
        cand = v | jnp.left_shift(jnp.int32(1), 30 - i)
        return jnp.where(count(bits >= cand) >= cap, cand, v)

    thr = lax.fori_loop(0, 31, thr_step, jnp.zeros((E, 1), I32))
    need = cap - count(bits > thr)
    eq = bits == thr
    idx = lax.broadcasted_iota(I32, (E, S), 1)
    nbits = max(1, (S - 1).bit_length())

    def tie_step(i, j):
        cand = j | jnp.left_shift(jnp.int32(1), nbits - 1 - i)
        return jnp.where(count(eq & (idx < cand)) < need, cand, j)

    tie = lax.fori_loop(0, nbits, tie_step, jnp.zeros((E, 1), I32))
    sel = (bits > thr) | (eq & (idx <= tie))
    thr_ref[0] = jnp.broadcast_to(thr, (E, LANES))
    tie_ref[0] = jnp.broadcast_to(tie, (E, LANES))

    upper = (lax.broadcasted_iota(I32, (chunk, chunk), 0)
             <= lax.broadcasted_iota(I32, (chunk, chunk), 1)).astype(BF16)
    carry = jnp.zeros((E, 1), F32)
    self_ = sel.astype(F32)
    lane = lax.broadcasted_iota(I32, (E, LANES), 1)
    boff = jnp.zeros((E, LANES), I32)
    for c in range(S // chunk):
        boff = jnp.where(lane == c, carry.astype(I32), boff)
        scf = self_[:, c * chunk:(c + 1) * chunk]
        incl = jnp.dot(scf.astype(BF16), upper, preferred_element_type=F32)
        pos = (incl - scf + carry).astype(I32)
        slot_ref[0, :, c * chunk:(c + 1) * chunk] = jnp.where(scf > 0.0, pos, -1)
        carry = carry + incl[:, chunk - 1:chunk]
    boff_ref[0] = jnp.where(lane == S // chunk, carry.astype(I32), boff)


def _select(aff_t, cap, chunk):
    B, E, S = aff_t.shape
    assert S // chunk < LANES
    return pl.pallas_call(
        functools.partial(_select_kernel, cap=cap, chunk=chunk),
        grid=(B,),
        in_specs=[pl.BlockSpec((1, E, S), lambda b: (b, 0, 0))],
        out_specs=[pl.BlockSpec((1, E, LANES), lambda b: (b, 0, 0)),
                   pl.BlockSpec((1, E, LANES), lambda b: (b, 0, 0)),
                   pl.BlockSpec((1, E, S), lambda b: (b, 0, 0)),
                   pl.BlockSpec((1, E, LANES), lambda b: (b, 0, 0))],
        out_shape=[jax.ShapeDtypeStruct((B, E, LANES), I32),
                   jax.ShapeDtypeStruct((B, E, LANES), I32),
                   jax.ShapeDtypeStruct((B, E, S), I32),
                   jax.ShapeDtypeStruct((B, E, LANES), I32)],
        compiler_params=_cparams(("parallel",)),
        name="ec_select",
    )(aff_t)


def _gather_kernel(boff_ref, slot_ref, afft_ref, h_ref, xe_ref, gate_ref, acc_ref, gacc_ref, *, sb):
    _, _, nblk, tbk = slot_ref.shape
    cap = xe_ref.shape[2]
    base = (pl.program_id(0) * pl.num_programs(1) + pl.program_id(1)) * (nblk + 1)
    for s0 in range(0, cap, sb):
        def count(i, c):
            o = boff_ref[base + i]
            return (c[0] + (o <= s0).astype(I32), c[1] + (o < s0 + sb).astype(I32))

        n_le, n_lt = lax.fori_loop(0, nblk, count, (jnp.int32(0), jnp.int32(0)))
        acc_ref[...] = jnp.zeros(acc_ref.shape, F32)
        gacc_ref[...] = jnp.zeros(gacc_ref.shape, F32)

        def body(tb, carry):
            hit = (lax.broadcasted_iota(I32, (sb, tbk), 0) + s0
                   == slot_ref[0, 0, pl.ds(tb, 1), :])
            off = pl.multiple_of(tb * tbk, tbk)
            acc_ref[...] += jnp.dot(hit.astype(BF16), h_ref[0, pl.ds(off, tbk), :],
                                    preferred_element_type=F32)
            gacc_ref[...] += jnp.sum(jnp.where(hit, afft_ref[0, 0, pl.ds(tb, 1), :], 0.0),
                                     axis=1, keepdims=True)
            return carry

        lax.fori_loop(n_le - 1, n_lt, body, 0)
        xe_ref[0, 0, s0:s0 + sb, :] = acc_ref[...].astype(BF16)
        gate_ref[0, 0, s0:s0 + sb, :] = gacc_ref[...]


def _gather(boff_flat, slot, aff_t, h2, cap, chunk, sb=256):
    B, E, S = slot.shape
    D = h2.shape[-1]
    rowblk = pl.BlockSpec((1, 1, S // chunk, chunk), lambda b, e, off: (b, e, 0, 0))
    return pl.pallas_call(
        functools.partial(_gather_kernel, sb=sb),
        grid_spec=pltpu.PrefetchScalarGridSpec(
            num_scalar_prefetch=1,
            grid=(B, E),
            in_specs=[rowblk, rowblk,
                      pl.BlockSpec((1, S, D), lambda b, e, off: (b, 0, 0))],
            out_specs=[pl.BlockSpec((1, 1, cap, D), lambda b, e, off: (b, e, 0, 0)),
                       pl.BlockSpec((1, 1, cap, 1), lambda b, e, off: (b, e, 0, 0))],
            scratch_shapes=[pltpu.VMEM((sb, D), F32), pltpu.VMEM((sb, 1), F32)]),
        out_shape=[jax.ShapeDtypeStruct((B, E, cap, D), BF16),
                   jax.ShapeDtypeStruct((B, E, cap, 1), F32)],
        compiler_params=_cparams(("parallel", "arbitrary")),
        name="ec_gather",
    )(boff_flat, slot.reshape(B, E, S // chunk, chunk), aff_t.reshape(B, E, S // chunk, chunk), h2)


def _ffn_kernel(xe_ref, gate_ref, wg_ref, wu_ref, wd_ref, ye_ref, acc_ref):
    f = pl.program_id(2)
    nb, _, cap, D = xe_ref.shape
    x = xe_ref[...].reshape(nb * cap, D)
    g = jnp.dot(x, wg_ref[0].astype(BF16), preferred_element_type=F32)
    u = jnp.dot(x, wu_ref[0].astype(BF16), preferred_element_type=F32)
    hid = (g * jax.nn.sigmoid(g) * u).astype(BF16)
    part = jnp.dot(hid, wd_ref[0].astype(BF16), preferred_element_type=F32)

    @pl.when(f == 0)
    def _():
        acc_ref[...] = part

    @pl.when(f > 0)
    def _():
        acc_ref[...] += part

    @pl.when(f == pl.num_programs(2) - 1)
    def _():
        gate = gate_ref[...].reshape(nb * cap, 1)
        ye_ref[...] = (acc_ref[...] * gate).astype(BF16).reshape(ye_ref.shape)


def _expert_ffn(xe, gates, w_gate, w_up, w_down, nb=2, tf=512):
    B, E, cap, D = xe.shape
    F = w_gate.shape[-1]
    return pl.pallas_call(
        _ffn_kernel,
        grid=(E, B // nb, F // tf),
        in_specs=[pl.BlockSpec((nb, 1, cap, D), lambda e, b, f: (b, e, 0, 0)),
                  pl.BlockSpec((nb, 1, cap, 1), lambda e, b, f: (b, e, 0, 0)),
                  pl.BlockSpec((1, D, tf), lambda e, b, f: (e, 0, f)),
                  pl.BlockSpec((1, D, tf), lambda e, b, f: (e, 0, f)),
                  pl.BlockSpec((1, tf, D), lambda e, b, f: (e, f, 0))],
        out_specs=pl.BlockSpec((nb, 1, cap, D), lambda e, b, f: (b, e, 0, 0)),
        out_shape=jax.ShapeDtypeStruct((B, E, cap, D), BF16),
        scratch_shapes=[pltpu.VMEM((nb * cap, D), F32)],
        compiler_params=_cparams(("parallel", "parallel", "arbitrary")),
        name="ec_ffn",
    )(xe, gates, w_gate, w_up, w_down)


def _combine_kernel(boff_ref, aff_ref, thr_ref, tie_ref, ye_ref, x1_ref, mod_ref, g_ref, out_ref,
                    slot_ref, carry_ref, *, tb, wslot, nblk):
    b = pl.program_id(0)
    t2 = pl.program_id(1)
    e = pl.program_id(2)
    nsub = aff_ref.shape[1] // tb
    lane = lax.broadcasted_iota(I32, (tb, LANES), 1)

    @pl.when(e == 0)
    def _():
        @pl.when(t2 == 0)
        def _():
            carry_ref[...] = jnp.zeros(carry_ref.shape, F32)

        lower = (lax.broadcasted_iota(I32, (tb, tb), 0)
                 >= lax.broadcasted_iota(I32, (tb, tb), 1)).astype(BF16)
        for u in range(nsub):
            rows = slice(u * tb, (u + 1) * tb)
            bits = lax.bitcast_convert_type(aff_ref[0, rows, :], I32)
            tok = (t2 * nsub + u) * tb + lax.broadcasted_iota(I32, (tb, LANES), 0)
            thr = thr_ref[0]
            sel = ((bits > thr) | ((bits == thr) & (tok <= tie_ref[0]))) & (lane < N_EXPERTS)
            self_ = sel.astype(F32)
            incl = jnp.dot(lower, self_.astype(BF16), preferred_element_type=F32)
            pos = (incl - self_ + carry_ref[...]).astype(I32)
            slot_ref[rows, :] = jnp.where(sel, pos, -1)
            carry_ref[...] = carry_ref[...] + incl[tb - 1:tb, :]
        out_ref[...] = jnp.zeros(out_ref.shape, F32)

    pick = lane == e
    base = (b * pl.num_programs(2) + e) * (nblk + 1) + t2 * nsub
    for u in range(nsub):
        rows = slice(u * tb, (u + 1) * tb)
        lo = boff_ref[base + u]
        hi = boff_ref[base + u + 1]
        slot_e = jnp.sum(jnp.where(pick, slot_ref[rows, :], 0), axis=1, keepdims=True)

        def body(w, carry):
            woff = pl.multiple_of(w * wslot, wslot)
            onehot = (lax.broadcasted_iota(I32, (tb, wslot), 1) + woff == slot_e).astype(BF16)
            out_ref[0, rows, :] += jnp.dot(onehot, ye_ref[0, 0, pl.ds(woff, wslot), :],
                                           preferred_element_type=F32)
            return carry

        lax.fori_loop(lo // wslot, (hi + wslot - 1) // wslot, body, 0)

    @pl.when(e == pl.num_programs(2) - 1)
    def _():
        for u in range(nsub):
            rows = slice(u * tb, (u + 1) * tb)
            x2 = x1_ref[0, rows, :] + mod_ref[0, 5:6, :] * out_ref[0, rows, :]
            out_ref[0, rows, :] = (x2 * lax.rsqrt(jnp.mean(x2 * x2, axis=-1, keepdims=True)
                                                  + NORM_EPS) * g_ref[...])


def _combine(boff_flat, aff, thr, tie, ye, x1, mod, g_final, tb, tsup=2048, wslot=256):
    B, S, D = x1.shape
    E, cap = ye.shape[1], ye.shape[2]
    tsup = min(tsup, S)
    return pl.pallas_call(
        functools.partial(_combine_kernel, tb=tb, wslot=wslot, nblk=S // tb),
        grid_spec=pltpu.PrefetchScalarGridSpec(
            num_scalar_prefetch=1,
            grid=(B, S // tsup, E),
            in_specs=[pl.BlockSpec((1, tsup, LANES), lambda b, t, e, off: (b, t, 0)),
                      pl.BlockSpec((1, 1, LANES), lambda b, t, e, off: (b, 0, 0)),
                      pl.BlockSpec((1, 1, LANES), lambda b, t, e, off: (b, 0, 0)),
                      pl.BlockSpec((1, 1, cap, D), lambda b, t, e, off: (b, e, 0, 0)),
                      pl.BlockSpec((1, tsup, D), lambda b, t, e, off: (b, t, 0)),
                      pl.BlockSpec((1, 6, D), lambda b, t, e, off: (b, 0, 0)),
                      pl.BlockSpec((1, D), lambda b, t, e, off: (0, 0))],
            out_specs=pl.BlockSpec((1, tsup, D), lambda b, t, e, off: (b, t, 0)),
            scratch_shapes=[pltpu.VMEM((tsup, LANES), I32), pltpu.VMEM((1, LANES), F32)]),
        out_shape=jax.ShapeDtypeStruct((B, S, D), F32),
        compiler_params=_cparams(("parallel", "arbitrary", "arbitrary")),
        name="ec_combine",
    )(boff_flat, aff, thr, tie, ye, x1, mod, g_final)


def _prep_weights(w_in, w_uq, w_ukv):
    D = w_in.shape[0]
    H = MLA_HEADS
    n_lat = MLA_Q_RANK + MLA_KV_RANK + MLA_ROPE
    n_grp = 3 * DIL_HEADS * DIL_HEAD_DIM
    n_dil = len(DIL_GROUPS) * n_grp
    w_lat = jnp.zeros((D, LAT_PAD), F32).at[:, :n_lat].set(w_in[:, :n_lat]).astype(BF16)
    w_grp = [w_in[:, n_lat + i * n_grp:n_lat + (i + 1) * n_grp].astype(BF16)
             for i in range(len(DIL_GROUPS))]
    w_gates = w_in[:, n_lat + n_dil:].astype(BF16)
    half = MLA_ROPE // 2
    uq = w_uq.reshape(MLA_Q_RANK, H, MLA_NOPE + MLA_ROPE)
    pe = uq[:, :, MLA_NOPE:]
    zq = jnp.zeros((MLA_Q_RANK, H, QK_PAD - MLA_NOPE - MLA_ROPE), F32)
    wq = jnp.concatenate([uq[:, :, :MLA_NOPE], pe, zq], axis=2)
    wqs = jnp.concatenate([jnp.zeros_like(uq[:, :, :MLA_NOPE]), pe[:, :, half:], pe[:, :, :half], zq],
                          axis=2)
    ukv = w_ukv.reshape(MLA_KV_RANK, H, MLA_NOPE + MLA_V)
    wk = jnp.concatenate([ukv[:, :, :MLA_NOPE],
                          jnp.zeros((MLA_KV_RANK, H, QK_PAD - MLA_NOPE), F32)], axis=2)
    wv = jnp.concatenate([ukv[:, :, MLA_NOPE:],
                          jnp.zeros((MLA_KV_RANK, H, QK_PAD - MLA_V), F32)], axis=2)
    return (w_lat, w_grp, w_gates, wq.reshape(MLA_Q_RANK, H * QK_PAD).astype(BF16),
            wqs.reshape(MLA_Q_RANK, H * QK_PAD).astype(BF16),
            wk.reshape(MLA_KV_RANK, H * QK_PAD).astype(BF16),
            wv.reshape(MLA_KV_RANK, H * QK_PAD).astype(BF16))


def kernel(x, c, positions, w_ada, b_ada, g_norm_mix, w_in, g_q_lat, g_kv_lat, w_uq, w_ukv,
           rel_bias, w_out, g_norm_ffn, w_router, w_gate, w_up, w_down, g_final):
    B, S, D = x.shape
    assert w_ada.shape[0] == 1, "the final norm is fused into the (single) layer's last kernel"
    assert DIL_GROUPS[0][1] == 1, "the gates ride along with the undilated group's projection"
    for l in range(w_ada.shape[0]):
        mod = _modulation(c, w_ada[l], b_ada[l])
        w_lat, w_grp, w_gates, wq, wqs, wk, wv = _prep_weights(w_in[l], w_uq[l], w_ukv[l])
        g_mix = g_norm_mix[l].reshape(1, D)
        n_qkv = w_grp[0].shape[1] // D
        lat, main = _in_projection(x, mod, g_mix, jnp.concatenate([w_grp[0], w_gates], axis=1),
                                   d=1, n_plain=n_qkv, w_lat=w_lat)
        qkvs = [main] + [_in_projection(x, mod, g_mix, w_grp[gi], d=d, n_plain=n_qkv)[0]
                         for gi, (_, d) in enumerate(DIL_GROUPS) if gi > 0]
        q, k, v = _mla_prep(lat, positions, g_q_lat[l].reshape(1, -1), g_kv_lat[l].reshape(1, -1),
                            wq, wqs, wk, wv)
        o_a = _flash_attention(q, k, v)
        dil_outs = [_dilated_group(qkvs[gi], positions, rel_bias, gi, window, d)
                    for gi, (window, d) in enumerate(DIL_GROUPS)]
        x1, h2, aff, aff_t = _out_projection(o_a, dil_outs, main, n_qkv, x, mod,
                                             g_norm_ffn[l].reshape(1, D),
                                             w_out[l].astype(BF16), w_router[l])
        cap = EC_CAPACITY_FACTOR * S // N_EXPERTS
        chunk = 512
        thr, tie, slot, boff = _select(aff_t, cap, chunk)
        boff_flat = boff[:, :, :S // chunk + 1].reshape(-1)
        xe, gates = _gather(boff_flat, slot, aff_t, h2, cap, chunk)
        ye = _expert_ffn(xe, gates, w_gate[l], w_up[l], w_down[l])
        pad = jnp.zeros((B, 1, LANES - N_EXPERTS), I32)
        thr_l = jnp.concatenate([thr[:, :, 0].reshape(B, 1, N_EXPERTS), pad], axis=2)
        tie_l = jnp.concatenate([tie[:, :, 0].reshape(B, 1, N_EXPERTS), pad], axis=2)
        x = _combine(boff_flat, aff, thr_l, tie_l, ye, x1, mod, g_final.reshape(1, D), tb=chunk)
    return x
```

```python
import functools
import math

import jax
import jax.numpy as jnp
from jax import lax
from jax.experimental import pallas as pl
from jax.experimental.pallas import tpu as pltpu

F32 = jnp.float32
BF16 = jnp.bfloat16
I32 = jnp.int32

MLA_HEADS = 8
MLA_Q_RANK = 384
MLA_KV_RANK = 256
MLA_NOPE = 128
MLA_ROPE = 64
MLA_V = 128
ROPE_THETA = 10000.0
DIL_GROUPS = ((128, 1), (512, 4), (2048, 16))
DIL_HEADS = 8
DIL_HEAD_DIM = 128
REL_BUCKETS = 32
REL_MAX_DIST = 1024
N_EXPERTS = 16
EC_CAPACITY_FACTOR = 2
NORM_EPS = 1e-6
NEG_INF = -1e30

LANES = 128
QK_PAD = 256
LAT_PAD = 768
VMEM_LIMIT = 56 * 1024 * 1024


def _cparams(sem):
    return pltpu.CompilerParams(dimension_semantics=sem, vmem_limit_bytes=VMEM_LIMIT)


def _nt_dot(a, b):
    return lax.dot_general(a, b, (((1,), (1,)), ((), ())), preferred_element_type=F32)


def _mod_kernel(c_ref, w_ref, b_ref, o_ref):
    c = c_ref[...]
    cond = c * jax.nn.sigmoid(c)
    o_ref[...] = jnp.dot(cond, w_ref[...], preferred_element_type=F32,
                         precision=lax.Precision.HIGHEST) + b_ref[...]


def _modulation(c, w_ada, b_ada):
    B, D = c.shape
    rows = 8
    c8 = jnp.zeros((rows, D), F32).at[:B].set(c)
    n6 = w_ada.shape[1]
    out = pl.pallas_call(
        _mod_kernel,
        grid=(n6 // D,),
        in_specs=[pl.BlockSpec((rows, D), lambda j: (0, 0)),
                  pl.BlockSpec((D, D), lambda j: (0, j)),
                  pl.BlockSpec((1, D), lambda j: (0, j))],
        out_specs=pl.BlockSpec((rows, D), lambda j: (0, j)),
        out_shape=jax.ShapeDtypeStruct((rows, n6), F32),
        compiler_params=_cparams(("arbitrary",)),
        name="modulation",
    )(c8, w_ada, b_ada.reshape(1, n6))
    return out[:B].reshape(B, 6, D)


def _norm_kernel(*refs, dils):
    nx = len(refs) - 2 - len(dils)
    x_refs = refs[:nx]
    mod_ref, g_ref = refs[nx:nx + 2]
    h_refs = refs[nx + 2:]
    tm = x_refs[0].shape[1]
    for h_ref, d in zip(h_refs, dils):
        rows = tm // d
        for g in range(d):
            src = pl.ds(g, rows, stride=d) if d > 1 else slice(None)
            x = jnp.concatenate([xr[0, src, :] for xr in x_refs], axis=1)
            y = x * lax.rsqrt(jnp.mean(x * x, axis=-1, keepdims=True) + NORM_EPS) * g_ref[...]
            h = y * (1.0 + mod_ref[0, 1:2, :]) + mod_ref[0, 0:1, :]
            h_ref[0, g] = h.astype(BF16)


def _norm_modulate(x, mod, g, dils, tm=1024):
    B, S, D = x.shape
    in_specs = [pl.BlockSpec((1, tm, LANES), functools.partial(lambda b, i, j: (b, i, j), j=j))
                for j in range(D // LANES)]
    in_specs += [pl.BlockSpec((1, 6, D), lambda b, i: (b, 0, 0)),
                 pl.BlockSpec((1, D), lambda b, i: (0, 0))]
    return pl.pallas_call(
        functools.partial(_norm_kernel, dils=dils),
        grid=(B, S // tm),
        in_specs=in_specs,
        out_specs=[pl.BlockSpec((1, d, tm // d, D), lambda b, i: (b, 0, i, 0)) for d in dils],
        out_shape=[jax.ShapeDtypeStruct((B, d, S // d, D), BF16) for d in dils],
        compiler_params=_cparams(("parallel", "parallel")),
        name="norm_modulate",
    )(*([x] * (D // LANES) + [mod, g]))


def _proj_kernel(h_ref, w_ref, o_ref, *, n_plain, n_tiles):
    acc = jnp.dot(h_ref[...], w_ref[...], preferred_element_type=F32)
    if n_plain == n_tiles:
        o_ref[...] = acc.astype(o_ref.dtype)
    else:
        n = pl.program_id(1)

        @pl.when(n < n_plain)
        def _():
            o_ref[...] = acc.astype(o_ref.dtype)

        @pl.when(n >= n_plain)
        def _():
            o_ref[...] = jax.nn.sigmoid(acc).astype(o_ref.dtype)


def _projection(h, w, name, n_plain=None, out_dtype=BF16, tm=2048, tn=1024):
    B, d, L, D = h.shape
    N = w.shape[1]
    tn = min(tn, N)
    tm = min(tm, B * d * L)
    n_tiles = N // tn
    n_plain = n_tiles if n_plain is None else n_plain
    out = pl.pallas_call(
        functools.partial(_proj_kernel, n_plain=n_plain, n_tiles=n_tiles),
        grid=(B * d * L // tm, n_tiles),
        in_specs=[pl.BlockSpec((tm, D), lambda i, n: (i, 0)),
                  pl.BlockSpec((D, tn), lambda i, n: (0, n))],
        out_specs=pl.BlockSpec((tm, tn), lambda i, n: (i, n)),
        out_shape=jax.ShapeDtypeStruct((B * d * L, N), out_dtype),
        compiler_params=_cparams(("parallel", "arbitrary")),
        name=name,
    )(h.reshape(B * d * L, D), w)
    return out.reshape(B, d, L, N)


def _mla_prep_kernel(lat_ref, pos_ref, gq_ref, gkv_ref, wq_ref, wqs_ref, wk_ref, wv_ref,
                     freq_ref, sgn_ref, sel_ref, sels_ref, one_ref, q_ref, k_ref, v_ref, *, scale):
    lat = lat_ref[0]
    cq = lat[:, :MLA_Q_RANK]
    ckv = lat[:, MLA_Q_RANK:MLA_Q_RANK + MLA_KV_RANK]
    kpe = lat[:, MLA_Q_RANK + MLA_KV_RANK:]
    cqn = (cq * lax.rsqrt(jnp.mean(cq * cq, axis=-1, keepdims=True) + NORM_EPS)
           * gq_ref[...]).astype(BF16)
    ckvn = (ckv * lax.rsqrt(jnp.mean(ckv * ckv, axis=-1, keepdims=True) + NORM_EPS)
            * gkv_ref[...]).astype(BF16)
    ang = pos_ref[0].astype(F32) * freq_ref[...]
    cos = jnp.cos(ang)
    sin = jnp.sin(ang) * sgn_ref[...]
    qa = jnp.dot(cqn, wq_ref[...], preferred_element_type=F32)
    qs = jnp.dot(cqn, wqs_ref[...], preferred_element_type=F32)
    kn = jnp.dot(ckvn, wk_ref[...], preferred_element_type=F32)
    v_ref[0] = (jnp.dot(ckvn, wv_ref[...], preferred_element_type=F32) + one_ref[...]).astype(BF16)
    hp = lax.Precision.HIGHEST
    rk = (jnp.dot(kpe, sel_ref[...], preferred_element_type=F32, precision=hp) * cos
          + jnp.dot(kpe, sels_ref[...], preferred_element_type=F32, precision=hp) * sin)
    cos_q = cos * scale
    sin_q = sin * scale
    for h in range(MLA_HEADS):
        sl = slice(h * QK_PAD, (h + 1) * QK_PAD)
        q_ref[0, :, sl] = (qa[:, sl] * cos_q + qs[:, sl] * sin_q).astype(BF16)
        k_ref[0, :, sl] = (kn[:, sl] + rk).astype(BF16)


def _mla_prep(lat, positions, g_q, g_kv, wq, wqs, wk, wv, tm=512):
    B, S, _ = lat.shape
    H = MLA_HEADS
    half = MLA_ROPE // 2
    inv_freq = ROPE_THETA ** (-jnp.arange(0, MLA_ROPE, 2, dtype=F32) / MLA_ROPE)
    freq = jnp.zeros((1, QK_PAD), F32).at[0, MLA_NOPE:MLA_NOPE + MLA_ROPE].set(
        jnp.concatenate([inv_freq, inv_freq]))
    sgn = jnp.zeros((1, QK_PAD), F32).at[0, MLA_NOPE:MLA_NOPE + half].set(-1.0)
    sgn = sgn.at[0, MLA_NOPE + half:MLA_NOPE + MLA_ROPE].set(1.0)
    r = jnp.arange(MLA_ROPE)
    sel = jnp.zeros((LANES, QK_PAD), F32).at[r, MLA_NOPE + r].set(1.0)
    sels = jnp.zeros((LANES, QK_PAD), F32).at[r, MLA_NOPE + (r + half) % MLA_ROPE].set(1.0)
    scale = (MLA_NOPE + MLA_ROPE) ** -0.5 * math.log2(math.e)
    ones_col = jnp.zeros((H, QK_PAD), F32).at[:, MLA_V].set(1.0).reshape(1, H * QK_PAD)
    const = lambda b, i: (0, 0)
    return pl.pallas_call(
        functools.partial(_mla_prep_kernel, scale=scale),
        grid=(B, S // tm),
        in_specs=[pl.BlockSpec((1, tm, LAT_PAD), lambda b, i: (b, i, 0)),
                  pl.BlockSpec((1, tm, 1), lambda b, i: (b, i, 0)),
                  pl.BlockSpec((1, MLA_Q_RANK), const),
                  pl.BlockSpec((1, MLA_KV_RANK), const),
                  pl.BlockSpec((MLA_Q_RANK, H * QK_PAD), const),
                  pl.BlockSpec((MLA_Q_RANK, H * QK_PAD), const),
                  pl.BlockSpec((MLA_KV_RANK, H * QK_PAD), const),
                  pl.BlockSpec((MLA_KV_RANK, H * QK_PAD), const),
                  pl.BlockSpec((1, QK_PAD), const),
                  pl.BlockSpec((1, QK_PAD), const),
                  pl.BlockSpec((LANES, QK_PAD), const),
                  pl.BlockSpec((LANES, QK_PAD), const),
                  pl.BlockSpec((1, H * QK_PAD), const)],
        out_specs=[pl.BlockSpec((1, tm, H * QK_PAD), lambda b, i: (b, i, 0)),
                   pl.BlockSpec((1, tm, H * QK_PAD), lambda b, i: (b, i, 0)),
                   pl.BlockSpec((1, tm, H * QK_PAD), lambda b, i: (b, i, 0))],
        out_shape=[jax.ShapeDtypeStruct((B, S, H * QK_PAD), BF16),
                   jax.ShapeDtypeStruct((B, S, H * QK_PAD), BF16),
                   jax.ShapeDtypeStruct((B, S, H * QK_PAD), BF16)],
        compiler_params=_cparams(("parallel", "parallel")),
        name="mla_prep",
    )(lat, positions.reshape(B, S, 1), g_q, g_kv, wq, wqs, wk, wv, freq, sgn, sel, sels, ones_col)


def _flash_kernel(q_ref, k_ref, v_ref, o_ref, m_ref, acc_ref, s_ref, *, tq, tkc):
    S = k_ref.shape[1]
    nk = S // tkc
    nq = S // tq
    half = nk // 2
    acc_ref[...] = jnp.zeros(acc_ref.shape, F32)
    m_ref[...] = jnp.full(m_ref.shape, -jnp.inf, F32)

    def scores(qi, kj):
        qoff = pl.multiple_of(qi * tq, tq)
        koff = pl.multiple_of(kj * tkc, tkc)
        return _nt_dot(q_ref[0, pl.ds(qoff, tq), :], k_ref[0, pl.ds(koff, tkc), :])

    def consume(kj, slot, first):
        koff = pl.multiple_of(kj * tkc, tkc)
        s = s_ref[slot]
        m_prev = m_ref[...]
        if first is not None:
            m_prev = jnp.where(first, -jnp.inf, m_prev)
        m_new = jnp.maximum(m_prev, jnp.max(s, axis=-1, keepdims=True))
        alpha = jnp.exp2(m_prev - m_new)
        p = jnp.exp2(s - m_new).astype(BF16)
        acc_ref[...] = alpha * acc_ref[...] + jnp.dot(p, v_ref[0, pl.ds(koff, tkc), :],
                                                      preferred_element_type=F32)
        m_ref[...] = m_new

    s_ref[0] = scores(0, 0)

    def body(t, carry):
        qi = t // half
        kj = 2 * (t % half)
        s_ref[1] = scores(qi, kj + 1)
        consume(kj, 0, kj == 0)
        t1 = jnp.minimum(t + 1, nq * half - 1)
        s_ref[0] = scores(t1 // half, 2 * (t1 % half))
        consume(kj + 1, 1, None)

        @pl.when(kj + 2 == nk)
        def _():
            acc = acc_ref[...]
            qoff = pl.multiple_of(qi * tq, tq)
            o_ref[0, pl.ds(qoff, tq), :] = (acc[:, :MLA_V] / acc[:, MLA_V:MLA_V + 1]).astype(BF16)
        return carry

    lax.fori_loop(0, nq * half, body, 0)


def _flash_attention(q, k, v, tq=1024, tkc=1024):
    B, S, _ = q.shape
    H = MLA_HEADS
    assert (S // tkc) % 2 == 0 and S % tq == 0
    whole = pl.BlockSpec((1, S, QK_PAD), lambda b, h: (b, 0, h))
    return pl.pallas_call(
        functools.partial(_flash_kernel, tq=tq, tkc=tkc),
        grid=(B, H),
        in_specs=[whole, whole, whole],
        out_specs=pl.BlockSpec((1, S, MLA_V), lambda b, h: (b, 0, h)),
        out_shape=jax.ShapeDtypeStruct((B, S, H * MLA_V), BF16),
        scratch_shapes=[pltpu.VMEM((tq, 1), F32), pltpu.VMEM((tq, QK_PAD), F32),
                        pltpu.VMEM((2, tq, tkc), F32)],
        compiler_params=_cparams(("parallel", "parallel")),
        name="mla_flash",
    )(q, k, v)


def _t5_bucket(rel):
    nb = REL_BUCKETS // 2
    max_exact = nb // 2
    ret = jnp.where(rel > 0, nb, 0)
    n = jnp.abs(rel)
    nf = jnp.maximum(n, 1).astype(F32)
    large = max_exact + (jnp.log(nf / max_exact) / math.log(REL_MAX_DIST / max_exact)
                         * (nb - max_exact)).astype(I32)
    large = jnp.minimum(large, nb - 1)
    return ret + jnp.where(n < max_exact, n, large)


def _dilated_kernel(q_ref, kp_ref, kc_ref, kn_ref, vp_ref, vc_ref, vn_ref, pq_ref, pk_ref, tab_ref,
                    o_ref, l_ref, kcat_ref, vcat_ref, s_scr, *, r, tq, ts, n_rows):
    n = pl.program_id(2)
    nsub = tq // ts
    nk = ts + 2 * r
    kcat_ref[0:r, :] = kp_ref[0, 0]
    kcat_ref[r:r + tq, :] = kc_ref[0, 0]
    kcat_ref[r + tq:tq + 2 * r, :] = kn_ref[0, 0]
    ones = jnp.ones((tq + 2 * r, DIL_HEAD_DIM), BF16)
    for h in range(DIL_HEADS):
        sl = slice(h * DIL_HEAD_DIM, (h + 1) * DIL_HEAD_DIM)
        dst = slice(2 * h * DIL_HEAD_DIM, (2 * h + 1) * DIL_HEAD_DIM)
        vcat_ref[0:r, dst] = vp_ref[0, 0, :, sl]
        vcat_ref[r:r + tq, dst] = vc_ref[0, 0, :, sl]
        vcat_ref[r + tq:tq + 2 * r, dst] = vn_ref[0, 0, :, sl]
        vcat_ref[:, (2 * h + 1) * DIL_HEAD_DIM:(2 * h + 2) * DIL_HEAD_DIM] = ones
    lane_grp = lax.broadcasted_iota(I32, (ts, LANES), 1) // (LANES // DIL_HEADS)
    scale = DIL_HEAD_DIM ** -0.5 * math.log2(math.e)
    for j in range(nsub):
        qi = n * tq + j * ts + lax.broadcasted_iota(I32, (ts, nk), 0)
        kj = n * tq + j * ts - r + lax.broadcasted_iota(I32, (ts, nk), 1)
        valid = (jnp.abs(kj - qi) <= r) & (kj >= 0) & (kj < n_rows)
        rel = pk_ref[0, 0, j] - pq_ref[0, 0, j * ts:(j + 1) * ts, :]
        bucket = jnp.where(valid, _t5_bucket(rel), REL_BUCKETS)
        for h in range(DIL_HEADS):
            sl = slice(h * DIL_HEAD_DIM, (h + 1) * DIL_HEAD_DIM)
            s = _nt_dot(q_ref[0, 0, j * ts:(j + 1) * ts, sl], kcat_ref[j * ts:j * ts + nk, sl]) * scale
            tab = jnp.broadcast_to(tab_ref[h:h + 1, :], (ts, LANES))
            bias = jnp.concatenate(
                [jnp.take_along_axis(tab, bucket[:, c * LANES:(c + 1) * LANES], axis=1)
                 for c in range(nk // LANES)], axis=1)
            s_scr[j * DIL_HEADS + h] = s + bias
    for j in range(nsub):
        lse_tile = jnp.zeros((ts, LANES), F32)
        for h in range(DIL_HEADS):
            sl = slice(h * DIL_HEAD_DIM, (h + 1) * DIL_HEAD_DIM)
            s = s_scr[j * DIL_HEADS + h]
            m = jnp.max(s, axis=-1, keepdims=True)
            p = jnp.exp2(s - m).astype(BF16)
            res = jnp.dot(p, vcat_ref[j * ts:j * ts + nk,
                                      2 * h * DIL_HEAD_DIM:(2 * h + 2) * DIL_HEAD_DIM],
                          preferred_element_type=F32)
            den = res[:, DIL_HEAD_DIM:]
            o_ref[0, 0, j * ts:(j + 1) * ts, sl] = (res[:, :DIL_HEAD_DIM] / den).astype(BF16)
            lse_tile = jnp.where(lane_grp == h, m * math.log(2.0) + jnp.log(den), lse_tile)
        l_ref[0, 0, j * ts:(j + 1) * ts, :] = lse_tile


def _dilated_group(qkv, positions, table, gi, window, d, tq=512, ts=128):
    B, _, L, _ = qkv.shape
    HD = DIL_HEADS * DIL_HEAD_DIM
    r = window // (2 * d)
    tq = min(tq, L)
    nt = L // tq
    nsub = tq // ts
    rb = tq // r
    nk = ts + 2 * r
    pcls = positions.reshape(B, L, d).transpose(0, 2, 1)
    pq = pcls.reshape(B, d, L, 1)
    ppad = jnp.pad(pcls, ((0, 0), (0, 0), (r, ts + r)))
    pk = jnp.concatenate([ppad[:, :, :L].reshape(B, d, L // ts, ts),
                          ppad[:, :, ts:ts + L].reshape(B, d, L // ts, ts)[..., :2 * r]], axis=-1)
    pk = pk.reshape(B, d, L // ts, 1, nk)
    tab = jnp.zeros((DIL_HEADS, LANES), F32).at[:, :REL_BUCKETS].set(
        table[:, gi * DIL_HEADS:(gi + 1) * DIL_HEADS].T * math.log2(math.e))
    tab = tab.at[:, REL_BUCKETS].set(NEG_INF)

    def cur(c):
        return pl.BlockSpec((1, 1, tq, HD), lambda b, g, n: (b, g, n, c))

    def prv(c):
        return pl.BlockSpec((1, 1, r, HD), lambda b, g, n: (b, g, jnp.maximum(n * rb - 1, 0), c))

    def nxt(c):
        return pl.BlockSpec((1, 1, r, HD),
                            lambda b, g, n: (b, g, jnp.minimum((n + 1) * rb, L // r - 1), c))

    return pl.pallas_call(
        functools.partial(_dilated_kernel, r=r, tq=tq, ts=ts, n_rows=L),
        grid=(B, d, nt),
        in_specs=[cur(0), prv(1), cur(1), nxt(1), prv(2), cur(2), nxt(2),
                  pl.BlockSpec((1, 1, tq, 1), lambda b, g, n: (b, g, n, 0)),
                  pl.BlockSpec((1, 1, nsub, 1, nk), lambda b, g, n: (b, g, n, 0, 0)),
                  pl.BlockSpec((DIL_HEADS, LANES), lambda b, g, n: (0, 0))],
        out_specs=[pl.BlockSpec((1, 1, tq, HD), lambda b, g, n: (b, g, n, 0)),
                   pl.BlockSpec((1, 1, tq, LANES), lambda b, g, n: (b, g, n, 0))],
        out_shape=[jax.ShapeDtypeStruct((B, d, L, HD), BF16),
                   jax.ShapeDtypeStruct((B, d, L, LANES), F32)],
        scratch_shapes=[pltpu.VMEM((tq + 2 * r, HD), BF16), pltpu.VMEM((tq + 2 * r, 2 * HD), BF16),
                        pltpu.VMEM((nsub * DIL_HEADS, ts, nk), F32)],
        compiler_params=_cparams(("parallel", "parallel", "parallel")),
        name=f"dilated_g{gi}",
    )(*([qkv] * 7 + [pq, pk, tab]))


def _outproj_kernel(*refs, dils):
    ng = len(dils)
    oa_ref = refs[0]
    og_refs = refs[1:1 + ng]
    lg_refs = refs[1 + ng:1 + 2 * ng]
    (ga_ref, gb_ref, x_ref, mod_ref, g_ref, wo_ref, wr_ref,
     x1_ref, h2_ref, aff_ref, afft_ref, o_scr, l_scr) = refs[1 + 2 * ng:]
    tm = x_ref.shape[1]
    for i, d in enumerate(dils):
        rows = tm // d
        for g in range(d):
            dst = pl.ds(g, rows, stride=d) if d > 1 else slice(None)
            og = og_refs[i][0, g].astype(F32)
            for h in range(DIL_HEADS):
                o_scr[i, h, dst, :] = og[:, h * DIL_HEAD_DIM:(h + 1) * DIL_HEAD_DIM]
            l_scr[i, dst, :] = lg_refs[i][0, g]
    lses = [l_scr[i] for i in range(ng)]
    mx = functools.reduce(jnp.maximum, lses)
    es = [jnp.exp(l - mx) for l in lses]
    tot = functools.reduce(lambda a, b: a + b, es)
    wts = [e / tot for e in es]
    cols = []
    for h in range(DIL_HEADS):
        c = h * (LANES // DIL_HEADS)
        cols.append(functools.reduce(
            lambda a, b: a + b, [wts[i][:, c:c + 1] * o_scr[i, h] for i in range(ng)]))
    o_b = jnp.concatenate(cols, axis=1)
    comb = (ga_ref[0, 0].astype(F32) * oa_ref[0].astype(F32)
            + gb_ref[0, 0].astype(F32) * o_b).astype(BF16)
    x1 = x_ref[0] + mod_ref[0, 2:3, :] * jnp.dot(comb, wo_ref[...], preferred_element_type=F32)
    x1_ref[0] = x1
    y = x1 * lax.rsqrt(jnp.mean(x1 * x1, axis=-1, keepdims=True) + NORM_EPS) * g_ref[...]
    h2 = y * (1.0 + mod_ref[0, 4:5, :]) + mod_ref[0, 3:4, :]
    h2_ref[0] = h2.astype(BF16)
    logits = jnp.dot(h2, wr_ref[...], preferred_element_type=F32,
                     precision=lax.Precision.HIGHEST)
    lane = lax.broadcasted_iota(I32, logits.shape, 1)
    logits = jnp.where(lane < N_EXPERTS, logits, -jnp.inf)
    e = jnp.exp(logits - jnp.max(logits, axis=-1, keepdims=True))
    aff = e / jnp.sum(e, axis=-1, keepdims=True)
    aff_ref[0] = aff
    afft_ref[0] = aff.T[:N_EXPERTS, :]


def _out_projection(o_a, dil_outs, main, gate_col, x, mod, g, w_out, w_router, tm=512):
    B, S, D = x.shape
    const = lambda b, i: (0, 0)
    row = lambda b, i: (b, i, 0)
    wr = jnp.zeros((D, LANES), F32).at[:, :N_EXPERTS].set(w_router)
    dils = tuple(o.shape[1] for o, _ in dil_outs)
    og_specs = [pl.BlockSpec((1, d, tm // d, D), lambda b, i: (b, 0, i, 0)) for d in dils]
    lg_specs = [pl.BlockSpec((1, d, tm // d, LANES), lambda b, i: (b, 0, i, 0)) for d in dils]
    return pl.pallas_call(
        functools.partial(_outproj_kernel, dils=dils),
        grid=(B, S // tm),
        in_specs=[pl.BlockSpec((1, tm, D), row)] + og_specs + lg_specs + [
                  pl.BlockSpec((1, 1, tm, D), lambda b, i: (b, 0, i, gate_col)),
                  pl.BlockSpec((1, 1, tm, D), lambda b, i: (b, 0, i, gate_col + 1)),
                  pl.BlockSpec((1, tm, D), row),
                  pl.BlockSpec((1, 6, D), lambda b, i: (b, 0, 0)),
                  pl.BlockSpec((1, D), const),
                  pl.BlockSpec((D, D), const),
                  pl.BlockSpec((D, LANES), const)],
        out_specs=[pl.BlockSpec((1, tm, D), row),
                   pl.BlockSpec((1, tm, D), row),
                   pl.BlockSpec((1, tm, LANES), row),
                   pl.BlockSpec((1, N_EXPERTS, tm), lambda b, i: (b, 0, i))],
        out_shape=[jax.ShapeDtypeStruct((B, S, D), F32),
                   jax.ShapeDtypeStruct((B, S, D), BF16),
                   jax.ShapeDtypeStruct((B, S, LANES), F32),
                   jax.ShapeDtypeStruct((B, N_EXPERTS, S), F32)],
        scratch_shapes=[pltpu.VMEM((len(dils), DIL_HEADS, tm, DIL_HEAD_DIM), F32),
                        pltpu.VMEM((len(dils), tm, LANES), F32)],
        compiler_params=_cparams(("parallel", "parallel")),
        name="out_projection",
    )(o_a, *[o for o, _ in dil_outs], *[l for _, l in dil_outs], main, main, x, mod, g, w_out, wr)


def _select_kernel(afft_ref, thr_ref, tie_ref, slot_ref, boff_ref, *, cap, chunk):
    aff = afft_ref[0]
    E, S = aff.shape
    bits = lax.bitcast_convert_type(aff, I32)

    def count(mask):
        return jnp.sum(mask.astype(I32), axis=1, keepdims=True)

    def thr_step(i, v):
        cand = v | jnp.left_shift(jnp.int32(1), 30 - i)
        return jnp.where(count(bits >= cand) >= cap, cand, v)

    thr = lax.fori_loop(0, 31, thr_step, jnp.zeros((E, 1), I32))
    need = cap - count(bits > thr)
    eq = bits == thr
    idx = lax.broadcasted_iota(I32, (E, S), 1)
    nbits = max(1, (S - 1).bit_length())

    def tie_step(i, j):
        cand = j | jnp.left_shift(jnp.int32(1), nbits - 1 - i)
        return jnp.where(count(eq & (idx < cand)) < need, cand, j)

    tie = lax.fori_loop(0, nbits, tie_step, jnp.zeros((E, 1), I32))
    sel = (bits > thr) | (eq & (idx <= tie))
    thr_ref[0] = jnp.broadcast_to(thr, (E, LANES))
    tie_ref[0] = jnp.broadcast_to(tie, (E, LANES))

    upper = (lax.broadcasted_iota(I32, (chunk, chunk), 0)
             <= lax.broadcasted_iota(I32, (chunk, chunk), 1)).astype(BF16)
    carry = jnp.zeros((E, 1), F32)
    self_ = sel.astype(F32)
    lane = lax.broadcasted_iota(I32, (E, LANES), 1)
    boff = jnp.zeros((E, LANES), I32)
    for c in range(S // chunk):
        boff = jnp.where(lane == c, carry.astype(I32), boff)
        scf = self_[:, c * chunk:(c + 1) * chunk]
        incl = jnp.dot(scf.astype(BF16), upper, preferred_element_type=F32)
        pos = (incl - scf + carry).astype(I32)
        slot_ref[0, :, c * chunk:(c + 1) * chunk] = jnp.where(scf > 0.0, pos, -1)
        carry = carry + incl[:, chunk - 1:chunk]
    boff_ref[0] = jnp.where(lane == S // chunk, carry.astype(I32), boff)


def _select(aff_t, cap, chunk):
    B, E, S = aff_t.shape
    assert S // chunk < LANES
    return pl.pallas_call(
        functools.partial(_select_kernel, cap=cap, chunk=chunk),
        grid=(B,),
        in_specs=[pl.BlockSpec((1, E, S), lambda b: (b, 0, 0))],
        out_specs=[pl.BlockSpec((1, E, LANES), lambda b: (b, 0, 0)),
                   pl.BlockSpec((1, E, LANES), lambda b: (b, 0, 0)),
                   pl.BlockSpec((1, E, S), lambda b: (b, 0, 0)),
                   pl.BlockSpec((1, E, LANES), lambda b: (b, 0, 0))],
        out_shape=[jax.ShapeDtypeStruct((B, E, LANES), I32),
                   jax.ShapeDtypeStruct((B, E, LANES), I32),
                   jax.ShapeDtypeStruct((B, E, S), I32),
                   jax.ShapeDtypeStruct((B, E, LANES), I32)],
        compiler_params=_cparams(("parallel",)),
        name="ec_select",
    )(aff_t)


def _gather_kernel(boff_ref, slot_ref, afft_ref, h_ref, xe_ref, gate_ref, acc_ref, gacc_ref, *, sb):
    _, _, nblk, tbk = slot_ref.shape
    cap = xe_ref.shape[2]
    base = (pl.program_id(0) * pl.num_programs(1) + pl.program_id(1)) * (nblk + 1)
    for s0 in range(0, cap, sb):
        def count(i, c):
            o = boff_ref[base + i]
            return (c[0] + (o <= s0).astype(I32), c[1] + (o < s0 + sb).astype(I32))

        n_le, n_lt = lax.fori_loop(0, nblk, count, (jnp.int32(0), jnp.int32(0)))
        acc_ref[...] = jnp.zeros(acc_ref.shape, F32)
        gacc_ref[...] = jnp.zeros(gacc_ref.shape, F32)

        def body(tb, carry):
            hit = (lax.broadcasted_iota(I32, (sb, tbk), 0) + s0
                   == slot_ref[0, 0, pl.ds(tb, 1), :])
            off = pl.multiple_of(tb * tbk, tbk)
            acc_ref[...] += jnp.dot(hit.astype(BF16), h_ref[0, pl.ds(off, tbk), :],
                                    preferred_element_type=F32)
            gacc_ref[...] += jnp.sum(jnp.where(hit, afft_ref[0, 0, pl.ds(tb, 1), :], 0.0),
                                     axis=1, keepdims=True)
            return carry

        lax.fori_loop(n_le - 1, n_lt, body, 0)
        xe_ref[0, 0, s0:s0 + sb, :] = acc_ref[...].astype(BF16)
        gate_ref[0, 0, s0:s0 + sb, :] = gacc_ref[...]


def _gather(boff_flat, slot, aff_t, h2, cap, chunk, sb=256):
    B, E, S = slot.shape
    D = h2.shape[-1]
    rowblk = pl.BlockSpec((1, 1, S // chunk, chunk), lambda b, e, off: (b, e, 0, 0))
    return pl.pallas_call(
        functools.partial(_gather_kernel, sb=sb),
        grid_spec=pltpu.PrefetchScalarGridSpec(
            num_scalar_prefetch=1,
            grid=(B, E),
            in_specs=[rowblk, rowblk,
                      pl.BlockSpec((1, S, D), lambda b, e, off: (b, 0, 0))],
            out_specs=[pl.BlockSpec((1, 1, cap, D), lambda b, e, off: (b, e, 0, 0)),
                       pl.BlockSpec((1, 1, cap, 1), lambda b, e, off: (b, e, 0, 0))],
            scratch_shapes=[pltpu.VMEM((sb, D), F32), pltpu.VMEM((sb, 1), F32)]),
        out_shape=[jax.ShapeDtypeStruct((B, E, cap, D), BF16),
                   jax.ShapeDtypeStruct((B, E, cap, 1), F32)],
        compiler_params=_cparams(("parallel", "arbitrary")),
        name="ec_gather",
    )(boff_flat, slot.reshape(B, E, S // chunk, chunk), aff_t.reshape(B, E, S // chunk, chunk), h2)


def _ffn_kernel(xe_ref, gate_ref, wg_ref, wu_ref, wd_ref, ye_ref, acc_ref):
    f = pl.program_id(2)
    nb, _, cap, D = xe_ref.shape
    x = xe_ref[...].reshape(nb * cap, D)
    g = jnp.dot(x, wg_ref[0].astype(BF16), preferred_element_type=F32)
    u = jnp.dot(x, wu_ref[0].astype(BF16), preferred_element_type=F32)
    hid = (g * jax.nn.sigmoid(g) * u).astype(BF16)
    part = jnp.dot(hid, wd_ref[0].astype(BF16), preferred_element_type=F32)

    @pl.when(f == 0)
    def _():
        acc_ref[...] = part

    @pl.when(f > 0)
    def _():
        acc_ref[...] += part

    @pl.when(f == pl.num_programs(2) - 1)
    def _():
        gate = gate_ref[...].reshape(nb * cap, 1)
        ye_ref[...] = (acc_ref[...] * gate).astype(BF16).reshape(ye_ref.shape)


def _expert_ffn(xe, gates, w_gate, w_up, w_down, nb=2, tf=512):
    B, E, cap, D = xe.shape
    F = w_gate.shape[-1]
    return pl.pallas_call(
        _ffn_kernel,
        grid=(E, B // nb, F // tf),
        in_specs=[pl.BlockSpec((nb, 1, cap, D), lambda e, b, f: (b, e, 0, 0)),
                  pl.BlockSpec((nb, 1, cap, 1), lambda e, b, f: (b, e, 0, 0)),
                  pl.BlockSpec((1, D, tf), lambda e, b, f: (e, 0, f)),
                  pl.BlockSpec((1, D, tf), lambda e, b, f: (e, 0, f)),
                  pl.BlockSpec((1, tf, D), lambda e, b, f: (e, f, 0))],
        out_specs=pl.BlockSpec((nb, 1, cap, D), lambda e, b, f: (b, e, 0, 0)),
        out_shape=jax.ShapeDtypeStruct((B, E, cap, D), BF16),
        scratch_shapes=[pltpu.VMEM((nb * cap, D), F32)],
        compiler_params=_cparams(("parallel", "parallel", "arbitrary")),
        name="ec_ffn",
    )(xe, gates, w_gate, w_up, w_down)


def _combine_kernel(boff_ref, aff_ref, thr_ref, tie_ref, ye_ref, x1_ref, mod_ref, g_ref, out_ref,
                    slot_ref, carry_ref, *, tb, wslot, nblk):
    b = pl.program_id(0)
    t2 = pl.program_id(1)
    e = pl.program_id(2)
    nsub = aff_ref.shape[1] // tb
    lane = lax.broadcasted_iota(I32, (tb, LANES), 1)

    @pl.when(e == 0)
    def _():
        @pl.when(t2 == 0)
        def _():
            carry_ref[...] = jnp.zeros(carry_ref.shape, F32)

        lower = (lax.broadcasted_iota(I32, (tb, tb), 0)
                 >= lax.broadcasted_iota(I32, (tb, tb), 1)).astype(BF16)
        for u in range(nsub):
            rows = slice(u * tb, (u + 1) * tb)
            bits = lax.bitcast_convert_type(aff_ref[0, rows, :], I32)
            tok = (t2 * nsub + u) * tb + lax.broadcasted_iota(I32, (tb, LANES), 0)
            thr = thr_ref[0]
            sel = ((bits > thr) | ((bits == thr) & (tok <= tie_ref[0]))) & (lane < N_EXPERTS)
            self_ = sel.astype(F32)
            incl = jnp.dot(lower, self_.astype(BF16), preferred_element_type=F32)
            pos = (incl - self_ + carry_ref[...]).astype(I32)
            slot_ref[rows, :] = jnp.where(sel, pos, -1)
            carry_ref[...] = carry_ref[...] + incl[tb - 1:tb, :]
        out_ref[...] = jnp.zeros(out_ref.shape, F32)

    pick = lane == e
    base = (b * pl.num_programs(2) + e) * (nblk + 1) + t2 * nsub
    for u in range(nsub):
        rows = slice(u * tb, (u + 1) * tb)
        lo = boff_ref[base + u]
        hi = boff_ref[base + u + 1]
        slot_e = jnp.sum(jnp.where(pick, slot_ref[rows, :], 0), axis=1, keepdims=True)

        def body(w, carry):
            woff = pl.multiple_of(w * wslot, wslot)
            onehot = (lax.broadcasted_iota(I32, (tb, wslot), 1) + woff == slot_e).astype(BF16)
            out_ref[0, rows, :] += jnp.dot(onehot, ye_ref[0, 0, pl.ds(woff, wslot), :],
                                           preferred_element_type=F32)
            return carry

        lax.fori_loop(lo // wslot, (hi + wslot - 1) // wslot, body, 0)

    @pl.when(e == pl.num_programs(2) - 1)
    def _():
        for u in range(nsub):
            rows = slice(u * tb, (u + 1) * tb)
            x2 = x1_ref[0, rows, :] + mod_ref[0, 5:6, :] * out_ref[0, rows, :]
            out_ref[0, rows, :] = (x2 * lax.rsqrt(jnp.mean(x2 * x2, axis=-1, keepdims=True)
                                                  + NORM_EPS) * g_ref[...])


def _combine(boff_flat, aff, thr, tie, ye, x1, mod, g_final, tb, tsup=2048, wslot=256):
    B, S, D = x1.shape
    E, cap = ye.shape[1], ye.shape[2]
    tsup = min(tsup, S)
    return pl.pallas_call(
        functools.partial(_combine_kernel, tb=tb, wslot=wslot, nblk=S // tb),
        grid_spec=pltpu.PrefetchScalarGridSpec(
            num_scalar_prefetch=1,
            grid=(B, S // tsup, E),
            in_specs=[pl.BlockSpec((1, tsup, LANES), lambda b, t, e, off: (b, t, 0)),
                      pl.BlockSpec((1, 1, LANES), lambda b, t, e, off: (b, 0, 0)),
                      pl.BlockSpec((1, 1, LANES), lambda b, t, e, off: (b, 0, 0)),
                      pl.BlockSpec((1, 1, cap, D), lambda b, t, e, off: (b, e, 0, 0)),
                      pl.BlockSpec((1, tsup, D), lambda b, t, e, off: (b, t, 0)),
                      pl.BlockSpec((1, 6, D), lambda b, t, e, off: (b, 0, 0)),
                      pl.BlockSpec((1, D), lambda b, t, e, off: (0, 0))],
            out_specs=pl.BlockSpec((1, tsup, D), lambda b, t, e, off: (b, t, 0)),
            scratch_shapes=[pltpu.VMEM((tsup, LANES), I32), pltpu.VMEM((1, LANES), F32)]),
        out_shape=jax.ShapeDtypeStruct((B, S, D), F32),
        compiler_params=_cparams(("parallel", "arbitrary", "arbitrary")),
        name="ec_combine",
    )(boff_flat, aff, thr, tie, ye, x1, mod, g_final)


def _prep_weights(w_in, w_uq, w_ukv):
    D = w_in.shape[0]
    H = MLA_HEADS
    n_lat = MLA_Q_RANK + MLA_KV_RANK + MLA_ROPE
    n_grp = 3 * DIL_HEADS * DIL_HEAD_DIM
    n_dil = len(DIL_GROUPS) * n_grp
    w_lat = jnp.zeros((D, LAT_PAD), F32).at[:, :n_lat].set(w_in[:, :n_lat]).astype(BF16)
    w_grp = [w_in[:, n_lat + i * n_grp:n_lat + (i + 1) * n_grp].astype(BF16)
             for i in range(len(DIL_GROUPS))]
    w_gates = w_in[:, n_lat + n_dil:].astype(BF16)
    half = MLA_ROPE // 2
    uq = w_uq.reshape(MLA_Q_RANK, H, MLA_NOPE + MLA_ROPE)
    pe = uq[:, :, MLA_NOPE:]
    zq = jnp.zeros((MLA_Q_RANK, H, QK_PAD - MLA_NOPE - MLA_ROPE), F32)
    wq = jnp.concatenate([uq[:, :, :MLA_NOPE], pe, zq], axis=2)
    wqs = jnp.concatenate([jnp.zeros_like(uq[:, :, :MLA_NOPE]), pe[:, :, half:], pe[:, :, :half], zq],
                          axis=2)
    ukv = w_ukv.reshape(MLA_KV_RANK, H, MLA_NOPE + MLA_V)
    wk = jnp.concatenate([ukv[:, :, :MLA_NOPE],
                          jnp.zeros((MLA_KV_RANK, H, QK_PAD - MLA_NOPE), F32)], axis=2)
    wv = jnp.concatenate([ukv[:, :, MLA_NOPE:],
                          jnp.zeros((MLA_KV_RANK, H, QK_PAD - MLA_V), F32)], axis=2)
    return (w_lat, w_grp, w_gates, wq.reshape(MLA_Q_RANK, H * QK_PAD).astype(BF16),
            wqs.reshape(MLA_Q_RANK, H * QK_PAD).astype(BF16),
            wk.reshape(MLA_KV_RANK, H * QK_PAD).astype(BF16),
            wv.reshape(MLA_KV_RANK, H * QK_PAD).astype(BF16))


def kernel(x, c, positions, w_ada, b_ada, g_norm_mix, w_in, g_q_lat, g_kv_lat, w_uq, w_ukv,
           rel_bias, w_out, g_norm_ffn, w_router, w_gate, w_up, w_down, g_final):
    B, S, D = x.shape
    assert w_ada.shape[0] == 1, "the final norm is fused into the (single) layer's last kernel"
    assert DIL_GROUPS[0][1] == 1, "the gates ride along with the undilated group's projection"
    for l in range(w_ada.shape[0]):
        mod = _modulation(c, w_ada[l], b_ada[l])
        w_lat, w_grp, w_gates, wq, wqs, wk, wv = _prep_weights(w_in[l], w_uq[l], w_ukv[l])
        g_mix = g_norm_mix[l].reshape(1, D)
        n_qkv = w_grp[0].shape[1] // D
        hs = _norm_modulate(x, mod, g_mix, tuple(d for _, d in DIL_GROUPS))
        lat = _projection(hs[0], w_lat, "lat_projection", out_dtype=F32)
        main = _projection(hs[0], jnp.concatenate([w_grp[0], w_gates], axis=1), "in_projection_g0",
                           n_plain=n_qkv)
        qkvs = [main] + [_projection(hs[gi], w_grp[gi], f"in_projection_g{gi}")
                         for gi in range(1, len(DIL_GROUPS))]
        q, k, v = _mla_prep(lat.reshape(B, S, LAT_PAD), positions, g_q_lat[l].reshape(1, -1), g_kv_lat[l].reshape(1, -1),
                            wq, wqs, wk, wv)
        o_a = _flash_attention(q, k, v)
        dil_outs = [_dilated_group(qkvs[gi], positions, rel_bias, gi, window, d)
                    for gi, (window, d) in enumerate(DIL_GROUPS)]
        x1, h2, aff, aff_t = _out_projection(o_a, dil_outs, main, n_qkv, x, mod,
                                             g_norm_ffn[l].reshape(1, D),
                                             w_out[l].astype(BF16), w_router[l])
        cap = EC_CAPACITY_FACTOR * S // N_EXPERTS
        chunk = 512
        thr, tie, slot, boff = _select(aff_t, cap, chunk)
        boff_flat = boff[:, :, :S // chunk + 1].reshape(-1)
        xe, gates = _gather(boff_flat, slot, aff_t, h2, cap, chunk)
        ye = _expert_ffn(xe, gates, w_gate[l], w_up[l], w_down[l])
        pad = jnp.zeros((B, 1, LANES - N_EXPERTS), I32)
        thr_l = jnp.concatenate([thr[:, :, 0].reshape(B, 1, N_EXPERTS), pad], axis=2)
        tie_l = jnp.concatenate([tie[:, :, 0].reshape(B, 1, N_EXPERTS), pad], axis=2)
        x = _combine(boff_flat, aff, thr_l, tie_l, ye, x1, mod, g_final.reshape(1, D), tb=chunk)
    return x
```

```python
import functools
import math

import jax
import jax.numpy as jnp
from jax import lax
from jax.experimental import pallas as pl
from jax.experimental.pallas import tpu as pltpu

F32 = jnp.float32
BF16 = jnp.bfloat16
I32 = jnp.int32

MLA_HEADS = 8
MLA_Q_RANK = 384
MLA_KV_RANK = 256
MLA_NOPE = 128
MLA_ROPE = 64
MLA_V = 128
ROPE_THETA = 10000.0
DIL_GROUPS = ((128, 1), (512, 4), (2048, 16))
DIL_HEADS = 8
DIL_HEAD_DIM = 128
REL_BUCKETS = 32
REL_MAX_DIST = 1024
N_EXPERTS = 16
EC_CAPACITY_FACTOR = 2
NORM_EPS = 1e-6
NEG_INF = -1e30

LANES = 128
SUBLANES_BF16 = 16
QK_PAD = 256
LAT_PAD = 768
VMEM_LIMIT = 56 * 1024 * 1024


def _cparams(sem):
    return pltpu.CompilerParams(dimension_semantics=sem, vmem_limit_bytes=VMEM_LIMIT)


def _nt_dot(a, b):
    return lax.dot_general(a, b, (((1,), (1,)), ((), ())), preferred_element_type=F32)


def _mod_kernel(c_ref, w_ref, b_ref, o_ref):
    c = c_ref[...]
    cond = c * jax.nn.sigmoid(c)
    o_ref[...] = jnp.dot(cond, w_ref[...], preferred_element_type=F32,
                         precision=lax.Precision.HIGHEST) + b_ref[...]


def _modulation(c, w_ada, b_ada):
    B, D = c.shape
    rows = 8
    c8 = jnp.zeros((rows, D), F32).at[:B].set(c)
    n6 = w_ada.shape[1]
    out = pl.pallas_call(
        _mod_kernel,
        grid=(n6 // D,),
        in_specs=[pl.BlockSpec((rows, D), lambda j: (0, 0)),
                  pl.BlockSpec((D, D), lambda j: (0, j)),
                  pl.BlockSpec((1, D), lambda j: (0, j))],
        out_specs=pl.BlockSpec((rows, D), lambda j: (0, j)),
        out_shape=jax.ShapeDtypeStruct((rows, n6), F32),
        compiler_params=_cparams(("arbitrary",)),
        name="modulation",
    )(c8, w_ada, b_ada.reshape(1, n6))
    return out[:B].reshape(B, 6, D)


def _norm_kernel(*refs, dils):
    nx = len(refs) - 2 - len(dils)
    x_refs = refs[:nx]
    mod_ref, g_ref = refs[nx:nx + 2]
    h_refs = refs[nx + 2:]
    tm = x_refs[0].shape[1]
    for h_ref, d in zip(h_refs, dils):
        rows = tm // d
        for g in range(d):
            src = pl.ds(g, rows, stride=d) if d > 1 else slice(None)
            x = jnp.concatenate([xr[0, src, :] for xr in x_refs], axis=1)
            y = x * lax.rsqrt(jnp.mean(x * x, axis=-1, keepdims=True) + NORM_EPS) * g_ref[...]
            h = y * (1.0 + mod_ref[0, 1:2, :]) + mod_ref[0, 0:1, :]
            h_ref[0, g] = h.astype(BF16)


def _norm_modulate(x, mod, g, dils, tm=1024):
    B, S, D = x.shape
    in_specs = [pl.BlockSpec((1, tm, LANES), functools.partial(lambda b, i, j: (b, i, j), j=j))
                for j in range(D // LANES)]
    in_specs += [pl.BlockSpec((1, 6, D), lambda b, i: (b, 0, 0)),
                 pl.BlockSpec((1, D), lambda b, i: (0, 0))]
    return pl.pallas_call(
        functools.partial(_norm_kernel, dils=dils),
        grid=(B, S // tm),
        in_specs=in_specs,
        out_specs=[pl.BlockSpec((1, d, tm // d, D), lambda b, i: (b, 0, i, 0)) for d in dils],
        out_shape=[jax.ShapeDtypeStruct((B, d, S // d, D), BF16) for d in dils],
        compiler_params=_cparams(("parallel", "parallel")),
        name="norm_modulate",
    )(*([x] * (D // LANES) + [mod, g]))


def _proj_kernel(h_ref, w_ref, o_ref, *, n_plain, n_tiles):
    acc = jnp.dot(h_ref[...], w_ref[...], preferred_element_type=F32)
    if n_plain == n_tiles:
        o_ref[...] = acc.astype(o_ref.dtype)
    else:
        n = pl.program_id(1)

        @pl.when(n < n_plain)
        def _():
            o_ref[...] = acc.astype(o_ref.dtype)

        @pl.when(n >= n_plain)
        def _():
            o_ref[...] = jax.nn.sigmoid(acc).astype(o_ref.dtype)


def _projection(h, w, name, n_plain=None, out_dtype=BF16, tm=2048, tn=1024):
    B, d, L, D = h.shape
    N = w.shape[1]
    tn = min(tn, N)
    tm = min(tm, B * d * L)
    n_tiles = N // tn
    n_plain = n_tiles if n_plain is None else n_plain
    out = pl.pallas_call(
        functools.partial(_proj_kernel, n_plain=n_plain, n_tiles=n_tiles),
        grid=(B * d * L // tm, n_tiles),
        in_specs=[pl.BlockSpec((tm, D), lambda i, n: (i, 0)),
                  pl.BlockSpec((D, tn), lambda i, n: (0, n))],
        out_specs=pl.BlockSpec((tm, tn), lambda i, n: (i, n)),
        out_shape=jax.ShapeDtypeStruct((B * d * L, N), out_dtype),
        compiler_params=_cparams(("parallel", "arbitrary")),
        name=name,
    )(h.reshape(B * d * L, D), w)
    return out.reshape(B, d, L, N)


def _mla_prep_kernel(lat_ref, pos_ref, gq_ref, gkv_ref, wq_ref, wqs_ref, wk_ref, wv_ref,
                     freq_ref, sgn_ref, sel_ref, sels_ref, one_ref, q_ref, k_ref, v_ref, *, scale):
    lat = lat_ref[0]
    cq = lat[:, :MLA_Q_RANK]
    ckv = lat[:, MLA_Q_RANK:MLA_Q_RANK + MLA_KV_RANK]
    kpe = lat[:, MLA_Q_RANK + MLA_KV_RANK:]
    cqn = (cq * lax.rsqrt(jnp.mean(cq * cq, axis=-1, keepdims=True) + NORM_EPS)
           * gq_ref[...]).astype(BF16)
    ckvn = (ckv * lax.rsqrt(jnp.mean(ckv * ckv, axis=-1, keepdims=True) + NORM_EPS)
            * gkv_ref[...]).astype(BF16)
    ang = pos_ref[0].astype(F32) * freq_ref[...]
    cos = jnp.cos(ang)
    sin = jnp.sin(ang) * sgn_ref[...]
    qa = jnp.dot(cqn, wq_ref[...], preferred_element_type=F32)
    qs = jnp.dot(cqn, wqs_ref[...], preferred_element_type=F32)
    kn = jnp.dot(ckvn, wk_ref[...], preferred_element_type=F32)
    v_ref[0] = (jnp.dot(ckvn, wv_ref[...], preferred_element_type=F32) + one_ref[...]).astype(BF16)
    hp = lax.Precision.HIGHEST
    rk = (jnp.dot(kpe, sel_ref[...], preferred_element_type=F32, precision=hp) * cos
          + jnp.dot(kpe, sels_ref[...], preferred_element_type=F32, precision=hp) * sin)
    cos_q = cos * scale
    sin_q = sin * scale
    for h in range(MLA_HEADS):
        sl = slice(h * QK_PAD, (h + 1) * QK_PAD)
        q_ref[0, :, sl] = (qa[:, sl] * cos_q + qs[:, sl] * sin_q).astype(BF16)
        k_ref[0, :, sl] = (kn[:, sl] + rk).astype(BF16)


def _mla_prep(lat, positions, g_q, g_kv, wq, wqs, wk, wv, tm=512):
    B, S, _ = lat.shape
    H = MLA_HEADS
    half = MLA_ROPE // 2
    inv_freq = ROPE_THETA ** (-jnp.arange(0, MLA_ROPE, 2, dtype=F32) / MLA_ROPE)
    freq = jnp.zeros((1, QK_PAD), F32).at[0, MLA_NOPE:MLA_NOPE + MLA_ROPE].set(
        jnp.concatenate([inv_freq, inv_freq]))
    sgn = jnp.zeros((1, QK_PAD), F32).at[0, MLA_NOPE:MLA_NOPE + half].set(-1.0)
    sgn = sgn.at[0, MLA_NOPE + half:MLA_NOPE + MLA_ROPE].set(1.0)
    r = jnp.arange(MLA_ROPE)
    sel = jnp.zeros((LANES, QK_PAD), F32).at[r, MLA_NOPE + r].set(1.0)
    sels = jnp.zeros((LANES, QK_PAD), F32).at[r, MLA_NOPE + (r + half) % MLA_ROPE].set(1.0)
    scale = (MLA_NOPE + MLA_ROPE) ** -0.5 * math.log2(math.e)
    ones_col = jnp.zeros((H, QK_PAD), F32).at[:, MLA_V].set(1.0).reshape(1, H * QK_PAD)
    const = lambda b, i: (0, 0)
    return pl.pallas_call(
        functools.partial(_mla_prep_kernel, scale=scale),
        grid=(B, S // tm),
        in_specs=[pl.BlockSpec((1, tm, LAT_PAD), lambda b, i: (b, i, 0)),
                  pl.BlockSpec((1, tm, 1), lambda b, i: (b, i, 0)),
                  pl.BlockSpec((1, MLA_Q_RANK), const),
                  pl.BlockSpec((1, MLA_KV_RANK), const),
                  pl.BlockSpec((MLA_Q_RANK, H * QK_PAD), const),
                  pl.BlockSpec((MLA_Q_RANK, H * QK_PAD), const),
                  pl.BlockSpec((MLA_KV_RANK, H * QK_PAD), const),
                  pl.BlockSpec((MLA_KV_RANK, H * QK_PAD), const),
                  pl.BlockSpec((1, QK_PAD), const),
                  pl.BlockSpec((1, QK_PAD), const),
                  pl.BlockSpec((LANES, QK_PAD), const),
                  pl.BlockSpec((LANES, QK_PAD), const),
                  pl.BlockSpec((1, H * QK_PAD), const)],
        out_specs=[pl.BlockSpec((1, tm, H * QK_PAD), lambda b, i: (b, i, 0)),
                   pl.BlockSpec((1, tm, H * QK_PAD), lambda b, i: (b, i, 0)),
                   pl.BlockSpec((1, tm, H * QK_PAD), lambda b, i: (b, i, 0))],
        out_shape=[jax.ShapeDtypeStruct((B, S, H * QK_PAD), BF16),
                   jax.ShapeDtypeStruct((B, S, H * QK_PAD), BF16),
                   jax.ShapeDtypeStruct((B, S, H * QK_PAD), BF16)],
        compiler_params=_cparams(("parallel", "parallel")),
        name="mla_prep",
    )(lat, positions.reshape(B, S, 1), g_q, g_kv, wq, wqs, wk, wv, freq, sgn, sel, sels, ones_col)


def _flash_kernel(q_ref, k_ref, v_ref, o_ref, m_ref, acc_ref, s_ref, *, tq, tkc):
    S = k_ref.shape[1]
    nk = S // tkc
    nq = S // tq
    half = nk // 2
    acc_ref[...] = jnp.zeros(acc_ref.shape, F32)
    m_ref[...] = jnp.full(m_ref.shape, -jnp.inf, F32)

    def scores(qi, kj):
        qoff = pl.multiple_of(qi * tq, tq)
        koff = pl.multiple_of(kj * tkc, tkc)
        return _nt_dot(q_ref[0, pl.ds(qoff, tq), :], k_ref[0, pl.ds(koff, tkc), :])

    def consume(kj, slot, first):
        koff = pl.multiple_of(kj * tkc, tkc)
        s = s_ref[slot]
        m_prev = m_ref[...]
        if first is not None:
            m_prev = jnp.where(first, -jnp.inf, m_prev)
        m_new = jnp.maximum(m_prev, jnp.max(s, axis=-1, keepdims=True))
        alpha = jnp.exp2(m_prev - m_new)
        p = jnp.exp2(s - m_new).astype(BF16)
        acc_ref[...] = alpha * acc_ref[...] + jnp.dot(p, v_ref[0, pl.ds(koff, tkc), :],
                                                      preferred_element_type=F32)
        m_ref[...] = m_new

    s_ref[0] = scores(0, 0)

    def body(t, carry):
        qi = t // half
        kj = 2 * (t % half)
        s_ref[1] = scores(qi, kj + 1)
        consume(kj, 0, kj == 0)
        t1 = jnp.minimum(t + 1, nq * half - 1)
        s_ref[0] = scores(t1 // half, 2 * (t1 % half))
        consume(kj + 1, 1, None)

        @pl.when(kj + 2 == nk)
        def _():
            acc = acc_ref[...]
            qoff = pl.multiple_of(qi * tq, tq)
            o_ref[0, pl.ds(qoff, tq), :] = (acc[:, :MLA_V] / acc[:, MLA_V:MLA_V + 1]).astype(BF16)
        return carry

    lax.fori_loop(0, nq * half, body, 0)


def _flash_attention(q, k, v, tq=1024, tkc=1024):
    B, S, _ = q.shape
    H = MLA_HEADS
    assert (S // tkc) % 2 == 0 and S % tq == 0
    whole = pl.BlockSpec((1, S, QK_PAD), lambda b, h: (b, 0, h))
    return pl.pallas_call(
        functools.partial(_flash_kernel, tq=tq, tkc=tkc),
        grid=(B, H),
        in_specs=[whole, whole, whole],
        out_specs=pl.BlockSpec((1, S, MLA_V), lambda b, h: (b, 0, h)),
        out_shape=jax.ShapeDtypeStruct((B, S, H * MLA_V), BF16),
        scratch_shapes=[pltpu.VMEM((tq, 1), F32), pltpu.VMEM((tq, QK_PAD), F32),
                        pltpu.VMEM((2, tq, tkc), F32)],
        compiler_params=_cparams(("parallel", "parallel")),
        name="mla_flash",
    )(q, k, v)


def _t5_bucket(rel):
    nb = REL_BUCKETS // 2
    max_exact = nb // 2
    ret = jnp.where(rel > 0, nb, 0)
    n = jnp.abs(rel)
    nf = jnp.maximum(n, 1).astype(F32)
    large = max_exact + (jnp.log(nf / max_exact) / math.log(REL_MAX_DIST / max_exact)
                         * (nb - max_exact)).astype(I32)
    large = jnp.minimum(large, nb - 1)
    return ret + jnp.where(n < max_exact, n, large)


def _dilated_kernel(q_ref, kp_ref, kc_ref, kn_ref, vp_ref, vc_ref, vn_ref, pq_ref, pk_ref, tab_ref,
                    o_ref, l_ref, kcat_ref, vcat_ref, s_scr, *, r, tq, ts, n_rows):
    n = pl.program_id(2)
    nsub = tq // ts
    nk = ts + 2 * r
    kcat_ref[0:r, :] = kp_ref[0, 0]
    kcat_ref[r:r + tq, :] = kc_ref[0, 0]
    kcat_ref[r + tq:tq + 2 * r, :] = kn_ref[0, 0]
    ones = jnp.ones((tq + 2 * r, DIL_HEAD_DIM), BF16)
    for h in range(DIL_HEADS):
        sl = slice(h * DIL_HEAD_DIM, (h + 1) * DIL_HEAD_DIM)
        dst = slice(2 * h * DIL_HEAD_DIM, (2 * h + 1) * DIL_HEAD_DIM)
        vcat_ref[0:r, dst] = vp_ref[0, 0, :, sl]
        vcat_ref[r:r + tq, dst] = vc_ref[0, 0, :, sl]
        vcat_ref[r + tq:tq + 2 * r, dst] = vn_ref[0, 0, :, sl]
        vcat_ref[:, (2 * h + 1) * DIL_HEAD_DIM:(2 * h + 2) * DIL_HEAD_DIM] = ones
    lane_grp = lax.broadcasted_iota(I32, (ts, LANES), 1) // (LANES // DIL_HEADS)
    scale = DIL_HEAD_DIM ** -0.5 * math.log2(math.e)
    for j in range(nsub):
        qi = n * tq + j * ts + lax.broadcasted_iota(I32, (ts, nk), 0)
        kj = n * tq + j * ts - r + lax.broadcasted_iota(I32, (ts, nk), 1)
        valid = (jnp.abs(kj - qi) <= r) & (kj >= 0) & (kj < n_rows)
        rel = pk_ref[0, 0, j] - pq_ref[0, 0, j * ts:(j + 1) * ts, :]
        bucket = jnp.where(valid, _t5_bucket(rel), REL_BUCKETS)
        for h in range(DIL_HEADS):
            sl = slice(h * DIL_HEAD_DIM, (h + 1) * DIL_HEAD_DIM)
            s = _nt_dot(q_ref[0, 0, j * ts:(j + 1) * ts, sl], kcat_ref[j * ts:j * ts + nk, sl]) * scale
            tab = jnp.broadcast_to(tab_ref[h:h + 1, :], (ts, LANES))
            bias = jnp.concatenate(
                [jnp.take_along_axis(tab, bucket[:, c * LANES:(c + 1) * LANES], axis=1)
                 for c in range(nk // LANES)], axis=1)
            s_scr[j * DIL_HEADS + h] = s + bias
    for j in range(nsub):
        lse_tile = jnp.zeros((ts, LANES), F32)
        for h in range(DIL_HEADS):
            sl = slice(h * DIL_HEAD_DIM, (h + 1) * DIL_HEAD_DIM)
            s = s_scr[j * DIL_HEADS + h]
            m = jnp.max(s, axis=-1, keepdims=True)
            p = jnp.exp2(s - m).astype(BF16)
            res = jnp.dot(p, vcat_ref[j * ts:j * ts + nk,
                                      2 * h * DIL_HEAD_DIM:(2 * h + 2) * DIL_HEAD_DIM],
                          preferred_element_type=F32)
            den = res[:, DIL_HEAD_DIM:]
            o_ref[0, 0, j * ts:(j + 1) * ts, sl] = (res[:, :DIL_HEAD_DIM] / den).astype(BF16)
            lse_tile = jnp.where(lane_grp == h, m * math.log(2.0) + jnp.log(den), lse_tile)
        l_ref[0, 0, j * ts:(j + 1) * ts, :] = lse_tile


def _dilated_group(qkv, positions, table, gi, window, d, tq=512, ts=128):
    B, _, L, _ = qkv.shape
    HD = DIL_HEADS * DIL_HEAD_DIM
    r = window // (2 * d)
    tq = min(tq, L)
    nt = L // tq
    nsub = tq // ts
    rb = tq // r
    nk = ts + 2 * r
    pcls = positions.reshape(B, L, d).transpose(0, 2, 1)
    pq = pcls.reshape(B, d, L, 1)
    ppad = jnp.pad(pcls, ((0, 0), (0, 0), (r, ts + r)))
    pk = jnp.concatenate([ppad[:, :, :L].reshape(B, d, L // ts, ts),
                          ppad[:, :, ts:ts + L].reshape(B, d, L // ts, ts)[..., :2 * r]], axis=-1)
    pk = pk.reshape(B, d, L // ts, 1, nk)
    tab = jnp.zeros((DIL_HEADS, LANES), F32).at[:, :REL_BUCKETS].set(
        table[:, gi * DIL_HEADS:(gi + 1) * DIL_HEADS].T * math.log2(math.e))
    tab = tab.at[:, REL_BUCKETS].set(NEG_INF)

    def cur(c):
        return pl.BlockSpec((1, 1, tq, HD), lambda b, g, n: (b, g, n, c))

    def prv(c):
        return pl.BlockSpec((1, 1, r, HD), lambda b, g, n: (b, g, jnp.maximum(n * rb - 1, 0), c))

    def nxt(c):
        return pl.BlockSpec((1, 1, r, HD),
                            lambda b, g, n: (b, g, jnp.minimum((n + 1) * rb, L // r - 1), c))

    return pl.pallas_call(
        functools.partial(_dilated_kernel, r=r, tq=tq, ts=ts, n_rows=L),
        grid=(B, d, nt),
        in_specs=[cur(0), prv(1), cur(1), nxt(1), prv(2), cur(2), nxt(2),
                  pl.BlockSpec((1, 1, tq, 1), lambda b, g, n: (b, g, n, 0)),
                  pl.BlockSpec((1, 1, nsub, 1, nk), lambda b, g, n: (b, g, n, 0, 0)),
                  pl.BlockSpec((DIL_HEADS, LANES), lambda b, g, n: (0, 0))],
        out_specs=[pl.BlockSpec((1, 1, tq, HD), lambda b, g, n: (b, g, n, 0)),
                   pl.BlockSpec((1, 1, tq, LANES), lambda b, g, n: (b, g, n, 0))],
        out_shape=[jax.ShapeDtypeStruct((B, d, L, HD), BF16),
                   jax.ShapeDtypeStruct((B, d, L, LANES), F32)],
        scratch_shapes=[pltpu.VMEM((tq + 2 * r, HD), BF16), pltpu.VMEM((tq + 2 * r, 2 * HD), BF16),
                        pltpu.VMEM((nsub * DIL_HEADS, ts, nk), F32)],
        compiler_params=_cparams(("parallel", "parallel", "parallel")),
        name=f"dilated_g{gi}",
    )(*([qkv] * 7 + [pq, pk, tab]))


def _outproj_kernel(*refs, dils):
    ng = len(dils)
    oa_ref = refs[0]
    og_refs = refs[1:1 + ng]
    lg_refs = refs[1 + ng:1 + 2 * ng]
    (ga_ref, gb_ref, x_ref, mod_ref, g_ref, wo_ref, wr_ref,
     x1_ref, h2_ref, aff_ref, afft_ref, o_scr, l_scr) = refs[1 + 2 * ng:]
    tm = x_ref.shape[1]
    for i, d in enumerate(dils):
        rows = tm // d
        for g in range(d):
            dst = pl.ds(g, rows, stride=d) if d > 1 else slice(None)
            og = og_refs[i][0, g].astype(F32)
            for h in range(DIL_HEADS):
                o_scr[i, h, dst, :] = og[:, h * DIL_HEAD_DIM:(h + 1) * DIL_HEAD_DIM]
            l_scr[i, dst, :] = lg_refs[i][0, g]
    lses = [l_scr[i] for i in range(ng)]
    mx = functools.reduce(jnp.maximum, lses)
    es = [jnp.exp(l - mx) for l in lses]
    tot = functools.reduce(lambda a, b: a + b, es)
    wts = [e / tot for e in es]
    cols = []
    for h in range(DIL_HEADS):
        c = h * (LANES // DIL_HEADS)
        cols.append(functools.reduce(
            lambda a, b: a + b, [wts[i][:, c:c + 1] * o_scr[i, h] for i in range(ng)]))
    o_b = jnp.concatenate(cols, axis=1)
    comb = (ga_ref[0, 0].astype(F32) * oa_ref[0].astype(F32)
            + gb_ref[0, 0].astype(F32) * o_b).astype(BF16)
    x1 = x_ref[0] + mod_ref[0, 2:3, :] * jnp.dot(comb, wo_ref[...], preferred_element_type=F32)
    x1_ref[0] = x1
    y = x1 * lax.rsqrt(jnp.mean(x1 * x1, axis=-1, keepdims=True) + NORM_EPS) * g_ref[...]
    h2 = y * (1.0 + mod_ref[0, 4:5, :]) + mod_ref[0, 3:4, :]
    h2_ref[0] = h2.astype(BF16)
    logits = jnp.dot(h2, wr_ref[...], preferred_element_type=F32,
                     precision=lax.Precision.HIGHEST)
    lane = lax.broadcasted_iota(I32, logits.shape, 1)
    logits = jnp.where(lane < N_EXPERTS, logits, -jnp.inf)
    e = jnp.exp(logits - jnp.max(logits, axis=-1, keepdims=True))
    aff = e / jnp.sum(e, axis=-1, keepdims=True)
    aff_ref[0] = aff
    afft_ref[0] = aff.T[:N_EXPERTS, :]


def _out_projection(o_a, dil_outs, main, gate_col, x, mod, g, w_out, w_router, tm=512):
    B, S, D = x.shape
    const = lambda b, i: (0, 0)
    row = lambda b, i: (b, i, 0)
    wr = jnp.zeros((D, LANES), F32).at[:, :N_EXPERTS].set(w_router)
    dils = tuple(o.shape[1] for o, _ in dil_outs)
    og_specs = [pl.BlockSpec((1, d, tm // d, D), lambda b, i: (b, 0, i, 0)) for d in dils]
    lg_specs = [pl.BlockSpec((1, d, tm // d, LANES), lambda b, i: (b, 0, i, 0)) for d in dils]
    return pl.pallas_call(
        functools.partial(_outproj_kernel, dils=dils),
        grid=(B, S // tm),
        in_specs=[pl.BlockSpec((1, tm, D), row)] + og_specs + lg_specs + [
                  pl.BlockSpec((1, 1, tm, D), lambda b, i: (b, 0, i, gate_col)),
                  pl.BlockSpec((1, 1, tm, D), lambda b, i: (b, 0, i, gate_col + 1)),
                  pl.BlockSpec((1, tm, D), row),
                  pl.BlockSpec((1, 6, D), lambda b, i: (b, 0, 0)),
                  pl.BlockSpec((1, D), const),
                  pl.BlockSpec((D, D), const),
                  pl.BlockSpec((D, LANES), const)],
        out_specs=[pl.BlockSpec((1, tm, D), row),
                   pl.BlockSpec((1, tm, D), row),
                   pl.BlockSpec((1, tm, LANES), row),
                   pl.BlockSpec((1, N_EXPERTS, tm), lambda b, i: (b, 0, i))],
        out_shape=[jax.ShapeDtypeStruct((B, S, D), F32),
                   jax.ShapeDtypeStruct((B, S, D), BF16),
                   jax.ShapeDtypeStruct((B, S, LANES), F32),
                   jax.ShapeDtypeStruct((B, N_EXPERTS, S), F32)],
        scratch_shapes=[pltpu.VMEM((len(dils), DIL_HEADS, tm, DIL_HEAD_DIM), F32),
                        pltpu.VMEM((len(dils), tm, LANES), F32)],
        compiler_params=_cparams(("parallel", "parallel")),
        name="out_projection",
    )(o_a, *[o for o, _ in dil_outs], *[l for _, l in dil_outs], main, main, x, mod, g, w_out, wr)


def _select_kernel(afft_ref, thr_ref, tie_ref, slot_ref, boff_ref, *, cap, chunk):
    aff = afft_ref[0]
    E, S = aff.shape
    bits = lax.bitcast_convert_type(aff, I32)

    def count(mask):
        return jnp.sum(mask.astype(I32), axis=1, keepdims=True)

    def thr_step(i, v):
        cand = v | jnp.left_shift(jnp.int32(1), 30 - i)
        return jnp.where(count(bits >= cand) >= cap, cand, v)

    thr = lax.fori_loop(0, 31, thr_step, jnp.zeros((E, 1), I32))
    need = cap - count(bits > thr)
    eq = bits == thr
    idx = lax.broadcasted_iota(I32, (E, S), 1)
    nbits = max(1, (S - 1).bit_length())

    def tie_step(i, j):
        cand = j | jnp.left_shift(jnp.int32(1), nbits - 1 - i)
        return jnp.where(count(eq & (idx < cand)) < need, cand, j)

    tie = lax.fori_loop(0, nbits, tie_step, jnp.zeros((E, 1), I32))
    sel = (bits > thr) | (eq & (idx <= tie))
    thr_ref[0] = jnp.broadcast_to(thr, (E, LANES))
    tie_ref[0] = jnp.broadcast_to(tie, (E, LANES))

    upper = (lax.broadcasted_iota(I32, (chunk, chunk), 0)
             <= lax.broadcasted_iota(I32, (chunk, chunk), 1)).astype(BF16)
    carry = jnp.zeros((E, 1), F32)
    self_ = sel.astype(F32)
    lane = lax.broadcasted_iota(I32, (E, LANES), 1)
    boff = jnp.zeros((E, LANES), I32)
    for c in range(S // chunk):
        boff = jnp.where(lane == c, carry.astype(I32), boff)
        scf = self_[:, c * chunk:(c + 1) * chunk]
        incl = jnp.dot(scf.astype(BF16), upper, preferred_element_type=F32)
        pos = (incl - scf + carry).astype(I32)
        slot_ref[0, :, c * chunk:(c + 1) * chunk] = jnp.where(scf > 0.0, pos, -1)
        carry = carry + incl[:, chunk - 1:chunk]
    boff_ref[0] = jnp.where(lane == S // chunk, carry.astype(I32), boff)


def _select(aff_t, cap, chunk):
    B, E, S = aff_t.shape
    assert S // chunk < LANES
    return pl.pallas_call(
        functools.partial(_select_kernel, cap=cap, chunk=chunk),
        grid=(B,),
        in_specs=[pl.BlockSpec((1, E, S), lambda b: (b, 0, 0))],
        out_specs=[pl.BlockSpec((1, E, LANES), lambda b: (b, 0, 0)),
                   pl.BlockSpec((1, E, LANES), lambda b: (b, 0, 0)),
                   pl.BlockSpec((1, E, S), lambda b: (b, 0, 0)),
                   pl.BlockSpec((1, E, LANES), lambda b: (b, 0, 0))],
        out_shape=[jax.ShapeDtypeStruct((B, E, LANES), I32),
                   jax.ShapeDtypeStruct((B, E, LANES), I32),
                   jax.ShapeDtypeStruct((B, E, S), I32),
                   jax.ShapeDtypeStruct((B, E, LANES), I32)],
        compiler_params=_cparams(("parallel",)),
        name="ec_select",
    )(aff_t)


def _gather_kernel(boff_ref, slot_ref, afft_ref, h_ref, xe_ref, gate_ref, acc_ref, gacc_ref, *, win):
    _, _, nblk, tbk = slot_ref.shape
    cap = xe_ref.shape[2]
    base = (pl.program_id(0) * pl.num_programs(1) + pl.program_id(1)) * (nblk + 1)
    acc_ref[...] = jnp.zeros(acc_ref.shape, F32)
    gacc_ref[...] = jnp.zeros(gacc_ref.shape, F32)

    def chunk(tb, carry):
        lo = boff_ref[base + tb]
        hi = boff_ref[base + tb + 1]
        first = (lo // 8) * 8
        off = pl.multiple_of(tb * tbk, tbk)
        slot_row = slot_ref[0, 0, pl.ds(tb, 1), :]
        aff_row = afft_ref[0, 0, pl.ds(tb, 1), :]

        def window(k, c):
            w0 = pl.multiple_of(first + k * win, 8)
            hit = lax.broadcasted_iota(I32, (win, tbk), 0) + w0 == slot_row
            acc_ref[pl.ds(w0, win), :] += jnp.dot(hit.astype(BF16), h_ref[0, pl.ds(off, tbk), :],
                                                  preferred_element_type=F32)
            gacc_ref[pl.ds(w0, win), :] += jnp.sum(jnp.where(hit, aff_row, 0.0),
                                                   axis=1, keepdims=True)
            return c

        lax.fori_loop(0, (hi - first + win - 1) // win, window, 0)
        return carry

    lax.fori_loop(0, nblk, chunk, 0)
    xe_ref[0, 0] = acc_ref[0:cap, :].astype(BF16)
    gate_ref[0, 0] = gacc_ref[0:cap, :]


def _gather(boff_flat, slot, aff_t, h2, cap, chunk, win=128):
    B, E, S = slot.shape
    D = h2.shape[-1]
    rowblk = pl.BlockSpec((1, 1, S // chunk, chunk), lambda b, e, off: (b, e, 0, 0))
    return pl.pallas_call(
        functools.partial(_gather_kernel, win=win),
        grid_spec=pltpu.PrefetchScalarGridSpec(
            num_scalar_prefetch=1,
            grid=(B, E),
            in_specs=[rowblk, rowblk,
                      pl.BlockSpec((1, S, D), lambda b, e, off: (b, 0, 0))],
            out_specs=[pl.BlockSpec((1, 1, cap, D), lambda b, e, off: (b, e, 0, 0)),
                       pl.BlockSpec((1, 1, cap, 1), lambda b, e, off: (b, e, 0, 0))],
            scratch_shapes=[pltpu.VMEM((cap + win, D), F32), pltpu.VMEM((cap + win, 1), F32)]),
        out_shape=[jax.ShapeDtypeStruct((B, E, cap, D), BF16),
                   jax.ShapeDtypeStruct((B, E, cap, 1), F32)],
        compiler_params=_cparams(("parallel", "arbitrary")),
        name="ec_gather",
    )(boff_flat, slot.reshape(B, E, S // chunk, chunk), aff_t.reshape(B, E, S // chunk, chunk), h2)


def _ffn_kernel(xe_ref, gate_ref, wg_ref, wu_ref, wd_ref, ye_ref, acc_ref):
    f = pl.program_id(2)
    nb, _, cap, D = xe_ref.shape

    @pl.when((pl.program_id(0) == 0) & (pl.program_id(1) == 0) & (f == 0))
    def _():
        acc_ref[...] = jnp.zeros(acc_ref.shape, F32)

    x = xe_ref[...].reshape(nb * cap, D)
    g = jnp.dot(x, wg_ref[0].astype(BF16), preferred_element_type=F32)
    u = jnp.dot(x, wu_ref[0].astype(BF16), preferred_element_type=F32)
    hid = (g * jax.nn.sigmoid(g) * u).astype(BF16)
    part = jnp.dot(hid, wd_ref[0].astype(BF16), preferred_element_type=F32)
    acc_ref[...] = jnp.where(f == 0, 0.0, acc_ref[...]) + part

    @pl.when(f == pl.num_programs(2) - 1)
    def _():
        gate = gate_ref[...].reshape(nb * cap, 1)
        ye_ref[...] = (acc_ref[...] * gate).astype(BF16).reshape(ye_ref.shape)


def _expert_ffn(xe, gates, w_gate, w_up, w_down, nb=2, tf=512):
    B, E, cap, D = xe.shape
    F = w_gate.shape[-1]
    return pl.pallas_call(
        _ffn_kernel,
        grid=(E, B // nb, F // tf),
        in_specs=[pl.BlockSpec((nb, 1, cap, D), lambda e, b, f: (b, e, 0, 0)),
                  pl.BlockSpec((nb, 1, cap, 1), lambda e, b, f: (b, e, 0, 0)),
                  pl.BlockSpec((1, D, tf), lambda e, b, f: (e, 0, f)),
                  pl.BlockSpec((1, D, tf), lambda e, b, f: (e, 0, f)),
                  pl.BlockSpec((1, tf, D), lambda e, b, f: (e, f, 0))],
        out_specs=pl.BlockSpec((nb, 1, cap, D), lambda e, b, f: (b, e, 0, 0)),
        out_shape=jax.ShapeDtypeStruct((B, E, cap, D), BF16),
        scratch_shapes=[pltpu.VMEM((nb * cap, D), F32)],
        compiler_params=_cparams(("arbitrary", "arbitrary", "arbitrary")),
        name="ec_ffn",
    )(xe, gates, w_gate, w_up, w_down)


def _combine_kernel(boff_ref, aff_ref, thr_ref, tie_ref, ye_ref, x1_ref, mod_ref, g_ref, out_ref,
                    slot_ref, carry_ref, *, tb, wslot, nblk):
    b = pl.program_id(0)
    t2 = pl.program_id(1)
    e = pl.program_id(2)
    nsub = aff_ref.shape[1] // tb
    cap = ye_ref.shape[2]
    lane = lax.broadcasted_iota(I32, (tb, LANES), 1)

    @pl.when(e == 0)
    def _():
        @pl.when(t2 == 0)
        def _():
            carry_ref[...] = jnp.zeros(carry_ref.shape, F32)

        lower = (lax.broadcasted_iota(I32, (tb, tb), 0)
                 >= lax.broadcasted_iota(I32, (tb, tb), 1)).astype(BF16)
        for u in range(nsub):
            rows = slice(u * tb, (u + 1) * tb)
            bits = lax.bitcast_convert_type(aff_ref[0, rows, :], I32)
            tok = (t2 * nsub + u) * tb + lax.broadcasted_iota(I32, (tb, LANES), 0)
            thr = thr_ref[0]
            sel = ((bits > thr) | ((bits == thr) & (tok <= tie_ref[0]))) & (lane < N_EXPERTS)
            self_ = sel.astype(F32)
            incl = jnp.dot(lower, self_.astype(BF16), preferred_element_type=F32)
            pos = (incl - self_ + carry_ref[...]).astype(I32)
            slot_ref[rows, :] = jnp.where(sel, pos, -1)
            carry_ref[...] = carry_ref[...] + incl[tb - 1:tb, :]
        out_ref[...] = jnp.zeros(out_ref.shape, F32)

    pick = lane == e
    base = (b * pl.num_programs(2) + e) * (nblk + 1) + t2 * nsub
    for u in range(nsub):
        rows = slice(u * tb, (u + 1) * tb)
        lo = boff_ref[base + u]
        hi = boff_ref[base + u + 1]
        slot_e = jnp.sum(jnp.where(pick, slot_ref[rows, :], 0), axis=1, keepdims=True)

        first = (lo // SUBLANES_BF16) * SUBLANES_BF16

        def body(k, carry):
            start = first + k * wslot
            woff = pl.multiple_of(jnp.minimum(start, cap - wslot), SUBLANES_BF16)
            todo = jnp.where(slot_e >= start, slot_e, -1)
            onehot = (lax.broadcasted_iota(I32, (tb, wslot), 1) + woff == todo).astype(BF16)
            out_ref[0, rows, :] += jnp.dot(onehot, ye_ref[0, 0, pl.ds(woff, wslot), :],
                                           preferred_element_type=F32)
            return carry

        lax.fori_loop(0, (hi - first + wslot - 1) // wslot, body, 0)

    @pl.when(e == pl.num_programs(2) - 1)
    def _():
        for u in range(nsub):
            rows = slice(u * tb, (u + 1) * tb)
            x2 = x1_ref[0, rows, :] + mod_ref[0, 5:6, :] * out_ref[0, rows, :]
            out_ref[0, rows, :] = (x2 * lax.rsqrt(jnp.mean(x2 * x2, axis=-1, keepdims=True)
                                                  + NORM_EPS) * g_ref[...])


def _combine(boff_flat, aff, thr, tie, ye, x1, mod, g_final, tb, tsup=2048, wslot=128):
    B, S, D = x1.shape
    E, cap = ye.shape[1], ye.shape[2]
    tsup = min(tsup, S)
    return pl.pallas_call(
        functools.partial(_combine_kernel, tb=tb, wslot=wslot, nblk=S // tb),
        grid_spec=pltpu.PrefetchScalarGridSpec(
            num_scalar_prefetch=1,
            grid=(B, S // tsup, E),
            in_specs=[pl.BlockSpec((1, tsup, LANES), lambda b, t, e, off: (b, t, 0)),
                      pl.BlockSpec((1, 1, LANES), lambda b, t, e, off: (b, 0, 0)),
                      pl.BlockSpec((1, 1, LANES), lambda b, t, e, off: (b, 0, 0)),
                      pl.BlockSpec((1, 1, cap, D), lambda b, t, e, off: (b, e, 0, 0)),
                      pl.BlockSpec((1, tsup, D), lambda b, t, e, off: (b, t, 0)),
                      pl.BlockSpec((1, 6, D), lambda b, t, e, off: (b, 0, 0)),
                      pl.BlockSpec((1, D), lambda b, t, e, off: (0, 0))],
            out_specs=pl.BlockSpec((1, tsup, D), lambda b, t, e, off: (b, t, 0)),
            scratch_shapes=[pltpu.VMEM((tsup, LANES), I32), pltpu.VMEM((1, LANES), F32)]),
        out_shape=jax.ShapeDtypeStruct((B, S, D), F32),
        compiler_params=_cparams(("parallel", "arbitrary", "arbitrary")),
        name="ec_combine",
    )(boff_flat, aff, thr, tie, ye, x1, mod, g_final)


def _prep_weights(w_in, w_uq, w_ukv):
    D = w_in.shape[0]
    H = MLA_HEADS
    n_lat = MLA_Q_RANK + MLA_KV_RANK + MLA_ROPE
    n_grp = 3 * DIL_HEADS * DIL_HEAD_DIM
    n_dil = len(DIL_GROUPS) * n_grp
    w_lat = jnp.zeros((D, LAT_PAD), F32).at[:, :n_lat].set(w_in[:, :n_lat]).astype(BF16)
    w_grp = [w_in[:, n_lat + i * n_grp:n_lat + (i + 1) * n_grp].astype(BF16)
             for i in range(len(DIL_GROUPS))]
    w_gates = w_in[:, n_lat + n_dil:].astype(BF16)
    half = MLA_ROPE // 2
    uq = w_uq.reshape(MLA_Q_RANK, H, MLA_NOPE + MLA_ROPE)
    pe = uq[:, :, MLA_NOPE:]
    zq = jnp.zeros((MLA_Q_RANK, H, QK_PAD - MLA_NOPE - MLA_ROPE), F32)
    wq = jnp.concatenate([uq[:, :, :MLA_NOPE], pe, zq], axis=2)
    wqs = jnp.concatenate([jnp.zeros_like(uq[:, :, :MLA_NOPE]), pe[:, :, half:], pe[:, :, :half], zq],
                          axis=2)
    ukv = w_ukv.reshape(MLA_KV_RANK, H, MLA_NOPE + MLA_V)
    wk = jnp.concatenate([ukv[:, :, :MLA_NOPE],
                          jnp.zeros((MLA_KV_RANK, H, QK_PAD - MLA_NOPE), F32)], axis=2)
    wv = jnp.concatenate([ukv[:, :, MLA_NOPE:],
                          jnp.zeros((MLA_KV_RANK, H, QK_PAD - MLA_V), F32)], axis=2)
    return (w_lat, w_grp, w_gates, wq.reshape(MLA_Q_RANK, H * QK_PAD).astype(BF16),
            wqs.reshape(MLA_Q_RANK, H * QK_PAD).astype(BF16),
            wk.reshape(MLA_KV_RANK, H * QK_PAD).astype(BF16),
            wv.reshape(MLA_KV_RANK, H * QK_PAD).astype(BF16))


def kernel(x, c, positions, w_ada, b_ada, g_norm_mix, w_in, g_q_lat, g_kv_lat, w_uq, w_ukv,
           rel_bias, w_out, g_norm_ffn, w_router, w_gate, w_up, w_down, g_final):
    B, S, D = x.shape
    assert w_ada.shape[0] == 1, "the final norm is fused into the (single) layer's last kernel"
    assert DIL_GROUPS[0][1] == 1, "the gates ride along with the undilated group's projection"
    for l in range(w_ada.shape[0]):
        mod = _modulation(c, w_ada[l], b_ada[l])
        w_lat, w_grp, w_gates, wq, wqs, wk, wv = _prep_weights(w_in[l], w_uq[l], w_ukv[l])
        g_mix = g_norm_mix[l].reshape(1, D)
        n_qkv = w_grp[0].shape[1] // D
        hs = _norm_modulate(x, mod, g_mix, tuple(d for _, d in DIL_GROUPS))
        lat = _projection(hs[0], w_lat, "lat_projection", out_dtype=F32)
        main = _projection(hs[0], jnp.concatenate([w_grp[0], w_gates], axis=1), "in_projection_g0",
                           n_plain=n_qkv)
        qkvs = [main] + [_projection(hs[gi], w_grp[gi], f"in_projection_g{gi}")
                         for gi in range(1, len(DIL_GROUPS))]
        q, k, v = _mla_prep(lat.reshape(B, S, LAT_PAD), positions, g_q_lat[l].reshape(1, -1), g_kv_lat[l].reshape(1, -1),
                            wq, wqs, wk, wv)
        o_a = _flash_attention(q, k, v)
        dil_outs = [_dilated_group(qkvs[gi], positions, rel_bias, gi, window, d)
                    for gi, (window, d) in enumerate(DIL_GROUPS)]
        x1, h2, aff, aff_t = _out_projection(o_a, dil_outs, main, n_qkv, x, mod,
                                             g_norm_ffn[l].reshape(1, D),
                                             w_out[l].astype(BF16), w_router[l])
        cap = EC_CAPACITY_FACTOR * S // N_EXPERTS
        chunk = 512
        thr, tie, slot, boff = _select(aff_t, cap, chunk)
        boff_flat = boff[:, :, :S // chunk + 1].reshape(-1)
        xe, gates = _gather(boff_flat, slot, aff_t, h2, cap, chunk)
        ye = _expert_ffn(xe, gates, w_gate[l], w_up[l], w_down[l])
        pad = jnp.zeros((B, 1, LANES - N_EXPERTS), I32)
        thr_l = jnp.concatenate([thr[:, :, 0].reshape(B, 1, N_EXPERTS), pad], axis=2)
        tie_l = jnp.concatenate([tie[:, :, 0].reshape(B, 1, N_EXPERTS), pad], axis=2)
        x = _combine(boff_flat, aff, thr_l, tie_l, ye, x1, mod, g_final.reshape(1, D), tb=chunk)
    return x
```

```python
import functools
import math

import jax
import jax.numpy as jnp
from jax import lax
from jax.experimental import pallas as pl
from jax.experimental.pallas import tpu as pltpu

F32 = jnp.float32
BF16 = jnp.bfloat16
I32 = jnp.int32

MLA_HEADS = 8
MLA_Q_RANK = 384
MLA_KV_RANK = 256
MLA_NOPE = 128
MLA_ROPE = 64
MLA_V = 128
ROPE_THETA = 10000.0
DIL_GROUPS = ((128, 1), (512, 4), (2048, 16))
DIL_HEADS = 8
DIL_HEAD_DIM = 128
REL_BUCKETS = 32
REL_MAX_DIST = 1024
N_EXPERTS = 16
EC_CAPACITY_FACTOR = 2
NORM_EPS = 1e-6
NEG_INF = -1e30

LANES = 128
SUBLANES_BF16 = 16
QK_PAD = 256
LAT_PAD = 768
VMEM_LIMIT = 56 * 1024 * 1024


def _cparams(sem):
    return pltpu.CompilerParams(dimension_semantics=sem, vmem_limit_bytes=VMEM_LIMIT)


def _nt_dot(a, b):
    return lax.dot_general(a, b, (((1,), (1,)), ((), ())), preferred_element_type=F32)


def _mod_kernel(c_ref, w_ref, b_ref, o_ref):
    c = c_ref[...]
    cond = c * jax.nn.sigmoid(c)
    o_ref[...] = jnp.dot(cond, w_ref[...], preferred_element_type=F32,
                         precision=lax.Precision.HIGHEST) + b_ref[...]


def _modulation(c, w_ada, b_ada):
    B, D = c.shape
    rows = 8
    c8 = jnp.zeros((rows, D), F32).at[:B].set(c)
    n6 = w_ada.shape[1]
    out = pl.pallas_call(
        _mod_kernel,
        grid=(n6 // D,),
        in_specs=[pl.BlockSpec((rows, D), lambda j: (0, 0)),
                  pl.BlockSpec((D, D), lambda j: (0, j)),
                  pl.BlockSpec((1, D), lambda j: (0, j))],
        out_specs=pl.BlockSpec((rows, D), lambda j: (0, j)),
        out_shape=jax.ShapeDtypeStruct((rows, n6), F32),
        compiler_params=_cparams(("arbitrary",)),
        name="modulation",
    )(c8, w_ada, b_ada.reshape(1, n6))
    return out[:B].reshape(B, 6, D)


def _norm_kernel(*refs, dils):
    nx = len(refs) - 2 - len(dils)
    x_refs = refs[:nx]
    mod_ref, g_ref = refs[nx:nx + 2]
    h_refs = refs[nx + 2:]
    tm = x_refs[0].shape[1]
    for h_ref, d in zip(h_refs, dils):
        rows = tm // d
        for g in range(d):
            src = pl.ds(g, rows, stride=d) if d > 1 else slice(None)
            x = jnp.concatenate([xr[0, src, :] for xr in x_refs], axis=1)
            y = x * lax.rsqrt(jnp.mean(x * x, axis=-1, keepdims=True) + NORM_EPS) * g_ref[...]
            h = y * (1.0 + mod_ref[0, 1:2, :]) + mod_ref[0, 0:1, :]
            h_ref[0, g] = h.astype(BF16)


def _norm_modulate(x, mod, g, dils, tm=1024):
    B, S, D = x.shape
    in_specs = [pl.BlockSpec((1, tm, LANES), functools.partial(lambda b, i, j: (b, i, j), j=j))
                for j in range(D // LANES)]
    in_specs += [pl.BlockSpec((1, 6, D), lambda b, i: (b, 0, 0)),
                 pl.BlockSpec((1, D), lambda b, i: (0, 0))]
    return pl.pallas_call(
        functools.partial(_norm_kernel, dils=dils),
        grid=(B, S // tm),
        in_specs=in_specs,
        out_specs=[pl.BlockSpec((1, d, tm // d, D), lambda b, i: (b, 0, i, 0)) for d in dils],
        out_shape=[jax.ShapeDtypeStruct((B, d, S // d, D), BF16) for d in dils],
        compiler_params=_cparams(("parallel", "parallel")),
        name="norm_modulate",
    )(*([x] * (D // LANES) + [mod, g]))


def _proj_kernel(h_ref, w_ref, o_ref, *, n_plain, n_tiles):
    acc = jnp.dot(h_ref[...], w_ref[...], preferred_element_type=F32)
    if n_plain == n_tiles:
        o_ref[...] = acc.astype(o_ref.dtype)
    else:
        n = pl.program_id(1)

        @pl.when(n < n_plain)
        def _():
            o_ref[...] = acc.astype(o_ref.dtype)

        @pl.when(n >= n_plain)
        def _():
            o_ref[...] = jax.nn.sigmoid(acc).astype(o_ref.dtype)


def _projection(h, w, name, n_plain=None, out_dtype=BF16, tm=2048, tn=1024):
    B, d, L, D = h.shape
    N = w.shape[1]
    tn = min(tn, N)
    tm = min(tm, B * d * L)
    n_tiles = N // tn
    n_plain = n_tiles if n_plain is None else n_plain
    out = pl.pallas_call(
        functools.partial(_proj_kernel, n_plain=n_plain, n_tiles=n_tiles),
        grid=(B * d * L // tm, n_tiles),
        in_specs=[pl.BlockSpec((tm, D), lambda i, n: (i, 0)),
                  pl.BlockSpec((D, tn), lambda i, n: (0, n))],
        out_specs=pl.BlockSpec((tm, tn), lambda i, n: (i, n)),
        out_shape=jax.ShapeDtypeStruct((B * d * L, N), out_dtype),
        compiler_params=_cparams(("parallel", "arbitrary")),
        name=name,
    )(h.reshape(B * d * L, D), w)
    return out.reshape(B, d, L, N)


def _mla_prep_kernel(lat_ref, pos_ref, gq_ref, gkv_ref, wq_ref, wqs_ref, wk_ref, wv_ref,
                     freq_ref, sgn_ref, sel_ref, sels_ref, one_ref, q_ref, k_ref, v_ref, *, scale):
    lat = lat_ref[0]
    cq = lat[:, :MLA_Q_RANK]
    ckv = lat[:, MLA_Q_RANK:MLA_Q_RANK + MLA_KV_RANK]
    kpe = lat[:, MLA_Q_RANK + MLA_KV_RANK:]
    cqn = (cq * lax.rsqrt(jnp.mean(cq * cq, axis=-1, keepdims=True) + NORM_EPS)
           * gq_ref[...]).astype(BF16)
    ckvn = (ckv * lax.rsqrt(jnp.mean(ckv * ckv, axis=-1, keepdims=True) + NORM_EPS)
            * gkv_ref[...]).astype(BF16)
    ang = pos_ref[0].astype(F32) * freq_ref[...]
    cos = jnp.cos(ang)
    sin = jnp.sin(ang) * sgn_ref[...]
    qa = jnp.dot(cqn, wq_ref[...], preferred_element_type=F32)
    qs = jnp.dot(cqn, wqs_ref[...], preferred_element_type=F32)
    kn = jnp.dot(ckvn, wk_ref[...], preferred_element_type=F32)
    v_ref[0] = (jnp.dot(ckvn, wv_ref[...], preferred_element_type=F32) + one_ref[...]).astype(BF16)
    hp = lax.Precision.HIGHEST
    rk = (jnp.dot(kpe, sel_ref[...], preferred_element_type=F32, precision=hp) * cos
          + jnp.dot(kpe, sels_ref[...], preferred_element_type=F32, precision=hp) * sin)
    cos_q = cos * scale
    sin_q = sin * scale
    for h in range(MLA_HEADS):
        sl = slice(h * QK_PAD, (h + 1) * QK_PAD)
        q_ref[0, :, sl] = (qa[:, sl] * cos_q + qs[:, sl] * sin_q).astype(BF16)
        k_ref[0, :, sl] = (kn[:, sl] + rk).astype(BF16)


def _mla_prep(lat, positions, g_q, g_kv, wq, wqs, wk, wv, tm=512):
    B, S, _ = lat.shape
    H = MLA_HEADS
    half = MLA_ROPE // 2
    inv_freq = ROPE_THETA ** (-jnp.arange(0, MLA_ROPE, 2, dtype=F32) / MLA_ROPE)
    freq = jnp.zeros((1, QK_PAD), F32).at[0, MLA_NOPE:MLA_NOPE + MLA_ROPE].set(
        jnp.concatenate([inv_freq, inv_freq]))
    sgn = jnp.zeros((1, QK_PAD), F32).at[0, MLA_NOPE:MLA_NOPE + half].set(-1.0)
    sgn = sgn.at[0, MLA_NOPE + half:MLA_NOPE + MLA_ROPE].set(1.0)
    r = jnp.arange(MLA_ROPE)
    sel = jnp.zeros((LANES, QK_PAD), F32).at[r, MLA_NOPE + r].set(1.0)
    sels = jnp.zeros((LANES, QK_PAD), F32).at[r, MLA_NOPE + (r + half) % MLA_ROPE].set(1.0)
    scale = (MLA_NOPE + MLA_ROPE) ** -0.5 * math.log2(math.e)
    ones_col = jnp.zeros((H, QK_PAD), F32).at[:, MLA_V].set(1.0).reshape(1, H * QK_PAD)
    const = lambda b, i: (0, 0)
    return pl.pallas_call(
        functools.partial(_mla_prep_kernel, scale=scale),
        grid=(B, S // tm),
        in_specs=[pl.BlockSpec((1, tm, LAT_PAD), lambda b, i: (b, i, 0)),
                  pl.BlockSpec((1, tm, 1), lambda b, i: (b, i, 0)),
                  pl.BlockSpec((1, MLA_Q_RANK), const),
                  pl.BlockSpec((1, MLA_KV_RANK), const),
                  pl.BlockSpec((MLA_Q_RANK, H * QK_PAD), const),
                  pl.BlockSpec((MLA_Q_RANK, H * QK_PAD), const),
                  pl.BlockSpec((MLA_KV_RANK, H * QK_PAD), const),
                  pl.BlockSpec((MLA_KV_RANK, H * QK_PAD), const),
                  pl.BlockSpec((1, QK_PAD), const),
                  pl.BlockSpec((1, QK_PAD), const),
                  pl.BlockSpec((LANES, QK_PAD), const),
                  pl.BlockSpec((LANES, QK_PAD), const),
                  pl.BlockSpec((1, H * QK_PAD), const)],
        out_specs=[pl.BlockSpec((1, tm, H * QK_PAD), lambda b, i: (b, i, 0)),
                   pl.BlockSpec((1, tm, H * QK_PAD), lambda b, i: (b, i, 0)),
                   pl.BlockSpec((1, tm, H * QK_PAD), lambda b, i: (b, i, 0))],
        out_shape=[jax.ShapeDtypeStruct((B, S, H * QK_PAD), BF16),
                   jax.ShapeDtypeStruct((B, S, H * QK_PAD), BF16),
                   jax.ShapeDtypeStruct((B, S, H * QK_PAD), BF16)],
        compiler_params=_cparams(("parallel", "parallel")),
        name="mla_prep",
    )(lat, positions.reshape(B, S, 1), g_q, g_kv, wq, wqs, wk, wv, freq, sgn, sel, sels, ones_col)


def _flash_kernel(q_ref, k_ref, v_ref, o_ref, m_ref, acc_ref, s_ref, *, tq, tkc):
    S = k_ref.shape[1]
    nk = S // tkc
    nq = S // tq
    half = nk // 2
    acc_ref[...] = jnp.zeros(acc_ref.shape, F32)
    m_ref[...] = jnp.full(m_ref.shape, -jnp.inf, F32)

    def scores(qi, kj):
        qoff = pl.multiple_of(qi * tq, tq)
        koff = pl.multiple_of(kj * tkc, tkc)
        return _nt_dot(q_ref[0, pl.ds(qoff, tq), :], k_ref[0, pl.ds(koff, tkc), :])

    def consume(kj, slot, first):
        koff = pl.multiple_of(kj * tkc, tkc)
        s = s_ref[slot]
        m_prev = m_ref[...]
        if first is not None:
            m_prev = jnp.where(first, -jnp.inf, m_prev)
        m_new = jnp.maximum(m_prev, jnp.max(s, axis=-1, keepdims=True))
        alpha = jnp.exp2(m_prev - m_new)
        p = jnp.exp2(s - m_new).astype(BF16)
        acc_ref[...] = alpha * acc_ref[...] + jnp.dot(p, v_ref[0, pl.ds(koff, tkc), :],
                                                      preferred_element_type=F32)
        m_ref[...] = m_new

    s_ref[0] = scores(0, 0)

    def body(t, carry):
        qi = t // half
        kj = 2 * (t % half)
        s_ref[1] = scores(qi, kj + 1)
        consume(kj, 0, kj == 0)
        t1 = jnp.minimum(t + 1, nq * half - 1)
        s_ref[0] = scores(t1 // half, 2 * (t1 % half))
        consume(kj + 1, 1, None)

        @pl.when(kj + 2 == nk)
        def _():
            acc = acc_ref[...]
            qoff = pl.multiple_of(qi * tq, tq)
            o_ref[0, pl.ds(qoff, tq), :] = (acc[:, :MLA_V] / acc[:, MLA_V:MLA_V + 1]).astype(BF16)
        return carry

    lax.fori_loop(0, nq * half, body, 0)


def _flash_attention(q, k, v, tq=1024, tkc=1024):
    B, S, _ = q.shape
    H = MLA_HEADS
    assert (S // tkc) % 2 == 0 and S % tq == 0
    whole = pl.BlockSpec((1, S, QK_PAD), lambda b, h: (b, 0, h))
    return pl.pallas_call(
        functools.partial(_flash_kernel, tq=tq, tkc=tkc),
        grid=(B, H),
        in_specs=[whole, whole, whole],
        out_specs=pl.BlockSpec((1, S, MLA_V), lambda b, h: (b, 0, h)),
        out_shape=jax.ShapeDtypeStruct((B, S, H * MLA_V), BF16),
        scratch_shapes=[pltpu.VMEM((tq, 1), F32), pltpu.VMEM((tq, QK_PAD), F32),
                        pltpu.VMEM((2, tq, tkc), F32)],
        compiler_params=_cparams(("parallel", "parallel")),
        name="mla_flash",
    )(q, k, v)


def _t5_bucket(rel):
    nb = REL_BUCKETS // 2
    max_exact = nb // 2
    ret = jnp.where(rel > 0, nb, 0)
    n = jnp.abs(rel)
    nf = jnp.maximum(n, 1).astype(F32)
    large = max_exact + (jnp.log(nf / max_exact) / math.log(REL_MAX_DIST / max_exact)
                         * (nb - max_exact)).astype(I32)
    large = jnp.minimum(large, nb - 1)
    return ret + jnp.where(n < max_exact, n, large)


def _bias_lookup(tab_ref, h, bucket):
    rows, nk = bucket.shape
    tab = jnp.broadcast_to(tab_ref[h:h + 1, :], (rows, LANES))
    return jnp.concatenate(
        [jnp.take_along_axis(tab, bucket[:, c * LANES:(c + 1) * LANES], axis=1)
         for c in range(nk // LANES)], axis=1)


def _band_bias_kernel(tab_ref, band_ref, *, r, dil):
    _, ts, nk = band_ref.shape
    off = lax.broadcasted_iota(I32, (ts, nk), 1) - r - lax.broadcasted_iota(I32, (ts, nk), 0)
    bucket = jnp.where(jnp.abs(off) <= r, _t5_bucket(off * dil), REL_BUCKETS)
    for h in range(DIL_HEADS):
        band_ref[h] = _bias_lookup(tab_ref, h, bucket)


def _band_bias(tab, r, dil, ts):
    nk = ts + 2 * r
    return pl.pallas_call(
        functools.partial(_band_bias_kernel, r=r, dil=dil),
        grid=(1,),
        in_specs=[pl.BlockSpec((DIL_HEADS, LANES), lambda i: (0, 0))],
        out_specs=pl.BlockSpec((DIL_HEADS, ts, nk), lambda i: (0, 0, 0)),
        out_shape=jax.ShapeDtypeStruct((DIL_HEADS, ts, nk), F32),
        compiler_params=_cparams(("arbitrary",)),
        name=f"band_bias_d{dil}",
    )(tab)


def _dilated_kernel(q_ref, kp_ref, kc_ref, kn_ref, vp_ref, vc_ref, vn_ref, pq_ref, pk_ref, tab_ref,
                    band_ref, o_ref, l_ref, kcat_ref, vcat_ref, s_scr, *, r, tq, ts, n_rows, dil):
    n = pl.program_id(2)
    nsub = tq // ts
    nk = ts + 2 * r
    kcat_ref[0:r, :] = kp_ref[0, 0]
    kcat_ref[r:r + tq, :] = kc_ref[0, 0]
    kcat_ref[r + tq:tq + 2 * r, :] = kn_ref[0, 0]
    ones = jnp.ones((tq + 2 * r, DIL_HEAD_DIM), BF16)
    for h in range(DIL_HEADS):
        sl = slice(h * DIL_HEAD_DIM, (h + 1) * DIL_HEAD_DIM)
        dst = slice(2 * h * DIL_HEAD_DIM, (2 * h + 1) * DIL_HEAD_DIM)
        vcat_ref[0:r, dst] = vp_ref[0, 0, :, sl]
        vcat_ref[r:r + tq, dst] = vc_ref[0, 0, :, sl]
        vcat_ref[r + tq:tq + 2 * r, dst] = vn_ref[0, 0, :, sl]
        vcat_ref[:, (2 * h + 1) * DIL_HEAD_DIM:(2 * h + 2) * DIL_HEAD_DIM] = ones
    lane_grp = lax.broadcasted_iota(I32, (ts, LANES), 1) // (LANES // DIL_HEADS)
    scale = DIL_HEAD_DIM ** -0.5 * math.log2(math.e)
    for j in range(nsub):
        qi = n * tq + j * ts + lax.broadcasted_iota(I32, (ts, nk), 0)
        kj = n * tq + j * ts - r + lax.broadcasted_iota(I32, (ts, nk), 1)
        inside = (kj >= 0) & (kj < n_rows)
        valid = (jnp.abs(kj - qi) <= r) & inside
        rel = pk_ref[0, 0, j] - pq_ref[0, 0, j * ts:(j + 1) * ts, :]
        regular = jnp.max(jnp.where(valid & (rel != (kj - qi) * dil), 1, 0)) == 0

        def logits(h):
            sl = slice(h * DIL_HEAD_DIM, (h + 1) * DIL_HEAD_DIM)
            return _nt_dot(q_ref[0, 0, j * ts:(j + 1) * ts, sl], kcat_ref[j * ts:j * ts + nk, sl]) * scale

        @pl.when(regular)
        def _():
            for h in range(DIL_HEADS):
                s_scr[j * DIL_HEADS + h] = jnp.where(inside, logits(h) + band_ref[h], NEG_INF)

        @pl.when(jnp.logical_not(regular))
        def _():
            bucket = jnp.where(valid, _t5_bucket(rel), REL_BUCKETS)
            for h in range(DIL_HEADS):
                s_scr[j * DIL_HEADS + h] = logits(h) + _bias_lookup(tab_ref, h, bucket)
    for j in range(nsub):
        lse_tile = jnp.zeros((ts, LANES), F32)
        for h in range(DIL_HEADS):
            sl = slice(h * DIL_HEAD_DIM, (h + 1) * DIL_HEAD_DIM)
            s = s_scr[j * DIL_HEADS + h]
            m = jnp.max(s, axis=-1, keepdims=True)
            p = jnp.exp2(s - m).astype(BF16)
            res = jnp.dot(p, vcat_ref[j * ts:j * ts + nk,
                                      2 * h * DIL_HEAD_DIM:(2 * h + 2) * DIL_HEAD_DIM],
                          preferred_element_type=F32)
            den = res[:, DIL_HEAD_DIM:]
            o_ref[0, 0, j * ts:(j + 1) * ts, sl] = (res[:, :DIL_HEAD_DIM] / den).astype(BF16)
            lse_tile = jnp.where(lane_grp == h, m * math.log(2.0) + jnp.log(den), lse_tile)
        l_ref[0, 0, j * ts:(j + 1) * ts, :] = lse_tile


def _dilated_group(qkv, positions, table, gi, window, d, tq=512, ts=128):
    B, _, L, _ = qkv.shape
    HD = DIL_HEADS * DIL_HEAD_DIM
    r = window // (2 * d)
    tq = min(tq, L)
    nt = L // tq
    nsub = tq // ts
    rb = tq // r
    nk = ts + 2 * r
    pcls = positions.reshape(B, L, d).transpose(0, 2, 1)
    pq = pcls.reshape(B, d, L, 1)
    ppad = jnp.pad(pcls, ((0, 0), (0, 0), (r, ts + r)))
    pk = jnp.concatenate([ppad[:, :, :L].reshape(B, d, L // ts, ts),
                          ppad[:, :, ts:ts + L].reshape(B, d, L // ts, ts)[..., :2 * r]], axis=-1)
    pk = pk.reshape(B, d, L // ts, 1, nk)
    tab = jnp.zeros((DIL_HEADS, LANES), F32).at[:, :REL_BUCKETS].set(
        table[:, gi * DIL_HEADS:(gi + 1) * DIL_HEADS].T * math.log2(math.e))
    tab = tab.at[:, REL_BUCKETS].set(NEG_INF)

    def cur(c):
        return pl.BlockSpec((1, 1, tq, HD), lambda b, g, n: (b, g, n, c))

    def prv(c):
        return pl.BlockSpec((1, 1, r, HD), lambda b, g, n: (b, g, jnp.maximum(n * rb - 1, 0), c))

    def nxt(c):
        return pl.BlockSpec((1, 1, r, HD),
                            lambda b, g, n: (b, g, jnp.minimum((n + 1) * rb, L // r - 1), c))

    return pl.pallas_call(
        functools.partial(_dilated_kernel, r=r, tq=tq, ts=ts, n_rows=L, dil=d),
        grid=(B, d, nt),
        in_specs=[cur(0), prv(1), cur(1), nxt(1), prv(2), cur(2), nxt(2),
                  pl.BlockSpec((1, 1, tq, 1), lambda b, g, n: (b, g, n, 0)),
                  pl.BlockSpec((1, 1, nsub, 1, nk), lambda b, g, n: (b, g, n, 0, 0)),
                  pl.BlockSpec((DIL_HEADS, LANES), lambda b, g, n: (0, 0)),
                  pl.BlockSpec((DIL_HEADS, ts, nk), lambda b, g, n: (0, 0, 0))],
        out_specs=[pl.BlockSpec((1, 1, tq, HD), lambda b, g, n: (b, g, n, 0)),
                   pl.BlockSpec((1, 1, tq, LANES), lambda b, g, n: (b, g, n, 0))],
        out_shape=[jax.ShapeDtypeStruct((B, d, L, HD), BF16),
                   jax.ShapeDtypeStruct((B, d, L, LANES), F32)],
        scratch_shapes=[pltpu.VMEM((tq + 2 * r, HD), BF16), pltpu.VMEM((tq + 2 * r, 2 * HD), BF16),
                        pltpu.VMEM((nsub * DIL_HEADS, ts, nk), F32)],
        compiler_params=_cparams(("parallel", "parallel", "parallel")),
        name=f"dilated_g{gi}",
    )(*([qkv] * 7 + [pq, pk, tab, _band_bias(tab, r, d, ts)]))


def _outproj_kernel(*refs, dils):
    ng = len(dils)
    oa_ref = refs[0]
    og_refs = refs[1:1 + ng]
    lg_refs = refs[1 + ng:1 + 2 * ng]
    (ga_ref, gb_ref, x_ref, mod_ref, g_ref, wo_ref, wr_ref,
     x1_ref, h2_ref, aff_ref, afft_ref, o_scr, l_scr) = refs[1 + 2 * ng:]
    tm = x_ref.shape[1]
    for i, d in enumerate(dils):
        rows = tm // d
        for g in range(d):
            dst = pl.ds(g, rows, stride=d) if d > 1 else slice(None)
            og = og_refs[i][0, g].astype(F32)
            for h in range(DIL_HEADS):
                o_scr[i, h, dst, :] = og[:, h * DIL_HEAD_DIM:(h + 1) * DIL_HEAD_DIM]
            l_scr[i, dst, :] = lg_refs[i][0, g]
    lses = [l_scr[i] for i in range(ng)]
    mx = functools.reduce(jnp.maximum, lses)
    es = [jnp.exp(l - mx) for l in lses]
    tot = functools.reduce(lambda a, b: a + b, es)
    wts = [e / tot for e in es]
    cols = []
    for h in range(DIL_HEADS):
        c = h * (LANES // DIL_HEADS)
        cols.append(functools.reduce(
            lambda a, b: a + b, [wts[i][:, c:c + 1] * o_scr[i, h] for i in range(ng)]))
    o_b = jnp.concatenate(cols, axis=1)
    comb = (ga_ref[0, 0].astype(F32) * oa_ref[0].astype(F32)
            + gb_ref[0, 0].astype(F32) * o_b).astype(BF16)
    x1 = x_ref[0] + mod_ref[0, 2:3, :] * jnp.dot(comb, wo_ref[...], preferred_element_type=F32)
    x1_ref[0] = x1
    y = x1 * lax.rsqrt(jnp.mean(x1 * x1, axis=-1, keepdims=True) + NORM_EPS) * g_ref[...]
    h2 = y * (1.0 + mod_ref[0, 4:5, :]) + mod_ref[0, 3:4, :]
    h2_ref[0] = h2.astype(BF16)
    logits = jnp.dot(h2, wr_ref[...], preferred_element_type=F32,
                     precision=lax.Precision.HIGHEST)
    lane = lax.broadcasted_iota(I32, logits.shape, 1)
    logits = jnp.where(lane < N_EXPERTS, logits, -jnp.inf)
    e = jnp.exp(logits - jnp.max(logits, axis=-1, keepdims=True))
    aff = e / jnp.sum(e, axis=-1, keepdims=True)
    aff_ref[0] = aff
    afft_ref[0, 0] = aff.T[:N_EXPERTS, :]


def _out_projection(o_a, dil_outs, main, gate_col, x, mod, g, w_out, w_router, tm=512):
    B, S, D = x.shape
    const = lambda b, i: (0, 0)
    row = lambda b, i: (b, i, 0)
    wr = jnp.zeros((D, LANES), F32).at[:, :N_EXPERTS].set(w_router)
    dils = tuple(o.shape[1] for o, _ in dil_outs)
    og_specs = [pl.BlockSpec((1, d, tm // d, D), lambda b, i: (b, 0, i, 0)) for d in dils]
    lg_specs = [pl.BlockSpec((1, d, tm // d, LANES), lambda b, i: (b, 0, i, 0)) for d in dils]
    return pl.pallas_call(
        functools.partial(_outproj_kernel, dils=dils),
        grid=(B, S // tm),
        in_specs=[pl.BlockSpec((1, tm, D), row)] + og_specs + lg_specs + [
                  pl.BlockSpec((1, 1, tm, D), lambda b, i: (b, 0, i, gate_col)),
                  pl.BlockSpec((1, 1, tm, D), lambda b, i: (b, 0, i, gate_col + 1)),
                  pl.BlockSpec((1, tm, D), row),
                  pl.BlockSpec((1, 6, D), lambda b, i: (b, 0, 0)),
                  pl.BlockSpec((1, D), const),
                  pl.BlockSpec((D, D), const),
                  pl.BlockSpec((D, LANES), const)],
        out_specs=[pl.BlockSpec((1, tm, D), row),
                   pl.BlockSpec((1, tm, D), row),
                   pl.BlockSpec((1, tm, LANES), row),
                   pl.BlockSpec((1, 1, N_EXPERTS, tm), lambda b, i: (b, i, 0, 0))],
        out_shape=[jax.ShapeDtypeStruct((B, S, D), F32),
                   jax.ShapeDtypeStruct((B, S, D), BF16),
                   jax.ShapeDtypeStruct((B, S, LANES), F32),
                   jax.ShapeDtypeStruct((B, S // tm, N_EXPERTS, tm), F32)],
        scratch_shapes=[pltpu.VMEM((len(dils), DIL_HEADS, tm, DIL_HEAD_DIM), F32),
                        pltpu.VMEM((len(dils), tm, LANES), F32)],
        compiler_params=_cparams(("parallel", "parallel")),
        name="out_projection",
    )(o_a, *[o for o, _ in dil_outs], *[l for _, l in dil_outs], main, main, x, mod, g, w_out, wr)


def _select_kernel(afft_ref, thr_ref, tie_ref, slot_ref, boff_ref, *, cap, chunk):
    aff = jnp.concatenate([afft_ref[0, c] for c in range(afft_ref.shape[1])], axis=1)
    E, S = aff.shape
    bits = lax.bitcast_convert_type(aff, I32)

    def count(mask):
        return jnp.sum(mask.astype(I32), axis=1, keepdims=True)

    def thr_step(i, v):
        cand = v | jnp.left_shift(jnp.int32(1), 30 - i)
        return jnp.where(count(bits >= cand) >= cap, cand, v)

    thr = lax.fori_loop(0, 31, thr_step, jnp.zeros((E, 1), I32))
    need = cap - count(bits > thr)
    eq = bits == thr
    idx = lax.broadcasted_iota(I32, (E, S), 1)
    nbits = max(1, (S - 1).bit_length())

    def tie_step(i, j):
        cand = j | jnp.left_shift(jnp.int32(1), nbits - 1 - i)
        return jnp.where(count(eq & (idx < cand)) < need, cand, j)

    tie = lax.fori_loop(0, nbits, tie_step, jnp.zeros((E, 1), I32))
    sel = (bits > thr) | (eq & (idx <= tie))
    thr_ref[0] = jnp.broadcast_to(thr, (E, LANES))
    tie_ref[0] = jnp.broadcast_to(tie, (E, LANES))

    upper = (lax.broadcasted_iota(I32, (chunk, chunk), 0)
             <= lax.broadcasted_iota(I32, (chunk, chunk), 1)).astype(BF16)
    carry = jnp.zeros((E, 1), F32)
    self_ = sel.astype(F32)
    lane = lax.broadcasted_iota(I32, (E, LANES), 1)
    boff = jnp.zeros((E, LANES), I32)
    for c in range(S // chunk):
        boff = jnp.where(lane == c, carry.astype(I32), boff)
        scf = self_[:, c * chunk:(c + 1) * chunk]
        incl = jnp.dot(scf.astype(BF16), upper, preferred_element_type=F32)
        pos = (incl - scf + carry).astype(I32)
        slot_ref[0, c] = jnp.where(scf > 0.0, pos, -1)
        carry = carry + incl[:, chunk - 1:chunk]
    boff_ref[0] = jnp.where(lane == S // chunk, carry.astype(I32), boff)


def _select(aff_t, cap, chunk):
    B, nblk, E, _ = aff_t.shape
    assert aff_t.shape[3] == chunk and nblk < LANES
    rows = pl.BlockSpec((1, nblk, E, chunk), lambda b: (b, 0, 0, 0))
    return pl.pallas_call(
        functools.partial(_select_kernel, cap=cap, chunk=chunk),
        grid=(B,),
        in_specs=[rows],
        out_specs=[pl.BlockSpec((1, E, LANES), lambda b: (b, 0, 0)),
                   pl.BlockSpec((1, E, LANES), lambda b: (b, 0, 0)),
                   rows,
                   pl.BlockSpec((1, E, LANES), lambda b: (b, 0, 0))],
        out_shape=[jax.ShapeDtypeStruct((B, E, LANES), I32),
                   jax.ShapeDtypeStruct((B, E, LANES), I32),
                   jax.ShapeDtypeStruct((B, nblk, E, chunk), I32),
                   jax.ShapeDtypeStruct((B, E, LANES), I32)],
        compiler_params=_cparams(("parallel",)),
        name="ec_select",
    )(aff_t)


def _gather_kernel(boff_ref, slot_ref, afft_ref, h_ref, xe_ref, gate_ref, acc_ref, gacc_ref, *, win):
    _, nblk, _, tbk = slot_ref.shape
    cap = xe_ref.shape[2]
    e = pl.program_id(1)
    base = (pl.program_id(0) * pl.num_programs(1) + e) * (nblk + 1)
    acc_ref[...] = jnp.zeros(acc_ref.shape, F32)
    gacc_ref[...] = jnp.zeros(gacc_ref.shape, F32)

    def chunk(tb, carry):
        lo = boff_ref[base + tb]
        hi = boff_ref[base + tb + 1]
        first = (lo // 8) * 8
        off = pl.multiple_of(tb * tbk, tbk)
        slot_row = slot_ref[0, tb, pl.ds(e, 1), :]
        aff_row = afft_ref[0, tb, pl.ds(e, 1), :]

        def window(k, c):
            w0 = pl.multiple_of(first + k * win, 8)
            hit = lax.broadcasted_iota(I32, (win, tbk), 0) + w0 == slot_row
            acc_ref[pl.ds(w0, win), :] += jnp.dot(hit.astype(BF16), h_ref[0, pl.ds(off, tbk), :],
                                                  preferred_element_type=F32)
            gacc_ref[pl.ds(w0, win), :] += jnp.sum(jnp.where(hit, aff_row, 0.0),
                                                   axis=1, keepdims=True)
            return c

        lax.fori_loop(0, (hi - first + win - 1) // win, window, 0)
        return carry

    lax.fori_loop(0, nblk, chunk, 0)
    xe_ref[0, 0] = acc_ref[0:cap, :].astype(BF16)
    gate_ref[0, 0] = gacc_ref[0:cap, :]


def _gather(boff_flat, slot, aff_t, h2, cap, chunk, win=128):
    B, nblk, E, _ = slot.shape
    S, D = h2.shape[1:]
    rowblk = pl.BlockSpec((1, nblk, E, chunk), lambda b, e, off: (b, 0, 0, 0))
    return pl.pallas_call(
        functools.partial(_gather_kernel, win=win),
        grid_spec=pltpu.PrefetchScalarGridSpec(
            num_scalar_prefetch=1,
            grid=(B, E),
            in_specs=[rowblk, rowblk,
                      pl.BlockSpec((1, S, D), lambda b, e, off: (b, 0, 0))],
            out_specs=[pl.BlockSpec((1, 1, cap, D), lambda b, e, off: (b, e, 0, 0)),
                       pl.BlockSpec((1, 1, cap, 1), lambda b, e, off: (b, e, 0, 0))],
            scratch_shapes=[pltpu.VMEM((cap + win, D), F32), pltpu.VMEM((cap + win, 1), F32)]),
        out_shape=[jax.ShapeDtypeStruct((B, E, cap, D), BF16),
                   jax.ShapeDtypeStruct((B, E, cap, 1), F32)],
        compiler_params=_cparams(("parallel", "arbitrary")),
        name="ec_gather",
    )(boff_flat, slot, aff_t, h2)


def _ffn_kernel(xe_ref, gate_ref, wg_ref, wu_ref, wd_ref, ye_ref, acc_ref):
    f = pl.program_id(2)
    nb, _, cap, D = xe_ref.shape

    @pl.when((pl.program_id(0) == 0) & (pl.program_id(1) == 0) & (f == 0))
    def _():
        acc_ref[...] = jnp.zeros(acc_ref.shape, F32)

    x = xe_ref[...].reshape(nb * cap, D)
    g = jnp.dot(x, wg_ref[0].astype(BF16), preferred_element_type=F32)
    u = jnp.dot(x, wu_ref[0].astype(BF16), preferred_element_type=F32)
    hid = (g * jax.nn.sigmoid(g) * u).astype(BF16)
    part = jnp.dot(hid, wd_ref[0].astype(BF16), preferred_element_type=F32)
    acc_ref[...] = jnp.where(f == 0, 0.0, acc_ref[...]) + part

    @pl.when(f == pl.num_programs(2) - 1)
    def _():
        gate = gate_ref[...].reshape(nb * cap, 1)
        ye_ref[...] = (acc_ref[...] * gate).astype(BF16).reshape(ye_ref.shape)


def _expert_ffn(xe, gates, w_gate, w_up, w_down, nb=2, tf=512):
    B, E, cap, D = xe.shape
    F = w_gate.shape[-1]
    return pl.pallas_call(
        _ffn_kernel,
        grid=(E, B // nb, F // tf),
        in_specs=[pl.BlockSpec((nb, 1, cap, D), lambda e, b, f: (b, e, 0, 0)),
                  pl.BlockSpec((nb, 1, cap, 1), lambda e, b, f: (b, e, 0, 0)),
                  pl.BlockSpec((1, D, tf), lambda e, b, f: (e, 0, f)),
                  pl.BlockSpec((1, D, tf), lambda e, b, f: (e, 0, f)),
                  pl.BlockSpec((1, tf, D), lambda e, b, f: (e, f, 0))],
        out_specs=pl.BlockSpec((nb, 1, cap, D), lambda e, b, f: (b, e, 0, 0)),
        out_shape=jax.ShapeDtypeStruct((B, E, cap, D), BF16),
        scratch_shapes=[pltpu.VMEM((nb * cap, D), F32)],
        compiler_params=_cparams(("arbitrary", "arbitrary", "arbitrary")),
        name="ec_ffn",
    )(xe, gates, w_gate, w_up, w_down)


def _combine_kernel(boff_ref, aff_ref, thr_ref, tie_ref, ye_ref, x1_ref, mod_ref, g_ref, out_ref,
                    slot_ref, carry_ref, *, tb, wslot, nblk):
    b = pl.program_id(0)
    t2 = pl.program_id(1)
    e = pl.program_id(2)
    nsub = aff_ref.shape[1] // tb
    cap = ye_ref.shape[2]
    lane = lax.broadcasted_iota(I32, (tb, LANES), 1)

    @pl.when(e == 0)
    def _():
        @pl.when(t2 == 0)
        def _():
            carry_ref[...] = jnp.zeros(carry_ref.shape, F32)

        lower = (lax.broadcasted_iota(I32, (tb, tb), 0)
                 >= lax.broadcasted_iota(I32, (tb, tb), 1)).astype(BF16)
        for u in range(nsub):
            rows = slice(u * tb, (u + 1) * tb)
            bits = lax.bitcast_convert_type(aff_ref[0, rows, :], I32)
            tok = (t2 * nsub + u) * tb + lax.broadcasted_iota(I32, (tb, LANES), 0)
            thr = thr_ref[0]
            sel = ((bits > thr) | ((bits == thr) & (tok <= tie_ref[0]))) & (lane < N_EXPERTS)
            self_ = sel.astype(F32)
            incl = jnp.dot(lower, self_.astype(BF16), preferred_element_type=F32)
            pos = (incl - self_ + carry_ref[...]).astype(I32)
            slot_ref[rows, :] = jnp.where(sel, pos, -1)
            carry_ref[...] = carry_ref[...] + incl[tb - 1:tb, :]
        out_ref[...] = jnp.zeros(out_ref.shape, F32)

    pick = lane == e
    base = (b * pl.num_programs(2) + e) * (nblk + 1) + t2 * nsub
    for u in range(nsub):
        rows = slice(u * tb, (u + 1) * tb)
        lo = boff_ref[base + u]
        hi = boff_ref[base + u + 1]
        slot_e = jnp.sum(jnp.where(pick, slot_ref[rows, :], 0), axis=1, keepdims=True)

        first = (lo // SUBLANES_BF16) * SUBLANES_BF16

        def body(k, carry):
            start = first + k * wslot
            woff = pl.multiple_of(jnp.minimum(start, cap - wslot), SUBLANES_BF16)
            todo = jnp.where(slot_e >= start, slot_e, -1)
            onehot = (lax.broadcasted_iota(I32, (tb, wslot), 1) + woff == todo).astype(BF16)
            out_ref[0, rows, :] += jnp.dot(onehot, ye_ref[0, 0, pl.ds(woff, wslot), :],
                                           preferred_element_type=F32)
            return carry

        lax.fori_loop(0, (hi - first + wslot - 1) // wslot, body, 0)

    @pl.when(e == pl.num_programs(2) - 1)
    def _():
        for u in range(nsub):
            rows = slice(u * tb, (u + 1) * tb)
            x2 = x1_ref[0, rows, :] + mod_ref[0, 5:6, :] * out_ref[0, rows, :]
            out_ref[0, rows, :] = (x2 * lax.rsqrt(jnp.mean(x2 * x2, axis=-1, keepdims=True)
                                                  + NORM_EPS) * g_ref[...])


def _combine(boff_flat, aff, thr, tie, ye, x1, mod, g_final, tb, tsup=2048, wslot=128):
    B, S, D = x1.shape
    E, cap = ye.shape[1], ye.shape[2]
    tsup = min(tsup, S)
    return pl.pallas_call(
        functools.partial(_combine_kernel, tb=tb, wslot=wslot, nblk=S // tb),
        grid_spec=pltpu.PrefetchScalarGridSpec(
            num_scalar_prefetch=1,
            grid=(B, S // tsup, E),
            in_specs=[pl.BlockSpec((1, tsup, LANES), lambda b, t, e, off: (b, t, 0)),
                      pl.BlockSpec((1, 1, LANES), lambda b, t, e, off: (b, 0, 0)),
                      pl.BlockSpec((1, 1, LANES), lambda b, t, e, off: (b, 0, 0)),
                      pl.BlockSpec((1, 1, cap, D), lambda b, t, e, off: (b, e, 0, 0)),
                      pl.BlockSpec((1, tsup, D), lambda b, t, e, off: (b, t, 0)),
                      pl.BlockSpec((1, 6, D), lambda b, t, e, off: (b, 0, 0)),
                      pl.BlockSpec((1, D), lambda b, t, e, off: (0, 0))],
            out_specs=pl.BlockSpec((1, tsup, D), lambda b, t, e, off: (b, t, 0)),
            scratch_shapes=[pltpu.VMEM((tsup, LANES), I32), pltpu.VMEM((1, LANES), F32)]),
        out_shape=jax.ShapeDtypeStruct((B, S, D), F32),
        compiler_params=_cparams(("parallel", "arbitrary", "arbitrary")),
        name="ec_combine",
    )(boff_flat, aff, thr, tie, ye, x1, mod, g_final)


def _prep_weights(w_in, w_uq, w_ukv):
    D = w_in.shape[0]
    H = MLA_HEADS
    n_lat = MLA_Q_RANK + MLA_KV_RANK + MLA_ROPE
    n_grp = 3 * DIL_HEADS * DIL_HEAD_DIM
    n_dil = len(DIL_GROUPS) * n_grp
    w_lat = jnp.zeros((D, LAT_PAD), F32).at[:, :n_lat].set(w_in[:, :n_lat]).astype(BF16)
    w_grp = [w_in[:, n_lat + i * n_grp:n_lat + (i + 1) * n_grp].astype(BF16)
             for i in range(len(DIL_GROUPS))]
    w_gates = w_in[:, n_lat + n_dil:].astype(BF16)
    half = MLA_ROPE // 2
    uq = w_uq.reshape(MLA_Q_RANK, H, MLA_NOPE + MLA_ROPE)
    pe = uq[:, :, MLA_NOPE:]
    zq = jnp.zeros((MLA_Q_RANK, H, QK_PAD - MLA_NOPE - MLA_ROPE), F32)
    wq = jnp.concatenate([uq[:, :, :MLA_NOPE], pe, zq], axis=2)
    wqs = jnp.concatenate([jnp.zeros_like(uq[:, :, :MLA_NOPE]), pe[:, :, half:], pe[:, :, :half], zq],
                          axis=2)
    ukv = w_ukv.reshape(MLA_KV_RANK, H, MLA_NOPE + MLA_V)
    wk = jnp.concatenate([ukv[:, :, :MLA_NOPE],
                          jnp.zeros((MLA_KV_RANK, H, QK_PAD - MLA_NOPE), F32)], axis=2)
    wv = jnp.concatenate([ukv[:, :, MLA_NOPE:],
                          jnp.zeros((MLA_KV_RANK, H, QK_PAD - MLA_V), F32)], axis=2)
    return (w_lat, w_grp, w_gates, wq.reshape(MLA_Q_RANK, H * QK_PAD).astype(BF16),
            wqs.reshape(MLA_Q_RANK, H * QK_PAD).astype(BF16),
            wk.reshape(MLA_KV_RANK, H * QK_PAD).astype(BF16),
            wv.reshape(MLA_KV_RANK, H * QK_PAD).astype(BF16))


def kernel(x, c, positions, w_ada, b_ada, g_norm_mix, w_in, g_q_lat, g_kv_lat, w_uq, w_ukv,
           rel_bias, w_out, g_norm_ffn, w_router, w_gate, w_up, w_down, g_final):
    B, S, D = x.shape
    assert w_ada.shape[0] == 1, "the final norm is fused into the (single) layer's last kernel"
    assert DIL_GROUPS[0][1] == 1, "the gates ride along with the undilated group's projection"
    for l in range(w_ada.shape[0]):
        mod = _modulation(c, w_ada[l], b_ada[l])
        w_lat, w_grp, w_gates, wq, wqs, wk, wv = _prep_weights(w_in[l], w_uq[l], w_ukv[l])
        g_mix = g_norm_mix[l].reshape(1, D)
        n_qkv = w_grp[0].shape[1] // D
        hs = _norm_modulate(x, mod, g_mix, tuple(d for _, d in DIL_GROUPS))
        lat = _projection(hs[0], w_lat, "lat_projection", out_dtype=F32)
        main = _projection(hs[0], jnp.concatenate([w_grp[0], w_gates], axis=1), "in_projection_g0",
                           n_plain=n_qkv)
        qkvs = [main] + [_projection(hs[gi], w_grp[gi], f"in_projection_g{gi}")
                         for gi in range(1, len(DIL_GROUPS))]
        q, k, v = _mla_prep(lat.reshape(B, S, LAT_PAD), positions, g_q_lat[l].reshape(1, -1), g_kv_lat[l].reshape(1, -1),
                            wq, wqs, wk, wv)
        o_a = _flash_attention(q, k, v)
        dil_outs = [_dilated_group(qkvs[gi], positions, rel_bias, gi, window, d)
                    for gi, (window, d) in enumerate(DIL_GROUPS)]
        x1, h2, aff, aff_t = _out_projection(o_a, dil_outs, main, n_qkv, x, mod,
                                             g_norm_ffn[l].reshape(1, D),
                                             w_out[l].astype(BF16), w_router[l])
        cap = EC_CAPACITY_FACTOR * S // N_EXPERTS
        chunk = 512
        thr, tie, slot, boff = _select(aff_t, cap, chunk)
        boff_flat = boff[:, :, :S // chunk + 1].reshape(-1)
        xe, gates = _gather(boff_flat, slot, aff_t, h2, cap, chunk)
        ye = _expert_ffn(xe, gates, w_gate[l], w_up[l], w_down[l])
        pad = jnp.zeros((B, 1, LANES - N_EXPERTS), I32)
        thr_l = jnp.concatenate([thr[:, :, 0].reshape(B, 1, N_EXPERTS), pad], axis=2)
        tie_l = jnp.concatenate([tie[:, :, 0].reshape(B, 1, N_EXPERTS), pad], axis=2)
        x = _combine(boff_flat, aff, thr_l, tie_l, ye, x1, mod, g_final.reshape(1, D), tb=chunk)
    return x
```

```python
import functools
import math

import jax
import jax.numpy as jnp
from jax import lax
from jax.experimental import pallas as pl
from jax.experimental.pallas import tpu as pltpu

F32 = jnp.float32
BF16 = jnp.bfloat16
I32 = jnp.int32

MLA_HEADS = 8
MLA_Q_RANK = 384
MLA_KV_RANK = 256
MLA_NOPE = 128
MLA_ROPE = 64
MLA_V = 128
ROPE_THETA = 10000.0
DIL_GROUPS = ((128, 1), (512, 4), (2048, 16))
DIL_HEADS = 8
DIL_HEAD_DIM = 128
REL_BUCKETS = 32
REL_MAX_DIST = 1024
N_EXPERTS = 16
EC_CAPACITY_FACTOR = 2
NORM_EPS = 1e-6
NEG_INF = -1e30

LANES = 128
SUBLANES_BF16 = 16
QK_PAD = 256
LAT_PAD = 768
VMEM_LIMIT = 56 * 1024 * 1024


def _cparams(sem):
    return pltpu.CompilerParams(dimension_semantics=sem, vmem_limit_bytes=VMEM_LIMIT)


def _nt_dot(a, b):
    return lax.dot_general(a, b, (((1,), (1,)), ((), ())), preferred_element_type=F32)


def _mod_kernel(c_ref, w_ref, b_ref, o_ref):
    c = c_ref[...]
    cond = c * jax.nn.sigmoid(c)
    o_ref[...] = jnp.dot(cond, w_ref[...], preferred_element_type=F32,
                         precision=lax.Precision.HIGHEST) + b_ref[...]


def _modulation(c, w_ada, b_ada):
    B, D = c.shape
    rows = 8
    c8 = jnp.zeros((rows, D), F32).at[:B].set(c)
    n6 = w_ada.shape[1]
    out = pl.pallas_call(
        _mod_kernel,
        grid=(n6 // D,),
        in_specs=[pl.BlockSpec((rows, D), lambda j: (0, 0)),
                  pl.BlockSpec((D, D), lambda j: (0, j)),
                  pl.BlockSpec((1, D), lambda j: (0, j))],
        out_specs=pl.BlockSpec((rows, D), lambda j: (0, j)),
        out_shape=jax.ShapeDtypeStruct((rows, n6), F32),
        compiler_params=_cparams(("arbitrary",)),
        name="modulation",
    )(c8, w_ada, b_ada.reshape(1, n6))
    return out[:B].reshape(B, 6, D)


def _norm_kernel(*refs, dils):
    nx = len(refs) - 2 - len(dils)
    x_refs = refs[:nx]
    mod_ref, g_ref = refs[nx:nx + 2]
    h_refs = refs[nx + 2:]
    tm = x_refs[0].shape[1]
    for h_ref, d in zip(h_refs, dils):
        rows = tm // d
        for g in range(d):
            src = pl.ds(g, rows, stride=d) if d > 1 else slice(None)
            x = jnp.concatenate([xr[0, src, :] for xr in x_refs], axis=1)
            y = x * lax.rsqrt(jnp.mean(x * x, axis=-1, keepdims=True) + NORM_EPS) * g_ref[...]
            h = y * (1.0 + mod_ref[0, 1:2, :]) + mod_ref[0, 0:1, :]
            h_ref[0, g] = h.astype(BF16)


def _norm_modulate(x, mod, g, dils, tm=1024):
    B, S, D = x.shape
    in_specs = [pl.BlockSpec((1, tm, LANES), functools.partial(lambda b, i, j: (b, i, j), j=j))
                for j in range(D // LANES)]
    in_specs += [pl.BlockSpec((1, 6, D), lambda b, i: (b, 0, 0)),
                 pl.BlockSpec((1, D), lambda b, i: (0, 0))]
    return pl.pallas_call(
        functools.partial(_norm_kernel, dils=dils),
        grid=(B, S // tm),
        in_specs=in_specs,
        out_specs=[pl.BlockSpec((1, d, tm // d, D), lambda b, i: (b, 0, i, 0)) for d in dils],
        out_shape=[jax.ShapeDtypeStruct((B, d, S // d, D), BF16) for d in dils],
        compiler_params=_cparams(("parallel", "parallel")),
        name="norm_modulate",
    )(*([x] * (D // LANES) + [mod, g]))


def _proj_kernel(h_ref, w_ref, o_ref, *, n_plain, n_tiles):
    acc = jnp.dot(h_ref[...], w_ref[...], preferred_element_type=F32)
    if n_plain == n_tiles:
        o_ref[...] = acc.astype(o_ref.dtype)
    else:
        n = pl.program_id(1)

        @pl.when(n < n_plain)
        def _():
            o_ref[...] = acc.astype(o_ref.dtype)

        @pl.when(n >= n_plain)
        def _():
            o_ref[...] = jax.nn.sigmoid(acc).astype(o_ref.dtype)


def _projection(h, w, name, n_plain=None, out_dtype=BF16, tm=2048, tn=1024):
    B, d, L, D = h.shape
    N = w.shape[1]
    tn = min(tn, N)
    tm = min(tm, B * d * L)
    n_tiles = N // tn
    n_plain = n_tiles if n_plain is None else n_plain
    out = pl.pallas_call(
        functools.partial(_proj_kernel, n_plain=n_plain, n_tiles=n_tiles),
        grid=(B * d * L // tm, n_tiles),
        in_specs=[pl.BlockSpec((tm, D), lambda i, n: (i, 0)),
                  pl.BlockSpec((D, tn), lambda i, n: (0, n))],
        out_specs=pl.BlockSpec((tm, tn), lambda i, n: (i, n)),
        out_shape=jax.ShapeDtypeStruct((B * d * L, N), out_dtype),
        compiler_params=_cparams(("parallel", "arbitrary")),
        name=name,
    )(h.reshape(B * d * L, D), w)
    return out.reshape(B, d, L, N)


def _mla_prep_kernel(lat_ref, pos_ref, gq_ref, gkv_ref, wq_ref, wqs_ref, wk_ref, wv_ref,
                     freq_ref, sgn_ref, sel_ref, sels_ref, one_ref, q_ref, k_ref, v_ref, *, scale):
    lat = lat_ref[0]
    cq = lat[:, :MLA_Q_RANK]
    ckv = lat[:, MLA_Q_RANK:MLA_Q_RANK + MLA_KV_RANK]
    kpe = lat[:, MLA_Q_RANK + MLA_KV_RANK:]
    cqn = (cq * lax.rsqrt(jnp.mean(cq * cq, axis=-1, keepdims=True) + NORM_EPS)
           * gq_ref[...]).astype(BF16)
    ckvn = (ckv * lax.rsqrt(jnp.mean(ckv * ckv, axis=-1, keepdims=True) + NORM_EPS)
            * gkv_ref[...]).astype(BF16)
    tm = lat.shape[0]
    pos = jnp.concatenate(
        [jnp.broadcast_to(pos_ref[0, :, c * LANES:(c + 1) * LANES], (LANES, LANES)).T
         for c in range(tm // LANES)], axis=0).astype(F32)
    ang = pos * freq_ref[...]
    cos = jnp.cos(ang)
    sin = jnp.sin(ang) * sgn_ref[...]
    qa = jnp.dot(cqn, wq_ref[...], preferred_element_type=F32)
    qs = jnp.dot(cqn, wqs_ref[...], preferred_element_type=F32)
    kn = jnp.dot(ckvn, wk_ref[...], preferred_element_type=F32)
    vv = jnp.dot(ckvn, wv_ref[...], preferred_element_type=F32)
    hp = lax.Precision.HIGHEST
    rk = (jnp.dot(kpe, sel_ref[...], preferred_element_type=F32, precision=hp) * cos
          + jnp.dot(kpe, sels_ref[...], preferred_element_type=F32, precision=hp) * sin).astype(BF16)
    cos_q = cos * scale
    sin_q = sin * scale
    ones = jnp.broadcast_to(one_ref[...], (tm, LANES)).astype(BF16)
    for h in range(MLA_HEADS):
        lo = slice(h * QK_PAD, h * QK_PAD + LANES)
        hi = slice(h * QK_PAD + LANES, (h + 1) * QK_PAD)
        hd = slice(h * LANES, (h + 1) * LANES)
        q_ref[0, :, lo] = (qa[:, lo] * scale).astype(BF16)
        q_ref[0, :, hi] = (qa[:, hi] * cos_q + qs[:, hd] * sin_q).astype(BF16)
        k_ref[0, :, lo] = kn[:, hd].astype(BF16)
        k_ref[0, :, hi] = rk
        v_ref[0, :, lo] = vv[:, hd].astype(BF16)
        v_ref[0, :, hi] = ones


def _mla_prep(lat, positions, g_q, g_kv, wq, wqs, wk, wv, tm=512):
    B, S, _ = lat.shape
    H = MLA_HEADS
    half = MLA_ROPE // 2
    assert MLA_NOPE == LANES and MLA_V == LANES and QK_PAD == 2 * LANES and tm % LANES == 0
    inv_freq = ROPE_THETA ** (-jnp.arange(0, MLA_ROPE, 2, dtype=F32) / MLA_ROPE)
    freq = jnp.zeros((1, LANES), F32).at[0, :MLA_ROPE].set(jnp.concatenate([inv_freq, inv_freq]))
    sgn = jnp.zeros((1, LANES), F32).at[0, :half].set(-1.0).at[0, half:MLA_ROPE].set(1.0)
    r = jnp.arange(MLA_ROPE)
    sel = jnp.zeros((LANES, LANES), F32).at[r, r].set(1.0)
    sels = jnp.zeros((LANES, LANES), F32).at[r, (r + half) % MLA_ROPE].set(1.0)
    scale = (MLA_NOPE + MLA_ROPE) ** -0.5 * math.log2(math.e)
    ones_col = jnp.zeros((1, LANES), F32).at[0, 0].set(1.0)
    const = lambda b, i: (0, 0)
    return pl.pallas_call(
        functools.partial(_mla_prep_kernel, scale=scale),
        grid=(B, S // tm),
        in_specs=[pl.BlockSpec((1, tm, LAT_PAD), lambda b, i: (b, i, 0)),
                  pl.BlockSpec((1, 1, tm), lambda b, i: (b, 0, i)),
                  pl.BlockSpec((1, MLA_Q_RANK), const),
                  pl.BlockSpec((1, MLA_KV_RANK), const),
                  pl.BlockSpec((MLA_Q_RANK, H * QK_PAD), const),
                  pl.BlockSpec((MLA_Q_RANK, H * LANES), const),
                  pl.BlockSpec((MLA_KV_RANK, H * LANES), const),
                  pl.BlockSpec((MLA_KV_RANK, H * LANES), const),
                  pl.BlockSpec((1, LANES), const),
                  pl.BlockSpec((1, LANES), const),
                  pl.BlockSpec((LANES, LANES), const),
                  pl.BlockSpec((LANES, LANES), const),
                  pl.BlockSpec((1, LANES), const)],
        out_specs=[pl.BlockSpec((1, tm, H * QK_PAD), lambda b, i: (b, i, 0)),
                   pl.BlockSpec((1, tm, H * QK_PAD), lambda b, i: (b, i, 0)),
                   pl.BlockSpec((1, tm, H * QK_PAD), lambda b, i: (b, i, 0))],
        out_shape=[jax.ShapeDtypeStruct((B, S, H * QK_PAD), BF16),
                   jax.ShapeDtypeStruct((B, S, H * QK_PAD), BF16),
                   jax.ShapeDtypeStruct((B, S, H * QK_PAD), BF16)],
        compiler_params=_cparams(("parallel", "parallel")),
        name="mla_prep",
    )(lat, positions.reshape(B, 1, S), g_q, g_kv, wq, wqs, wk, wv, freq, sgn, sel, sels, ones_col)


def _flash_kernel(q_ref, k_ref, v_ref, o_ref, m_ref, acc_ref, s_ref, *, tq, tkc):
    S = k_ref.shape[1]
    nk = S // tkc
    nq = S // tq
    half = nk // 2
    acc_ref[...] = jnp.zeros(acc_ref.shape, F32)
    m_ref[...] = jnp.full(m_ref.shape, -jnp.inf, F32)

    def scores(qi, kj):
        qoff = pl.multiple_of(qi * tq, tq)
        koff = pl.multiple_of(kj * tkc, tkc)
        return _nt_dot(q_ref[0, pl.ds(qoff, tq), :], k_ref[0, pl.ds(koff, tkc), :])

    def consume(kj, slot, first):
        koff = pl.multiple_of(kj * tkc, tkc)
        s = s_ref[slot]
        m_prev = m_ref[...]
        if first is not None:
            m_prev = jnp.where(first, -jnp.inf, m_prev)
        m_new = jnp.maximum(m_prev, jnp.max(s, axis=-1, keepdims=True))
        alpha = jnp.exp2(m_prev - m_new)
        p = jnp.exp2(s - m_new).astype(BF16)
        acc_ref[...] = alpha * acc_ref[...] + jnp.dot(p, v_ref[0, pl.ds(koff, tkc), :],
                                                      preferred_element_type=F32)
        m_ref[...] = m_new

    s_ref[0] = scores(0, 0)

    def body(t, carry):
        qi = t // half
        kj = 2 * (t % half)
        s_ref[1] = scores(qi, kj + 1)
        consume(kj, 0, kj == 0)
        t1 = jnp.minimum(t + 1, nq * half - 1)
        s_ref[0] = scores(t1 // half, 2 * (t1 % half))
        consume(kj + 1, 1, None)

        @pl.when(kj + 2 == nk)
        def _():
            acc = acc_ref[...]
            qoff = pl.multiple_of(qi * tq, tq)
            o_ref[0, pl.ds(qoff, tq), :] = (acc[:, :MLA_V] / acc[:, MLA_V:MLA_V + 1]).astype(BF16)
        return carry

    lax.fori_loop(0, nq * half, body, 0)


def _flash_attention(q, k, v, tq=1024, tkc=1024):
    B, S, _ = q.shape
    H = MLA_HEADS
    assert (S // tkc) % 2 == 0 and S % tq == 0
    whole = pl.BlockSpec((1, S, QK_PAD), lambda b, h: (b, 0, h))
    return pl.pallas_call(
        functools.partial(_flash_kernel, tq=tq, tkc=tkc),
        grid=(B, H),
        in_specs=[whole, whole, whole],
        out_specs=pl.BlockSpec((1, S, MLA_V), lambda b, h: (b, 0, h)),
        out_shape=jax.ShapeDtypeStruct((B, S, H * MLA_V), BF16),
        scratch_shapes=[pltpu.VMEM((tq, 1), F32), pltpu.VMEM((tq, QK_PAD), F32),
                        pltpu.VMEM((2, tq, tkc), F32)],
        compiler_params=_cparams(("parallel", "parallel")),
        name="mla_flash",
    )(q, k, v)


def _t5_bucket(rel):
    nb = REL_BUCKETS // 2
    max_exact = nb // 2
    ret = jnp.where(rel > 0, nb, 0)
    n = jnp.abs(rel)
    nf = jnp.maximum(n, 1).astype(F32)
    large = max_exact + (jnp.log(nf / max_exact) / math.log(REL_MAX_DIST / max_exact)
                         * (nb - max_exact)).astype(I32)
    large = jnp.minimum(large, nb - 1)
    return ret + jnp.where(n < max_exact, n, large)


def _bias_lookup(tab_ref, h, bucket):
    rows, nk = bucket.shape
    tab = jnp.broadcast_to(tab_ref[h:h + 1, :], (rows, LANES))
    return jnp.concatenate(
        [jnp.take_along_axis(tab, bucket[:, c * LANES:(c + 1) * LANES], axis=1)
         for c in range(nk // LANES)], axis=1)


def _band_bias_kernel(tab_ref, band_ref, *, r, dil):
    _, ts, nk = band_ref.shape
    off = lax.broadcasted_iota(I32, (ts, nk), 1) - r - lax.broadcasted_iota(I32, (ts, nk), 0)
    bucket = jnp.where(jnp.abs(off) <= r, _t5_bucket(off * dil), REL_BUCKETS)
    for h in range(DIL_HEADS):
        band_ref[h] = _bias_lookup(tab_ref, h, bucket)


def _band_bias(tab, r, dil, ts):
    nk = ts + 2 * r
    return pl.pallas_call(
        functools.partial(_band_bias_kernel, r=r, dil=dil),
        grid=(1,),
        in_specs=[pl.BlockSpec((DIL_HEADS, LANES), lambda i: (0, 0))],
        out_specs=pl.BlockSpec((DIL_HEADS, ts, nk), lambda i: (0, 0, 0)),
        out_shape=jax.ShapeDtypeStruct((DIL_HEADS, ts, nk), F32),
        compiler_params=_cparams(("arbitrary",)),
        name=f"band_bias_d{dil}",
    )(tab)


def _dilated_kernel(q_ref, kp_ref, kc_ref, kn_ref, vp_ref, vc_ref, vn_ref, pq_ref, pk_ref, tab_ref,
                    band_ref, o_ref, l_ref, kcat_ref, vcat_ref, s_scr, *, r, tq, ts, n_rows, dil):
    n = pl.program_id(2)
    nsub = tq // ts
    nk = ts + 2 * r
    kcat_ref[0:r, :] = kp_ref[0, 0]
    kcat_ref[r:r + tq, :] = kc_ref[0, 0]
    kcat_ref[r + tq:tq + 2 * r, :] = kn_ref[0, 0]
    ones = jnp.ones((tq + 2 * r, DIL_HEAD_DIM), BF16)
    for h in range(DIL_HEADS):
        sl = slice(h * DIL_HEAD_DIM, (h + 1) * DIL_HEAD_DIM)
        dst = slice(2 * h * DIL_HEAD_DIM, (2 * h + 1) * DIL_HEAD_DIM)
        vcat_ref[0:r, dst] = vp_ref[0, 0, :, sl]
        vcat_ref[r:r + tq, dst] = vc_ref[0, 0, :, sl]
        vcat_ref[r + tq:tq + 2 * r, dst] = vn_ref[0, 0, :, sl]
        vcat_ref[:, (2 * h + 1) * DIL_HEAD_DIM:(2 * h + 2) * DIL_HEAD_DIM] = ones
    lane_grp = lax.broadcasted_iota(I32, (ts, LANES), 1) // (LANES // DIL_HEADS)
    scale = DIL_HEAD_DIM ** -0.5 * math.log2(math.e)
    for j in range(nsub):
        qi = n * tq + j * ts + lax.broadcasted_iota(I32, (ts, nk), 0)
        kj = n * tq + j * ts - r + lax.broadcasted_iota(I32, (ts, nk), 1)
        inside = (kj >= 0) & (kj < n_rows)
        valid = (jnp.abs(kj - qi) <= r) & inside
        pq = jnp.broadcast_to(pq_ref[0, 0, :, j * ts:(j + 1) * ts], (LANES, ts)).T
        pk = pk_ref[0, 0, j]
        rel = jnp.concatenate([pk[:, c * LANES:(c + 1) * LANES] - pq
                               for c in range(nk // LANES)], axis=1)
        regular = jnp.max(jnp.where(valid & (rel != (kj - qi) * dil), 1, 0)) == 0

        def logits(h):
            sl = slice(h * DIL_HEAD_DIM, (h + 1) * DIL_HEAD_DIM)
            return _nt_dot(q_ref[0, 0, j * ts:(j + 1) * ts, sl], kcat_ref[j * ts:j * ts + nk, sl]) * scale

        @pl.when(regular)
        def _():
            for h in range(DIL_HEADS):
                s_scr[j * DIL_HEADS + h] = jnp.where(inside, logits(h) + band_ref[h], NEG_INF)

        @pl.when(jnp.logical_not(regular))
        def _():
            bucket = jnp.where(valid, _t5_bucket(rel), REL_BUCKETS)
            for h in range(DIL_HEADS):
                s_scr[j * DIL_HEADS + h] = logits(h) + _bias_lookup(tab_ref, h, bucket)
    for j in range(nsub):
        lse_tile = jnp.zeros((ts, LANES), F32)
        for h in range(DIL_HEADS):
            sl = slice(h * DIL_HEAD_DIM, (h + 1) * DIL_HEAD_DIM)
            s = s_scr[j * DIL_HEADS + h]
            m = jnp.max(s, axis=-1, keepdims=True)
            p = jnp.exp2(s - m).astype(BF16)
            res = jnp.dot(p, vcat_ref[j * ts:j * ts + nk,
                                      2 * h * DIL_HEAD_DIM:(2 * h + 2) * DIL_HEAD_DIM],
                          preferred_element_type=F32)
            den = res[:, DIL_HEAD_DIM:]
            o_ref[0, 0, j * ts:(j + 1) * ts, sl] = (res[:, :DIL_HEAD_DIM] / den).astype(BF16)
            lse_tile = jnp.where(lane_grp == h, m * math.log(2.0) + jnp.log(den), lse_tile)
        l_ref[0, 0, j * ts:(j + 1) * ts, :] = lse_tile


def _dilated_group(qkv, positions, table, gi, window, d, tq=512, ts=128):
    B, _, L, _ = qkv.shape
    HD = DIL_HEADS * DIL_HEAD_DIM
    r = window // (2 * d)
    tq = min(tq, L)
    nt = L // tq
    nsub = tq // ts
    rb = tq // r
    nk = ts + 2 * r
    pcls = positions.reshape(B, L, d).transpose(0, 2, 1)
    pq = pcls.reshape(B, d, 1, L)
    ppad = jnp.pad(pcls, ((0, 0), (0, 0), (r, ts + r)))
    pk = jnp.concatenate([ppad[:, :, :L].reshape(B, d, L // ts, ts),
                          ppad[:, :, ts:ts + L].reshape(B, d, L // ts, ts)[..., :2 * r]], axis=-1)
    pk = pk.reshape(B, d, L // ts, 1, nk)
    tab = jnp.zeros((DIL_HEADS, LANES), F32).at[:, :REL_BUCKETS].set(
        table[:, gi * DIL_HEADS:(gi + 1) * DIL_HEADS].T * math.log2(math.e))
    tab = tab.at[:, REL_BUCKETS].set(NEG_INF)

    def cur(c):
        return pl.BlockSpec((1, 1, tq, HD), lambda b, g, n: (b, g, n, c))

    def prv(c):
        return pl.BlockSpec((1, 1, r, HD), lambda b, g, n: (b, g, jnp.maximum(n * rb - 1, 0), c))

    def nxt(c):
        return pl.BlockSpec((1, 1, r, HD),
                            lambda b, g, n: (b, g, jnp.minimum((n + 1) * rb, L // r - 1), c))

    return pl.pallas_call(
        functools.partial(_dilated_kernel, r=r, tq=tq, ts=ts, n_rows=L, dil=d),
        grid=(B, d, nt),
        in_specs=[cur(0), prv(1), cur(1), nxt(1), prv(2), cur(2), nxt(2),
                  pl.BlockSpec((1, 1, 1, tq), lambda b, g, n: (b, g, 0, n)),
                  pl.BlockSpec((1, 1, nsub, 1, nk), lambda b, g, n: (b, g, n, 0, 0)),
                  pl.BlockSpec((DIL_HEADS, LANES), lambda b, g, n: (0, 0)),
                  pl.BlockSpec((DIL_HEADS, ts, nk), lambda b, g, n: (0, 0, 0))],
        out_specs=[pl.BlockSpec((1, 1, tq, HD), lambda b, g, n: (b, g, n, 0)),
                   pl.BlockSpec((1, 1, tq, LANES), lambda b, g, n: (b, g, n, 0))],
        out_shape=[jax.ShapeDtypeStruct((B, d, L, HD), BF16),
                   jax.ShapeDtypeStruct((B, d, L, LANES), F32)],
        scratch_shapes=[pltpu.VMEM((tq + 2 * r, HD), BF16), pltpu.VMEM((tq + 2 * r, 2 * HD), BF16),
                        pltpu.VMEM((nsub * DIL_HEADS, ts, nk), F32)],
        compiler_params=_cparams(("parallel", "parallel", "parallel")),
        name=f"dilated_g{gi}",
    )(*([qkv] * 7 + [pq, pk, tab, _band_bias(tab, r, d, ts)]))


def _outproj_kernel(*refs, dils):
    ng = len(dils)
    oa_ref = refs[0]
    og_refs = refs[1:1 + ng]
    lg_refs = refs[1 + ng:1 + 2 * ng]
    (ga_ref, gb_ref, x_ref, mod_ref, g_ref, wo_ref, wr_ref,
     x1_ref, h2_ref, aff_ref, afft_ref, o_scr, l_scr) = refs[1 + 2 * ng:]
    tm = x_ref.shape[1]
    for i, d in enumerate(dils):
        rows = tm // d
        for g in range(d):
            dst = pl.ds(g, rows, stride=d) if d > 1 else slice(None)
            og = og_refs[i][0, g].astype(F32)
            for h in range(DIL_HEADS):
                o_scr[i, h, dst, :] = og[:, h * DIL_HEAD_DIM:(h + 1) * DIL_HEAD_DIM]
            l_scr[i, dst, :] = lg_refs[i][0, g]
    lses = [l_scr[i] for i in range(ng)]
    mx = functools.reduce(jnp.maximum, lses)
    es = [jnp.exp(l - mx) for l in lses]
    tot = functools.reduce(lambda a, b: a + b, es)
    wts = [e / tot for e in es]
    cols = []
    for h in range(DIL_HEADS):
        c = h * (LANES // DIL_HEADS)
        cols.append(functools.reduce(
            lambda a, b: a + b, [wts[i][:, c:c + 1] * o_scr[i, h] for i in range(ng)]))
    o_b = jnp.concatenate(cols, axis=1)
    comb = (ga_ref[0, 0].astype(F32) * oa_ref[0].astype(F32)
            + gb_ref[0, 0].astype(F32) * o_b).astype(BF16)
    x1 = x_ref[0] + mod_ref[0, 2:3, :] * jnp.dot(comb, wo_ref[...], preferred_element_type=F32)
    x1_ref[0] = x1
    y = x1 * lax.rsqrt(jnp.mean(x1 * x1, axis=-1, keepdims=True) + NORM_EPS) * g_ref[...]
    h2 = y * (1.0 + mod_ref[0, 4:5, :]) + mod_ref[0, 3:4, :]
    h2_ref[0] = h2.astype(BF16)
    logits = jnp.dot(h2, wr_ref[...], preferred_element_type=F32,
                     precision=lax.Precision.HIGHEST)
    lane = lax.broadcasted_iota(I32, logits.shape, 1)
    logits = jnp.where(lane < N_EXPERTS, logits, -jnp.inf)
    e = jnp.exp(logits - jnp.max(logits, axis=-1, keepdims=True))
    aff = e / jnp.sum(e, axis=-1, keepdims=True)
    aff_ref[0] = aff
    afft_ref[0, 0] = aff.T[:N_EXPERTS, :]


def _out_projection(o_a, dil_outs, main, gate_col, x, mod, g, w_out, w_router, tm=512):
    B, S, D = x.shape
    const = lambda b, i: (0, 0)
    row = lambda b, i: (b, i, 0)
    wr = jnp.zeros((D, LANES), F32).at[:, :N_EXPERTS].set(w_router)
    dils = tuple(o.shape[1] for o, _ in dil_outs)
    og_specs = [pl.BlockSpec((1, d, tm // d, D), lambda b, i: (b, 0, i, 0)) for d in dils]
    lg_specs = [pl.BlockSpec((1, d, tm // d, LANES), lambda b, i: (b, 0, i, 0)) for d in dils]
    return pl.pallas_call(
        functools.partial(_outproj_kernel, dils=dils),
        grid=(B, S // tm),
        in_specs=[pl.BlockSpec((1, tm, D), row)] + og_specs + lg_specs + [
                  pl.BlockSpec((1, 1, tm, D), lambda b, i: (b, 0, i, gate_col)),
                  pl.BlockSpec((1, 1, tm, D), lambda b, i: (b, 0, i, gate_col + 1)),
                  pl.BlockSpec((1, tm, D), row),
                  pl.BlockSpec((1, 6, D), lambda b, i: (b, 0, 0)),
                  pl.BlockSpec((1, D), const),
                  pl.BlockSpec((D, D), const),
                  pl.BlockSpec((D, LANES), const)],
        out_specs=[pl.BlockSpec((1, tm, D), row),
                   pl.BlockSpec((1, tm, D), row),
                   pl.BlockSpec((1, tm, LANES), row),
                   pl.BlockSpec((1, 1, N_EXPERTS, tm), lambda b, i: (b, i, 0, 0))],
        out_shape=[jax.ShapeDtypeStruct((B, S, D), F32),
                   jax.ShapeDtypeStruct((B, S, D), BF16),
                   jax.ShapeDtypeStruct((B, S, LANES), F32),
                   jax.ShapeDtypeStruct((B, S // tm, N_EXPERTS, tm), F32)],
        scratch_shapes=[pltpu.VMEM((len(dils), DIL_HEADS, tm, DIL_HEAD_DIM), F32),
                        pltpu.VMEM((len(dils), tm, LANES), F32)],
        compiler_params=_cparams(("parallel", "parallel")),
        name="out_projection",
    )(o_a, *[o for o, _ in dil_outs], *[l for _, l in dil_outs], main, main, x, mod, g, w_out, wr)


def _select_kernel(afft_ref, thr_ref, tie_ref, slot_ref, boff_ref, *, cap, chunk):
    aff = jnp.concatenate([afft_ref[0, c] for c in range(afft_ref.shape[1])], axis=1)
    E, S = aff.shape
    bits = lax.bitcast_convert_type(aff, I32)

    def count(mask):
        return jnp.sum(mask.astype(I32), axis=1, keepdims=True)

    def thr_step(i, v):
        cand = v | jnp.left_shift(jnp.int32(1), 30 - i)
        return jnp.where(count(bits >= cand) >= cap, cand, v)

    thr = lax.fori_loop(0, 31, thr_step, jnp.zeros((E, 1), I32))
    need = cap - count(bits > thr)
    eq = bits == thr
    idx = lax.broadcasted_iota(I32, (E, S), 1)
    nbits = max(1, (S - 1).bit_length())

    def tie_step(i, j):
        cand = j | jnp.left_shift(jnp.int32(1), nbits - 1 - i)
        return jnp.where(count(eq & (idx < cand)) < need, cand, j)

    tie = lax.fori_loop(0, nbits, tie_step, jnp.zeros((E, 1), I32))
    sel = (bits > thr) | (eq & (idx <= tie))
    thr_ref[0] = jnp.broadcast_to(thr, (E, LANES))
    tie_ref[0] = jnp.broadcast_to(tie, (E, LANES))

    upper = (lax.broadcasted_iota(I32, (chunk, chunk), 0)
             <= lax.broadcasted_iota(I32, (chunk, chunk), 1)).astype(BF16)
    carry = jnp.zeros((E, 1), F32)
    self_ = sel.astype(F32)
    lane = lax.broadcasted_iota(I32, (E, LANES), 1)
    boff = jnp.zeros((E, LANES), I32)
    for c in range(S // chunk):
        boff = jnp.where(lane == c, carry.astype(I32), boff)
        scf = self_[:, c * chunk:(c + 1) * chunk]
        incl = jnp.dot(scf.astype(BF16), upper, preferred_element_type=F32)
        pos = (incl - scf + carry).astype(I32)
        slot_ref[0, c] = jnp.where(scf > 0.0, pos, -1)
        carry = carry + incl[:, chunk - 1:chunk]
    boff_ref[0] = jnp.where(lane == S // chunk, carry.astype(I32), boff)


def _select(aff_t, cap, chunk):
    B, nblk, E, _ = aff_t.shape
    assert aff_t.shape[3] == chunk and nblk < LANES
    rows = pl.BlockSpec((1, nblk, E, chunk), lambda b: (b, 0, 0, 0))
    return pl.pallas_call(
        functools.partial(_select_kernel, cap=cap, chunk=chunk),
        grid=(B,),
        in_specs=[rows],
        out_specs=[pl.BlockSpec((1, E, LANES), lambda b: (b, 0, 0)),
                   pl.BlockSpec((1, E, LANES), lambda b: (b, 0, 0)),
                   rows,
                   pl.BlockSpec((1, E, LANES), lambda b: (b, 0, 0))],
        out_shape=[jax.ShapeDtypeStruct((B, E, LANES), I32),
                   jax.ShapeDtypeStruct((B, E, LANES), I32),
                   jax.ShapeDtypeStruct((B, nblk, E, chunk), I32),
                   jax.ShapeDtypeStruct((B, E, LANES), I32)],
        compiler_params=_cparams(("parallel",)),
        name="ec_select",
    )(aff_t)


def _gather_kernel(boff_ref, slot_ref, afft_ref, h_ref, xe_ref, gate_ref, acc_ref, gacc_ref, *, win):
    _, nblk, _, tbk = slot_ref.shape
    cap = xe_ref.shape[2]
    e = pl.program_id(1)
    base = (pl.program_id(0) * pl.num_programs(1) + e) * (nblk + 1)
    acc_ref[...] = jnp.zeros(acc_ref.shape, F32)
    gacc_ref[...] = jnp.zeros(gacc_ref.shape, F32)

    def chunk(tb, carry):
        lo = boff_ref[base + tb]
        hi = boff_ref[base + tb + 1]
        first = (lo // 8) * 8
        off = pl.multiple_of(tb * tbk, tbk)
        slot_row = slot_ref[0, tb, pl.ds(e, 1), :]
        aff_row = afft_ref[0, tb, pl.ds(e, 1), :]

        def window(k, c):
            w0 = pl.multiple_of(first + k * win, 8)
            hit = lax.broadcasted_iota(I32, (win, tbk), 0) + w0 == slot_row
            acc_ref[pl.ds(w0, win), :] += jnp.dot(hit.astype(BF16), h_ref[0, pl.ds(off, tbk), :],
                                                  preferred_element_type=F32)
            gacc_ref[pl.ds(w0, win), :] += jnp.sum(jnp.where(hit, aff_row, 0.0),
                                                   axis=1, keepdims=True)
            return c

        lax.fori_loop(0, (hi - first + win - 1) // win, window, 0)
        return carry

    lax.fori_loop(0, nblk, chunk, 0)
    xe_ref[0, 0] = acc_ref[0:cap, :].astype(BF16)
    gate_ref[0, 0] = gacc_ref[0:cap, :]


def _gather(boff_flat, slot, aff_t, h2, cap, chunk, win=128):
    B, nblk, E, _ = slot.shape
    S, D = h2.shape[1:]
    rowblk = pl.BlockSpec((1, nblk, E, chunk), lambda b, e, off: (b, 0, 0, 0))
    return pl.pallas_call(
        functools.partial(_gather_kernel, win=win),
        grid_spec=pltpu.PrefetchScalarGridSpec(
            num_scalar_prefetch=1,
            grid=(B, E),
            in_specs=[rowblk, rowblk,
                      pl.BlockSpec((1, S, D), lambda b, e, off: (b, 0, 0))],
            out_specs=[pl.BlockSpec((1, 1, cap, D), lambda b, e, off: (b, e, 0, 0)),
                       pl.BlockSpec((1, 1, cap, 1), lambda b, e, off: (b, e, 0, 0))],
            scratch_shapes=[pltpu.VMEM((cap + win, D), F32), pltpu.VMEM((cap + win, 1), F32)]),
        out_shape=[jax.ShapeDtypeStruct((B, E, cap, D), BF16),
                   jax.ShapeDtypeStruct((B, E, cap, 1), F32)],
        compiler_params=_cparams(("parallel", "arbitrary")),
        name="ec_gather",
    )(boff_flat, slot, aff_t, h2)


def _ffn_kernel(xe_ref, gate_ref, wg_ref, wu_ref, wd_ref, ye_ref, acc_ref):
    f = pl.program_id(2)
    nb, _, cap, D = xe_ref.shape

    @pl.when((pl.program_id(0) == 0) & (pl.program_id(1) == 0) & (f == 0))
    def _():
        acc_ref[...] = jnp.zeros(acc_ref.shape, F32)

    x = xe_ref[...].reshape(nb * cap, D)
    g = jnp.dot(x, wg_ref[0].astype(BF16), preferred_element_type=F32)
    u = jnp.dot(x, wu_ref[0].astype(BF16), preferred_element_type=F32)
    hid = (g * jax.nn.sigmoid(g) * u).astype(BF16)
    part = jnp.dot(hid, wd_ref[0].astype(BF16), preferred_element_type=F32)
    acc_ref[...] = jnp.where(f == 0, 0.0, acc_ref[...]) + part

    @pl.when(f == pl.num_programs(2) - 1)
    def _():
        gate = gate_ref[...].reshape(nb * cap, 1)
        ye_ref[...] = (acc_ref[...] * gate).astype(BF16).reshape(ye_ref.shape)


def _expert_ffn(xe, gates, w_gate, w_up, w_down, nb=2, tf=512):
    B, E, cap, D = xe.shape
    F = w_gate.shape[-1]
    return pl.pallas_call(
        _ffn_kernel,
        grid=(E, B // nb, F // tf),
        in_specs=[pl.BlockSpec((nb, 1, cap, D), lambda e, b, f: (b, e, 0, 0)),
                  pl.BlockSpec((nb, 1, cap, 1), lambda e, b, f: (b, e, 0, 0)),
                  pl.BlockSpec((1, D, tf), lambda e, b, f: (e, 0, f)),
                  pl.BlockSpec((1, D, tf), lambda e, b, f: (e, 0, f)),
                  pl.BlockSpec((1, tf, D), lambda e, b, f: (e, f, 0))],
        out_specs=pl.BlockSpec((nb, 1, cap, D), lambda e, b, f: (b, e, 0, 0)),
        out_shape=jax.ShapeDtypeStruct((B, E, cap, D), BF16),
        scratch_shapes=[pltpu.VMEM((nb * cap, D), F32)],
        compiler_params=_cparams(("arbitrary", "arbitrary", "arbitrary")),
        name="ec_ffn",
    )(xe, gates, w_gate, w_up, w_down)


def _combine_kernel(boff_ref, aff_ref, thr_ref, tie_ref, ye_ref, x1_ref, mod_ref, g_ref, out_ref,
                    slot_ref, carry_ref, *, tb, wslot, nblk):
    b = pl.program_id(0)
    t2 = pl.program_id(1)
    e = pl.program_id(2)
    nsub = aff_ref.shape[1] // tb
    cap = ye_ref.shape[2]
    lane = lax.broadcasted_iota(I32, (tb, LANES), 1)

    @pl.when(e == 0)
    def _():
        @pl.when(t2 == 0)
        def _():
            carry_ref[...] = jnp.zeros(carry_ref.shape, F32)

        lower = (lax.broadcasted_iota(I32, (tb, tb), 0)
                 >= lax.broadcasted_iota(I32, (tb, tb), 1)).astype(BF16)
        for u in range(nsub):
            rows = slice(u * tb, (u + 1) * tb)
            bits = lax.bitcast_convert_type(aff_ref[0, rows, :], I32)
            tok = (t2 * nsub + u) * tb + lax.broadcasted_iota(I32, (tb, LANES), 0)
            thr = thr_ref[0]
            sel = ((bits > thr) | ((bits == thr) & (tok <= tie_ref[0]))) & (lane < N_EXPERTS)
            self_ = sel.astype(F32)
            incl = jnp.dot(lower, self_.astype(BF16), preferred_element_type=F32)
            pos = (incl - self_ + carry_ref[...]).astype(I32)
            slot_ref[rows, :] = jnp.where(sel, pos, -1)
            carry_ref[...] = carry_ref[...] + incl[tb - 1:tb, :]
        out_ref[...] = jnp.zeros(out_ref.shape, F32)

    pick = lane == e
    base = (b * pl.num_programs(2) + e) * (nblk + 1) + t2 * nsub
    for u in range(nsub):
        rows = slice(u * tb, (u + 1) * tb)
        lo = boff_ref[base + u]
        hi = boff_ref[base + u + 1]
        slot_e = jnp.sum(jnp.where(pick, slot_ref[rows, :], 0), axis=1, keepdims=True)

        first = (lo // SUBLANES_BF16) * SUBLANES_BF16

        def body(k, carry):
            start = first + k * wslot
            woff = pl.multiple_of(jnp.minimum(start, cap - wslot), SUBLANES_BF16)
            todo = jnp.where(slot_e >= start, slot_e, -1)
            onehot = (lax.broadcasted_iota(I32, (tb, wslot), 1) + woff == todo).astype(BF16)
            out_ref[0, rows, :] += jnp.dot(onehot, ye_ref[0, 0, pl.ds(woff, wslot), :],
                                           preferred_element_type=F32)
            return carry

        lax.fori_loop(0, (hi - first + wslot - 1) // wslot, body, 0)

    @pl.when(e == pl.num_programs(2) - 1)
    def _():
        for u in range(nsub):
            rows = slice(u * tb, (u + 1) * tb)
            x2 = x1_ref[0, rows, :] + mod_ref[0, 5:6, :] * out_ref[0, rows, :]
            out_ref[0, rows, :] = (x2 * lax.rsqrt(jnp.mean(x2 * x2, axis=-1, keepdims=True)
                                                  + NORM_EPS) * g_ref[...])


def _combine(boff_flat, aff, thr, tie, ye, x1, mod, g_final, tb, tsup=2048, wslot=128):
    B, S, D = x1.shape
    E, cap = ye.shape[1], ye.shape[2]
    assert cap % wslot == 0
    tsup = min(tsup, S)
    return pl.pallas_call(
        functools.partial(_combine_kernel, tb=tb, wslot=wslot, nblk=S // tb),
        grid_spec=pltpu.PrefetchScalarGridSpec(
            num_scalar_prefetch=1,
            grid=(B, S // tsup, E),
            in_specs=[pl.BlockSpec((1, tsup, LANES), lambda b, t, e, off: (b, t, 0)),
                      pl.BlockSpec((1, 1, LANES), lambda b, t, e, off: (b, 0, 0)),
                      pl.BlockSpec((1, 1, LANES), lambda b, t, e, off: (b, 0, 0)),
                      pl.BlockSpec((1, 1, cap, D), lambda b, t, e, off: (b, e, 0, 0)),
                      pl.BlockSpec((1, tsup, D), lambda b, t, e, off: (b, t, 0)),
                      pl.BlockSpec((1, 6, D), lambda b, t, e, off: (b, 0, 0)),
                      pl.BlockSpec((1, D), lambda b, t, e, off: (0, 0))],
            out_specs=pl.BlockSpec((1, tsup, D), lambda b, t, e, off: (b, t, 0)),
            scratch_shapes=[pltpu.VMEM((tsup, LANES), I32), pltpu.VMEM((1, LANES), F32)]),
        out_shape=jax.ShapeDtypeStruct((B, S, D), F32),
        compiler_params=_cparams(("parallel", "arbitrary", "arbitrary")),
        name="ec_combine",
    )(boff_flat, aff, thr, tie, ye, x1, mod, g_final)


def _prep_weights(w_in, w_uq, w_ukv):
    D = w_in.shape[0]
    H = MLA_HEADS
    n_lat = MLA_Q_RANK + MLA_KV_RANK + MLA_ROPE
    n_grp = 3 * DIL_HEADS * DIL_HEAD_DIM
    n_dil = len(DIL_GROUPS) * n_grp
    w_lat = jnp.zeros((D, LAT_PAD), F32).at[:, :n_lat].set(w_in[:, :n_lat]).astype(BF16)
    w_grp = [w_in[:, n_lat + i * n_grp:n_lat + (i + 1) * n_grp].astype(BF16)
             for i in range(len(DIL_GROUPS))]
    w_gates = w_in[:, n_lat + n_dil:].astype(BF16)
    half = MLA_ROPE // 2
    uq = w_uq.reshape(MLA_Q_RANK, H, MLA_NOPE + MLA_ROPE)
    pe = uq[:, :, MLA_NOPE:]
    zq = jnp.zeros((MLA_Q_RANK, H, QK_PAD - MLA_NOPE - MLA_ROPE), F32)
    wq = jnp.concatenate([uq[:, :, :MLA_NOPE], pe, zq], axis=2)
    wqs = jnp.concatenate([pe[:, :, half:], pe[:, :, :half], zq], axis=2)
    ukv = w_ukv.reshape(MLA_KV_RANK, H, MLA_NOPE + MLA_V)
    wk = ukv[:, :, :MLA_NOPE]
    wv = ukv[:, :, MLA_NOPE:]
    return (w_lat, w_grp, w_gates, wq.reshape(MLA_Q_RANK, H * QK_PAD).astype(BF16),
            wqs.reshape(MLA_Q_RANK, H * LANES).astype(BF16),
            wk.reshape(MLA_KV_RANK, H * MLA_NOPE).astype(BF16),
            wv.reshape(MLA_KV_RANK, H * MLA_V).astype(BF16))


def kernel(x, c, positions, w_ada, b_ada, g_norm_mix, w_in, g_q_lat, g_kv_lat, w_uq, w_ukv,
           rel_bias, w_out, g_norm_ffn, w_router, w_gate, w_up, w_down, g_final):
    B, S, D = x.shape
    assert w_ada.shape[0] == 1, "the final norm is fused into the (single) layer's last kernel"
    assert DIL_GROUPS[0][1] == 1, "the gates ride along with the undilated group's projection"
    for l in range(w_ada.shape[0]):
        mod = _modulation(c, w_ada[l], b_ada[l])
        w_lat, w_grp, w_gates, wq, wqs, wk, wv = _prep_weights(w_in[l], w_uq[l], w_ukv[l])
        g_mix = g_norm_mix[l].reshape(1, D)
        n_qkv = w_grp[0].shape[1] // D
        hs = _norm_modulate(x, mod, g_mix, tuple(d for _, d in DIL_GROUPS))
        lat = _projection(hs[0], w_lat, "lat_projection", out_dtype=F32)
        main = _projection(hs[0], jnp.concatenate([w_grp[0], w_gates], axis=1), "in_projection_g0",
                           n_plain=n_qkv)
        qkvs = [main] + [_projection(hs[gi], w_grp[gi], f"in_projection_g{gi}")
                         for gi in range(1, len(DIL_GROUPS))]
        q, k, v = _mla_prep(lat.reshape(B, S, LAT_PAD), positions, g_q_lat[l].reshape(1, -1), g_kv_lat[l].reshape(1, -1),
                            wq, wqs, wk, wv)
        o_a = _flash_attention(q, k, v)
        dil_outs = [_dilated_group(qkvs[gi], positions, rel_bias, gi, window, d)
                    for gi, (window, d) in enumerate(DIL_GROUPS)]
        x1, h2, aff, aff_t = _out_projection(o_a, dil_outs, main, n_qkv, x, mod,
                                             g_norm_ffn[l].reshape(1, D),
                                             w_out[l].astype(BF16), w_router[l])
        cap = EC_CAPACITY_FACTOR * S // N_EXPERTS
        chunk = 512
        thr, tie, slot, boff = _select(aff_t, cap, chunk)
        boff_flat = boff[:, :, :S // chunk + 1].reshape(-1)
        xe, gates = _gather(boff_flat, slot, aff_t, h2, cap, chunk)
        ye = _expert_ffn(xe, gates, w_gate[l], w_up[l], w_down[l])
        pad = jnp.zeros((B, 1, LANES - N_EXPERTS), I32)
        thr_l = jnp.concatenate([thr[:, :, 0].reshape(B, 1, N_EXPERTS), pad], axis=2)
        tie_l = jnp.concatenate([tie[:, :, 0].reshape(B, 1, N_EXPERTS), pad], axis=2)
        x = _combine(boff_flat, aff, thr_l, tie_l, ye, x1, mod, g_final.reshape(1, D), tb=chunk)
    return x
```

```python
import functools
import math

import jax
import jax.numpy as jnp
from jax import lax
from jax.experimental import pallas as pl
from jax.experimental.pallas import tpu as pltpu

F32 = jnp.float32
BF16 = jnp.bfloat16
I32 = jnp.int32

MLA_HEADS = 8
MLA_Q_RANK = 384
MLA_KV_RANK = 256
MLA_NOPE = 128
MLA_ROPE = 64
MLA_V = 128
ROPE_THETA = 10000.0
DIL_GROUPS = ((128, 1), (512, 4), (2048, 16))
DIL_HEADS = 8
DIL_HEAD_DIM = 128
REL_BUCKETS = 32
REL_MAX_DIST = 1024
N_EXPERTS = 16
EC_CAPACITY_FACTOR = 2
NORM_EPS = 1e-6
NEG_INF = -1e30

LANES = 128
SUBLANES_BF16 = 16
QK_PAD = 256
LAT_PAD = 768
VMEM_LIMIT = 56 * 1024 * 1024


def _cparams(sem):
    return pltpu.CompilerParams(dimension_semantics=sem, vmem_limit_bytes=VMEM_LIMIT)


def _nt_dot(a, b):
    return lax.dot_general(a, b, (((1,), (1,)), ((), ())), preferred_element_type=F32)


def _mod_kernel(c_ref, w_ref, b_ref, o_ref):
    c = c_ref[...]
    cond = c * jax.nn.sigmoid(c)
    o_ref[...] = jnp.dot(cond, w_ref[...], preferred_element_type=F32,
                         precision=lax.Precision.HIGHEST) + b_ref[...]


def _modulation(c, w_ada, b_ada):
    B, D = c.shape
    rows = 8
    c8 = jnp.zeros((rows, D), F32).at[:B].set(c)
    n6 = w_ada.shape[1]
    out = pl.pallas_call(
        _mod_kernel,
        grid=(n6 // D,),
        in_specs=[pl.BlockSpec((rows, D), lambda j: (0, 0)),
                  pl.BlockSpec((D, D), lambda j: (0, j)),
                  pl.BlockSpec((1, D), lambda j: (0, j))],
        out_specs=pl.BlockSpec((rows, D), lambda j: (0, j)),
        out_shape=jax.ShapeDtypeStruct((rows, n6), F32),
        compiler_params=_cparams(("arbitrary",)),
        name="modulation",
    )(c8, w_ada, b_ada.reshape(1, n6))
    return out[:B].reshape(B, 6, D)


def _norm_kernel(*refs, dils):
    nx = len(refs) - 2 - len(dils)
    x_refs = refs[:nx]
    mod_ref, g_ref = refs[nx:nx + 2]
    h_refs = refs[nx + 2:]
    tm = x_refs[0].shape[1]
    for h_ref, d in zip(h_refs, dils):
        rows = tm // d
        for g in range(d):
            src = pl.ds(g, rows, stride=d) if d > 1 else slice(None)
            x = jnp.concatenate([xr[0, src, :] for xr in x_refs], axis=1)
            y = x * lax.rsqrt(jnp.mean(x * x, axis=-1, keepdims=True) + NORM_EPS) * g_ref[...]
            h = y * (1.0 + mod_ref[0, 1:2, :]) + mod_ref[0, 0:1, :]
            h_ref[0, g] = h.astype(BF16)


def _norm_modulate(x, mod, g, dils, tm=1024):
    B, S, D = x.shape
    in_specs = [pl.BlockSpec((1, tm, LANES), functools.partial(lambda b, i, j: (b, i, j), j=j))
                for j in range(D // LANES)]
    in_specs += [pl.BlockSpec((1, 6, D), lambda b, i: (b, 0, 0)),
                 pl.BlockSpec((1, D), lambda b, i: (0, 0))]
    return pl.pallas_call(
        functools.partial(_norm_kernel, dils=dils),
        grid=(B, S // tm),
        in_specs=in_specs,
        out_specs=[pl.BlockSpec((1, d, tm // d, D), lambda b, i: (b, 0, i, 0)) for d in dils],
        out_shape=[jax.ShapeDtypeStruct((B, d, S // d, D), BF16) for d in dils],
        compiler_params=_cparams(("parallel", "parallel")),
        name="norm_modulate",
    )(*([x] * (D // LANES) + [mod, g]))


def _proj_kernel(h_ref, w_ref, o_ref, *, n_plain, n_tiles):
    acc = jnp.dot(h_ref[...], w_ref[...], preferred_element_type=F32)
    if n_plain == n_tiles:
        o_ref[...] = acc.astype(o_ref.dtype)
    else:
        n = pl.program_id(1)

        @pl.when(n < n_plain)
        def _():
            o_ref[...] = acc.astype(o_ref.dtype)

        @pl.when(n >= n_plain)
        def _():
            o_ref[...] = jax.nn.sigmoid(acc).astype(o_ref.dtype)


def _projection(h, w, name, n_plain=None, out_dtype=BF16, tm=2048, tn=1024):
    B, d, L, D = h.shape
    N = w.shape[1]
    tn = min(tn, N)
    tm = min(tm, B * d * L)
    n_tiles = N // tn
    n_plain = n_tiles if n_plain is None else n_plain
    out = pl.pallas_call(
        functools.partial(_proj_kernel, n_plain=n_plain, n_tiles=n_tiles),
        grid=(B * d * L // tm, n_tiles),
        in_specs=[pl.BlockSpec((tm, D), lambda i, n: (i, 0)),
                  pl.BlockSpec((D, tn), lambda i, n: (0, n))],
        out_specs=pl.BlockSpec((tm, tn), lambda i, n: (i, n)),
        out_shape=jax.ShapeDtypeStruct((B * d * L, N), out_dtype),
        compiler_params=_cparams(("parallel", "arbitrary")),
        name=name,
    )(h.reshape(B * d * L, D), w)
    return out.reshape(B, d, L, N)


def _mla_prep_kernel(lat_ref, pos_ref, gq_ref, gkv_ref, wq_ref, wqs_ref, wk_ref, wv_ref,
                     freq_ref, sgn_ref, sel_ref, sels_ref, one_ref, q_ref, k_ref, v_ref, *, scale):
    lat = lat_ref[0]
    cq = lat[:, :MLA_Q_RANK]
    ckv = lat[:, MLA_Q_RANK:MLA_Q_RANK + MLA_KV_RANK]
    kpe = lat[:, MLA_Q_RANK + MLA_KV_RANK:]
    cqn = (cq * lax.rsqrt(jnp.mean(cq * cq, axis=-1, keepdims=True) + NORM_EPS)
           * gq_ref[...]).astype(BF16)
    ckvn = (ckv * lax.rsqrt(jnp.mean(ckv * ckv, axis=-1, keepdims=True) + NORM_EPS)
            * gkv_ref[...]).astype(BF16)
    tm = lat.shape[0]
    pos = jnp.concatenate(
        [jnp.broadcast_to(pos_ref[0, :, c * LANES:(c + 1) * LANES], (LANES, LANES)).T
         for c in range(tm // LANES)], axis=0).astype(F32)
    ang = pos * freq_ref[...]
    cos = jnp.cos(ang)
    sin = jnp.sin(ang) * sgn_ref[...]
    qa = jnp.dot(cqn, wq_ref[...], preferred_element_type=F32)
    qs = jnp.dot(cqn, wqs_ref[...], preferred_element_type=F32)
    kn = jnp.dot(ckvn, wk_ref[...], preferred_element_type=F32)
    vv = jnp.dot(ckvn, wv_ref[...], preferred_element_type=F32)
    hp = lax.Precision.HIGHEST
    rk = (jnp.dot(kpe, sel_ref[...], preferred_element_type=F32, precision=hp) * cos
          + jnp.dot(kpe, sels_ref[...], preferred_element_type=F32, precision=hp) * sin).astype(BF16)
    cos_q = cos * scale
    sin_q = sin * scale
    ones = jnp.broadcast_to(one_ref[...], (tm, LANES)).astype(BF16)
    for h in range(MLA_HEADS):
        lo = slice(h * QK_PAD, h * QK_PAD + LANES)
        hi = slice(h * QK_PAD + LANES, (h + 1) * QK_PAD)
        hd = slice(h * LANES, (h + 1) * LANES)
        q_ref[0, :, lo] = (qa[:, lo] * scale).astype(BF16)
        q_ref[0, :, hi] = (qa[:, hi] * cos_q + qs[:, hd] * sin_q).astype(BF16)
        k_ref[0, :, lo] = kn[:, hd].astype(BF16)
        k_ref[0, :, hi] = rk
        v_ref[0, :, lo] = vv[:, hd].astype(BF16)
        v_ref[0, :, hi] = ones


def _mla_prep(lat, positions, g_q, g_kv, wq, wqs, wk, wv, tm=512):
    B, S, _ = lat.shape
    H = MLA_HEADS
    half = MLA_ROPE // 2
    assert MLA_NOPE == LANES and MLA_V == LANES and QK_PAD == 2 * LANES and tm % LANES == 0
    inv_freq = ROPE_THETA ** (-jnp.arange(0, MLA_ROPE, 2, dtype=F32) / MLA_ROPE)
    freq = jnp.zeros((1, LANES), F32).at[0, :MLA_ROPE].set(jnp.concatenate([inv_freq, inv_freq]))
    sgn = jnp.zeros((1, LANES), F32).at[0, :half].set(-1.0).at[0, half:MLA_ROPE].set(1.0)
    r = jnp.arange(MLA_ROPE)
    sel = jnp.zeros((LANES, LANES), F32).at[r, r].set(1.0)
    sels = jnp.zeros((LANES, LANES), F32).at[r, (r + half) % MLA_ROPE].set(1.0)
    scale = (MLA_NOPE + MLA_ROPE) ** -0.5 * math.log2(math.e)
    ones_col = jnp.zeros((1, LANES), F32).at[0, 0].set(1.0)
    const = lambda b, i: (0, 0)
    return pl.pallas_call(
        functools.partial(_mla_prep_kernel, scale=scale),
        grid=(B, S // tm),
        in_specs=[pl.BlockSpec((1, tm, LAT_PAD), lambda b, i: (b, i, 0)),
                  pl.BlockSpec((1, 1, tm), lambda b, i: (b, 0, i)),
                  pl.BlockSpec((1, MLA_Q_RANK), const),
                  pl.BlockSpec((1, MLA_KV_RANK), const),
                  pl.BlockSpec((MLA_Q_RANK, H * QK_PAD), const),
                  pl.BlockSpec((MLA_Q_RANK, H * LANES), const),
                  pl.BlockSpec((MLA_KV_RANK, H * LANES), const),
                  pl.BlockSpec((MLA_KV_RANK, H * LANES), const),
                  pl.BlockSpec((1, LANES), const),
                  pl.BlockSpec((1, LANES), const),
                  pl.BlockSpec((LANES, LANES), const),
                  pl.BlockSpec((LANES, LANES), const),
                  pl.BlockSpec((1, LANES), const)],
        out_specs=[pl.BlockSpec((1, tm, H * QK_PAD), lambda b, i: (b, i, 0)),
                   pl.BlockSpec((1, tm, H * QK_PAD), lambda b, i: (b, i, 0)),
                   pl.BlockSpec((1, tm, H * QK_PAD), lambda b, i: (b, i, 0))],
        out_shape=[jax.ShapeDtypeStruct((B, S, H * QK_PAD), BF16),
                   jax.ShapeDtypeStruct((B, S, H * QK_PAD), BF16),
                   jax.ShapeDtypeStruct((B, S, H * QK_PAD), BF16)],
        compiler_params=_cparams(("parallel", "parallel")),
        name="mla_prep",
    )(lat, positions.reshape(B, 1, S), g_q, g_kv, wq, wqs, wk, wv, freq, sgn, sel, sels, ones_col)


def _flash_kernel(q_ref, k_ref, v_ref, o_ref, m_ref, acc_ref, s_ref, *, tq, tkc):
    S = k_ref.shape[1]
    nk = S // tkc
    nq = S // tq
    half = nk // 2
    acc_ref[...] = jnp.zeros(acc_ref.shape, F32)
    m_ref[...] = jnp.full(m_ref.shape, -jnp.inf, F32)

    def scores(qi, kj):
        qoff = pl.multiple_of(qi * tq, tq)
        koff = pl.multiple_of(kj * tkc, tkc)
        return _nt_dot(q_ref[0, pl.ds(qoff, tq), :], k_ref[0, pl.ds(koff, tkc), :])

    def consume(kj, slot, first):
        koff = pl.multiple_of(kj * tkc, tkc)
        s = s_ref[slot]
        m_prev = m_ref[...]
        if first is not None:
            m_prev = jnp.where(first, -jnp.inf, m_prev)
        m_new = jnp.maximum(m_prev, jnp.max(s, axis=-1, keepdims=True))
        alpha = jnp.exp2(m_prev - m_new)
        p = jnp.exp2(s - m_new).astype(BF16)
        acc_ref[...] = alpha * acc_ref[...] + jnp.dot(p, v_ref[0, pl.ds(koff, tkc), :],
                                                      preferred_element_type=F32)
        m_ref[...] = m_new

    s_ref[0] = scores(0, 0)

    def body(t, carry):
        qi = t // half
        kj = 2 * (t % half)
        s_ref[1] = scores(qi, kj + 1)
        consume(kj, 0, kj == 0)
        t1 = jnp.minimum(t + 1, nq * half - 1)
        s_ref[0] = scores(t1 // half, 2 * (t1 % half))
        consume(kj + 1, 1, None)

        @pl.when(kj + 2 == nk)
        def _():
            acc = acc_ref[...]
            qoff = pl.multiple_of(qi * tq, tq)
            o_ref[0, pl.ds(qoff, tq), :] = (acc[:, :MLA_V] / acc[:, MLA_V:MLA_V + 1]).astype(BF16)
        return carry

    lax.fori_loop(0, nq * half, body, 0)


def _flash_attention(q, k, v, tq=1024, tkc=1024):
    B, S, _ = q.shape
    H = MLA_HEADS
    assert (S // tkc) % 2 == 0 and S % tq == 0
    whole = pl.BlockSpec((1, S, QK_PAD), lambda b, h: (b, 0, h))
    return pl.pallas_call(
        functools.partial(_flash_kernel, tq=tq, tkc=tkc),
        grid=(B, H),
        in_specs=[whole, whole, whole],
        out_specs=pl.BlockSpec((1, S, MLA_V), lambda b, h: (b, 0, h)),
        out_shape=jax.ShapeDtypeStruct((B, S, H * MLA_V), BF16),
        scratch_shapes=[pltpu.VMEM((tq, 1), F32), pltpu.VMEM((tq, QK_PAD), F32),
                        pltpu.VMEM((2, tq, tkc), F32)],
        compiler_params=_cparams(("parallel", "parallel")),
        name="mla_flash",
    )(q, k, v)


def _t5_bucket(rel):
    nb = REL_BUCKETS // 2
    max_exact = nb // 2
    ret = jnp.where(rel > 0, nb, 0)
    n = jnp.abs(rel)
    nf = jnp.maximum(n, 1).astype(F32)
    large = max_exact + (jnp.log(nf / max_exact) / math.log(REL_MAX_DIST / max_exact)
                         * (nb - max_exact)).astype(I32)
    large = jnp.minimum(large, nb - 1)
    return ret + jnp.where(n < max_exact, n, large)


def _bias_lookup(tab_ref, h, bucket):
    rows, nk = bucket.shape
    tab = jnp.broadcast_to(tab_ref[h:h + 1, :], (rows, LANES))
    return jnp.concatenate(
        [jnp.take_along_axis(tab, bucket[:, c * LANES:(c + 1) * LANES], axis=1)
         for c in range(nk // LANES)], axis=1)


def _band_bias_kernel(tab_ref, band_ref, *, r, dil):
    _, ts, nk = band_ref.shape
    off = lax.broadcasted_iota(I32, (ts, nk), 1) - r - lax.broadcasted_iota(I32, (ts, nk), 0)
    bucket = jnp.where(jnp.abs(off) <= r, _t5_bucket(off * dil), REL_BUCKETS)
    for h in range(DIL_HEADS):
        band_ref[h] = _bias_lookup(tab_ref, h, bucket)


def _band_bias(tab, r, dil, ts):
    nk = ts + 2 * r
    return pl.pallas_call(
        functools.partial(_band_bias_kernel, r=r, dil=dil),
        grid=(1,),
        in_specs=[pl.BlockSpec((DIL_HEADS, LANES), lambda i: (0, 0))],
        out_specs=pl.BlockSpec((DIL_HEADS, ts, nk), lambda i: (0, 0, 0)),
        out_shape=jax.ShapeDtypeStruct((DIL_HEADS, ts, nk), F32),
        compiler_params=_cparams(("arbitrary",)),
        name=f"band_bias_d{dil}",
    )(tab)


def _dilated_kernel(q_ref, kp_ref, kc_ref, kn_ref, vp_ref, vc_ref, vn_ref, pq_ref, pk_ref, tab_ref,
                    band_ref, o_ref, l_ref, kcat_ref, vcat_ref, s_scr, *, r, tq, ts, n_rows, dil):
    n = pl.program_id(2)
    nsub = tq // ts
    nk = ts + 2 * r
    kcat_ref[0:r, :] = kp_ref[0, 0]
    kcat_ref[r:r + tq, :] = kc_ref[0, 0]
    kcat_ref[r + tq:tq + 2 * r, :] = kn_ref[0, 0]
    ones = jnp.ones((tq + 2 * r, DIL_HEAD_DIM), BF16)
    for h in range(DIL_HEADS):
        sl = slice(h * DIL_HEAD_DIM, (h + 1) * DIL_HEAD_DIM)
        dst = slice(2 * h * DIL_HEAD_DIM, (2 * h + 1) * DIL_HEAD_DIM)
        vcat_ref[0:r, dst] = vp_ref[0, 0, :, sl]
        vcat_ref[r:r + tq, dst] = vc_ref[0, 0, :, sl]
        vcat_ref[r + tq:tq + 2 * r, dst] = vn_ref[0, 0, :, sl]
        vcat_ref[:, (2 * h + 1) * DIL_HEAD_DIM:(2 * h + 2) * DIL_HEAD_DIM] = ones
    lane_grp = lax.broadcasted_iota(I32, (ts, LANES), 1) // (LANES // DIL_HEADS)
    scale = DIL_HEAD_DIM ** -0.5 * math.log2(math.e)
    for j in range(nsub):
        qi = n * tq + j * ts + lax.broadcasted_iota(I32, (ts, nk), 0)
        kj = n * tq + j * ts - r + lax.broadcasted_iota(I32, (ts, nk), 1)
        inside = (kj >= 0) & (kj < n_rows)
        valid = (jnp.abs(kj - qi) <= r) & inside
        pq = jnp.broadcast_to(pq_ref[0, 0, :, j * ts:(j + 1) * ts], (LANES, ts)).T
        pk = pk_ref[0, 0, j]
        rel = jnp.concatenate([pk[:, c * LANES:(c + 1) * LANES] - pq
                               for c in range(nk // LANES)], axis=1)
        regular = jnp.max(jnp.where(valid & (rel != (kj - qi) * dil), 1, 0)) == 0

        def logits(h):
            sl = slice(h * DIL_HEAD_DIM, (h + 1) * DIL_HEAD_DIM)
            return _nt_dot(q_ref[0, 0, j * ts:(j + 1) * ts, sl], kcat_ref[j * ts:j * ts + nk, sl]) * scale

        @pl.when(regular)
        def _():
            for h in range(DIL_HEADS):
                s_scr[j * DIL_HEADS + h] = jnp.where(inside, logits(h) + band_ref[h], NEG_INF)

        @pl.when(jnp.logical_not(regular))
        def _():
            bucket = jnp.where(valid, _t5_bucket(rel), REL_BUCKETS)
            for h in range(DIL_HEADS):
                s_scr[j * DIL_HEADS + h] = logits(h) + _bias_lookup(tab_ref, h, bucket)
    for j in range(nsub):
        lse_tile = jnp.zeros((ts, LANES), F32)
        for h in range(DIL_HEADS):
            sl = slice(h * DIL_HEAD_DIM, (h + 1) * DIL_HEAD_DIM)
            s = s_scr[j * DIL_HEADS + h]
            m = jnp.max(s, axis=-1, keepdims=True)
            p = jnp.exp2(s - m).astype(BF16)
            res = jnp.dot(p, vcat_ref[j * ts:j * ts + nk,
                                      2 * h * DIL_HEAD_DIM:(2 * h + 2) * DIL_HEAD_DIM],
                          preferred_element_type=F32)
            den = res[:, DIL_HEAD_DIM:]
            o_ref[0, 0, j * ts:(j + 1) * ts, sl] = (res[:, :DIL_HEAD_DIM] / den).astype(BF16)
            lse_tile = jnp.where(lane_grp == h, m * math.log(2.0) + jnp.log(den), lse_tile)
        l_ref[0, 0, j * ts:(j + 1) * ts, :] = lse_tile


def _dilated_group(qkv, positions, table, gi, window, d, tq=512, ts=128):
    B, _, L, _ = qkv.shape
    HD = DIL_HEADS * DIL_HEAD_DIM
    r = window // (2 * d)
    tq = min(tq, L)
    nt = L // tq
    nsub = tq // ts
    rb = tq // r
    nk = ts + 2 * r
    pcls = positions.reshape(B, L, d).transpose(0, 2, 1)
    pq = pcls.reshape(B, d, 1, L)
    ppad = jnp.pad(pcls, ((0, 0), (0, 0), (r, ts + r)))
    pk = jnp.concatenate([ppad[:, :, :L].reshape(B, d, L // ts, ts),
                          ppad[:, :, ts:ts + L].reshape(B, d, L // ts, ts)[..., :2 * r]], axis=-1)
    pk = pk.reshape(B, d, L // ts, 1, nk)
    tab = jnp.zeros((DIL_HEADS, LANES), F32).at[:, :REL_BUCKETS].set(
        table[:, gi * DIL_HEADS:(gi + 1) * DIL_HEADS].T * math.log2(math.e))
    tab = tab.at[:, REL_BUCKETS].set(NEG_INF)

    def cur(c):
        return pl.BlockSpec((1, 1, tq, HD), lambda b, g, n: (b, g, n, c))

    def prv(c):
        return pl.BlockSpec((1, 1, r, HD), lambda b, g, n: (b, g, jnp.maximum(n * rb - 1, 0), c))

    def nxt(c):
        return pl.BlockSpec((1, 1, r, HD),
                            lambda b, g, n: (b, g, jnp.minimum((n + 1) * rb, L // r - 1), c))

    return pl.pallas_call(
        functools.partial(_dilated_kernel, r=r, tq=tq, ts=ts, n_rows=L, dil=d),
        grid=(B, d, nt),
        in_specs=[cur(0), prv(1), cur(1), nxt(1), prv(2), cur(2), nxt(2),
                  pl.BlockSpec((1, 1, 1, tq), lambda b, g, n: (b, g, 0, n)),
                  pl.BlockSpec((1, 1, nsub, 1, nk), lambda b, g, n: (b, g, n, 0, 0)),
                  pl.BlockSpec((DIL_HEADS, LANES), lambda b, g, n: (0, 0)),
                  pl.BlockSpec((DIL_HEADS, ts, nk), lambda b, g, n: (0, 0, 0))],
        out_specs=[pl.BlockSpec((1, 1, tq, HD), lambda b, g, n: (b, g, n, 0)),
                   pl.BlockSpec((1, 1, tq, LANES), lambda b, g, n: (b, g, n, 0))],
        out_shape=[jax.ShapeDtypeStruct((B, d, L, HD), BF16),
                   jax.ShapeDtypeStruct((B, d, L, LANES), F32)],
        scratch_shapes=[pltpu.VMEM((tq + 2 * r, HD), BF16), pltpu.VMEM((tq + 2 * r, 2 * HD), BF16),
                        pltpu.VMEM((nsub * DIL_HEADS, ts, nk), F32)],
        compiler_params=_cparams(("parallel", "parallel", "parallel")),
        name=f"dilated_g{gi}",
    )(*([qkv] * 7 + [pq, pk, tab, _band_bias(tab, r, d, ts)]))


def _outproj_kernel(*refs, dils):
    ng = len(dils)
    oa_ref = refs[0]
    og_refs = refs[1:1 + ng]
    lg_refs = refs[1 + ng:1 + 2 * ng]
    (ga_ref, gb_ref, x_ref, mod_ref, g_ref, wo_ref, wrh_ref, wrl_ref,
     x1_ref, h2_ref, aff_ref, afft_ref, o_scr, l_scr) = refs[1 + 2 * ng:]
    tm = x_ref.shape[1]
    for i, d in enumerate(dils):
        rows = tm // d
        for g in range(d):
            dst = pl.ds(g, rows, stride=d) if d > 1 else slice(None)
            og = og_refs[i][0, g].astype(F32)
            for h in range(DIL_HEADS):
                o_scr[i, h, dst, :] = og[:, h * DIL_HEAD_DIM:(h + 1) * DIL_HEAD_DIM]
            l_scr[i, dst, :] = lg_refs[i][0, g]
    lses = [l_scr[i] for i in range(ng)]
    mx = functools.reduce(jnp.maximum, lses)
    es = [jnp.exp(l - mx) for l in lses]
    tot = functools.reduce(lambda a, b: a + b, es)
    wts = [e / tot for e in es]
    cols = []
    for h in range(DIL_HEADS):
        c = h * (LANES // DIL_HEADS)
        cols.append(functools.reduce(
            lambda a, b: a + b, [wts[i][:, c:c + 1] * o_scr[i, h] for i in range(ng)]))
    o_b = jnp.concatenate(cols, axis=1)
    comb = (ga_ref[0, 0].astype(F32) * oa_ref[0].astype(F32)
            + gb_ref[0, 0].astype(F32) * o_b).astype(BF16)
    x1 = x_ref[0] + mod_ref[0, 2:3, :] * jnp.dot(comb, wo_ref[...], preferred_element_type=F32)
    x1_ref[0] = x1
    y = x1 * lax.rsqrt(jnp.mean(x1 * x1, axis=-1, keepdims=True) + NORM_EPS) * g_ref[...]
    h2 = y * (1.0 + mod_ref[0, 4:5, :]) + mod_ref[0, 3:4, :]
    h2_hi = h2.astype(BF16)
    h2_ref[0] = h2_hi
    h2_lo = (h2 - h2_hi.astype(F32)).astype(BF16)
    logits = (jnp.dot(h2_hi, wrh_ref[...], preferred_element_type=F32)
              + jnp.dot(h2_lo, wrh_ref[...], preferred_element_type=F32)
              + jnp.dot(h2_hi, wrl_ref[...], preferred_element_type=F32))
    lane = lax.broadcasted_iota(I32, logits.shape, 1)
    logits = jnp.where(lane < N_EXPERTS, logits, -jnp.inf)
    e = jnp.exp(logits - jnp.max(logits, axis=-1, keepdims=True))
    aff = e / jnp.sum(e, axis=-1, keepdims=True)
    aff_ref[0] = aff
    afft_ref[0, 0] = aff.T[:N_EXPERTS, :]


def _out_projection(o_a, dil_outs, main, gate_col, x, mod, g, w_out, w_router, tm=512):
    B, S, D = x.shape
    const = lambda b, i: (0, 0)
    row = lambda b, i: (b, i, 0)
    wr = jnp.zeros((D, LANES), F32).at[:, :N_EXPERTS].set(w_router)
    wr_hi = wr.astype(BF16)
    wr_lo = (wr - wr_hi.astype(F32)).astype(BF16)
    dils = tuple(o.shape[1] for o, _ in dil_outs)
    og_specs = [pl.BlockSpec((1, d, tm // d, D), lambda b, i: (b, 0, i, 0)) for d in dils]
    lg_specs = [pl.BlockSpec((1, d, tm // d, LANES), lambda b, i: (b, 0, i, 0)) for d in dils]
    return pl.pallas_call(
        functools.partial(_outproj_kernel, dils=dils),
        grid=(B, S // tm),
        in_specs=[pl.BlockSpec((1, tm, D), row)] + og_specs + lg_specs + [
                  pl.BlockSpec((1, 1, tm, D), lambda b, i: (b, 0, i, gate_col)),
                  pl.BlockSpec((1, 1, tm, D), lambda b, i: (b, 0, i, gate_col + 1)),
                  pl.BlockSpec((1, tm, D), row),
                  pl.BlockSpec((1, 6, D), lambda b, i: (b, 0, 0)),
                  pl.BlockSpec((1, D), const),
                  pl.BlockSpec((D, D), const),
                  pl.BlockSpec((D, LANES), const),
                  pl.BlockSpec((D, LANES), const)],
        out_specs=[pl.BlockSpec((1, tm, D), row),
                   pl.BlockSpec((1, tm, D), row),
                   pl.BlockSpec((1, tm, LANES), row),
                   pl.BlockSpec((1, 1, N_EXPERTS, tm), lambda b, i: (b, i, 0, 0))],
        out_shape=[jax.ShapeDtypeStruct((B, S, D), F32),
                   jax.ShapeDtypeStruct((B, S, D), BF16),
                   jax.ShapeDtypeStruct((B, S, LANES), F32),
                   jax.ShapeDtypeStruct((B, S // tm, N_EXPERTS, tm), F32)],
        scratch_shapes=[pltpu.VMEM((len(dils), DIL_HEADS, tm, DIL_HEAD_DIM), F32),
                        pltpu.VMEM((len(dils), tm, LANES), F32)],
        compiler_params=_cparams(("parallel", "parallel")),
        name="out_projection",
    )(o_a, *[o for o, _ in dil_outs], *[l for _, l in dil_outs], main, main, x, mod, g, w_out, wr_hi, wr_lo)


def _select_kernel(afft_ref, thr_ref, tie_ref, slot_ref, boff_ref, *, cap, chunk):
    aff = jnp.concatenate([afft_ref[0, c] for c in range(afft_ref.shape[1])], axis=1)
    E, S = aff.shape
    bits = lax.bitcast_convert_type(aff, I32)

    def count(mask):
        return jnp.sum(mask.astype(I32), axis=1, keepdims=True)

    def thr_step(i, v):
        cand = v | jnp.left_shift(jnp.int32(1), 30 - i)
        return jnp.where(count(bits >= cand) >= cap, cand, v)

    thr = lax.fori_loop(0, 31, thr_step, jnp.zeros((E, 1), I32))
    need = cap - count(bits > thr)
    eq = bits == thr
    idx = lax.broadcasted_iota(I32, (E, S), 1)
    nbits = max(1, (S - 1).bit_length())

    def tie_step(i, j):
        cand = j | jnp.left_shift(jnp.int32(1), nbits - 1 - i)
        return jnp.where(count(eq & (idx < cand)) < need, cand, j)

    tie = lax.fori_loop(0, nbits, tie_step, jnp.zeros((E, 1), I32))
    sel = (bits > thr) | (eq & (idx <= tie))
    thr_ref[0] = jnp.broadcast_to(thr, (E, LANES))
    tie_ref[0] = jnp.broadcast_to(tie, (E, LANES))

    upper = (lax.broadcasted_iota(I32, (chunk, chunk), 0)
             <= lax.broadcasted_iota(I32, (chunk, chunk), 1)).astype(BF16)
    carry = jnp.zeros((E, 1), F32)
    self_ = sel.astype(F32)
    lane = lax.broadcasted_iota(I32, (E, LANES), 1)
    boff = jnp.zeros((E, LANES), I32)
    for c in range(S // chunk):
        boff = jnp.where(lane == c, carry.astype(I32), boff)
        scf = self_[:, c * chunk:(c + 1) * chunk]
        incl = jnp.dot(scf.astype(BF16), upper, preferred_element_type=F32)
        pos = (incl - scf + carry).astype(I32)
        slot_ref[0, c] = jnp.where(scf > 0.0, pos, -1)
        carry = carry + incl[:, chunk - 1:chunk]
    boff_ref[0] = jnp.where(lane == S // chunk, carry.astype(I32), boff)


def _select(aff_t, cap, chunk):
    B, nblk, E, _ = aff_t.shape
    assert aff_t.shape[3] == chunk and nblk < LANES
    rows = pl.BlockSpec((1, nblk, E, chunk), lambda b: (b, 0, 0, 0))
    return pl.pallas_call(
        functools.partial(_select_kernel, cap=cap, chunk=chunk),
        grid=(B,),
        in_specs=[rows],
        out_specs=[pl.BlockSpec((1, E, LANES), lambda b: (b, 0, 0)),
                   pl.BlockSpec((1, E, LANES), lambda b: (b, 0, 0)),
                   rows,
                   pl.BlockSpec((1, E, LANES), lambda b: (b, 0, 0))],
        out_shape=[jax.ShapeDtypeStruct((B, E, LANES), I32),
                   jax.ShapeDtypeStruct((B, E, LANES), I32),
                   jax.ShapeDtypeStruct((B, nblk, E, chunk), I32),
                   jax.ShapeDtypeStruct((B, E, LANES), I32)],
        compiler_params=_cparams(("parallel",)),
        name="ec_select",
    )(aff_t)


def _gather_kernel(boff_ref, slot_ref, afft_ref, h_ref, xe_ref, gate_ref, acc_ref, gacc_ref, *, win):
    _, nblk, _, tbk = slot_ref.shape
    cap = xe_ref.shape[2]
    e = pl.program_id(1)
    base = (pl.program_id(0) * pl.num_programs(1) + e) * (nblk + 1)
    acc_ref[...] = jnp.zeros(acc_ref.shape, F32)
    gacc_ref[...] = jnp.zeros(gacc_ref.shape, F32)

    def chunk(tb, carry):
        lo = boff_ref[base + tb]
        hi = boff_ref[base + tb + 1]
        first = (lo // 8) * 8
        off = pl.multiple_of(tb * tbk, tbk)
        slot_row = slot_ref[0, tb, pl.ds(e, 1), :]
        aff_row = afft_ref[0, tb, pl.ds(e, 1), :]

        def window(k, c):
            w0 = pl.multiple_of(first + k * win, 8)
            hit = lax.broadcasted_iota(I32, (win, tbk), 0) + w0 == slot_row
            acc_ref[pl.ds(w0, win), :] += jnp.dot(hit.astype(BF16), h_ref[0, pl.ds(off, tbk), :],
                                                  preferred_element_type=F32)
            gacc_ref[pl.ds(w0, win), :] += jnp.sum(jnp.where(hit, aff_row, 0.0),
                                                   axis=1, keepdims=True)
            return c

        lax.fori_loop(0, (hi - first + win - 1) // win, window, 0)
        return carry

    lax.fori_loop(0, nblk, chunk, 0)
    xe_ref[0, 0] = acc_ref[0:cap, :].astype(BF16)
    gate_ref[0, 0] = gacc_ref[0:cap, :]


def _gather(boff_flat, slot, aff_t, h2, cap, chunk, win=128):
    B, nblk, E, _ = slot.shape
    S, D = h2.shape[1:]
    rowblk = pl.BlockSpec((1, nblk, E, chunk), lambda b, e, off: (b, 0, 0, 0))
    return pl.pallas_call(
        functools.partial(_gather_kernel, win=win),
        grid_spec=pltpu.PrefetchScalarGridSpec(
            num_scalar_prefetch=1,
            grid=(B, E),
            in_specs=[rowblk, rowblk,
                      pl.BlockSpec((1, S, D), lambda b, e, off: (b, 0, 0))],
            out_specs=[pl.BlockSpec((1, 1, cap, D), lambda b, e, off: (b, e, 0, 0)),
                       pl.BlockSpec((1, 1, cap, 1), lambda b, e, off: (b, e, 0, 0))],
            scratch_shapes=[pltpu.VMEM((cap + win, D), F32), pltpu.VMEM((cap + win, 1), F32)]),
        out_shape=[jax.ShapeDtypeStruct((B, E, cap, D), BF16),
                   jax.ShapeDtypeStruct((B, E, cap, 1), F32)],
        compiler_params=_cparams(("parallel", "arbitrary")),
        name="ec_gather",
    )(boff_flat, slot, aff_t, h2)


def _ffn_kernel(xe_ref, gate_ref, wg_ref, wu_ref, wd_ref, ye_ref, acc_ref):
    f = pl.program_id(2)
    nb, _, cap, D = xe_ref.shape

    @pl.when((pl.program_id(0) == 0) & (pl.program_id(1) == 0) & (f == 0))
    def _():
        acc_ref[...] = jnp.zeros(acc_ref.shape, F32)

    x = xe_ref[...].reshape(nb * cap, D)
    g = jnp.dot(x, wg_ref[0].astype(BF16), preferred_element_type=F32)
    u = jnp.dot(x, wu_ref[0].astype(BF16), preferred_element_type=F32)
    hid = (g * jax.nn.sigmoid(g) * u).astype(BF16)
    part = jnp.dot(hid, wd_ref[0].astype(BF16), preferred_element_type=F32)
    acc_ref[...] = jnp.where(f == 0, 0.0, acc_ref[...]) + part

    @pl.when(f == pl.num_programs(2) - 1)
    def _():
        gate = gate_ref[...].reshape(nb * cap, 1)
        ye_ref[...] = (acc_ref[...] * gate).astype(BF16).reshape(ye_ref.shape)


def _expert_ffn(xe, gates, w_gate, w_up, w_down, nb=2, tf=512):
    B, E, cap, D = xe.shape
    F = w_gate.shape[-1]
    return pl.pallas_call(
        _ffn_kernel,
        grid=(E, B // nb, F // tf),
        in_specs=[pl.BlockSpec((nb, 1, cap, D), lambda e, b, f: (b, e, 0, 0)),
                  pl.BlockSpec((nb, 1, cap, 1), lambda e, b, f: (b, e, 0, 0)),
                  pl.BlockSpec((1, D, tf), lambda e, b, f: (e, 0, f)),
                  pl.BlockSpec((1, D, tf), lambda e, b, f: (e, 0, f)),
                  pl.BlockSpec((1, tf, D), lambda e, b, f: (e, f, 0))],
        out_specs=pl.BlockSpec((nb, 1, cap, D), lambda e, b, f: (b, e, 0, 0)),
        out_shape=jax.ShapeDtypeStruct((B, E, cap, D), BF16),
        scratch_shapes=[pltpu.VMEM((nb * cap, D), F32)],
        compiler_params=_cparams(("arbitrary", "arbitrary", "arbitrary")),
        name="ec_ffn",
    )(xe, gates, w_gate, w_up, w_down)


def _combine_kernel(boff_ref, aff_ref, thr_ref, tie_ref, ye_ref, x1_ref, mod_ref, g_ref, out_ref,
                    slot_ref, carry_ref, *, tb, wslot, nblk, epg):
    b = pl.program_id(0)
    t2 = pl.program_id(1)
    e = pl.program_id(2)
    nsub = aff_ref.shape[1] // tb
    cap = ye_ref.shape[2]
    lane = lax.broadcasted_iota(I32, (tb, LANES), 1)

    @pl.when(e == 0)
    def _():
        @pl.when(t2 == 0)
        def _():
            carry_ref[...] = jnp.zeros(carry_ref.shape, F32)

        lower = (lax.broadcasted_iota(I32, (tb, tb), 0)
                 >= lax.broadcasted_iota(I32, (tb, tb), 1)).astype(BF16)
        for u in range(nsub):
            rows = slice(u * tb, (u + 1) * tb)
            bits = lax.bitcast_convert_type(aff_ref[0, rows, :], I32)
            tok = (t2 * nsub + u) * tb + lax.broadcasted_iota(I32, (tb, LANES), 0)
            thr = thr_ref[0]
            sel = ((bits > thr) | ((bits == thr) & (tok <= tie_ref[0]))) & (lane < N_EXPERTS)
            self_ = sel.astype(F32)
            incl = jnp.dot(lower, self_.astype(BF16), preferred_element_type=F32)
            pos = (incl - self_ + carry_ref[...]).astype(I32)
            slot_ref[rows, :] = jnp.where(sel, pos, -1)
            carry_ref[...] = carry_ref[...] + incl[tb - 1:tb, :]
        out_ref[...] = jnp.zeros(out_ref.shape, F32)

    n_e = pl.num_programs(2) * epg
    for u in range(nsub):
        rows = slice(u * tb, (u + 1) * tb)
        slot_x, first_x, trips = [], [], jnp.int32(0)
        for x in range(epg):
            ex = e * epg + x
            base = (b * n_e + ex) * (nblk + 1) + t2 * nsub + u
            first = (boff_ref[base] // SUBLANES_BF16) * SUBLANES_BF16
            slot_x.append(jnp.sum(jnp.where(lane == ex, slot_ref[rows, :], 0),
                                  axis=1, keepdims=True))
            first_x.append(first)
            trips = jnp.maximum(trips, (boff_ref[base + 1] - first + wslot - 1) // wslot)

        def body(k, carry):
            hots, wins = [], []
            for x in range(epg):
                start = first_x[x] + k * wslot
                woff = pl.multiple_of(jnp.minimum(start, cap - wslot), SUBLANES_BF16)
                todo = jnp.where(slot_x[x] >= start, slot_x[x], -1)
                hots.append((lax.broadcasted_iota(I32, (tb, wslot), 1) + woff == todo).astype(BF16))
                wins.append(ye_ref[0, x, pl.ds(woff, wslot), :])
            out_ref[0, rows, :] += jnp.dot(jnp.concatenate(hots, axis=1),
                                           jnp.concatenate(wins, axis=0),
                                           preferred_element_type=F32)
            return carry

        lax.fori_loop(0, trips, body, 0)

    @pl.when(e == pl.num_programs(2) - 1)
    def _():
        for u in range(nsub):
            rows = slice(u * tb, (u + 1) * tb)
            x2 = x1_ref[0, rows, :] + mod_ref[0, 5:6, :] * out_ref[0, rows, :]
            out_ref[0, rows, :] = (x2 * lax.rsqrt(jnp.mean(x2 * x2, axis=-1, keepdims=True)
                                                  + NORM_EPS) * g_ref[...])


def _combine(boff_flat, aff, thr, tie, ye, x1, mod, g_final, tb, tsup=2048, wslot=128, epg=2):
    B, S, D = x1.shape
    E, cap = ye.shape[1], ye.shape[2]
    assert cap % wslot == 0 and E % epg == 0
    tsup = min(tsup, S)
    return pl.pallas_call(
        functools.partial(_combine_kernel, tb=tb, wslot=wslot, nblk=S // tb, epg=epg),
        grid_spec=pltpu.PrefetchScalarGridSpec(
            num_scalar_prefetch=1,
            grid=(B, S // tsup, E // epg),
            in_specs=[pl.BlockSpec((1, tsup, LANES), lambda b, t, e, off: (b, t, 0)),
                      pl.BlockSpec((1, 1, LANES), lambda b, t, e, off: (b, 0, 0)),
                      pl.BlockSpec((1, 1, LANES), lambda b, t, e, off: (b, 0, 0)),
                      pl.BlockSpec((1, epg, cap, D), lambda b, t, e, off: (b, e, 0, 0)),
                      pl.BlockSpec((1, tsup, D), lambda b, t, e, off: (b, t, 0)),
                      pl.BlockSpec((1, 6, D), lambda b, t, e, off: (b, 0, 0)),
                      pl.BlockSpec((1, D), lambda b, t, e, off: (0, 0))],
            out_specs=pl.BlockSpec((1, tsup, D), lambda b, t, e, off: (b, t, 0)),
            scratch_shapes=[pltpu.VMEM((tsup, LANES), I32), pltpu.VMEM((1, LANES), F32)]),
        out_shape=jax.ShapeDtypeStruct((B, S, D), F32),
        compiler_params=_cparams(("parallel", "arbitrary", "arbitrary")),
        name="ec_combine",
    )(boff_flat, aff, thr, tie, ye, x1, mod, g_final)


def _prep_weights(w_in, w_uq, w_ukv):
    D = w_in.shape[0]
    H = MLA_HEADS
    n_lat = MLA_Q_RANK + MLA_KV_RANK + MLA_ROPE
    n_grp = 3 * DIL_HEADS * DIL_HEAD_DIM
    n_dil = len(DIL_GROUPS) * n_grp
    w_lat = jnp.zeros((D, LAT_PAD), F32).at[:, :n_lat].set(w_in[:, :n_lat]).astype(BF16)
    w_grp = [w_in[:, n_lat + i * n_grp:n_lat + (i + 1) * n_grp].astype(BF16)
             for i in range(len(DIL_GROUPS))]
    w_gates = w_in[:, n_lat + n_dil:].astype(BF16)
    half = MLA_ROPE // 2
    uq = w_uq.reshape(MLA_Q_RANK, H, MLA_NOPE + MLA_ROPE)
    pe = uq[:, :, MLA_NOPE:]
    zq = jnp.zeros((MLA_Q_RANK, H, QK_PAD - MLA_NOPE - MLA_ROPE), F32)
    wq = jnp.concatenate([uq[:, :, :MLA_NOPE], pe, zq], axis=2)
    wqs = jnp.concatenate([pe[:, :, half:], pe[:, :, :half], zq], axis=2)
    ukv = w_ukv.reshape(MLA_KV_RANK, H, MLA_NOPE + MLA_V)
    wk = ukv[:, :, :MLA_NOPE]
    wv = ukv[:, :, MLA_NOPE:]
    return (w_lat, w_grp, w_gates, wq.reshape(MLA_Q_RANK, H * QK_PAD).astype(BF16),
            wqs.reshape(MLA_Q_RANK, H * LANES).astype(BF16),
            wk.reshape(MLA_KV_RANK, H * MLA_NOPE).astype(BF16),
            wv.reshape(MLA_KV_RANK, H * MLA_V).astype(BF16))


def kernel(x, c, positions, w_ada, b_ada, g_norm_mix, w_in, g_q_lat, g_kv_lat, w_uq, w_ukv,
           rel_bias, w_out, g_norm_ffn, w_router, w_gate, w_up, w_down, g_final):
    B, S, D = x.shape
    assert w_ada.shape[0] == 1, "the final norm is fused into the (single) layer's last kernel"
    assert DIL_GROUPS[0][1] == 1, "the gates ride along with the undilated group's projection"
    for l in range(w_ada.shape[0]):
        mod = _modulation(c, w_ada[l], b_ada[l])
        w_lat, w_grp, w_gates, wq, wqs, wk, wv = _prep_weights(w_in[l], w_uq[l], w_ukv[l])
        g_mix = g_norm_mix[l].reshape(1, D)
        n_qkv = w_grp[0].shape[1] // D
        hs = _norm_modulate(x, mod, g_mix, tuple(d for _, d in DIL_GROUPS))
        lat = _projection(hs[0], w_lat, "lat_projection", out_dtype=F32)
        main = _projection(hs[0], jnp.concatenate([w_grp[0], w_gates], axis=1), "in_projection_g0",
                           n_plain=n_qkv)
        qkvs = [main] + [_projection(hs[gi], w_grp[gi], f"in_projection_g{gi}")
                         for gi in range(1, len(DIL_GROUPS))]
        q, k, v = _mla_prep(lat.reshape(B, S, LAT_PAD), positions, g_q_lat[l].reshape(1, -1), g_kv_lat[l].reshape(1, -1),
                            wq, wqs, wk, wv)
        o_a = _flash_attention(q, k, v)
        dil_outs = [_dilated_group(qkvs[gi], positions, rel_bias, gi, window, d)
                    for gi, (window, d) in enumerate(DIL_GROUPS)]
        x1, h2, aff, aff_t = _out_projection(o_a, dil_outs, main, n_qkv, x, mod,
                                             g_norm_ffn[l].reshape(1, D),
                                             w_out[l].astype(BF16), w_router[l])
        cap = EC_CAPACITY_FACTOR * S // N_EXPERTS
        chunk = 512
        thr, tie, slot, boff = _select(aff_t, cap, chunk)
        boff_flat = boff[:, :, :S // chunk + 1].reshape(-1)
        xe, gates = _gather(boff_flat, slot, aff_t, h2, cap, chunk)
        ye = _expert_ffn(xe, gates, w_gate[l], w_up[l], w_down[l])
        pad = jnp.zeros((B, 1, LANES - N_EXPERTS), I32)
        thr_l = jnp.concatenate([thr[:, :, 0].reshape(B, 1, N_EXPERTS), pad], axis=2)
        tie_l = jnp.concatenate([tie[:, :, 0].reshape(B, 1, N_EXPERTS), pad], axis=2)
        x = _combine(boff_flat, aff, thr_l, tie_l, ye, x1, mod, g_final.reshape(1, D), tb=chunk)
    return x
```

```python
import functools
import math

import jax
import jax.numpy as jnp
from jax import lax
from jax.experimental import pallas as pl
from jax.experimental.pallas import tpu as pltpu

F32 = jnp.float32
BF16 = jnp.bfloat16
I32 = jnp.int32

MLA_HEADS = 8
MLA_Q_RANK = 384
MLA_KV_RANK = 256
MLA_NOPE = 128
MLA_ROPE = 64
MLA_V = 128
ROPE_THETA = 10000.0
DIL_GROUPS = ((128, 1), (512, 4), (2048, 16))
DIL_HEADS = 8
DIL_HEAD_DIM = 128
REL_BUCKETS = 32
REL_MAX_DIST = 1024
N_EXPERTS = 16
EC_CAPACITY_FACTOR = 2
NORM_EPS = 1e-6
NEG_INF = -1e30

LANES = 128
SUBLANES_BF16 = 16
QK_PAD = 256
LAT_PAD = 768
VMEM_LIMIT = 56 * 1024 * 1024


def _cparams(sem):
    return pltpu.CompilerParams(dimension_semantics=sem, vmem_limit_bytes=VMEM_LIMIT)


def _nt_dot(a, b):
    return lax.dot_general(a, b, (((1,), (1,)), ((), ())), preferred_element_type=F32)


def _mod_kernel(c_ref, w_ref, b_ref, o_ref):
    c = c_ref[...]
    cond = c * jax.nn.sigmoid(c)
    o_ref[...] = jnp.dot(cond, w_ref[...], preferred_element_type=F32,
                         precision=lax.Precision.HIGHEST) + b_ref[...]


def _modulation(c, w_ada, b_ada):
    B, D = c.shape
    rows = 8
    c8 = jnp.zeros((rows, D), F32).at[:B].set(c)
    n6 = w_ada.shape[1]
    out = pl.pallas_call(
        _mod_kernel,
        grid=(n6 // D,),
        in_specs=[pl.BlockSpec((rows, D), lambda j: (0, 0)),
                  pl.BlockSpec((D, D), lambda j: (0, j)),
                  pl.BlockSpec((1, D), lambda j: (0, j))],
        out_specs=pl.BlockSpec((rows, D), lambda j: (0, j)),
        out_shape=jax.ShapeDtypeStruct((rows, n6), F32),
        compiler_params=_cparams(("arbitrary",)),
        name="modulation",
    )(c8, w_ada, b_ada.reshape(1, n6))
    return out[:B].reshape(B, 6, D)


def _norm_kernel(*refs, dils):
    nx = len(refs) - 2 - len(dils)
    x_refs = refs[:nx]
    mod_ref, g_ref = refs[nx:nx + 2]
    h_refs = refs[nx + 2:]
    tm = x_refs[0].shape[1]
    for h_ref, d in zip(h_refs, dils):
        rows = tm // d
        for g in range(d):
            src = pl.ds(g, rows, stride=d) if d > 1 else slice(None)
            x = jnp.concatenate([xr[0, src, :] for xr in x_refs], axis=1)
            y = x * lax.rsqrt(jnp.mean(x * x, axis=-1, keepdims=True) + NORM_EPS) * g_ref[...]
            h = y * (1.0 + mod_ref[0, 1:2, :]) + mod_ref[0, 0:1, :]
            h_ref[0, g] = h.astype(BF16)


def _norm_modulate(x, mod, g, dils, tm=1024):
    B, S, D = x.shape
    in_specs = [pl.BlockSpec((1, tm, LANES), functools.partial(lambda b, i, j: (b, i, j), j=j))
                for j in range(D // LANES)]
    in_specs += [pl.BlockSpec((1, 6, D), lambda b, i: (b, 0, 0)),
                 pl.BlockSpec((1, D), lambda b, i: (0, 0))]
    return pl.pallas_call(
        functools.partial(_norm_kernel, dils=dils),
        grid=(B, S // tm),
        in_specs=in_specs,
        out_specs=[pl.BlockSpec((1, d, tm // d, D), lambda b, i: (b, 0, i, 0)) for d in dils],
        out_shape=[jax.ShapeDtypeStruct((B, d, S // d, D), BF16) for d in dils],
        compiler_params=_cparams(("parallel", "parallel")),
        name="norm_modulate",
    )(*([x] * (D // LANES) + [mod, g]))


def _proj_kernel(h_ref, w_ref, o_ref, *, n_plain, n_tiles):
    acc = jnp.dot(h_ref[...], w_ref[...], preferred_element_type=F32)
    if n_plain == n_tiles:
        o_ref[...] = acc.astype(o_ref.dtype)
    else:
        n = pl.program_id(1)

        @pl.when(n < n_plain)
        def _():
            o_ref[...] = acc.astype(o_ref.dtype)

        @pl.when(n >= n_plain)
        def _():
            o_ref[...] = jax.nn.sigmoid(acc).astype(o_ref.dtype)


def _projection(h, w, name, n_plain=None, out_dtype=BF16, tm=2048, tn=1024):
    B, d, L, D = h.shape
    N = w.shape[1]
    tn = min(tn, N)
    tm = min(tm, B * d * L)
    n_tiles = N // tn
    n_plain = n_tiles if n_plain is None else n_plain
    out = pl.pallas_call(
        functools.partial(_proj_kernel, n_plain=n_plain, n_tiles=n_tiles),
        grid=(B * d * L // tm, n_tiles),
        in_specs=[pl.BlockSpec((tm, D), lambda i, n: (i, 0)),
                  pl.BlockSpec((D, tn), lambda i, n: (0, n))],
        out_specs=pl.BlockSpec((tm, tn), lambda i, n: (i, n)),
        out_shape=jax.ShapeDtypeStruct((B * d * L, N), out_dtype),
        compiler_params=_cparams(("parallel", "arbitrary")),
        name=name,
    )(h.reshape(B * d * L, D), w)
    return out.reshape(B, d, L, N)


def _mla_prep_kernel(lat_ref, pos_ref, gq_ref, gkv_ref, wq_ref, wqs_ref, wk_ref, wv_ref,
                     freq_ref, sgn_ref, sel_ref, sels_ref, one_ref, q_ref, k_ref, v_ref, *, scale):
    lat = lat_ref[0]
    cq = lat[:, :MLA_Q_RANK]
    ckv = lat[:, MLA_Q_RANK:MLA_Q_RANK + MLA_KV_RANK]
    kpe = lat[:, MLA_Q_RANK + MLA_KV_RANK:]
    cqn = (cq * lax.rsqrt(jnp.mean(cq * cq, axis=-1, keepdims=True) + NORM_EPS)
           * gq_ref[...]).astype(BF16)
    ckvn = (ckv * lax.rsqrt(jnp.mean(ckv * ckv, axis=-1, keepdims=True) + NORM_EPS)
            * gkv_ref[...]).astype(BF16)
    tm = lat.shape[0]
    pos = jnp.concatenate(
        [jnp.broadcast_to(pos_ref[0, :, c * LANES:(c + 1) * LANES], (LANES, LANES)).T
         for c in range(tm // LANES)], axis=0).astype(F32)
    ang = pos * freq_ref[...]
    cos = jnp.cos(ang)
    sin = jnp.sin(ang) * sgn_ref[...]
    qa = jnp.dot(cqn, wq_ref[...], preferred_element_type=F32)
    qs = jnp.dot(cqn, wqs_ref[...], preferred_element_type=F32)
    kn = jnp.dot(ckvn, wk_ref[...], preferred_element_type=F32)
    vv = jnp.dot(ckvn, wv_ref[...], preferred_element_type=F32)
    hp = lax.Precision.HIGHEST
    rk = (jnp.dot(kpe, sel_ref[...], preferred_element_type=F32, precision=hp) * cos
          + jnp.dot(kpe, sels_ref[...], preferred_element_type=F32, precision=hp) * sin).astype(BF16)
    cos_q = cos * scale
    sin_q = sin * scale
    ones = jnp.broadcast_to(one_ref[...], (tm, LANES)).astype(BF16)
    for h in range(MLA_HEADS):
        lo = slice(h * QK_PAD, h * QK_PAD + LANES)
        hi = slice(h * QK_PAD + LANES, (h + 1) * QK_PAD)
        hd = slice(h * LANES, (h + 1) * LANES)
        q_ref[0, :, lo] = (qa[:, lo] * scale).astype(BF16)
        q_ref[0, :, hi] = (qa[:, hi] * cos_q + qs[:, hd] * sin_q).astype(BF16)
        k_ref[0, :, lo] = kn[:, hd].astype(BF16)
        k_ref[0, :, hi] = rk
        v_ref[0, :, lo] = vv[:, hd].astype(BF16)
        v_ref[0, :, hi] = ones


def _mla_prep(lat, positions, g_q, g_kv, wq, wqs, wk, wv, tm=512):
    B, S, _ = lat.shape
    H = MLA_HEADS
    half = MLA_ROPE // 2
    assert MLA_NOPE == LANES and MLA_V == LANES and QK_PAD == 2 * LANES and tm % LANES == 0
    inv_freq = ROPE_THETA ** (-jnp.arange(0, MLA_ROPE, 2, dtype=F32) / MLA_ROPE)
    freq = jnp.zeros((1, LANES), F32).at[0, :MLA_ROPE].set(jnp.concatenate([inv_freq, inv_freq]))
    sgn = jnp.zeros((1, LANES), F32).at[0, :half].set(-1.0).at[0, half:MLA_ROPE].set(1.0)
    r = jnp.arange(MLA_ROPE)
    sel = jnp.zeros((LANES, LANES), F32).at[r, r].set(1.0)
    sels = jnp.zeros((LANES, LANES), F32).at[r, (r + half) % MLA_ROPE].set(1.0)
    scale = (MLA_NOPE + MLA_ROPE) ** -0.5 * math.log2(math.e)
    ones_col = jnp.zeros((1, LANES), F32).at[0, 0].set(1.0)
    const = lambda b, i: (0, 0)
    return pl.pallas_call(
        functools.partial(_mla_prep_kernel, scale=scale),
        grid=(B, S // tm),
        in_specs=[pl.BlockSpec((1, tm, LAT_PAD), lambda b, i: (b, i, 0)),
                  pl.BlockSpec((1, 1, tm), lambda b, i: (b, 0, i)),
                  pl.BlockSpec((1, MLA_Q_RANK), const),
                  pl.BlockSpec((1, MLA_KV_RANK), const),
                  pl.BlockSpec((MLA_Q_RANK, H * QK_PAD), const),
                  pl.BlockSpec((MLA_Q_RANK, H * LANES), const),
                  pl.BlockSpec((MLA_KV_RANK, H * LANES), const),
                  pl.BlockSpec((MLA_KV_RANK, H * LANES), const),
                  pl.BlockSpec((1, LANES), const),
                  pl.BlockSpec((1, LANES), const),
                  pl.BlockSpec((LANES, LANES), const),
                  pl.BlockSpec((LANES, LANES), const),
                  pl.BlockSpec((1, LANES), const)],
        out_specs=[pl.BlockSpec((1, tm, H * QK_PAD), lambda b, i: (b, i, 0)),
                   pl.BlockSpec((1, tm, H * QK_PAD), lambda b, i: (b, i, 0)),
                   pl.BlockSpec((1, tm, H * QK_PAD), lambda b, i: (b, i, 0))],
        out_shape=[jax.ShapeDtypeStruct((B, S, H * QK_PAD), BF16),
                   jax.ShapeDtypeStruct((B, S, H * QK_PAD), BF16),
                   jax.ShapeDtypeStruct((B, S, H * QK_PAD), BF16)],
        compiler_params=_cparams(("parallel", "parallel")),
        name="mla_prep",
    )(lat, positions.reshape(B, 1, S), g_q, g_kv, wq, wqs, wk, wv, freq, sgn, sel, sels, ones_col)


def _flash_kernel(q_ref, k_ref, v_ref, o_ref, m_ref, acc_ref, s_ref, *, tq, tkc):
    S = k_ref.shape[1]
    nk = S // tkc
    nq = S // tq
    half = nk // 2
    acc_ref[...] = jnp.zeros(acc_ref.shape, F32)
    m_ref[...] = jnp.full(m_ref.shape, -jnp.inf, F32)

    def scores(qi, kj):
        qoff = pl.multiple_of(qi * tq, tq)
        koff = pl.multiple_of(kj * tkc, tkc)
        return _nt_dot(q_ref[0, pl.ds(qoff, tq), :], k_ref[0, pl.ds(koff, tkc), :])

    def consume(kj, slot, first):
        koff = pl.multiple_of(kj * tkc, tkc)
        s = s_ref[slot]
        m_prev = m_ref[...]
        if first is not None:
            m_prev = jnp.where(first, -jnp.inf, m_prev)
        m_new = jnp.maximum(m_prev, jnp.max(s, axis=-1, keepdims=True))
        alpha = jnp.exp2(m_prev - m_new)
        p = jnp.exp2(s - m_new).astype(BF16)
        acc_ref[...] = alpha * acc_ref[...] + jnp.dot(p, v_ref[0, pl.ds(koff, tkc), :],
                                                      preferred_element_type=F32)
        m_ref[...] = m_new

    s_ref[0] = scores(0, 0)

    def body(t, carry):
        qi = t // half
        kj = 2 * (t % half)
        s_ref[1] = scores(qi, kj + 1)
        consume(kj, 0, kj == 0)
        t1 = jnp.minimum(t + 1, nq * half - 1)
        s_ref[0] = scores(t1 // half, 2 * (t1 % half))
        consume(kj + 1, 1, None)

        @pl.when(kj + 2 == nk)
        def _():
            acc = acc_ref[...]
            qoff = pl.multiple_of(qi * tq, tq)
            o_ref[0, pl.ds(qoff, tq), :] = (acc[:, :MLA_V] / acc[:, MLA_V:MLA_V + 1]).astype(BF16)
        return carry

    lax.fori_loop(0, nq * half, body, 0)


def _flash_attention(q, k, v, tq=1024, tkc=1024):
    B, S, _ = q.shape
    H = MLA_HEADS
    assert (S // tkc) % 2 == 0 and S % tq == 0
    whole = pl.BlockSpec((1, S, QK_PAD), lambda b, h: (b, 0, h))
    return pl.pallas_call(
        functools.partial(_flash_kernel, tq=tq, tkc=tkc),
        grid=(B, H),
        in_specs=[whole, whole, whole],
        out_specs=pl.BlockSpec((1, S, MLA_V), lambda b, h: (b, 0, h)),
        out_shape=jax.ShapeDtypeStruct((B, S, H * MLA_V), BF16),
        scratch_shapes=[pltpu.VMEM((tq, 1), F32), pltpu.VMEM((tq, QK_PAD), F32),
                        pltpu.VMEM((2, tq, tkc), F32)],
        compiler_params=_cparams(("parallel", "parallel")),
        name="mla_flash",
    )(q, k, v)


def _t5_bucket(rel):
    nb = REL_BUCKETS // 2
    max_exact = nb // 2
    ret = jnp.where(rel > 0, nb, 0)
    n = jnp.abs(rel)
    nf = jnp.maximum(n, 1).astype(F32)
    large = max_exact + (jnp.log(nf / max_exact) / math.log(REL_MAX_DIST / max_exact)
                         * (nb - max_exact)).astype(I32)
    large = jnp.minimum(large, nb - 1)
    return ret + jnp.where(n < max_exact, n, large)


def _bias_lookup(tab_ref, h, bucket):
    rows, nk = bucket.shape
    tab = jnp.broadcast_to(tab_ref[h:h + 1, :], (rows, LANES))
    return jnp.concatenate(
        [jnp.take_along_axis(tab, bucket[:, c * LANES:(c + 1) * LANES], axis=1)
         for c in range(nk // LANES)], axis=1)


def _band_bias_kernel(tab_ref, band_ref, *, r, dil):
    _, ts, nk = band_ref.shape
    off = lax.broadcasted_iota(I32, (ts, nk), 1) - r - lax.broadcasted_iota(I32, (ts, nk), 0)
    bucket = jnp.where(jnp.abs(off) <= r, _t5_bucket(off * dil), REL_BUCKETS)
    for h in range(DIL_HEADS):
        band_ref[h] = _bias_lookup(tab_ref, h, bucket)


def _band_bias(tab, r, dil, ts):
    nk = ts + 2 * r
    return pl.pallas_call(
        functools.partial(_band_bias_kernel, r=r, dil=dil),
        grid=(1,),
        in_specs=[pl.BlockSpec((DIL_HEADS, LANES), lambda i: (0, 0))],
        out_specs=pl.BlockSpec((DIL_HEADS, ts, nk), lambda i: (0, 0, 0)),
        out_shape=jax.ShapeDtypeStruct((DIL_HEADS, ts, nk), F32),
        compiler_params=_cparams(("arbitrary",)),
        name=f"band_bias_d{dil}",
    )(tab)


def _dilated_kernel(q_ref, kp_ref, kc_ref, kn_ref, vp_ref, vc_ref, vn_ref, pq_ref, pk_ref, tab_ref,
                    band_ref, o_ref, l_ref, kcat_ref, vcat_ref, s_scr, *, r, tq, ts, n_rows, dil):
    n = pl.program_id(2)
    nsub = tq // ts
    nk = ts + 2 * r
    kcat_ref[0:r, :] = kp_ref[0, 0]
    kcat_ref[r:r + tq, :] = kc_ref[0, 0]
    kcat_ref[r + tq:tq + 2 * r, :] = kn_ref[0, 0]
    ones = jnp.ones((tq + 2 * r, DIL_HEAD_DIM), BF16)
    for h in range(DIL_HEADS):
        sl = slice(h * DIL_HEAD_DIM, (h + 1) * DIL_HEAD_DIM)
        dst = slice(2 * h * DIL_HEAD_DIM, (2 * h + 1) * DIL_HEAD_DIM)
        vcat_ref[0:r, dst] = vp_ref[0, 0, :, sl]
        vcat_ref[r:r + tq, dst] = vc_ref[0, 0, :, sl]
        vcat_ref[r + tq:tq + 2 * r, dst] = vn_ref[0, 0, :, sl]
        vcat_ref[:, (2 * h + 1) * DIL_HEAD_DIM:(2 * h + 2) * DIL_HEAD_DIM] = ones
    lane_grp = lax.broadcasted_iota(I32, (ts, LANES), 1) // (LANES // DIL_HEADS)
    scale = DIL_HEAD_DIM ** -0.5 * math.log2(math.e)
    for j in range(nsub):
        qi = n * tq + j * ts + lax.broadcasted_iota(I32, (ts, nk), 0)
        kj = n * tq + j * ts - r + lax.broadcasted_iota(I32, (ts, nk), 1)
        inside = (kj >= 0) & (kj < n_rows)
        valid = (jnp.abs(kj - qi) <= r) & inside
        pq = jnp.broadcast_to(pq_ref[0, 0, :, j * ts:(j + 1) * ts], (LANES, ts)).T
        pk = pk_ref[0, 0, j]
        rel = jnp.concatenate([pk[:, c * LANES:(c + 1) * LANES] - pq
                               for c in range(nk // LANES)], axis=1)
        regular = jnp.max(jnp.where(valid & (rel != (kj - qi) * dil), 1, 0)) == 0

        def logits(h):
            sl = slice(h * DIL_HEAD_DIM, (h + 1) * DIL_HEAD_DIM)
            return _nt_dot(q_ref[0, 0, j * ts:(j + 1) * ts, sl], kcat_ref[j * ts:j * ts + nk, sl]) * scale

        @pl.when(regular)
        def _():
            for h in range(DIL_HEADS):
                s_scr[j * DIL_HEADS + h] = jnp.where(inside, logits(h) + band_ref[h], NEG_INF)

        @pl.when(jnp.logical_not(regular))
        def _():
            bucket = jnp.where(valid, _t5_bucket(rel), REL_BUCKETS)
            for h in range(DIL_HEADS):
                s_scr[j * DIL_HEADS + h] = logits(h) + _bias_lookup(tab_ref, h, bucket)
    for j in range(nsub):
        lse_tile = jnp.zeros((ts, LANES), F32)
        for h in range(DIL_HEADS):
            sl = slice(h * DIL_HEAD_DIM, (h + 1) * DIL_HEAD_DIM)
            s = s_scr[j * DIL_HEADS + h]
            m = jnp.max(s, axis=-1, keepdims=True)
            p = jnp.exp2(s - m).astype(BF16)
            res = jnp.dot(p, vcat_ref[j * ts:j * ts + nk,
                                      2 * h * DIL_HEAD_DIM:(2 * h + 2) * DIL_HEAD_DIM],
                          preferred_element_type=F32)
            den = res[:, DIL_HEAD_DIM:]
            o_ref[0, 0, j * ts:(j + 1) * ts, sl] = (res[:, :DIL_HEAD_DIM] / den).astype(BF16)
            lse_tile = jnp.where(lane_grp == h, m * math.log(2.0) + jnp.log(den), lse_tile)
        l_ref[0, 0, j * ts:(j + 1) * ts, :] = lse_tile


def _dilated_group(qkv, positions, table, gi, window, d, tq=512, ts=128):
    B, _, L, _ = qkv.shape
    HD = DIL_HEADS * DIL_HEAD_DIM
    r = window // (2 * d)
    tq = min(tq, L)
    nt = L // tq
    nsub = tq // ts
    rb = tq // r
    nk = ts + 2 * r
    pcls = positions.reshape(B, L, d).transpose(0, 2, 1)
    pq = pcls.reshape(B, d, 1, L)
    ppad = jnp.pad(pcls, ((0, 0), (0, 0), (r, ts + r)))
    pk = jnp.concatenate([ppad[:, :, :L].reshape(B, d, L // ts, ts),
                          ppad[:, :, ts:ts + L].reshape(B, d, L // ts, ts)[..., :2 * r]], axis=-1)
    pk = pk.reshape(B, d, L // ts, 1, nk)
    tab = jnp.zeros((DIL_HEADS, LANES), F32).at[:, :REL_BUCKETS].set(
        table[:, gi * DIL_HEADS:(gi + 1) * DIL_HEADS].T * math.log2(math.e))
    tab = tab.at[:, REL_BUCKETS].set(NEG_INF)

    def cur(c):
        return pl.BlockSpec((1, 1, tq, HD), lambda b, g, n: (b, g, n, c))

    def prv(c):
        return pl.BlockSpec((1, 1, r, HD), lambda b, g, n: (b, g, jnp.maximum(n * rb - 1, 0), c))

    def nxt(c):
        return pl.BlockSpec((1, 1, r, HD),
                            lambda b, g, n: (b, g, jnp.minimum((n + 1) * rb, L // r - 1), c))

    return pl.pallas_call(
        functools.partial(_dilated_kernel, r=r, tq=tq, ts=ts, n_rows=L, dil=d),
        grid=(B, d, nt),
        in_specs=[cur(0), prv(1), cur(1), nxt(1), prv(2), cur(2), nxt(2),
                  pl.BlockSpec((1, 1, 1, tq), lambda b, g, n: (b, g, 0, n)),
                  pl.BlockSpec((1, 1, nsub, 1, nk), lambda b, g, n: (b, g, n, 0, 0)),
                  pl.BlockSpec((DIL_HEADS, LANES), lambda b, g, n: (0, 0)),
                  pl.BlockSpec((DIL_HEADS, ts, nk), lambda b, g, n: (0, 0, 0))],
        out_specs=[pl.BlockSpec((1, 1, tq, HD), lambda b, g, n: (b, g, n, 0)),
                   pl.BlockSpec((1, 1, tq, LANES), lambda b, g, n: (b, g, n, 0))],
        out_shape=[jax.ShapeDtypeStruct((B, d, L, HD), BF16),
                   jax.ShapeDtypeStruct((B, d, L, LANES), F32)],
        scratch_shapes=[pltpu.VMEM((tq + 2 * r, HD), BF16), pltpu.VMEM((tq + 2 * r, 2 * HD), BF16),
                        pltpu.VMEM((nsub * DIL_HEADS, ts, nk), F32)],
        compiler_params=_cparams(("parallel", "parallel", "parallel")),
        name=f"dilated_g{gi}",
    )(*([qkv] * 7 + [pq, pk, tab, _band_bias(tab, r, d, ts)]))


def _outproj_kernel(*refs, dils):
    ng = len(dils)
    oa_ref = refs[0]
    og_refs = refs[1:1 + ng]
    lg_refs = refs[1 + ng:1 + 2 * ng]
    (ga_ref, gb_ref, x_ref, mod_ref, g_ref, wo_ref, wrh_ref, wrl_ref,
     x1_ref, h2_ref, aff_ref, afft_ref, o_scr, l_scr) = refs[1 + 2 * ng:]
    tm = x_ref.shape[1]
    for i, d in enumerate(dils):
        rows = tm // d
        for g in range(d):
            dst = pl.ds(g, rows, stride=d) if d > 1 else slice(None)
            og = og_refs[i][0, g].astype(F32)
            for h in range(DIL_HEADS):
                o_scr[i, h, dst, :] = og[:, h * DIL_HEAD_DIM:(h + 1) * DIL_HEAD_DIM]
            l_scr[i, dst, :] = lg_refs[i][0, g]
    lses = [l_scr[i] for i in range(ng)]
    mx = functools.reduce(jnp.maximum, lses)
    es = [jnp.exp(l - mx) for l in lses]
    tot = functools.reduce(lambda a, b: a + b, es)
    wts = [e / tot for e in es]
    cols = []
    for h in range(DIL_HEADS):
        c = h * (LANES // DIL_HEADS)
        cols.append(functools.reduce(
            lambda a, b: a + b, [wts[i][:, c:c + 1] * o_scr[i, h] for i in range(ng)]))
    o_b = jnp.concatenate(cols, axis=1)
    comb = (ga_ref[0, 0].astype(F32) * oa_ref[0].astype(F32)
            + gb_ref[0, 0].astype(F32) * o_b).astype(BF16)
    x1 = x_ref[0] + mod_ref[0, 2:3, :] * jnp.dot(comb, wo_ref[...], preferred_element_type=F32)
    x1_ref[0] = x1
    y = x1 * lax.rsqrt(jnp.mean(x1 * x1, axis=-1, keepdims=True) + NORM_EPS) * g_ref[...]
    h2 = y * (1.0 + mod_ref[0, 4:5, :]) + mod_ref[0, 3:4, :]
    h2_hi = h2.astype(BF16)
    h2_ref[0] = h2_hi
    h2_lo = (h2 - h2_hi.astype(F32)).astype(BF16)
    logits = (jnp.dot(h2_hi, wrh_ref[...], preferred_element_type=F32)
              + jnp.dot(h2_lo, wrh_ref[...], preferred_element_type=F32)
              + jnp.dot(h2_hi, wrl_ref[...], preferred_element_type=F32))
    lane = lax.broadcasted_iota(I32, logits.shape, 1)
    logits = jnp.where(lane < N_EXPERTS, logits, -jnp.inf)
    e = jnp.exp(logits - jnp.max(logits, axis=-1, keepdims=True))
    aff = e / jnp.sum(e, axis=-1, keepdims=True)
    aff_ref[0] = aff
    afft_ref[0, 0] = aff.T[:N_EXPERTS, :]


def _out_projection(o_a, dil_outs, main, gate_col, x, mod, g, w_out, w_router, tm=512):
    B, S, D = x.shape
    const = lambda b, i: (0, 0)
    row = lambda b, i: (b, i, 0)
    wr = jnp.zeros((D, LANES), F32).at[:, :N_EXPERTS].set(w_router)
    wr_hi = wr.astype(BF16)
    wr_lo = (wr - wr_hi.astype(F32)).astype(BF16)
    dils = tuple(o.shape[1] for o, _ in dil_outs)
    og_specs = [pl.BlockSpec((1, d, tm // d, D), lambda b, i: (b, 0, i, 0)) for d in dils]
    lg_specs = [pl.BlockSpec((1, d, tm // d, LANES), lambda b, i: (b, 0, i, 0)) for d in dils]
    return pl.pallas_call(
        functools.partial(_outproj_kernel, dils=dils),
        grid=(B, S // tm),
        in_specs=[pl.BlockSpec((1, tm, D), row)] + og_specs + lg_specs + [
                  pl.BlockSpec((1, 1, tm, D), lambda b, i: (b, 0, i, gate_col)),
                  pl.BlockSpec((1, 1, tm, D), lambda b, i: (b, 0, i, gate_col + 1)),
                  pl.BlockSpec((1, tm, D), row),
                  pl.BlockSpec((1, 6, D), lambda b, i: (b, 0, 0)),
                  pl.BlockSpec((1, D), const),
                  pl.BlockSpec((D, D), const),
                  pl.BlockSpec((D, LANES), const),
                  pl.BlockSpec((D, LANES), const)],
        out_specs=[pl.BlockSpec((1, tm, D), row),
                   pl.BlockSpec((1, tm, D), row),
                   pl.BlockSpec((1, tm, LANES), row),
                   pl.BlockSpec((1, 1, N_EXPERTS, tm), lambda b, i: (b, i, 0, 0))],
        out_shape=[jax.ShapeDtypeStruct((B, S, D), F32),
                   jax.ShapeDtypeStruct((B, S, D), BF16),
                   jax.ShapeDtypeStruct((B, S, LANES), F32),
                   jax.ShapeDtypeStruct((B, S // tm, N_EXPERTS, tm), F32)],
        scratch_shapes=[pltpu.VMEM((len(dils), DIL_HEADS, tm, DIL_HEAD_DIM), F32),
                        pltpu.VMEM((len(dils), tm, LANES), F32)],
        compiler_params=_cparams(("parallel", "parallel")),
        name="out_projection",
    )(o_a, *[o for o, _ in dil_outs], *[l for _, l in dil_outs], main, main, x, mod, g, w_out, wr_hi, wr_lo)


def _select_kernel(afft_ref, thr_ref, tie_ref, slot_ref, boff_ref, *, cap, chunk):
    aff = jnp.concatenate([afft_ref[0, c] for c in range(afft_ref.shape[1])], axis=1)
    E, S = aff.shape
    bits = lax.bitcast_convert_type(aff, I32)

    def count(mask):
        return jnp.sum(mask.astype(I32), axis=1, keepdims=True)

    def thr_step(i, v):
        cand = v | jnp.left_shift(jnp.int32(1), 30 - i)
        return jnp.where(count(bits >= cand) >= cap, cand, v)

    thr = lax.fori_loop(0, 31, thr_step, jnp.zeros((E, 1), I32))
    need = cap - count(bits > thr)
    eq = bits == thr
    idx = lax.broadcasted_iota(I32, (E, S), 1)
    nbits = max(1, (S - 1).bit_length())

    def tie_step(i, j):
        cand = j | jnp.left_shift(jnp.int32(1), nbits - 1 - i)
        return jnp.where(count(eq & (idx < cand)) < need, cand, j)

    tie = lax.fori_loop(0, nbits, tie_step, jnp.zeros((E, 1), I32))
    sel = (bits > thr) | (eq & (idx <= tie))
    thr_ref[0] = jnp.broadcast_to(thr, (E, LANES))
    tie_ref[0] = jnp.broadcast_to(tie, (E, LANES))

    upper = (lax.broadcasted_iota(I32, (chunk, chunk), 0)
             <= lax.broadcasted_iota(I32, (chunk, chunk), 1)).astype(BF16)
    carry = jnp.zeros((E, 1), F32)
    self_ = sel.astype(F32)
    lane = lax.broadcasted_iota(I32, (E, LANES), 1)
    boff = jnp.zeros((E, LANES), I32)
    for c in range(S // chunk):
        boff = jnp.where(lane == c, carry.astype(I32), boff)
        scf = self_[:, c * chunk:(c + 1) * chunk]
        incl = jnp.dot(scf.astype(BF16), upper, preferred_element_type=F32)
        pos = (incl - scf + carry).astype(I32)
        slot_ref[0, c] = jnp.where(scf > 0.0, pos, -1)
        carry = carry + incl[:, chunk - 1:chunk]
    boff_ref[0] = jnp.where(lane == S // chunk, carry.astype(I32), boff)


def _select(aff_t, cap, chunk):
    B, nblk, E, _ = aff_t.shape
    assert aff_t.shape[3] == chunk and nblk < LANES
    rows = pl.BlockSpec((1, nblk, E, chunk), lambda b: (b, 0, 0, 0))
    return pl.pallas_call(
        functools.partial(_select_kernel, cap=cap, chunk=chunk),
        grid=(B,),
        in_specs=[rows],
        out_specs=[pl.BlockSpec((1, E, LANES), lambda b: (b, 0, 0)),
                   pl.BlockSpec((1, E, LANES), lambda b: (b, 0, 0)),
                   rows,
                   pl.BlockSpec((1, E, LANES), lambda b: (b, 0, 0))],
        out_shape=[jax.ShapeDtypeStruct((B, E, LANES), I32),
                   jax.ShapeDtypeStruct((B, E, LANES), I32),
                   jax.ShapeDtypeStruct((B, nblk, E, chunk), I32),
                   jax.ShapeDtypeStruct((B, E, LANES), I32)],
        compiler_params=_cparams(("parallel",)),
        name="ec_select",
    )(aff_t)


def _gather_kernel(boff_ref, slot_ref, afft_ref, h_ref, xe_ref, gate_ref, acc_ref, gacc_ref, *,
                   win, epg):
    _, nblk, _, tbk = slot_ref.shape
    cap = xe_ref.shape[2]
    n_e = pl.num_programs(1) * epg
    acc_ref[...] = jnp.zeros(acc_ref.shape, F32)
    gacc_ref[...] = jnp.zeros(gacc_ref.shape, F32)

    def chunk(tb, carry):
        off = pl.multiple_of(tb * tbk, tbk)
        first_x, slot_x, aff_x, trips = [], [], [], jnp.int32(0)
        for x in range(epg):
            ex = pl.program_id(1) * epg + x
            base = (pl.program_id(0) * n_e + ex) * (nblk + 1) + tb
            first = (boff_ref[base] // 8) * 8
            first_x.append(first)
            slot_x.append(slot_ref[0, tb, pl.ds(ex, 1), :])
            aff_x.append(afft_ref[0, tb, pl.ds(ex, 1), :])
            trips = jnp.maximum(trips, (boff_ref[base + 1] - first + win - 1) // win)

        def window(k, c):
            w0 = [pl.multiple_of(jnp.minimum(first_x[x] + k * win, cap), 8) for x in range(epg)]
            hits = [lax.broadcasted_iota(I32, (win, tbk), 0) + w0[x] == slot_x[x] for x in range(epg)]
            rows = jnp.dot(jnp.concatenate(hits, axis=0).astype(BF16), h_ref[0, pl.ds(off, tbk), :],
                           preferred_element_type=F32)
            for x in range(epg):
                acc_ref[x, pl.ds(w0[x], win), :] += rows[x * win:(x + 1) * win]
                gacc_ref[x, pl.ds(w0[x], win), :] += jnp.sum(jnp.where(hits[x], aff_x[x], 0.0),
                                                             axis=1, keepdims=True)
            return c

        lax.fori_loop(0, trips, window, 0)
        return carry

    lax.fori_loop(0, nblk, chunk, 0)
    xe_ref[0] = acc_ref[:, 0:cap, :].astype(BF16)
    gate_ref[0] = gacc_ref[:, 0:cap, :]


def _gather(boff_flat, slot, aff_t, h2, cap, chunk, win=128, epg=2):
    B, nblk, E, _ = slot.shape
    S, D = h2.shape[1:]
    rowblk = pl.BlockSpec((1, nblk, E, chunk), lambda b, e, off: (b, 0, 0, 0))
    assert E % epg == 0 and cap % 8 == 0
    return pl.pallas_call(
        functools.partial(_gather_kernel, win=win, epg=epg),
        grid_spec=pltpu.PrefetchScalarGridSpec(
            num_scalar_prefetch=1,
            grid=(B, E // epg),
            in_specs=[rowblk, rowblk,
                      pl.BlockSpec((1, S, D), lambda b, e, off: (b, 0, 0))],
            out_specs=[pl.BlockSpec((1, epg, cap, D), lambda b, e, off: (b, e, 0, 0)),
                       pl.BlockSpec((1, epg, cap, 1), lambda b, e, off: (b, e, 0, 0))],
            scratch_shapes=[pltpu.VMEM((epg, cap + win, D), F32),
                            pltpu.VMEM((epg, cap + win, 1), F32)]),
        out_shape=[jax.ShapeDtypeStruct((B, E, cap, D), BF16),
                   jax.ShapeDtypeStruct((B, E, cap, 1), F32)],
        compiler_params=_cparams(("parallel", "arbitrary")),
        name="ec_gather",
    )(boff_flat, slot, aff_t, h2)


def _ffn_kernel(xe_ref, gate_ref, wg_ref, wu_ref, wd_ref, ye_ref, acc_ref):
    f = pl.program_id(2)
    nb, _, cap, D = xe_ref.shape

    @pl.when((pl.program_id(0) == 0) & (pl.program_id(1) == 0) & (f == 0))
    def _():
        acc_ref[...] = jnp.zeros(acc_ref.shape, F32)

    x = xe_ref[...].reshape(nb * cap, D)
    g = jnp.dot(x, wg_ref[0].astype(BF16), preferred_element_type=F32)
    u = jnp.dot(x, wu_ref[0].astype(BF16), preferred_element_type=F32)
    hid = (g * jax.nn.sigmoid(g) * u).astype(BF16)
    part = jnp.dot(hid, wd_ref[0].astype(BF16), preferred_element_type=F32)
    acc_ref[...] = jnp.where(f == 0, 0.0, acc_ref[...]) + part

    @pl.when(f == pl.num_programs(2) - 1)
    def _():
        gate = gate_ref[...].reshape(nb * cap, 1)
        ye_ref[...] = (acc_ref[...] * gate).astype(BF16).reshape(ye_ref.shape)


def _expert_ffn(xe, gates, w_gate, w_up, w_down, nb=2, tf=512):
    B, E, cap, D = xe.shape
    F = w_gate.shape[-1]
    return pl.pallas_call(
        _ffn_kernel,
        grid=(E, B // nb, F // tf),
        in_specs=[pl.BlockSpec((nb, 1, cap, D), lambda e, b, f: (b, e, 0, 0)),
                  pl.BlockSpec((nb, 1, cap, 1), lambda e, b, f: (b, e, 0, 0)),
                  pl.BlockSpec((1, D, tf), lambda e, b, f: (e, 0, f)),
                  pl.BlockSpec((1, D, tf), lambda e, b, f: (e, 0, f)),
                  pl.BlockSpec((1, tf, D), lambda e, b, f: (e, f, 0))],
        out_specs=pl.BlockSpec((nb, 1, cap, D), lambda e, b, f: (b, e, 0, 0)),
        out_shape=jax.ShapeDtypeStruct((B, E, cap, D), BF16),
        scratch_shapes=[pltpu.VMEM((nb * cap, D), F32)],
        compiler_params=_cparams(("arbitrary", "arbitrary", "arbitrary")),
        name="ec_ffn",
    )(xe, gates, w_gate, w_up, w_down)


def _combine_kernel(boff_ref, aff_ref, thr_ref, tie_ref, ye_ref, x1_ref, mod_ref, g_ref, out_ref,
                    slot_ref, carry_ref, *, tb, wslot, nblk, epg):
    b = pl.program_id(0)
    t2 = pl.program_id(1)
    e = pl.program_id(2)
    nsub = aff_ref.shape[1] // tb
    cap = ye_ref.shape[2]
    lane = lax.broadcasted_iota(I32, (tb, LANES), 1)

    @pl.when(e == 0)
    def _():
        @pl.when(t2 == 0)
        def _():
            carry_ref[...] = jnp.zeros(carry_ref.shape, F32)

        lower = (lax.broadcasted_iota(I32, (tb, tb), 0)
                 >= lax.broadcasted_iota(I32, (tb, tb), 1)).astype(BF16)
        for u in range(nsub):
            rows = slice(u * tb, (u + 1) * tb)
            bits = lax.bitcast_convert_type(aff_ref[0, rows, :], I32)
            tok = (t2 * nsub + u) * tb + lax.broadcasted_iota(I32, (tb, LANES), 0)
            thr = thr_ref[0]
            sel = ((bits > thr) | ((bits == thr) & (tok <= tie_ref[0]))) & (lane < N_EXPERTS)
            self_ = sel.astype(F32)
            incl = jnp.dot(lower, self_.astype(BF16), preferred_element_type=F32)
            pos = (incl - self_ + carry_ref[...]).astype(I32)
            slot_ref[rows, :] = jnp.where(sel, pos, -1)
            carry_ref[...] = carry_ref[...] + incl[tb - 1:tb, :]
        out_ref[...] = jnp.zeros(out_ref.shape, F32)

    n_e = pl.num_programs(2) * epg
    for u in range(nsub):
        rows = slice(u * tb, (u + 1) * tb)
        slot_x, first_x, trips = [], [], jnp.int32(0)
        for x in range(epg):
            ex = e * epg + x
            base = (b * n_e + ex) * (nblk + 1) + t2 * nsub + u
            first = (boff_ref[base] // SUBLANES_BF16) * SUBLANES_BF16
            slot_x.append(jnp.sum(jnp.where(lane == ex, slot_ref[rows, :], 0),
                                  axis=1, keepdims=True))
            first_x.append(first)
            trips = jnp.maximum(trips, (boff_ref[base + 1] - first + wslot - 1) // wslot)

        def body(k, carry):
            hots, wins = [], []
            for x in range(epg):
                start = first_x[x] + k * wslot
                woff = pl.multiple_of(jnp.minimum(start, cap - wslot), SUBLANES_BF16)
                todo = jnp.where(slot_x[x] >= start, slot_x[x], -1)
                hots.append((lax.broadcasted_iota(I32, (tb, wslot), 1) + woff == todo).astype(BF16))
                wins.append(ye_ref[0, x, pl.ds(woff, wslot), :])
            out_ref[0, rows, :] += jnp.dot(jnp.concatenate(hots, axis=1),
                                           jnp.concatenate(wins, axis=0),
                                           preferred_element_type=F32)
            return carry

        lax.fori_loop(0, trips, body, 0)

    @pl.when(e == pl.num_programs(2) - 1)
    def _():
        for u in range(nsub):
            rows = slice(u * tb, (u + 1) * tb)
            x2 = x1_ref[0, rows, :] + mod_ref[0, 5:6, :] * out_ref[0, rows, :]
            out_ref[0, rows, :] = (x2 * lax.rsqrt(jnp.mean(x2 * x2, axis=-1, keepdims=True)
                                                  + NORM_EPS) * g_ref[...])


def _combine(boff_flat, aff, thr, tie, ye, x1, mod, g_final, tb, tsup=2048, wslot=128, epg=2):
    B, S, D = x1.shape
    E, cap = ye.shape[1], ye.shape[2]
    assert cap % wslot == 0 and E % epg == 0
    tsup = min(tsup, S)
    return pl.pallas_call(
        functools.partial(_combine_kernel, tb=tb, wslot=wslot, nblk=S // tb, epg=epg),
        grid_spec=pltpu.PrefetchScalarGridSpec(
            num_scalar_prefetch=1,
            grid=(B, S // tsup, E // epg),
            in_specs=[pl.BlockSpec((1, tsup, LANES), lambda b, t, e, off: (b, t, 0)),
                      pl.BlockSpec((1, 1, LANES), lambda b, t, e, off: (b, 0, 0)),
                      pl.BlockSpec((1, 1, LANES), lambda b, t, e, off: (b, 0, 0)),
                      pl.BlockSpec((1, epg, cap, D), lambda b, t, e, off: (b, e, 0, 0)),
                      pl.BlockSpec((1, tsup, D), lambda b, t, e, off: (b, t, 0)),
                      pl.BlockSpec((1, 6, D), lambda b, t, e, off: (b, 0, 0)),
                      pl.BlockSpec((1, D), lambda b, t, e, off: (0, 0))],
            out_specs=pl.BlockSpec((1, tsup, D), lambda b, t, e, off: (b, t, 0)),
            scratch_shapes=[pltpu.VMEM((tsup, LANES), I32), pltpu.VMEM((1, LANES), F32)]),
        out_shape=jax.ShapeDtypeStruct((B, S, D), F32),
        compiler_params=_cparams(("parallel", "arbitrary", "arbitrary")),
        name="ec_combine",
    )(boff_flat, aff, thr, tie, ye, x1, mod, g_final)


def _prep_weights(w_in, w_uq, w_ukv):
    D = w_in.shape[0]
    H = MLA_HEADS
    n_lat = MLA_Q_RANK + MLA_KV_RANK + MLA_ROPE
    n_grp = 3 * DIL_HEADS * DIL_HEAD_DIM
    n_dil = len(DIL_GROUPS) * n_grp
    w_lat = jnp.zeros((D, LAT_PAD), F32).at[:, :n_lat].set(w_in[:, :n_lat]).astype(BF16)
    w_grp = [w_in[:, n_lat + i * n_grp:n_lat + (i + 1) * n_grp].astype(BF16)
             for i in range(len(DIL_GROUPS))]
    w_gates = w_in[:, n_lat + n_dil:].astype(BF16)
    half = MLA_ROPE // 2
    uq = w_uq.reshape(MLA_Q_RANK, H, MLA_NOPE + MLA_ROPE)
    pe = uq[:, :, MLA_NOPE:]
    zq = jnp.zeros((MLA_Q_RANK, H, QK_PAD - MLA_NOPE - MLA_ROPE), F32)
    wq = jnp.concatenate([uq[:, :, :MLA_NOPE], pe, zq], axis=2)
    wqs = jnp.concatenate([pe[:, :, half:], pe[:, :, :half], zq], axis=2)
    ukv = w_ukv.reshape(MLA_KV_RANK, H, MLA_NOPE + MLA_V)
    wk = ukv[:, :, :MLA_NOPE]
    wv = ukv[:, :, MLA_NOPE:]
    return (w_lat, w_grp, w_gates, wq.reshape(MLA_Q_RANK, H * QK_PAD).astype(BF16),
            wqs.reshape(MLA_Q_RANK, H * LANES).astype(BF16),
            wk.reshape(MLA_KV_RANK, H * MLA_NOPE).astype(BF16),
            wv.reshape(MLA_KV_RANK, H * MLA_V).astype(BF16))


def kernel(x, c, positions, w_ada, b_ada, g_norm_mix, w_in, g_q_lat, g_kv_lat, w_uq, w_ukv,
           rel_bias, w_out, g_norm_ffn, w_router, w_gate, w_up, w_down, g_final):
    B, S, D = x.shape
    assert w_ada.shape[0] == 1, "the final norm is fused into the (single) layer's last kernel"
    assert DIL_GROUPS[0][1] == 1, "the gates ride along with the undilated group's projection"
    for l in range(w_ada.shape[0]):
        mod = _modulation(c, w_ada[l], b_ada[l])
        w_lat, w_grp, w_gates, wq, wqs, wk, wv = _prep_weights(w_in[l], w_uq[l], w_ukv[l])
        g_mix = g_norm_mix[l].reshape(1, D)
        n_qkv = w_grp[0].shape[1] // D
        hs = _norm_modulate(x, mod, g_mix, tuple(d for _, d in DIL_GROUPS))
        lat = _projection(hs[0], w_lat, "lat_projection", out_dtype=F32)
        main = _projection(hs[0], jnp.concatenate([w_grp[0], w_gates], axis=1), "in_projection_g0",
                           n_plain=n_qkv)
        qkvs = [main] + [_projection(hs[gi], w_grp[gi], f"in_projection_g{gi}")
                         for gi in range(1, len(DIL_GROUPS))]
        q, k, v = _mla_prep(lat.reshape(B, S, LAT_PAD), positions, g_q_lat[l].reshape(1, -1), g_kv_lat[l].reshape(1, -1),
                            wq, wqs, wk, wv)
        o_a = _flash_attention(q, k, v)
        dil_outs = [_dilated_group(qkvs[gi], positions, rel_bias, gi, window, d)
                    for gi, (window, d) in enumerate(DIL_GROUPS)]
        x1, h2, aff, aff_t = _out_projection(o_a, dil_outs, main, n_qkv, x, mod,
                                             g_norm_ffn[l].reshape(1, D),
                                             w_out[l].astype(BF16), w_router[l])
        cap = EC_CAPACITY_FACTOR * S // N_EXPERTS
        chunk = 512
        thr, tie, slot, boff = _select(aff_t, cap, chunk)
        boff_flat = boff[:, :, :S // chunk + 1].reshape(-1)
        xe, gates = _gather(boff_flat, slot, aff_t, h2, cap, chunk)
        ye = _expert_ffn(xe, gates, w_gate[l], w_up[l], w_down[l])
        pad = jnp.zeros((B, 1, LANES - N_EXPERTS), I32)
        thr_l = jnp.concatenate([thr[:, :, 0].reshape(B, 1, N_EXPERTS), pad], axis=2)
        tie_l = jnp.concatenate([tie[:, :, 0].reshape(B, 1, N_EXPERTS), pad], axis=2)
        x = _combine(boff_flat, aff, thr_l, tie_l, ye, x1, mod, g_final.reshape(1, D), tb=chunk)
    return x
```

```python
import functools
import math

import jax
import jax.numpy as jnp
from jax import lax
from jax.experimental import pallas as pl
from jax.experimental.pallas import tpu as pltpu

F32 = jnp.float32
BF16 = jnp.bfloat16
I32 = jnp.int32

MLA_HEADS = 8
MLA_Q_RANK = 384
MLA_KV_RANK = 256
MLA_NOPE = 128
MLA_ROPE = 64
MLA_V = 128
ROPE_THETA = 10000.0
DIL_GROUPS = ((128, 1), (512, 4), (2048, 16))
DIL_HEADS = 8
DIL_HEAD_DIM = 128
REL_BUCKETS = 32
REL_MAX_DIST = 1024
N_EXPERTS = 16
EC_CAPACITY_FACTOR = 2
NORM_EPS = 1e-6
NEG_INF = -1e30

LANES = 128
SUBLANES = 8
SUBLANES_BF16 = 16
QK_PAD = 256
LAT_PAD = 768
VMEM_V7X = 64 * 1024 * 1024
VMEM_LIMIT = VMEM_V7X * 7 // 8


def _cparams(sem):
    return pltpu.CompilerParams(dimension_semantics=sem, vmem_limit_bytes=VMEM_LIMIT)


def _nt_dot(a, b):
    return lax.dot_general(a, b, (((1,), (1,)), ((), ())), preferred_element_type=F32)


def _mod_kernel(c_ref, w_ref, b_ref, o_ref):
    c = c_ref[...]
    cond = c * jax.nn.sigmoid(c)
    o_ref[...] = jnp.dot(cond, w_ref[...], preferred_element_type=F32,
                         precision=lax.Precision.HIGHEST) + b_ref[...]


def _modulation(c, w_ada, b_ada):
    B, D = c.shape
    rows = -(-B // SUBLANES) * SUBLANES
    c8 = jnp.zeros((rows, D), F32).at[:B].set(c)
    n6 = w_ada.shape[1]
    out = pl.pallas_call(
        _mod_kernel,
        grid=(n6 // D,),
        in_specs=[pl.BlockSpec((rows, D), lambda j: (0, 0)),
                  pl.BlockSpec((D, D), lambda j: (0, j)),
                  pl.BlockSpec((1, D), lambda j: (0, j))],
        out_specs=pl.BlockSpec((rows, D), lambda j: (0, j)),
        out_shape=jax.ShapeDtypeStruct((rows, n6), F32),
        compiler_params=_cparams(("arbitrary",)),
        name="modulation",
    )(c8, w_ada, b_ada.reshape(1, n6))
    return out[:B].reshape(B, 6, D)


def _norm_kernel(*refs, dils):
    nx = len(refs) - 2 - len(dils)
    x_refs = refs[:nx]
    mod_ref, g_ref = refs[nx:nx + 2]
    h_refs = refs[nx + 2:]
    tm = x_refs[0].shape[1]
    for h_ref, d in zip(h_refs, dils):
        rows = tm // d
        for g in range(d):
            src = pl.ds(g, rows, stride=d) if d > 1 else slice(None)
            x = jnp.concatenate([xr[0, src, :] for xr in x_refs], axis=1)
            y = x * lax.rsqrt(jnp.mean(x * x, axis=-1, keepdims=True) + NORM_EPS) * g_ref[...]
            h = y * (1.0 + mod_ref[0, 1:2, :]) + mod_ref[0, 0:1, :]
            h_ref[0, g] = h.astype(BF16)


def _norm_modulate(x, mod, g, dils, tm=1024):
    B, S, D = x.shape
    in_specs = [pl.BlockSpec((1, tm, LANES), functools.partial(lambda b, i, j: (b, i, j), j=j))
                for j in range(D // LANES)]
    in_specs += [pl.BlockSpec((1, 6, D), lambda b, i: (b, 0, 0)),
                 pl.BlockSpec((1, D), lambda b, i: (0, 0))]
    return pl.pallas_call(
        functools.partial(_norm_kernel, dils=dils),
        grid=(B, S // tm),
        in_specs=in_specs,
        out_specs=[pl.BlockSpec((1, d, tm // d, D), lambda b, i: (b, 0, i, 0)) for d in dils],
        out_shape=[jax.ShapeDtypeStruct((B, d, S // d, D), BF16) for d in dils],
        compiler_params=_cparams(("parallel", "parallel")),
        name="norm_modulate",
    )(*([x] * (D // LANES) + [mod, g]))


def _proj_kernel(h_ref, w_ref, o_ref, *, n_plain, n_tiles):
    acc = jnp.dot(h_ref[...], w_ref[...], preferred_element_type=F32)
    if n_plain == n_tiles:
        o_ref[...] = acc.astype(o_ref.dtype)
    else:
        n = pl.program_id(1)

        @pl.when(n < n_plain)
        def _():
            o_ref[...] = acc.astype(o_ref.dtype)

        @pl.when(n >= n_plain)
        def _():
            o_ref[...] = jax.nn.sigmoid(acc).astype(o_ref.dtype)


def _projection(h, w, name, n_plain=None, out_dtype=BF16, tm=2048, tn=1024):
    B, d, L, D = h.shape
    N = w.shape[1]
    tn = min(tn, N)
    tm = min(tm, B * d * L)
    n_tiles = N // tn
    n_plain = n_tiles if n_plain is None else n_plain
    out = pl.pallas_call(
        functools.partial(_proj_kernel, n_plain=n_plain, n_tiles=n_tiles),
        grid=(B * d * L // tm, n_tiles),
        in_specs=[pl.BlockSpec((tm, D), lambda i, n: (i, 0)),
                  pl.BlockSpec((D, tn), lambda i, n: (0, n))],
        out_specs=pl.BlockSpec((tm, tn), lambda i, n: (i, n)),
        out_shape=jax.ShapeDtypeStruct((B * d * L, N), out_dtype),
        compiler_params=_cparams(("parallel", "arbitrary")),
        name=name,
    )(h.reshape(B * d * L, D), w)
    return out.reshape(B, d, L, N)


def _mla_prep_kernel(lat_ref, pos_ref, gq_ref, gkv_ref, wq_ref, wqs_ref, wk_ref, wv_ref,
                     freq_ref, sgn_ref, sel_ref, sels_ref, one_ref, q_ref, k_ref, v_ref, *, scale):
    lat = lat_ref[0]
    cq = lat[:, :MLA_Q_RANK]
    ckv = lat[:, MLA_Q_RANK:MLA_Q_RANK + MLA_KV_RANK]
    kpe = lat[:, MLA_Q_RANK + MLA_KV_RANK:]
    cqn = (cq * lax.rsqrt(jnp.mean(cq * cq, axis=-1, keepdims=True) + NORM_EPS)
           * gq_ref[...]).astype(BF16)
    ckvn = (ckv * lax.rsqrt(jnp.mean(ckv * ckv, axis=-1, keepdims=True) + NORM_EPS)
            * gkv_ref[...]).astype(BF16)
    tm = lat.shape[0]
    pos = jnp.concatenate(
        [jnp.broadcast_to(pos_ref[0, :, c * LANES:(c + 1) * LANES], (LANES, LANES)).T
         for c in range(tm // LANES)], axis=0).astype(F32)
    ang = pos * freq_ref[...]
    cos = jnp.cos(ang)
    sin = jnp.sin(ang) * sgn_ref[...]
    qa = jnp.dot(cqn, wq_ref[...], preferred_element_type=F32)
    qs = jnp.dot(cqn, wqs_ref[...], preferred_element_type=F32)
    kn = jnp.dot(ckvn, wk_ref[...], preferred_element_type=F32)
    vv = jnp.dot(ckvn, wv_ref[...], preferred_element_type=F32)
    hp = lax.Precision.HIGHEST
    rk = (jnp.dot(kpe, sel_ref[...], preferred_element_type=F32, precision=hp) * cos
          + jnp.dot(kpe, sels_ref[...], preferred_element_type=F32, precision=hp) * sin).astype(BF16)
    cos_q = cos * scale
    sin_q = sin * scale
    ones = jnp.broadcast_to(one_ref[...], (tm, LANES)).astype(BF16)
    for h in range(MLA_HEADS):
        lo = slice(h * QK_PAD, h * QK_PAD + LANES)
        hi = slice(h * QK_PAD + LANES, (h + 1) * QK_PAD)
        hd = slice(h * LANES, (h + 1) * LANES)
        q_ref[0, :, lo] = (qa[:, lo] * scale).astype(BF16)
        q_ref[0, :, hi] = (qa[:, hi] * cos_q + qs[:, hd] * sin_q).astype(BF16)
        k_ref[0, :, lo] = kn[:, hd].astype(BF16)
        k_ref[0, :, hi] = rk
        v_ref[0, :, lo] = vv[:, hd].astype(BF16)
        v_ref[0, :, hi] = ones


def _mla_prep(lat, positions, g_q, g_kv, wq, wqs, wk, wv, tm=512):
    B, S, _ = lat.shape
    H = MLA_HEADS
    half = MLA_ROPE // 2
    assert MLA_NOPE == LANES and MLA_V == LANES and QK_PAD == 2 * LANES and tm % LANES == 0
    inv_freq = ROPE_THETA ** (-jnp.arange(0, MLA_ROPE, 2, dtype=F32) / MLA_ROPE)
    freq = jnp.zeros((1, LANES), F32).at[0, :MLA_ROPE].set(jnp.concatenate([inv_freq, inv_freq]))
    sgn = jnp.zeros((1, LANES), F32).at[0, :half].set(-1.0).at[0, half:MLA_ROPE].set(1.0)
    r = jnp.arange(MLA_ROPE)
    sel = jnp.zeros((LANES, LANES), F32).at[r, r].set(1.0)
    sels = jnp.zeros((LANES, LANES), F32).at[r, (r + half) % MLA_ROPE].set(1.0)
    scale = (MLA_NOPE + MLA_ROPE) ** -0.5 * math.log2(math.e)
    ones_col = jnp.zeros((1, LANES), F32).at[0, 0].set(1.0)
    const = lambda b, i: (0, 0)
    return pl.pallas_call(
        functools.partial(_mla_prep_kernel, scale=scale),
        grid=(B, S // tm),
        in_specs=[pl.BlockSpec((1, tm, LAT_PAD), lambda b, i: (b, i, 0)),
                  pl.BlockSpec((1, 1, tm), lambda b, i: (b, 0, i)),
                  pl.BlockSpec((1, MLA_Q_RANK), const),
                  pl.BlockSpec((1, MLA_KV_RANK), const),
                  pl.BlockSpec((MLA_Q_RANK, H * QK_PAD), const),
                  pl.BlockSpec((MLA_Q_RANK, H * LANES), const),
                  pl.BlockSpec((MLA_KV_RANK, H * LANES), const),
                  pl.BlockSpec((MLA_KV_RANK, H * LANES), const),
                  pl.BlockSpec((1, LANES), const),
                  pl.BlockSpec((1, LANES), const),
                  pl.BlockSpec((LANES, LANES), const),
                  pl.BlockSpec((LANES, LANES), const),
                  pl.BlockSpec((1, LANES), const)],
        out_specs=[pl.BlockSpec((1, tm, H * QK_PAD), lambda b, i: (b, i, 0)),
                   pl.BlockSpec((1, tm, H * QK_PAD), lambda b, i: (b, i, 0)),
                   pl.BlockSpec((1, tm, H * QK_PAD), lambda b, i: (b, i, 0))],
        out_shape=[jax.ShapeDtypeStruct((B, S, H * QK_PAD), BF16),
                   jax.ShapeDtypeStruct((B, S, H * QK_PAD), BF16),
                   jax.ShapeDtypeStruct((B, S, H * QK_PAD), BF16)],
        compiler_params=_cparams(("parallel", "parallel")),
        name="mla_prep",
    )(lat, positions.reshape(B, 1, S), g_q, g_kv, wq, wqs, wk, wv, freq, sgn, sel, sels, ones_col)


def _flash_kernel(q_ref, k_ref, v_ref, o_ref, m_ref, acc_ref, s_ref, *, tq, tkc):
    S = k_ref.shape[1]
    nk = S // tkc
    nq = S // tq
    half = nk // 2
    acc_ref[...] = jnp.zeros(acc_ref.shape, F32)
    m_ref[...] = jnp.full(m_ref.shape, -jnp.inf, F32)

    def scores(qi, kj):
        qoff = pl.multiple_of(qi * tq, tq)
        koff = pl.multiple_of(kj * tkc, tkc)
        return _nt_dot(q_ref[0, pl.ds(qoff, tq), :], k_ref[0, pl.ds(koff, tkc), :])

    def consume(kj, slot, first):
        koff = pl.multiple_of(kj * tkc, tkc)
        s = s_ref[slot]
        m_prev = m_ref[...]
        if first is not None:
            m_prev = jnp.where(first, -jnp.inf, m_prev)
        m_new = jnp.maximum(m_prev, jnp.max(s, axis=-1, keepdims=True))
        alpha = jnp.exp2(m_prev - m_new)
        p = jnp.exp2(s - m_new).astype(BF16)
        acc_ref[...] = alpha * acc_ref[...] + jnp.dot(p, v_ref[0, pl.ds(koff, tkc), :],
                                                      preferred_element_type=F32)
        m_ref[...] = m_new

    s_ref[0] = scores(0, 0)

    def body(t, carry):
        qi = t // half
        kj = 2 * (t % half)
        s_ref[1] = scores(qi, kj + 1)
        consume(kj, 0, kj == 0)
        t1 = jnp.minimum(t + 1, nq * half - 1)
        s_ref[0] = scores(t1 // half, 2 * (t1 % half))
        consume(kj + 1, 1, None)

        @pl.when(kj + 2 == nk)
        def _():
            acc = acc_ref[...]
            qoff = pl.multiple_of(qi * tq, tq)
            o_ref[0, pl.ds(qoff, tq), :] = (acc[:, :MLA_V] / acc[:, MLA_V:MLA_V + 1]).astype(BF16)
        return carry

    lax.fori_loop(0, nq * half, body, 0)


def _flash_attention(q, k, v, tq=1024, tkc=1024):
    B, S, _ = q.shape
    H = MLA_HEADS
    assert (S // tkc) % 2 == 0 and S % tq == 0
    whole = pl.BlockSpec((1, S, QK_PAD), lambda b, h: (b, 0, h))
    return pl.pallas_call(
        functools.partial(_flash_kernel, tq=tq, tkc=tkc),
        grid=(B, H),
        in_specs=[whole, whole, whole],
        out_specs=pl.BlockSpec((1, S, MLA_V), lambda b, h: (b, 0, h)),
        out_shape=jax.ShapeDtypeStruct((B, S, H * MLA_V), BF16),
        scratch_shapes=[pltpu.VMEM((tq, 1), F32), pltpu.VMEM((tq, QK_PAD), F32),
                        pltpu.VMEM((2, tq, tkc), F32)],
        compiler_params=_cparams(("parallel", "parallel")),
        name="mla_flash",
    )(q, k, v)


def _t5_bucket(rel):
    nb = REL_BUCKETS // 2
    max_exact = nb // 2
    ret = jnp.where(rel > 0, nb, 0)
    n = jnp.abs(rel)
    nf = jnp.maximum(n, 1).astype(F32)
    large = max_exact + (jnp.log(nf / max_exact) / math.log(REL_MAX_DIST / max_exact)
                         * (nb - max_exact)).astype(I32)
    large = jnp.minimum(large, nb - 1)
    return ret + jnp.where(n < max_exact, n, large)


def _bias_lookup(tab_ref, h, bucket):
    rows, nk = bucket.shape
    tab = jnp.broadcast_to(tab_ref[h:h + 1, :], (rows, LANES))
    return jnp.concatenate(
        [jnp.take_along_axis(tab, bucket[:, c * LANES:(c + 1) * LANES], axis=1)
         for c in range(nk // LANES)], axis=1)


def _band_bias_kernel(tab_ref, band_ref, *, r, dil):
    _, ts, nk = band_ref.shape
    off = lax.broadcasted_iota(I32, (ts, nk), 1) - r - lax.broadcasted_iota(I32, (ts, nk), 0)
    bucket = jnp.where(jnp.abs(off) <= r, _t5_bucket(off * dil), REL_BUCKETS)
    for h in range(DIL_HEADS):
        band_ref[h] = _bias_lookup(tab_ref, h, bucket)


def _band_bias(tab, r, dil, ts):
    nk = ts + 2 * r
    return pl.pallas_call(
        functools.partial(_band_bias_kernel, r=r, dil=dil),
        grid=(1,),
        in_specs=[pl.BlockSpec((DIL_HEADS, LANES), lambda i: (0, 0))],
        out_specs=pl.BlockSpec((DIL_HEADS, ts, nk), lambda i: (0, 0, 0)),
        out_shape=jax.ShapeDtypeStruct((DIL_HEADS, ts, nk), F32),
        compiler_params=_cparams(("arbitrary",)),
        name=f"band_bias_d{dil}",
    )(tab)


def _dilated_kernel(q_ref, kp_ref, kc_ref, kn_ref, vp_ref, vc_ref, vn_ref, pq_ref, pk_ref, tab_ref,
                    band_ref, o_ref, l_ref, kcat_ref, vcat_ref, s_scr, *, r, tq, ts, n_rows, dil):
    n = pl.program_id(2)
    nsub = tq // ts
    nk = ts + 2 * r
    kcat_ref[0:r, :] = kp_ref[0, 0]
    kcat_ref[r:r + tq, :] = kc_ref[0, 0]
    kcat_ref[r + tq:tq + 2 * r, :] = kn_ref[0, 0]
    ones = jnp.ones((tq + 2 * r, DIL_HEAD_DIM), BF16)
    for h in range(DIL_HEADS):
        sl = slice(h * DIL_HEAD_DIM, (h + 1) * DIL_HEAD_DIM)
        dst = slice(2 * h * DIL_HEAD_DIM, (2 * h + 1) * DIL_HEAD_DIM)
        vcat_ref[0:r, dst] = vp_ref[0, 0, :, sl]
        vcat_ref[r:r + tq, dst] = vc_ref[0, 0, :, sl]
        vcat_ref[r + tq:tq + 2 * r, dst] = vn_ref[0, 0, :, sl]
        vcat_ref[:, (2 * h + 1) * DIL_HEAD_DIM:(2 * h + 2) * DIL_HEAD_DIM] = ones
    lane_grp = lax.broadcasted_iota(I32, (ts, LANES), 1) // (LANES // DIL_HEADS)
    scale = DIL_HEAD_DIM ** -0.5 * math.log2(math.e)
    for j in range(nsub):
        qi = n * tq + j * ts + lax.broadcasted_iota(I32, (ts, nk), 0)
        kj = n * tq + j * ts - r + lax.broadcasted_iota(I32, (ts, nk), 1)
        inside = (kj >= 0) & (kj < n_rows)
        valid = (jnp.abs(kj - qi) <= r) & inside
        pq = jnp.broadcast_to(pq_ref[0, 0, :, j * ts:(j + 1) * ts], (LANES, ts)).T
        pk = pk_ref[0, 0, j]
        rel = jnp.concatenate([pk[:, c * LANES:(c + 1) * LANES] - pq
                               for c in range(nk // LANES)], axis=1)
        regular = jnp.max(jnp.where(valid & (rel != (kj - qi) * dil), 1, 0)) == 0

        def logits(h):
            sl = slice(h * DIL_HEAD_DIM, (h + 1) * DIL_HEAD_DIM)
            return _nt_dot(q_ref[0, 0, j * ts:(j + 1) * ts, sl], kcat_ref[j * ts:j * ts + nk, sl]) * scale

        @pl.when(regular)
        def _():
            for h in range(DIL_HEADS):
                s_scr[j * DIL_HEADS + h] = jnp.where(inside, logits(h) + band_ref[h], NEG_INF)

        @pl.when(jnp.logical_not(regular))
        def _():
            bucket = jnp.where(valid, _t5_bucket(rel), REL_BUCKETS)
            for h in range(DIL_HEADS):
                s_scr[j * DIL_HEADS + h] = logits(h) + _bias_lookup(tab_ref, h, bucket)
    for j in range(nsub):
        lse_tile = jnp.zeros((ts, LANES), F32)
        for h in range(DIL_HEADS):
            sl = slice(h * DIL_HEAD_DIM, (h + 1) * DIL_HEAD_DIM)
            s = s_scr[j * DIL_HEADS + h]
            m = jnp.max(s, axis=-1, keepdims=True)
            p = jnp.exp2(s - m).astype(BF16)
            res = jnp.dot(p, vcat_ref[j * ts:j * ts + nk,
                                      2 * h * DIL_HEAD_DIM:(2 * h + 2) * DIL_HEAD_DIM],
                          preferred_element_type=F32)
            den = res[:, DIL_HEAD_DIM:]
            o_ref[0, 0, j * ts:(j + 1) * ts, sl] = (res[:, :DIL_HEAD_DIM] / den).astype(BF16)
            lse_tile = jnp.where(lane_grp == h, m * math.log(2.0) + jnp.log(den), lse_tile)
        l_ref[0, 0, j * ts:(j + 1) * ts, :] = lse_tile


def _dilated_group(qkv, positions, table, gi, window, d, tq=512, ts=128):
    B, _, L, _ = qkv.shape
    HD = DIL_HEADS * DIL_HEAD_DIM
    r = window // (2 * d)
    tq = min(tq, L)
    nt = L // tq
    nsub = tq // ts
    rb = tq // r
    nk = ts + 2 * r
    pcls = positions.reshape(B, L, d).transpose(0, 2, 1)
    pq = pcls.reshape(B, d, 1, L)
    ppad = jnp.pad(pcls, ((0, 0), (0, 0), (r, ts + r)))
    pk = jnp.concatenate([ppad[:, :, :L].reshape(B, d, L // ts, ts),
                          ppad[:, :, ts:ts + L].reshape(B, d, L // ts, ts)[..., :2 * r]], axis=-1)
    pk = pk.reshape(B, d, L // ts, 1, nk)
    tab = jnp.zeros((DIL_HEADS, LANES), F32).at[:, :REL_BUCKETS].set(
        table[:, gi * DIL_HEADS:(gi + 1) * DIL_HEADS].T * math.log2(math.e))
    tab = tab.at[:, REL_BUCKETS].set(NEG_INF)

    def cur(c):
        return pl.BlockSpec((1, 1, tq, HD), lambda b, g, n: (b, g, n, c))

    def prv(c):
        return pl.BlockSpec((1, 1, r, HD), lambda b, g, n: (b, g, jnp.maximum(n * rb - 1, 0), c))

    def nxt(c):
        return pl.BlockSpec((1, 1, r, HD),
                            lambda b, g, n: (b, g, jnp.minimum((n + 1) * rb, L // r - 1), c))

    return pl.pallas_call(
        functools.partial(_dilated_kernel, r=r, tq=tq, ts=ts, n_rows=L, dil=d),
        grid=(B, d, nt),
        in_specs=[cur(0), prv(1), cur(1), nxt(1), prv(2), cur(2), nxt(2),
                  pl.BlockSpec((1, 1, 1, tq), lambda b, g, n: (b, g, 0, n)),
                  pl.BlockSpec((1, 1, nsub, 1, nk), lambda b, g, n: (b, g, n, 0, 0)),
                  pl.BlockSpec((DIL_HEADS, LANES), lambda b, g, n: (0, 0)),
                  pl.BlockSpec((DIL_HEADS, ts, nk), lambda b, g, n: (0, 0, 0))],
        out_specs=[pl.BlockSpec((1, 1, tq, HD), lambda b, g, n: (b, g, n, 0)),
                   pl.BlockSpec((1, 1, tq, LANES), lambda b, g, n: (b, g, n, 0))],
        out_shape=[jax.ShapeDtypeStruct((B, d, L, HD), BF16),
                   jax.ShapeDtypeStruct((B, d, L, LANES), F32)],
        scratch_shapes=[pltpu.VMEM((tq + 2 * r, HD), BF16), pltpu.VMEM((tq + 2 * r, 2 * HD), BF16),
                        pltpu.VMEM((nsub * DIL_HEADS, ts, nk), F32)],
        compiler_params=_cparams(("parallel", "parallel", "parallel")),
        name=f"dilated_g{gi}",
    )(*([qkv] * 7 + [pq, pk, tab, _band_bias(tab, r, d, ts)]))


def _outproj_kernel(*refs, dils):
    ng = len(dils)
    oa_ref = refs[0]
    og_refs = refs[1:1 + ng]
    lg_refs = refs[1 + ng:1 + 2 * ng]
    (ga_ref, gb_ref, x_ref, mod_ref, g_ref, wo_ref, wrh_ref, wrl_ref,
     x1_ref, h2_ref, aff_ref, afft_ref, o_scr, l_scr) = refs[1 + 2 * ng:]
    tm = x_ref.shape[1]
    for i, d in enumerate(dils):
        rows = tm // d
        for g in range(d):
            dst = pl.ds(g, rows, stride=d) if d > 1 else slice(None)
            og = og_refs[i][0, g].astype(F32)
            for h in range(DIL_HEADS):
                o_scr[i, h, dst, :] = og[:, h * DIL_HEAD_DIM:(h + 1) * DIL_HEAD_DIM]
            l_scr[i, dst, :] = lg_refs[i][0, g]
    lses = [l_scr[i] for i in range(ng)]
    mx = functools.reduce(jnp.maximum, lses)
    es = [jnp.exp(l - mx) for l in lses]
    tot = functools.reduce(lambda a, b: a + b, es)
    wts = [e / tot for e in es]
    cols = []
    for h in range(DIL_HEADS):
        c = h * (LANES // DIL_HEADS)
        cols.append(functools.reduce(
            lambda a, b: a + b, [wts[i][:, c:c + 1] * o_scr[i, h] for i in range(ng)]))
    o_b = jnp.concatenate(cols, axis=1)
    comb = (ga_ref[0, 0].astype(F32) * oa_ref[0].astype(F32)
            + gb_ref[0, 0].astype(F32) * o_b).astype(BF16)
    x1 = x_ref[0] + mod_ref[0, 2:3, :] * jnp.dot(comb, wo_ref[...], preferred_element_type=F32)
    x1_ref[0] = x1
    y = x1 * lax.rsqrt(jnp.mean(x1 * x1, axis=-1, keepdims=True) + NORM_EPS) * g_ref[...]
    h2 = y * (1.0 + mod_ref[0, 4:5, :]) + mod_ref[0, 3:4, :]
    h2_hi = h2.astype(BF16)
    h2_ref[0] = h2_hi
    h2_lo = (h2 - h2_hi.astype(F32)).astype(BF16)
    logits = (jnp.dot(h2_hi, wrh_ref[...], preferred_element_type=F32)
              + jnp.dot(h2_lo, wrh_ref[...], preferred_element_type=F32)
              + jnp.dot(h2_hi, wrl_ref[...], preferred_element_type=F32))
    lane = lax.broadcasted_iota(I32, logits.shape, 1)
    logits = jnp.where(lane < N_EXPERTS, logits, -jnp.inf)
    e = jnp.exp(logits - jnp.max(logits, axis=-1, keepdims=True))
    aff = e / jnp.sum(e, axis=-1, keepdims=True)
    aff_ref[0] = aff
    afft_ref[0, 0] = aff.T[:N_EXPERTS, :]


def _out_projection(o_a, dil_outs, main, gate_col, x, mod, g, w_out, w_router, tm=512):
    B, S, D = x.shape
    const = lambda b, i: (0, 0)
    row = lambda b, i: (b, i, 0)
    wr = jnp.zeros((D, LANES), F32).at[:, :N_EXPERTS].set(w_router)
    wr_hi = wr.astype(BF16)
    wr_lo = (wr - wr_hi.astype(F32)).astype(BF16)
    dils = tuple(o.shape[1] for o, _ in dil_outs)
    og_specs = [pl.BlockSpec((1, d, tm // d, D), lambda b, i: (b, 0, i, 0)) for d in dils]
    lg_specs = [pl.BlockSpec((1, d, tm // d, LANES), lambda b, i: (b, 0, i, 0)) for d in dils]
    return pl.pallas_call(
        functools.partial(_outproj_kernel, dils=dils),
        grid=(B, S // tm),
        in_specs=[pl.BlockSpec((1, tm, D), row)] + og_specs + lg_specs + [
                  pl.BlockSpec((1, 1, tm, D), lambda b, i: (b, 0, i, gate_col)),
                  pl.BlockSpec((1, 1, tm, D), lambda b, i: (b, 0, i, gate_col + 1)),
                  pl.BlockSpec((1, tm, D), row),
                  pl.BlockSpec((1, 6, D), lambda b, i: (b, 0, 0)),
                  pl.BlockSpec((1, D), const),
                  pl.BlockSpec((D, D), const),
                  pl.BlockSpec((D, LANES), const),
                  pl.BlockSpec((D, LANES), const)],
        out_specs=[pl.BlockSpec((1, tm, D), row),
                   pl.BlockSpec((1, tm, D), row),
                   pl.BlockSpec((1, tm, LANES), row),
                   pl.BlockSpec((1, 1, N_EXPERTS, tm), lambda b, i: (b, i, 0, 0))],
        out_shape=[jax.ShapeDtypeStruct((B, S, D), F32),
                   jax.ShapeDtypeStruct((B, S, D), BF16),
                   jax.ShapeDtypeStruct((B, S, LANES), F32),
                   jax.ShapeDtypeStruct((B, S // tm, N_EXPERTS, tm), F32)],
        scratch_shapes=[pltpu.VMEM((len(dils), DIL_HEADS, tm, DIL_HEAD_DIM), F32),
                        pltpu.VMEM((len(dils), tm, LANES), F32)],
        compiler_params=_cparams(("parallel", "parallel")),
        name="out_projection",
    )(o_a, *[o for o, _ in dil_outs], *[l for _, l in dil_outs], main, main, x, mod, g, w_out, wr_hi, wr_lo)


def _select_kernel(afft_ref, thr_ref, tie_ref, slot_ref, boff_ref, *, cap, chunk):
    aff = jnp.concatenate([afft_ref[0, c] for c in range(afft_ref.shape[1])], axis=1)
    E, S = aff.shape
    bits = lax.bitcast_convert_type(aff, I32)

    def count(mask):
        return jnp.sum(mask.astype(I32), axis=1, keepdims=True)

    def thr_step(i, v):
        cand = v | jnp.left_shift(jnp.int32(1), 30 - i)
        return jnp.where(count(bits >= cand) >= cap, cand, v)

    thr = lax.fori_loop(0, 31, thr_step, jnp.zeros((E, 1), I32))
    need = cap - count(bits > thr)
    eq = bits == thr
    idx = lax.broadcasted_iota(I32, (E, S), 1)
    nbits = max(1, (S - 1).bit_length())

    def tie_step(i, j):
        cand = j | jnp.left_shift(jnp.int32(1), nbits - 1 - i)
        return jnp.where(count(eq & (idx < cand)) < need, cand, j)

    tie = lax.fori_loop(0, nbits, tie_step, jnp.zeros((E, 1), I32))
    sel = (bits > thr) | (eq & (idx <= tie))
    thr_ref[0] = jnp.broadcast_to(thr, (E, LANES))
    tie_ref[0] = jnp.broadcast_to(tie, (E, LANES))

    upper = (lax.broadcasted_iota(I32, (chunk, chunk), 0)
             <= lax.broadcasted_iota(I32, (chunk, chunk), 1)).astype(BF16)
    carry = jnp.zeros((E, 1), F32)
    self_ = sel.astype(F32)
    lane = lax.broadcasted_iota(I32, (E, LANES), 1)
    boff = jnp.zeros((E, LANES), I32)
    for c in range(S // chunk):
        boff = jnp.where(lane == c, carry.astype(I32), boff)
        scf = self_[:, c * chunk:(c + 1) * chunk]
        incl = jnp.dot(scf.astype(BF16), upper, preferred_element_type=F32)
        pos = (incl - scf + carry).astype(I32)
        slot_ref[0, c] = jnp.where(scf > 0.0, pos, -1)
        carry = carry + incl[:, chunk - 1:chunk]
    boff_ref[0] = jnp.where(lane == S // chunk, carry.astype(I32), boff)


def _select(aff_t, cap, chunk):
    B, nblk, E, _ = aff_t.shape
    assert aff_t.shape[3] == chunk and nblk < LANES
    rows = pl.BlockSpec((1, nblk, E, chunk), lambda b: (b, 0, 0, 0))
    return pl.pallas_call(
        functools.partial(_select_kernel, cap=cap, chunk=chunk),
        grid=(B,),
        in_specs=[rows],
        out_specs=[pl.BlockSpec((1, E, LANES), lambda b: (b, 0, 0)),
                   pl.BlockSpec((1, E, LANES), lambda b: (b, 0, 0)),
                   rows,
                   pl.BlockSpec((1, E, LANES), lambda b: (b, 0, 0))],
        out_shape=[jax.ShapeDtypeStruct((B, E, LANES), I32),
                   jax.ShapeDtypeStruct((B, E, LANES), I32),
                   jax.ShapeDtypeStruct((B, nblk, E, chunk), I32),
                   jax.ShapeDtypeStruct((B, E, LANES), I32)],
        compiler_params=_cparams(("parallel",)),
        name="ec_select",
    )(aff_t)


def _gather_kernel(boff_ref, slot_ref, afft_ref, h_ref, xe_ref, gate_ref, acc_ref, gacc_ref, *,
                   win, epg):
    _, nblk, _, tbk = slot_ref.shape
    cap = xe_ref.shape[2]
    n_e = pl.num_programs(1) * epg
    acc_ref[...] = jnp.zeros(acc_ref.shape, F32)
    gacc_ref[...] = jnp.zeros(gacc_ref.shape, F32)

    def chunk(tb, carry):
        off = pl.multiple_of(tb * tbk, tbk)
        first_x, slot_x, aff_x, trips = [], [], [], jnp.int32(0)
        for x in range(epg):
            ex = pl.program_id(1) * epg + x
            base = (pl.program_id(0) * n_e + ex) * (nblk + 1) + tb
            first = (boff_ref[base] // SUBLANES) * SUBLANES
            first_x.append(first)
            slot_x.append(slot_ref[0, tb, pl.ds(ex, 1), :])
            aff_x.append(afft_ref[0, tb, pl.ds(ex, 1), :])
            trips = jnp.maximum(trips, (boff_ref[base + 1] - first + win - 1) // win)

        def window(k, c):
            w0 = [pl.multiple_of(jnp.minimum(first_x[x] + k * win, cap), SUBLANES)
                  for x in range(epg)]
            hits = [lax.broadcasted_iota(I32, (win, tbk), 0) + w0[x] == slot_x[x] for x in range(epg)]
            rows = jnp.dot(jnp.concatenate(hits, axis=0).astype(BF16), h_ref[0, pl.ds(off, tbk), :],
                           preferred_element_type=F32)
            for x in range(epg):
                acc_ref[x, pl.ds(w0[x], win), :] += rows[x * win:(x + 1) * win]
                gacc_ref[x, pl.ds(w0[x], win), :] += jnp.sum(jnp.where(hits[x], aff_x[x], 0.0),
                                                             axis=1, keepdims=True)
            return c

        lax.fori_loop(0, trips, window, 0)
        return carry

    lax.fori_loop(0, nblk, chunk, 0)
    xe_ref[0] = acc_ref[:, 0:cap, :].astype(BF16)
    gate_ref[0] = gacc_ref[:, 0:cap, :]


def _gather(boff_flat, slot, aff_t, h2, cap, chunk, win=128, epg=2):
    B, nblk, E, _ = slot.shape
    S, D = h2.shape[1:]
    rowblk = pl.BlockSpec((1, nblk, E, chunk), lambda b, e, off: (b, 0, 0, 0))
    assert E % epg == 0 and cap % SUBLANES == 0
    return pl.pallas_call(
        functools.partial(_gather_kernel, win=win, epg=epg),
        grid_spec=pltpu.PrefetchScalarGridSpec(
            num_scalar_prefetch=1,
            grid=(B, E // epg),
            in_specs=[rowblk, rowblk,
                      pl.BlockSpec((1, S, D), lambda b, e, off: (b, 0, 0))],
            out_specs=[pl.BlockSpec((1, epg, cap, D), lambda b, e, off: (b, e, 0, 0)),
                       pl.BlockSpec((1, epg, cap, 1), lambda b, e, off: (b, e, 0, 0))],
            scratch_shapes=[pltpu.VMEM((epg, cap + win, D), F32),
                            pltpu.VMEM((epg, cap + win, 1), F32)]),
        out_shape=[jax.ShapeDtypeStruct((B, E, cap, D), BF16),
                   jax.ShapeDtypeStruct((B, E, cap, 1), F32)],
        compiler_params=_cparams(("parallel", "arbitrary")),
        name="ec_gather",
    )(boff_flat, slot, aff_t, h2)


def _ffn_kernel(xe_ref, gate_ref, wg_ref, wu_ref, wd_ref, ye_ref, acc_ref):
    f = pl.program_id(2)
    nb, _, cap, D = xe_ref.shape

    @pl.when((pl.program_id(0) == 0) & (pl.program_id(1) == 0) & (f == 0))
    def _():
        acc_ref[...] = jnp.zeros(acc_ref.shape, F32)

    wg = wg_ref[0].astype(BF16)
    wu = wu_ref[0].astype(BF16)
    wd = wd_ref[0].astype(BF16)
    for i in range(nb):
        x = xe_ref[i, 0]
        g = jnp.dot(x, wg, preferred_element_type=F32)
        u = jnp.dot(x, wu, preferred_element_type=F32)
        hid = (g * jax.nn.sigmoid(g) * u).astype(BF16)
        part = jnp.dot(hid, wd, preferred_element_type=F32)
        rows = slice(i * cap, (i + 1) * cap)
        acc_ref[rows, :] = jnp.where(f == 0, 0.0, acc_ref[rows, :]) + part

    @pl.when(f == pl.num_programs(2) - 1)
    def _():
        gate = gate_ref[...].reshape(nb * cap, 1)
        ye_ref[...] = (acc_ref[...] * gate).astype(BF16).reshape(ye_ref.shape)


def _expert_ffn(xe, gates, w_gate, w_up, w_down, nb=2, tf=512):
    B, E, cap, D = xe.shape
    F = w_gate.shape[-1]
    return pl.pallas_call(
        _ffn_kernel,
        grid=(E, B // nb, F // tf),
        in_specs=[pl.BlockSpec((nb, 1, cap, D), lambda e, b, f: (b, e, 0, 0)),
                  pl.BlockSpec((nb, 1, cap, 1), lambda e, b, f: (b, e, 0, 0)),
                  pl.BlockSpec((1, D, tf), lambda e, b, f: (e, 0, f)),
                  pl.BlockSpec((1, D, tf), lambda e, b, f: (e, 0, f)),
                  pl.BlockSpec((1, tf, D), lambda e, b, f: (e, f, 0))],
        out_specs=pl.BlockSpec((nb, 1, cap, D), lambda e, b, f: (b, e, 0, 0)),
        out_shape=jax.ShapeDtypeStruct((B, E, cap, D), BF16),
        scratch_shapes=[pltpu.VMEM((nb * cap, D), F32)],
        compiler_params=_cparams(("arbitrary", "arbitrary", "arbitrary")),
        name="ec_ffn",
    )(xe, gates, w_gate, w_up, w_down)


def _combine_kernel(boff_ref, aff_ref, thr_ref, tie_ref, ye_ref, x1_ref, mod_ref, g_ref, out_ref,
                    slot_ref, carry_ref, *, tb, wslot, nblk, epg):
    b = pl.program_id(0)
    t2 = pl.program_id(1)
    e = pl.program_id(2)
    nsub = aff_ref.shape[1] // tb
    cap = ye_ref.shape[2]
    lane = lax.broadcasted_iota(I32, (tb, LANES), 1)

    @pl.when(e == 0)
    def _():
        @pl.when(t2 == 0)
        def _():
            carry_ref[...] = jnp.zeros(carry_ref.shape, F32)

        lower = (lax.broadcasted_iota(I32, (tb, tb), 0)
                 >= lax.broadcasted_iota(I32, (tb, tb), 1)).astype(BF16)
        for u in range(nsub):
            rows = slice(u * tb, (u + 1) * tb)
            bits = lax.bitcast_convert_type(aff_ref[0, rows, :], I32)
            tok = (t2 * nsub + u) * tb + lax.broadcasted_iota(I32, (tb, LANES), 0)
            thr = thr_ref[0]
            sel = ((bits > thr) | ((bits == thr) & (tok <= tie_ref[0]))) & (lane < N_EXPERTS)
            self_ = sel.astype(F32)
            incl = jnp.dot(lower, self_.astype(BF16), preferred_element_type=F32)
            pos = (incl - self_ + carry_ref[...]).astype(I32)
            slot_ref[rows, :] = jnp.where(sel, pos, -1)
            carry_ref[...] = carry_ref[...] + incl[tb - 1:tb, :]
        out_ref[...] = jnp.zeros(out_ref.shape, F32)

    n_e = pl.num_programs(2) * epg
    for u in range(nsub):
        rows = slice(u * tb, (u + 1) * tb)
        slot_x, first_x, trips = [], [], jnp.int32(0)
        for x in range(epg):
            ex = e * epg + x
            base = (b * n_e + ex) * (nblk + 1) + t2 * nsub + u
            first = (boff_ref[base] // SUBLANES_BF16) * SUBLANES_BF16
            slot_x.append(jnp.sum(jnp.where(lane == ex, slot_ref[rows, :], 0),
                                  axis=1, keepdims=True))
            first_x.append(first)
            trips = jnp.maximum(trips, (boff_ref[base + 1] - first + wslot - 1) // wslot)

        def body(k, carry):
            hots, wins = [], []
            for x in range(epg):
                start = first_x[x] + k * wslot
                woff = pl.multiple_of(jnp.minimum(start, cap - wslot), SUBLANES_BF16)
                todo = jnp.where(slot_x[x] >= start, slot_x[x], -1)
                hots.append((lax.broadcasted_iota(I32, (tb, wslot), 1) + woff == todo).astype(BF16))
                wins.append(ye_ref[0, x, pl.ds(woff, wslot), :])
            out_ref[0, rows, :] += jnp.dot(jnp.concatenate(hots, axis=1),
                                           jnp.concatenate(wins, axis=0),
                                           preferred_element_type=F32)
            return carry

        lax.fori_loop(0, trips, body, 0)

    @pl.when(e == pl.num_programs(2) - 1)
    def _():
        for u in range(nsub):
            rows = slice(u * tb, (u + 1) * tb)
            x2 = x1_ref[0, rows, :] + mod_ref[0, 5:6, :] * out_ref[0, rows, :]
            out_ref[0, rows, :] = (x2 * lax.rsqrt(jnp.mean(x2 * x2, axis=-1, keepdims=True)
                                                  + NORM_EPS) * g_ref[...])


def _combine(boff_flat, aff, thr, tie, ye, x1, mod, g_final, tb, tsup=2048, wslot=128, epg=2):
    B, S, D = x1.shape
    E, cap = ye.shape[1], ye.shape[2]
    assert cap % wslot == 0 and E % epg == 0
    tsup = min(tsup, S)
    return pl.pallas_call(
        functools.partial(_combine_kernel, tb=tb, wslot=wslot, nblk=S // tb, epg=epg),
        grid_spec=pltpu.PrefetchScalarGridSpec(
            num_scalar_prefetch=1,
            grid=(B, S // tsup, E // epg),
            in_specs=[pl.BlockSpec((1, tsup, LANES), lambda b, t, e, off: (b, t, 0)),
                      pl.BlockSpec((1, 1, LANES), lambda b, t, e, off: (b, 0, 0)),
                      pl.BlockSpec((1, 1, LANES), lambda b, t, e, off: (b, 0, 0)),
                      pl.BlockSpec((1, epg, cap, D), lambda b, t, e, off: (b, e, 0, 0)),
                      pl.BlockSpec((1, tsup, D), lambda b, t, e, off: (b, t, 0)),
                      pl.BlockSpec((1, 6, D), lambda b, t, e, off: (b, 0, 0)),
                      pl.BlockSpec((1, D), lambda b, t, e, off: (0, 0))],
            out_specs=pl.BlockSpec((1, tsup, D), lambda b, t, e, off: (b, t, 0)),
            scratch_shapes=[pltpu.VMEM((tsup, LANES), I32), pltpu.VMEM((1, LANES), F32)]),
        out_shape=jax.ShapeDtypeStruct((B, S, D), F32),
        compiler_params=_cparams(("parallel", "arbitrary", "arbitrary")),
        name="ec_combine",
    )(boff_flat, aff, thr, tie, ye, x1, mod, g_final)


def _prep_weights(w_in, w_uq, w_ukv):
    D = w_in.shape[0]
    H = MLA_HEADS
    n_lat = MLA_Q_RANK + MLA_KV_RANK + MLA_ROPE
    n_grp = 3 * DIL_HEADS * DIL_HEAD_DIM
    n_dil = len(DIL_GROUPS) * n_grp
    w_lat = jnp.zeros((D, LAT_PAD), F32).at[:, :n_lat].set(w_in[:, :n_lat]).astype(BF16)
    w_grp = [w_in[:, n_lat + i * n_grp:n_lat + (i + 1) * n_grp].astype(BF16)
             for i in range(len(DIL_GROUPS))]
    w_gates = w_in[:, n_lat + n_dil:].astype(BF16)
    half = MLA_ROPE // 2
    uq = w_uq.reshape(MLA_Q_RANK, H, MLA_NOPE + MLA_ROPE)
    pe = uq[:, :, MLA_NOPE:]
    zq = jnp.zeros((MLA_Q_RANK, H, QK_PAD - MLA_NOPE - MLA_ROPE), F32)
    wq = jnp.concatenate([uq[:, :, :MLA_NOPE], pe, zq], axis=2)
    wqs = jnp.concatenate([pe[:, :, half:], pe[:, :, :half], zq], axis=2)
    ukv = w_ukv.reshape(MLA_KV_RANK, H, MLA_NOPE + MLA_V)
    wk = ukv[:, :, :MLA_NOPE]
    wv = ukv[:, :, MLA_NOPE:]
    return (w_lat, w_grp, w_gates, wq.reshape(MLA_Q_RANK, H * QK_PAD).astype(BF16),
            wqs.reshape(MLA_Q_RANK, H * LANES).astype(BF16),
            wk.reshape(MLA_KV_RANK, H * MLA_NOPE).astype(BF16),
            wv.reshape(MLA_KV_RANK, H * MLA_V).astype(BF16))


def kernel(x, c, positions, w_ada, b_ada, g_norm_mix, w_in, g_q_lat, g_kv_lat, w_uq, w_ukv,
           rel_bias, w_out, g_norm_ffn, w_router, w_gate, w_up, w_down, g_final):
    B, S, D = x.shape
    assert w_ada.shape[0] == 1, "the final norm is fused into the (single) layer's last kernel"
    assert DIL_GROUPS[0][1] == 1, "the gates ride along with the undilated group's projection"
    for l in range(w_ada.shape[0]):
        mod = _modulation(c, w_ada[l], b_ada[l])
        w_lat, w_grp, w_gates, wq, wqs, wk, wv = _prep_weights(w_in[l], w_uq[l], w_ukv[l])
        g_mix = g_norm_mix[l].reshape(1, D)
        n_qkv = w_grp[0].shape[1] // D
        hs = _norm_modulate(x, mod, g_mix, tuple(d for _, d in DIL_GROUPS))
        lat = _projection(hs[0], w_lat, "lat_projection", out_dtype=F32)
        main = _projection(hs[0], jnp.concatenate([w_grp[0], w_gates], axis=1), "in_projection_g0",
                           n_plain=n_qkv)
        qkvs = [main] + [_projection(hs[gi], w_grp[gi], f"in_projection_g{gi}")
                         for gi in range(1, len(DIL_GROUPS))]
        q, k, v = _mla_prep(lat.reshape(B, S, LAT_PAD), positions, g_q_lat[l].reshape(1, -1), g_kv_lat[l].reshape(1, -1),
                            wq, wqs, wk, wv)
        o_a = _flash_attention(q, k, v)
        dil_outs = [_dilated_group(qkvs[gi], positions, rel_bias, gi, window, d)
                    for gi, (window, d) in enumerate(DIL_GROUPS)]
        x1, h2, aff, aff_t = _out_projection(o_a, dil_outs, main, n_qkv, x, mod,
                                             g_norm_ffn[l].reshape(1, D),
                                             w_out[l].astype(BF16), w_router[l])
        cap = EC_CAPACITY_FACTOR * S // N_EXPERTS
        chunk = 512
        thr, tie, slot, boff = _select(aff_t, cap, chunk)
        boff_flat = boff[:, :, :S // chunk + 1].reshape(-1)
        xe, gates = _gather(boff_flat, slot, aff_t, h2, cap, chunk)
        ye = _expert_ffn(xe, gates, w_gate[l], w_up[l], w_down[l])
        pad = jnp.zeros((B, 1, LANES - N_EXPERTS), I32)
        thr_l = jnp.concatenate([thr[:, :, 0].reshape(B, 1, N_EXPERTS), pad], axis=2)
        tie_l = jnp.concatenate([tie[:, :, 0].reshape(B, 1, N_EXPERTS), pad], axis=2)
        x = _combine(boff_flat, aff, thr_l, tie_l, ye, x1, mod, g_final.reshape(1, D), tb=chunk)
    return x
```

```python
import functools
import math

import jax
import jax.numpy as jnp
from jax import lax
from jax.experimental import pallas as pl
from jax.experimental.pallas import tpu as pltpu

F32 = jnp.float32
BF16 = jnp.bfloat16
I32 = jnp.int32

MLA_HEADS = 8
MLA_Q_RANK = 384
MLA_KV_RANK = 256
MLA_NOPE = 128
MLA_ROPE = 64
MLA_V = 128
ROPE_THETA = 10000.0
DIL_GROUPS = ((128, 1), (512, 4), (2048, 16))
DIL_HEADS = 8
DIL_HEAD_DIM = 128
REL_BUCKETS = 32
REL_MAX_DIST = 1024
N_EXPERTS = 16
EC_CAPACITY_FACTOR = 2
NORM_EPS = 1e-6
NEG_INF = -1e30

LANES = 128
SUBLANES = 8
SUBLANES_BF16 = 16
QK_PAD = 256
LAT_PAD = 768
VMEM_V7X = 64 * 1024 * 1024
VMEM_LIMIT = VMEM_V7X * 7 // 8


def _cparams(sem):
    return pltpu.CompilerParams(dimension_semantics=sem, vmem_limit_bytes=VMEM_LIMIT)


def _nt_dot(a, b):
    return lax.dot_general(a, b, (((1,), (1,)), ((), ())), preferred_element_type=F32)


def _mod_kernel(c_ref, w_ref, b_ref, o_ref):
    c = c_ref[...]
    cond = c * jax.nn.sigmoid(c)
    o_ref[...] = jnp.dot(cond, w_ref[...], preferred_element_type=F32,
                         precision=lax.Precision.HIGHEST) + b_ref[...]


def _modulation(c, w_ada, b_ada):
    B, D = c.shape
    rows = -(-B // SUBLANES) * SUBLANES
    c8 = jnp.zeros((rows, D), F32).at[:B].set(c)
    n6 = w_ada.shape[1]
    out = pl.pallas_call(
        _mod_kernel,
        grid=(n6 // D,),
        in_specs=[pl.BlockSpec((rows, D), lambda j: (0, 0)),
                  pl.BlockSpec((D, D), lambda j: (0, j)),
                  pl.BlockSpec((1, D), lambda j: (0, j))],
        out_specs=pl.BlockSpec((rows, D), lambda j: (0, j)),
        out_shape=jax.ShapeDtypeStruct((rows, n6), F32),
        compiler_params=_cparams(("arbitrary",)),
        name="modulation",
    )(c8, w_ada, b_ada.reshape(1, n6))
    return out[:B].reshape(B, 6, D)


def _norm_kernel(*refs, dils):
    nx = len(refs) - 2 - len(dils)
    x_refs = refs[:nx]
    mod_ref, g_ref = refs[nx:nx + 2]
    h_refs = refs[nx + 2:]
    tm = x_refs[0].shape[1]
    for h_ref, d in zip(h_refs, dils):
        rows = tm // d
        for g in range(d):
            src = pl.ds(g, rows, stride=d) if d > 1 else slice(None)
            x = jnp.concatenate([xr[0, src, :] for xr in x_refs], axis=1)
            y = x * lax.rsqrt(jnp.mean(x * x, axis=-1, keepdims=True) + NORM_EPS) * g_ref[...]
            h = y * (1.0 + mod_ref[0, 1:2, :]) + mod_ref[0, 0:1, :]
            h_ref[0, g] = h.astype(BF16)


def _norm_modulate(x, mod, g, dils, tm=1024):
    B, S, D = x.shape
    in_specs = [pl.BlockSpec((1, tm, LANES), functools.partial(lambda b, i, j: (b, i, j), j=j))
                for j in range(D // LANES)]
    in_specs += [pl.BlockSpec((1, 6, D), lambda b, i: (b, 0, 0)),
                 pl.BlockSpec((1, D), lambda b, i: (0, 0))]
    return pl.pallas_call(
        functools.partial(_norm_kernel, dils=dils),
        grid=(B, S // tm),
        in_specs=in_specs,
        out_specs=[pl.BlockSpec((1, d, tm // d, D), lambda b, i: (b, 0, i, 0)) for d in dils],
        out_shape=[jax.ShapeDtypeStruct((B, d, S // d, D), BF16) for d in dils],
        compiler_params=_cparams(("parallel", "parallel")),
        name="norm_modulate",
    )(*([x] * (D // LANES) + [mod, g]))


def _proj_kernel(h_ref, w_ref, o_ref, *, n_plain, n_tiles):
    acc = jnp.dot(h_ref[...], w_ref[...], preferred_element_type=F32)
    if n_plain == n_tiles:
        o_ref[...] = acc.astype(o_ref.dtype)
    else:
        n = pl.program_id(1)

        @pl.when(n < n_plain)
        def _():
            o_ref[...] = acc.astype(o_ref.dtype)

        @pl.when(n >= n_plain)
        def _():
            o_ref[...] = jax.nn.sigmoid(acc).astype(o_ref.dtype)


def _projection(h, w, name, n_plain=None, out_dtype=BF16, tm=2048, tn=1024):
    B, d, L, D = h.shape
    N = w.shape[1]
    tn = min(tn, N)
    tm = min(tm, B * d * L)
    n_tiles = N // tn
    n_plain = n_tiles if n_plain is None else n_plain
    out = pl.pallas_call(
        functools.partial(_proj_kernel, n_plain=n_plain, n_tiles=n_tiles),
        grid=(B * d * L // tm, n_tiles),
        in_specs=[pl.BlockSpec((tm, D), lambda i, n: (i, 0)),
                  pl.BlockSpec((D, tn), lambda i, n: (0, n))],
        out_specs=pl.BlockSpec((tm, tn), lambda i, n: (i, n)),
        out_shape=jax.ShapeDtypeStruct((B * d * L, N), out_dtype),
        compiler_params=_cparams(("parallel", "arbitrary")),
        name=name,
    )(h.reshape(B * d * L, D), w)
    return out.reshape(B, d, L, N)


def _mla_prep_kernel(lat_ref, pos_ref, gq_ref, gkv_ref, wq_ref, wqs_ref, wk_ref, wv_ref,
                     freq_ref, sgn_ref, sel_ref, sels_ref, one_ref, q_ref, k_ref, v_ref, *, scale):
    lat = lat_ref[0]
    cq = lat[:, :MLA_Q_RANK]
    ckv = lat[:, MLA_Q_RANK:MLA_Q_RANK + MLA_KV_RANK]
    kpe = lat[:, MLA_Q_RANK + MLA_KV_RANK:]
    cqn = (cq * lax.rsqrt(jnp.mean(cq * cq, axis=-1, keepdims=True) + NORM_EPS)
           * gq_ref[...]).astype(BF16)
    ckvn = (ckv * lax.rsqrt(jnp.mean(ckv * ckv, axis=-1, keepdims=True) + NORM_EPS)
            * gkv_ref[...]).astype(BF16)
    tm = lat.shape[0]
    pos = jnp.concatenate(
        [jnp.broadcast_to(pos_ref[0, :, c * LANES:(c + 1) * LANES], (LANES, LANES)).T
         for c in range(tm // LANES)], axis=0).astype(F32)
    ang = pos * freq_ref[...]
    cos = jnp.cos(ang)
    sin = jnp.sin(ang) * sgn_ref[...]
    qa = jnp.dot(cqn, wq_ref[...], preferred_element_type=F32)
    qs = jnp.dot(cqn, wqs_ref[...], preferred_element_type=F32)
    kn = jnp.dot(ckvn, wk_ref[...], preferred_element_type=F32)
    vv = jnp.dot(ckvn, wv_ref[...], preferred_element_type=F32)
    hp = lax.Precision.HIGHEST
    rk = (jnp.dot(kpe, sel_ref[...], preferred_element_type=F32, precision=hp) * cos
          + jnp.dot(kpe, sels_ref[...], preferred_element_type=F32, precision=hp) * sin).astype(BF16)
    cos_q = cos * scale
    sin_q = sin * scale
    ones = jnp.broadcast_to(one_ref[...], (tm, LANES)).astype(BF16)
    for h in range(MLA_HEADS):
        lo = slice(h * QK_PAD, h * QK_PAD + LANES)
        hi = slice(h * QK_PAD + LANES, (h + 1) * QK_PAD)
        hd = slice(h * LANES, (h + 1) * LANES)
        q_ref[0, :, lo] = (qa[:, lo] * scale).astype(BF16)
        q_ref[0, :, hi] = (qa[:, hi] * cos_q + qs[:, hd] * sin_q).astype(BF16)
        k_ref[0, :, lo] = kn[:, hd].astype(BF16)
        k_ref[0, :, hi] = rk
        v_ref[0, :, lo] = vv[:, hd].astype(BF16)
        v_ref[0, :, hi] = ones


def _mla_prep(lat, positions, g_q, g_kv, wq, wqs, wk, wv, tm=512):
    B, S, _ = lat.shape
    H = MLA_HEADS
    half = MLA_ROPE // 2
    assert MLA_NOPE == LANES and MLA_V == LANES and QK_PAD == 2 * LANES and tm % LANES == 0
    inv_freq = ROPE_THETA ** (-jnp.arange(0, MLA_ROPE, 2, dtype=F32) / MLA_ROPE)
    freq = jnp.zeros((1, LANES), F32).at[0, :MLA_ROPE].set(jnp.concatenate([inv_freq, inv_freq]))
    sgn = jnp.zeros((1, LANES), F32).at[0, :half].set(-1.0).at[0, half:MLA_ROPE].set(1.0)
    r = jnp.arange(MLA_ROPE)
    sel = jnp.zeros((LANES, LANES), F32).at[r, r].set(1.0)
    sels = jnp.zeros((LANES, LANES), F32).at[r, (r + half) % MLA_ROPE].set(1.0)
    scale = (MLA_NOPE + MLA_ROPE) ** -0.5 * math.log2(math.e)
    ones_col = jnp.zeros((1, LANES), F32).at[0, 0].set(1.0)
    const = lambda b, i: (0, 0)
    return pl.pallas_call(
        functools.partial(_mla_prep_kernel, scale=scale),
        grid=(B, S // tm),
        in_specs=[pl.BlockSpec((1, tm, LAT_PAD), lambda b, i: (b, i, 0)),
                  pl.BlockSpec((1, 1, tm), lambda b, i: (b, 0, i)),
                  pl.BlockSpec((1, MLA_Q_RANK), const),
                  pl.BlockSpec((1, MLA_KV_RANK), const),
                  pl.BlockSpec((MLA_Q_RANK, H * QK_PAD), const),
                  pl.BlockSpec((MLA_Q_RANK, H * LANES), const),
                  pl.BlockSpec((MLA_KV_RANK, H * LANES), const),
                  pl.BlockSpec((MLA_KV_RANK, H * LANES), const),
                  pl.BlockSpec((1, LANES), const),
                  pl.BlockSpec((1, LANES), const),
                  pl.BlockSpec((LANES, LANES), const),
                  pl.BlockSpec((LANES, LANES), const),
                  pl.BlockSpec((1, LANES), const)],
        out_specs=[pl.BlockSpec((1, tm, H * QK_PAD), lambda b, i: (b, i, 0)),
                   pl.BlockSpec((1, tm, H * QK_PAD), lambda b, i: (b, i, 0)),
                   pl.BlockSpec((1, tm, H * QK_PAD), lambda b, i: (b, i, 0))],
        out_shape=[jax.ShapeDtypeStruct((B, S, H * QK_PAD), BF16),
                   jax.ShapeDtypeStruct((B, S, H * QK_PAD), BF16),
                   jax.ShapeDtypeStruct((B, S, H * QK_PAD), BF16)],
        compiler_params=_cparams(("parallel", "parallel")),
        name="mla_prep",
    )(lat, positions.reshape(B, 1, S), g_q, g_kv, wq, wqs, wk, wv, freq, sgn, sel, sels, ones_col)


def _flash_kernel(q_ref, k_ref, v_ref, o_ref, m_ref, acc_ref, s_ref, *, tq, tkc):
    S = k_ref.shape[1]
    nk = S // tkc
    nq = S // tq
    n_items = nq * nk
    acc_ref[...] = jnp.zeros(acc_ref.shape, F32)
    m_ref[...] = jnp.full(m_ref.shape, -jnp.inf, F32)

    def scores(t, slot):
        t = jnp.minimum(t, n_items - 1)
        qoff = pl.multiple_of((t // nk) * tq, tq)
        koff = pl.multiple_of((t % nk) * tkc, tkc)
        s_ref[slot] = _nt_dot(q_ref[0, pl.ds(qoff, tq), :], k_ref[0, pl.ds(koff, tkc), :])

    def consume(t, slot, first):
        koff = pl.multiple_of((t % nk) * tkc, tkc)
        s = s_ref[slot]
        m_prev = m_ref[...]
        if first:
            m_prev = jnp.where(t % nk == 0, -jnp.inf, m_prev)
        m_new = jnp.maximum(m_prev, jnp.max(s, axis=-1, keepdims=True))
        alpha = jnp.exp2(m_prev - m_new)
        p = jnp.exp2(s - m_new).astype(BF16)
        acc_ref[...] = alpha * acc_ref[...] + jnp.dot(p, v_ref[0, pl.ds(koff, tkc), :],
                                                      preferred_element_type=F32)
        m_ref[...] = m_new

    scores(0, 0)

    def body(qi, carry):
        t0 = qi * nk
        for c in range(nk):
            scores(t0 + c + 1, (c + 1) % 2)
            consume(t0 + c, c % 2, c == 0)
        acc = acc_ref[...]
        qoff = pl.multiple_of(qi * tq, tq)
        o_ref[0, pl.ds(qoff, tq), :] = (acc[:, :MLA_V] / acc[:, MLA_V:MLA_V + 1]).astype(BF16)
        return carry

    lax.fori_loop(0, nq, body, 0)


def _flash_attention(q, k, v, tq=1024, tkc=1024):
    B, S, _ = q.shape
    H = MLA_HEADS
    assert (S // tkc) % 2 == 0 and S % tq == 0
    whole = pl.BlockSpec((1, S, QK_PAD), lambda b, h: (b, 0, h))
    return pl.pallas_call(
        functools.partial(_flash_kernel, tq=tq, tkc=tkc),
        grid=(B, H),
        in_specs=[whole, whole, whole],
        out_specs=pl.BlockSpec((1, S, MLA_V), lambda b, h: (b, 0, h)),
        out_shape=jax.ShapeDtypeStruct((B, S, H * MLA_V), BF16),
        scratch_shapes=[pltpu.VMEM((tq, 1), F32), pltpu.VMEM((tq, QK_PAD), F32),
                        pltpu.VMEM((2, tq, tkc), F32)],
        compiler_params=_cparams(("parallel", "parallel")),
        name="mla_flash",
    )(q, k, v)


def _t5_bucket(rel):
    nb = REL_BUCKETS // 2
    max_exact = nb // 2
    ret = jnp.where(rel > 0, nb, 0)
    n = jnp.abs(rel)
    nf = jnp.maximum(n, 1).astype(F32)
    large = max_exact + (jnp.log(nf / max_exact) / math.log(REL_MAX_DIST / max_exact)
                         * (nb - max_exact)).astype(I32)
    large = jnp.minimum(large, nb - 1)
    return ret + jnp.where(n < max_exact, n, large)


def _bias_lookup(tab_ref, h, bucket):
    rows, nk = bucket.shape
    tab = jnp.broadcast_to(tab_ref[h:h + 1, :], (rows, LANES))
    return jnp.concatenate(
        [jnp.take_along_axis(tab, bucket[:, c * LANES:(c + 1) * LANES], axis=1)
         for c in range(nk // LANES)], axis=1)


def _band_bias_kernel(tab_ref, band_ref, *, r, dil):
    _, ts, nk = band_ref.shape
    off = lax.broadcasted_iota(I32, (ts, nk), 1) - r - lax.broadcasted_iota(I32, (ts, nk), 0)
    bucket = jnp.where(jnp.abs(off) <= r, _t5_bucket(off * dil), REL_BUCKETS)
    for h in range(DIL_HEADS):
        band_ref[h] = _bias_lookup(tab_ref, h, bucket)


def _band_bias(tab, r, dil, ts):
    nk = ts + 2 * r
    return pl.pallas_call(
        functools.partial(_band_bias_kernel, r=r, dil=dil),
        grid=(1,),
        in_specs=[pl.BlockSpec((DIL_HEADS, LANES), lambda i: (0, 0))],
        out_specs=pl.BlockSpec((DIL_HEADS, ts, nk), lambda i: (0, 0, 0)),
        out_shape=jax.ShapeDtypeStruct((DIL_HEADS, ts, nk), F32),
        compiler_params=_cparams(("arbitrary",)),
        name=f"band_bias_d{dil}",
    )(tab)


def _dilated_kernel(q_ref, kp_ref, kc_ref, kn_ref, vp_ref, vc_ref, vn_ref, pq_ref, pk_ref, tab_ref,
                    band_ref, o_ref, l_ref, kcat_ref, vcat_ref, s_scr, *, r, tq, ts, n_rows, dil):
    n = pl.program_id(2)
    nsub = tq // ts
    nk = ts + 2 * r
    kcat_ref[0:r, :] = kp_ref[0, 0]
    kcat_ref[r:r + tq, :] = kc_ref[0, 0]
    kcat_ref[r + tq:tq + 2 * r, :] = kn_ref[0, 0]
    ones = jnp.ones((tq + 2 * r, DIL_HEAD_DIM), BF16)
    for h in range(DIL_HEADS):
        sl = slice(h * DIL_HEAD_DIM, (h + 1) * DIL_HEAD_DIM)
        dst = slice(2 * h * DIL_HEAD_DIM, (2 * h + 1) * DIL_HEAD_DIM)
        vcat_ref[0:r, dst] = vp_ref[0, 0, :, sl]
        vcat_ref[r:r + tq, dst] = vc_ref[0, 0, :, sl]
        vcat_ref[r + tq:tq + 2 * r, dst] = vn_ref[0, 0, :, sl]
        vcat_ref[:, (2 * h + 1) * DIL_HEAD_DIM:(2 * h + 2) * DIL_HEAD_DIM] = ones
    lane_grp = lax.broadcasted_iota(I32, (ts, LANES), 1) // (LANES // DIL_HEADS)
    scale = DIL_HEAD_DIM ** -0.5 * math.log2(math.e)
    for j in range(nsub):
        qi = n * tq + j * ts + lax.broadcasted_iota(I32, (ts, nk), 0)
        kj = n * tq + j * ts - r + lax.broadcasted_iota(I32, (ts, nk), 1)
        inside = (kj >= 0) & (kj < n_rows)
        valid = (jnp.abs(kj - qi) <= r) & inside
        pq = jnp.broadcast_to(pq_ref[0, 0, :, j * ts:(j + 1) * ts], (LANES, ts)).T
        pk = pk_ref[0, 0, j]
        rel = jnp.concatenate([pk[:, c * LANES:(c + 1) * LANES] - pq
                               for c in range(nk // LANES)], axis=1)
        regular = jnp.max(jnp.where(valid & (rel != (kj - qi) * dil), 1, 0)) == 0

        def logits(h):
            sl = slice(h * DIL_HEAD_DIM, (h + 1) * DIL_HEAD_DIM)
            return _nt_dot(q_ref[0, 0, j * ts:(j + 1) * ts, sl], kcat_ref[j * ts:j * ts + nk, sl]) * scale

        @pl.when(regular)
        def _():
            for h in range(DIL_HEADS):
                s_scr[j * DIL_HEADS + h] = jnp.where(inside, logits(h) + band_ref[h], NEG_INF)

        @pl.when(jnp.logical_not(regular))
        def _():
            bucket = jnp.where(valid, _t5_bucket(rel), REL_BUCKETS)
            for h in range(DIL_HEADS):
                s_scr[j * DIL_HEADS + h] = logits(h) + _bias_lookup(tab_ref, h, bucket)
    for j in range(nsub):
        lse_tile = jnp.zeros((ts, LANES), F32)
        for h in range(DIL_HEADS):
            sl = slice(h * DIL_HEAD_DIM, (h + 1) * DIL_HEAD_DIM)
            s = s_scr[j * DIL_HEADS + h]
            m = jnp.max(s, axis=-1, keepdims=True)
            p = jnp.exp2(s - m).astype(BF16)
            res = jnp.dot(p, vcat_ref[j * ts:j * ts + nk,
                                      2 * h * DIL_HEAD_DIM:(2 * h + 2) * DIL_HEAD_DIM],
                          preferred_element_type=F32)
            den = res[:, DIL_HEAD_DIM:]
            o_ref[0, 0, j * ts:(j + 1) * ts, sl] = (res[:, :DIL_HEAD_DIM] / den).astype(BF16)
            lse_tile = jnp.where(lane_grp == h, m * math.log(2.0) + jnp.log(den), lse_tile)
        l_ref[0, 0, j * ts:(j + 1) * ts, :] = lse_tile


def _dilated_group(qkv, positions, table, gi, window, d, tq=512, ts=128):
    B, _, L, _ = qkv.shape
    HD = DIL_HEADS * DIL_HEAD_DIM
    r = window // (2 * d)
    tq = min(tq, L)
    nt = L // tq
    nsub = tq // ts
    rb = tq // r
    nk = ts + 2 * r
    pcls = positions.reshape(B, L, d).transpose(0, 2, 1)
    pq = pcls.reshape(B, d, 1, L)
    ppad = jnp.pad(pcls, ((0, 0), (0, 0), (r, ts + r)))
    pk = jnp.concatenate([ppad[:, :, :L].reshape(B, d, L // ts, ts),
                          ppad[:, :, ts:ts + L].reshape(B, d, L // ts, ts)[..., :2 * r]], axis=-1)
    pk = pk.reshape(B, d, L // ts, 1, nk)
    tab = jnp.zeros((DIL_HEADS, LANES), F32).at[:, :REL_BUCKETS].set(
        table[:, gi * DIL_HEADS:(gi + 1) * DIL_HEADS].T * math.log2(math.e))
    tab = tab.at[:, REL_BUCKETS].set(NEG_INF)

    def cur(c):
        return pl.BlockSpec((1, 1, tq, HD), lambda b, g, n: (b, g, n, c))

    def prv(c):
        return pl.BlockSpec((1, 1, r, HD), lambda b, g, n: (b, g, jnp.maximum(n * rb - 1, 0), c))

    def nxt(c):
        return pl.BlockSpec((1, 1, r, HD),
                            lambda b, g, n: (b, g, jnp.minimum((n + 1) * rb, L // r - 1), c))

    return pl.pallas_call(
        functools.partial(_dilated_kernel, r=r, tq=tq, ts=ts, n_rows=L, dil=d),
        grid=(B, d, nt),
        in_specs=[cur(0), prv(1), cur(1), nxt(1), prv(2), cur(2), nxt(2),
                  pl.BlockSpec((1, 1, 1, tq), lambda b, g, n: (b, g, 0, n)),
                  pl.BlockSpec((1, 1, nsub, 1, nk), lambda b, g, n: (b, g, n, 0, 0)),
                  pl.BlockSpec((DIL_HEADS, LANES), lambda b, g, n: (0, 0)),
                  pl.BlockSpec((DIL_HEADS, ts, nk), lambda b, g, n: (0, 0, 0))],
        out_specs=[pl.BlockSpec((1, 1, tq, HD), lambda b, g, n: (b, g, n, 0)),
                   pl.BlockSpec((1, 1, tq, LANES), lambda b, g, n: (b, g, n, 0))],
        out_shape=[jax.ShapeDtypeStruct((B, d, L, HD), BF16),
                   jax.ShapeDtypeStruct((B, d, L, LANES), F32)],
        scratch_shapes=[pltpu.VMEM((tq + 2 * r, HD), BF16), pltpu.VMEM((tq + 2 * r, 2 * HD), BF16),
                        pltpu.VMEM((nsub * DIL_HEADS, ts, nk), F32)],
        compiler_params=_cparams(("parallel", "parallel", "parallel")),
        name=f"dilated_g{gi}",
    )(*([qkv] * 7 + [pq, pk, tab, _band_bias(tab, r, d, ts)]))


def _outproj_kernel(*refs, dils):
    ng = len(dils)
    oa_ref = refs[0]
    og_refs = refs[1:1 + ng]
    lg_refs = refs[1 + ng:1 + 2 * ng]
    (ga_ref, gb_ref, x_ref, mod_ref, g_ref, wo_ref, wrh_ref, wrl_ref,
     x1_ref, h2_ref, aff_ref, afft_ref, o_scr, l_scr) = refs[1 + 2 * ng:]
    tm = x_ref.shape[1]
    for i, d in enumerate(dils):
        rows = tm // d
        for g in range(d):
            dst = pl.ds(g, rows, stride=d) if d > 1 else slice(None)
            og = og_refs[i][0, g].astype(F32)
            for h in range(DIL_HEADS):
                o_scr[i, h, dst, :] = og[:, h * DIL_HEAD_DIM:(h + 1) * DIL_HEAD_DIM]
            l_scr[i, dst, :] = lg_refs[i][0, g]
    lses = [l_scr[i] for i in range(ng)]
    mx = functools.reduce(jnp.maximum, lses)
    es = [jnp.exp(l - mx) for l in lses]
    tot = functools.reduce(lambda a, b: a + b, es)
    wts = [e / tot for e in es]
    cols = []
    for h in range(DIL_HEADS):
        c = h * (LANES // DIL_HEADS)
        cols.append(functools.reduce(
            lambda a, b: a + b, [wts[i][:, c:c + 1] * o_scr[i, h] for i in range(ng)]))
    o_b = jnp.concatenate(cols, axis=1)
    comb = (ga_ref[0, 0].astype(F32) * oa_ref[0].astype(F32)
            + gb_ref[0, 0].astype(F32) * o_b).astype(BF16)
    x1 = x_ref[0] + mod_ref[0, 2:3, :] * jnp.dot(comb, wo_ref[...], preferred_element_type=F32)
    x1_ref[0] = x1
    y = x1 * lax.rsqrt(jnp.mean(x1 * x1, axis=-1, keepdims=True) + NORM_EPS) * g_ref[...]
    h2 = y * (1.0 + mod_ref[0, 4:5, :]) + mod_ref[0, 3:4, :]
    h2_hi = h2.astype(BF16)
    h2_ref[0] = h2_hi
    h2_lo = (h2 - h2_hi.astype(F32)).astype(BF16)
    logits = (jnp.dot(h2_hi, wrh_ref[...], preferred_element_type=F32)
              + jnp.dot(h2_lo, wrh_ref[...], preferred_element_type=F32)
              + jnp.dot(h2_hi, wrl_ref[...], preferred_element_type=F32))
    lane = lax.broadcasted_iota(I32, logits.shape, 1)
    logits = jnp.where(lane < N_EXPERTS, logits, -jnp.inf)
    e = jnp.exp(logits - jnp.max(logits, axis=-1, keepdims=True))
    aff = e / jnp.sum(e, axis=-1, keepdims=True)
    aff_ref[0] = aff
    afft_ref[0, 0] = aff.T[:N_EXPERTS, :]


def _out_projection(o_a, dil_outs, main, gate_col, x, mod, g, w_out, w_router, tm=512):
    B, S, D = x.shape
    const = lambda b, i: (0, 0)
    row = lambda b, i: (b, i, 0)
    wr = jnp.zeros((D, LANES), F32).at[:, :N_EXPERTS].set(w_router)
    wr_hi = wr.astype(BF16)
    wr_lo = (wr - wr_hi.astype(F32)).astype(BF16)
    dils = tuple(o.shape[1] for o, _ in dil_outs)
    og_specs = [pl.BlockSpec((1, d, tm // d, D), lambda b, i: (b, 0, i, 0)) for d in dils]
    lg_specs = [pl.BlockSpec((1, d, tm // d, LANES), lambda b, i: (b, 0, i, 0)) for d in dils]
    return pl.pallas_call(
        functools.partial(_outproj_kernel, dils=dils),
        grid=(B, S // tm),
        in_specs=[pl.BlockSpec((1, tm, D), row)] + og_specs + lg_specs + [
                  pl.BlockSpec((1, 1, tm, D), lambda b, i: (b, 0, i, gate_col)),
                  pl.BlockSpec((1, 1, tm, D), lambda b, i: (b, 0, i, gate_col + 1)),
                  pl.BlockSpec((1, tm, D), row),
                  pl.BlockSpec((1, 6, D), lambda b, i: (b, 0, 0)),
                  pl.BlockSpec((1, D), const),
                  pl.BlockSpec((D, D), const),
                  pl.BlockSpec((D, LANES), const),
                  pl.BlockSpec((D, LANES), const)],
        out_specs=[pl.BlockSpec((1, tm, D), row),
                   pl.BlockSpec((1, tm, D), row),
                   pl.BlockSpec((1, tm, LANES), row),
                   pl.BlockSpec((1, 1, N_EXPERTS, tm), lambda b, i: (b, i, 0, 0))],
        out_shape=[jax.ShapeDtypeStruct((B, S, D), F32),
                   jax.ShapeDtypeStruct((B, S, D), BF16),
                   jax.ShapeDtypeStruct((B, S, LANES), F32),
                   jax.ShapeDtypeStruct((B, S // tm, N_EXPERTS, tm), F32)],
        scratch_shapes=[pltpu.VMEM((len(dils), DIL_HEADS, tm, DIL_HEAD_DIM), F32),
                        pltpu.VMEM((len(dils), tm, LANES), F32)],
        compiler_params=_cparams(("parallel", "parallel")),
        name="out_projection",
    )(o_a, *[o for o, _ in dil_outs], *[l for _, l in dil_outs], main, main, x, mod, g, w_out, wr_hi, wr_lo)


def _select_kernel(afft_ref, thr_ref, tie_ref, slot_ref, boff_ref, *, cap, chunk):
    aff = jnp.concatenate([afft_ref[0, c] for c in range(afft_ref.shape[1])], axis=1)
    E, S = aff.shape
    bits = lax.bitcast_convert_type(aff, I32)

    def count(mask):
        return jnp.sum(mask.astype(I32), axis=1, keepdims=True)

    def thr_step(i, v):
        cand = v | jnp.left_shift(jnp.int32(1), 30 - i)
        return jnp.where(count(bits >= cand) >= cap, cand, v)

    thr = lax.fori_loop(0, 31, thr_step, jnp.zeros((E, 1), I32))
    need = cap - count(bits > thr)
    eq = bits == thr
    idx = lax.broadcasted_iota(I32, (E, S), 1)
    nbits = max(1, (S - 1).bit_length())

    def tie_step(i, j):
        cand = j | jnp.left_shift(jnp.int32(1), nbits - 1 - i)
        return jnp.where(count(eq & (idx < cand)) < need, cand, j)

    tie = lax.fori_loop(0, nbits, tie_step, jnp.zeros((E, 1), I32))
    sel = (bits > thr) | (eq & (idx <= tie))
    thr_ref[0] = jnp.broadcast_to(thr, (E, LANES))
    tie_ref[0] = jnp.broadcast_to(tie, (E, LANES))

    upper = (lax.broadcasted_iota(I32, (chunk, chunk), 0)
             <= lax.broadcasted_iota(I32, (chunk, chunk), 1)).astype(BF16)
    carry = jnp.zeros((E, 1), F32)
    self_ = sel.astype(F32)
    lane = lax.broadcasted_iota(I32, (E, LANES), 1)
    boff = jnp.zeros((E, LANES), I32)
    for c in range(S // chunk):
        boff = jnp.where(lane == c, carry.astype(I32), boff)
        scf = self_[:, c * chunk:(c + 1) * chunk]
        incl = jnp.dot(scf.astype(BF16), upper, preferred_element_type=F32)
        pos = (incl - scf + carry).astype(I32)
        slot_ref[0, c] = jnp.where(scf > 0.0, pos, -1)
        carry = carry + incl[:, chunk - 1:chunk]
    boff_ref[0] = jnp.where(lane == S // chunk, carry.astype(I32), boff)


def _select(aff_t, cap, chunk):
    B, nblk, E, _ = aff_t.shape
    assert aff_t.shape[3] == chunk and nblk < LANES
    rows = pl.BlockSpec((1, nblk, E, chunk), lambda b: (b, 0, 0, 0))
    return pl.pallas_call(
        functools.partial(_select_kernel, cap=cap, chunk=chunk),
        grid=(B,),
        in_specs=[rows],
        out_specs=[pl.BlockSpec((1, E, LANES), lambda b: (b, 0, 0)),
                   pl.BlockSpec((1, E, LANES), lambda b: (b, 0, 0)),
                   rows,
                   pl.BlockSpec((1, E, LANES), lambda b: (b, 0, 0))],
        out_shape=[jax.ShapeDtypeStruct((B, E, LANES), I32),
                   jax.ShapeDtypeStruct((B, E, LANES), I32),
                   jax.ShapeDtypeStruct((B, nblk, E, chunk), I32),
                   jax.ShapeDtypeStruct((B, E, LANES), I32)],
        compiler_params=_cparams(("parallel",)),
        name="ec_select",
    )(aff_t)


def _gather_kernel(boff_ref, slot_ref, afft_ref, h_ref, xe_ref, gate_ref, acc_ref, gacc_ref, *,
                   win, epg):
    _, nblk, _, tbk = slot_ref.shape
    cap = xe_ref.shape[2]
    n_e = pl.num_programs(1) * epg
    acc_ref[...] = jnp.zeros(acc_ref.shape, F32)
    gacc_ref[...] = jnp.zeros(gacc_ref.shape, F32)

    def chunk(tb, carry):
        off = pl.multiple_of(tb * tbk, tbk)
        first_x, slot_x, aff_x, trips = [], [], [], jnp.int32(0)
        for x in range(epg):
            ex = pl.program_id(1) * epg + x
            base = (pl.program_id(0) * n_e + ex) * (nblk + 1) + tb
            first = (boff_ref[base] // SUBLANES) * SUBLANES
            first_x.append(first)
            slot_x.append(slot_ref[0, tb, pl.ds(ex, 1), :])
            aff_x.append(afft_ref[0, tb, pl.ds(ex, 1), :])
            trips = jnp.maximum(trips, (boff_ref[base + 1] - first + win - 1) // win)

        def window(k, c):
            w0 = [pl.multiple_of(jnp.minimum(first_x[x] + k * win, cap), SUBLANES)
                  for x in range(epg)]
            hits = [lax.broadcasted_iota(I32, (win, tbk), 0) + w0[x] == slot_x[x] for x in range(epg)]
            rows = jnp.dot(jnp.concatenate(hits, axis=0).astype(BF16), h_ref[0, pl.ds(off, tbk), :],
                           preferred_element_type=F32)
            for x in range(epg):
                acc_ref[x, pl.ds(w0[x], win), :] += rows[x * win:(x + 1) * win]
                gacc_ref[x, pl.ds(w0[x], win), :] += jnp.sum(jnp.where(hits[x], aff_x[x], 0.0),
                                                             axis=1, keepdims=True)
            return c

        lax.fori_loop(0, trips, window, 0)
        return carry

    lax.fori_loop(0, nblk, chunk, 0)
    xe_ref[0] = acc_ref[:, 0:cap, :].astype(BF16)
    gate_ref[0] = gacc_ref[:, 0:cap, :]


def _gather(boff_flat, slot, aff_t, h2, cap, chunk, win=128, epg=2):
    B, nblk, E, _ = slot.shape
    S, D = h2.shape[1:]
    rowblk = pl.BlockSpec((1, nblk, E, chunk), lambda b, e, off: (b, 0, 0, 0))
    assert E % epg == 0 and cap % SUBLANES == 0
    return pl.pallas_call(
        functools.partial(_gather_kernel, win=win, epg=epg),
        grid_spec=pltpu.PrefetchScalarGridSpec(
            num_scalar_prefetch=1,
            grid=(B, E // epg),
            in_specs=[rowblk, rowblk,
                      pl.BlockSpec((1, S, D), lambda b, e, off: (b, 0, 0))],
            out_specs=[pl.BlockSpec((1, epg, cap, D), lambda b, e, off: (b, e, 0, 0)),
                       pl.BlockSpec((1, epg, cap, 1), lambda b, e, off: (b, e, 0, 0))],
            scratch_shapes=[pltpu.VMEM((epg, cap + win, D), F32),
                            pltpu.VMEM((epg, cap + win, 1), F32)]),
        out_shape=[jax.ShapeDtypeStruct((B, E, cap, D), BF16),
                   jax.ShapeDtypeStruct((B, E, cap, 1), F32)],
        compiler_params=_cparams(("parallel", "arbitrary")),
        name="ec_gather",
    )(boff_flat, slot, aff_t, h2)


def _ffn_kernel(xe_ref, gate_ref, wg_ref, wu_ref, wd_ref, ye_ref, acc_ref):
    f = pl.program_id(2)
    nb, _, cap, D = xe_ref.shape

    @pl.when((pl.program_id(0) == 0) & (pl.program_id(1) == 0) & (f == 0))
    def _():
        acc_ref[...] = jnp.zeros(acc_ref.shape, F32)

    wg = wg_ref[0].astype(BF16)
    wu = wu_ref[0].astype(BF16)
    wd = wd_ref[0].astype(BF16)
    for i in range(nb):
        x = xe_ref[i, 0]
        g = jnp.dot(x, wg, preferred_element_type=F32)
        u = jnp.dot(x, wu, preferred_element_type=F32)
        hid = (g * jax.nn.sigmoid(g) * u).astype(BF16)
        part = jnp.dot(hid, wd, preferred_element_type=F32)
        rows = slice(i * cap, (i + 1) * cap)
        acc_ref[rows, :] = jnp.where(f == 0, 0.0, acc_ref[rows, :]) + part

    @pl.when(f == pl.num_programs(2) - 1)
    def _():
        gate = gate_ref[...].reshape(nb * cap, 1)
        ye_ref[...] = (acc_ref[...] * gate).astype(BF16).reshape(ye_ref.shape)


def _expert_ffn(xe, gates, w_gate, w_up, w_down, nb=2, tf=512):
    B, E, cap, D = xe.shape
    F = w_gate.shape[-1]
    return pl.pallas_call(
        _ffn_kernel,
        grid=(E, B // nb, F // tf),
        in_specs=[pl.BlockSpec((nb, 1, cap, D), lambda e, b, f: (b, e, 0, 0)),
                  pl.BlockSpec((nb, 1, cap, 1), lambda e, b, f: (b, e, 0, 0)),
                  pl.BlockSpec((1, D, tf), lambda e, b, f: (e, 0, f)),
                  pl.BlockSpec((1, D, tf), lambda e, b, f: (e, 0, f)),
                  pl.BlockSpec((1, tf, D), lambda e, b, f: (e, f, 0))],
        out_specs=pl.BlockSpec((nb, 1, cap, D), lambda e, b, f: (b, e, 0, 0)),
        out_shape=jax.ShapeDtypeStruct((B, E, cap, D), BF16),
        scratch_shapes=[pltpu.VMEM((nb * cap, D), F32)],
        compiler_params=_cparams(("arbitrary", "arbitrary", "arbitrary")),
        name="ec_ffn",
    )(xe, gates, w_gate, w_up, w_down)


def _combine_kernel(boff_ref, aff_ref, thr_ref, tie_ref, ye_ref, x1_ref, mod_ref, g_ref, out_ref,
                    slot_ref, carry_ref, *, tb, wslot, nblk, epg):
    b = pl.program_id(0)
    t2 = pl.program_id(1)
    e = pl.program_id(2)
    nsub = aff_ref.shape[1] // tb
    cap = ye_ref.shape[2]
    lane = lax.broadcasted_iota(I32, (tb, LANES), 1)

    @pl.when(e == 0)
    def _():
        @pl.when(t2 == 0)
        def _():
            carry_ref[...] = jnp.zeros(carry_ref.shape, F32)

        lower = (lax.broadcasted_iota(I32, (tb, tb), 0)
                 >= lax.broadcasted_iota(I32, (tb, tb), 1)).astype(BF16)
        for u in range(nsub):
            rows = slice(u * tb, (u + 1) * tb)
            bits = lax.bitcast_convert_type(aff_ref[0, rows, :], I32)
            tok = (t2 * nsub + u) * tb + lax.broadcasted_iota(I32, (tb, LANES), 0)
            thr = thr_ref[0]
            sel = ((bits > thr) | ((bits == thr) & (tok <= tie_ref[0]))) & (lane < N_EXPERTS)
            self_ = sel.astype(F32)
            incl = jnp.dot(lower, self_.astype(BF16), preferred_element_type=F32)
            pos = (incl - self_ + carry_ref[...]).astype(I32)
            slot_ref[rows, :] = jnp.where(sel, pos, -1)
            carry_ref[...] = carry_ref[...] + incl[tb - 1:tb, :]
        out_ref[...] = jnp.zeros(out_ref.shape, F32)

    n_e = pl.num_programs(2) * epg
    for u in range(nsub):
        rows = slice(u * tb, (u + 1) * tb)
        slot_x, first_x, trips = [], [], jnp.int32(0)
        for x in range(epg):
            ex = e * epg + x
            base = (b * n_e + ex) * (nblk + 1) + t2 * nsub + u
            first = (boff_ref[base] // SUBLANES_BF16) * SUBLANES_BF16
            slot_x.append(jnp.sum(jnp.where(lane == ex, slot_ref[rows, :], 0),
                                  axis=1, keepdims=True))
            first_x.append(first)
            trips = jnp.maximum(trips, (boff_ref[base + 1] - first + wslot - 1) // wslot)

        def body(k, carry):
            hots, wins = [], []
            for x in range(epg):
                start = first_x[x] + k * wslot
                woff = pl.multiple_of(jnp.minimum(start, cap - wslot), SUBLANES_BF16)
                todo = jnp.where(slot_x[x] >= start, slot_x[x], -1)
                hots.append((lax.broadcasted_iota(I32, (tb, wslot), 1) + woff == todo).astype(BF16))
                wins.append(ye_ref[0, x, pl.ds(woff, wslot), :])
            out_ref[0, rows, :] += jnp.dot(jnp.concatenate(hots, axis=1),
                                           jnp.concatenate(wins, axis=0),
                                           preferred_element_type=F32)
            return carry

        lax.fori_loop(0, trips, body, 0)

    @pl.when(e == pl.num_programs(2) - 1)
    def _():
        for u in range(nsub):
            rows = slice(u * tb, (u + 1) * tb)
            x2 = x1_ref[0, rows, :] + mod_ref[0, 5:6, :] * out_ref[0, rows, :]
            out_ref[0, rows, :] = (x2 * lax.rsqrt(jnp.mean(x2 * x2, axis=-1, keepdims=True)
                                                  + NORM_EPS) * g_ref[...])


def _combine(boff_flat, aff, thr, tie, ye, x1, mod, g_final, tb, tsup=2048, wslot=128, epg=2):
    B, S, D = x1.shape
    E, cap = ye.shape[1], ye.shape[2]
    assert cap % wslot == 0 and E % epg == 0
    tsup = min(tsup, S)
    return pl.pallas_call(
        functools.partial(_combine_kernel, tb=tb, wslot=wslot, nblk=S // tb, epg=epg),
        grid_spec=pltpu.PrefetchScalarGridSpec(
            num_scalar_prefetch=1,
            grid=(B, S // tsup, E // epg),
            in_specs=[pl.BlockSpec((1, tsup, LANES), lambda b, t, e, off: (b, t, 0)),
                      pl.BlockSpec((1, 1, LANES), lambda b, t, e, off: (b, 0, 0)),
                      pl.BlockSpec((1, 1, LANES), lambda b, t, e, off: (b, 0, 0)),
                      pl.BlockSpec((1, epg, cap, D), lambda b, t, e, off: (b, e, 0, 0)),
                      pl.BlockSpec((1, tsup, D), lambda b, t, e, off: (b, t, 0)),
                      pl.BlockSpec((1, 6, D), lambda b, t, e, off: (b, 0, 0)),
                      pl.BlockSpec((1, D), lambda b, t, e, off: (0, 0))],
            out_specs=pl.BlockSpec((1, tsup, D), lambda b, t, e, off: (b, t, 0)),
            scratch_shapes=[pltpu.VMEM((tsup, LANES), I32), pltpu.VMEM((1, LANES), F32)]),
        out_shape=jax.ShapeDtypeStruct((B, S, D), F32),
        compiler_params=_cparams(("parallel", "arbitrary", "arbitrary")),
        name="ec_combine",
    )(boff_flat, aff, thr, tie, ye, x1, mod, g_final)


def _prep_weights(w_in, w_uq, w_ukv):
    D = w_in.shape[0]
    H = MLA_HEADS
    n_lat = MLA_Q_RANK + MLA_KV_RANK + MLA_ROPE
    n_grp = 3 * DIL_HEADS * DIL_HEAD_DIM
    n_dil = len(DIL_GROUPS) * n_grp
    w_lat = jnp.zeros((D, LAT_PAD), F32).at[:, :n_lat].set(w_in[:, :n_lat]).astype(BF16)
    w_grp = [w_in[:, n_lat + i * n_grp:n_lat + (i + 1) * n_grp].astype(BF16)
             for i in range(len(DIL_GROUPS))]
    w_gates = w_in[:, n_lat + n_dil:].astype(BF16)
    half = MLA_ROPE // 2
    uq = w_uq.reshape(MLA_Q_RANK, H, MLA_NOPE + MLA_ROPE)
    pe = uq[:, :, MLA_NOPE:]
    zq = jnp.zeros((MLA_Q_RANK, H, QK_PAD - MLA_NOPE - MLA_ROPE), F32)
    wq = jnp.concatenate([uq[:, :, :MLA_NOPE], pe, zq], axis=2)
    wqs = jnp.concatenate([pe[:, :, half:], pe[:, :, :half], zq], axis=2)
    ukv = w_ukv.reshape(MLA_KV_RANK, H, MLA_NOPE + MLA_V)
    wk = ukv[:, :, :MLA_NOPE]
    wv = ukv[:, :, MLA_NOPE:]
    return (w_lat, w_grp, w_gates, wq.reshape(MLA_Q_RANK, H * QK_PAD).astype(BF16),
            wqs.reshape(MLA_Q_RANK, H * LANES).astype(BF16),
            wk.reshape(MLA_KV_RANK, H * MLA_NOPE).astype(BF16),
            wv.reshape(MLA_KV_RANK, H * MLA_V).astype(BF16))


def kernel(x, c, positions, w_ada, b_ada, g_norm_mix, w_in, g_q_lat, g_kv_lat, w_uq, w_ukv,
           rel_bias, w_out, g_norm_ffn, w_router, w_gate, w_up, w_down, g_final):
    B, S, D = x.shape
    assert w_ada.shape[0] == 1, "the final norm is fused into the (single) layer's last kernel"
    assert DIL_GROUPS[0][1] == 1, "the gates ride along with the undilated group's projection"
    for l in range(w_ada.shape[0]):
        mod = _modulation(c, w_ada[l], b_ada[l])
        w_lat, w_grp, w_gates, wq, wqs, wk, wv = _prep_weights(w_in[l], w_uq[l], w_ukv[l])
        g_mix = g_norm_mix[l].reshape(1, D)
        n_qkv = w_grp[0].shape[1] // D
        hs = _norm_modulate(x, mod, g_mix, tuple(d for _, d in DIL_GROUPS))
        lat = _projection(hs[0], w_lat, "lat_projection", out_dtype=F32)
        main = _projection(hs[0], jnp.concatenate([w_grp[0], w_gates], axis=1), "in_projection_g0",
                           n_plain=n_qkv)
        qkvs = [main] + [_projection(hs[gi], w_grp[gi], f"in_projection_g{gi}")
                         for gi in range(1, len(DIL_GROUPS))]
        q, k, v = _mla_prep(lat.reshape(B, S, LAT_PAD), positions, g_q_lat[l].reshape(1, -1), g_kv_lat[l].reshape(1, -1),
                            wq, wqs, wk, wv)
        o_a = _flash_attention(q, k, v)
        dil_outs = [_dilated_group(qkvs[gi], positions, rel_bias, gi, window, d)
                    for gi, (window, d) in enumerate(DIL_GROUPS)]
        x1, h2, aff, aff_t = _out_projection(o_a, dil_outs, main, n_qkv, x, mod,
                                             g_norm_ffn[l].reshape(1, D),
                                             w_out[l].astype(BF16), w_router[l])
        cap = EC_CAPACITY_FACTOR * S // N_EXPERTS
        chunk = 512
        thr, tie, slot, boff = _select(aff_t, cap, chunk)
        boff_flat = boff[:, :, :S // chunk + 1].reshape(-1)
        xe, gates = _gather(boff_flat, slot, aff_t, h2, cap, chunk)
        ye = _expert_ffn(xe, gates, w_gate[l], w_up[l], w_down[l])
        pad = jnp.zeros((B, 1, LANES - N_EXPERTS), I32)
        thr_l = jnp.concatenate([thr[:, :, 0].reshape(B, 1, N_EXPERTS), pad], axis=2)
        tie_l = jnp.concatenate([tie[:, :, 0].reshape(B, 1, N_EXPERTS), pad], axis=2)
        x = _combine(boff_flat, aff, thr_l, tie_l, ye, x1, mod, g_final.reshape(1, D), tb=chunk)
    return x
```

```python
import functools
import math

import jax
import jax.numpy as jnp
from jax import lax
from jax.experimental import pallas as pl
from jax.experimental.pallas import tpu as pltpu

F32 = jnp.float32
BF16 = jnp.bfloat16
I32 = jnp.int32

MLA_HEADS = 8
MLA_Q_RANK = 384
MLA_KV_RANK = 256
MLA_NOPE = 128
MLA_ROPE = 64
MLA_V = 128
ROPE_THETA = 10000.0
DIL_GROUPS = ((128, 1), (512, 4), (2048, 16))
DIL_HEADS = 8
DIL_HEAD_DIM = 128
REL_BUCKETS = 32
REL_MAX_DIST = 1024
N_EXPERTS = 16
EC_CAPACITY_FACTOR = 2
NORM_EPS = 1e-6
NEG_INF = -1e30

LANES = 128
SUBLANES = 8
SUBLANES_BF16 = 16
QK_PAD = 256
LAT_PAD = 768
VMEM_V7X = 64 * 1024 * 1024
VMEM_LIMIT = VMEM_V7X * 7 // 8


def _cparams(sem):
    return pltpu.CompilerParams(dimension_semantics=sem, vmem_limit_bytes=VMEM_LIMIT)


def _nt_dot(a, b):
    return lax.dot_general(a, b, (((1,), (1,)), ((), ())), preferred_element_type=F32)


def _mod_kernel(c_ref, w_ref, b_ref, o_ref):
    c = c_ref[...]
    cond = c * jax.nn.sigmoid(c)
    o_ref[...] = jnp.dot(cond, w_ref[...], preferred_element_type=F32,
                         precision=lax.Precision.HIGHEST) + b_ref[...]


def _modulation(c, w_ada, b_ada):
    B, D = c.shape
    rows = -(-B // SUBLANES) * SUBLANES
    c8 = jnp.zeros((rows, D), F32).at[:B].set(c)
    n6 = w_ada.shape[1]
    out = pl.pallas_call(
        _mod_kernel,
        grid=(n6 // D,),
        in_specs=[pl.BlockSpec((rows, D), lambda j: (0, 0)),
                  pl.BlockSpec((D, D), lambda j: (0, j)),
                  pl.BlockSpec((1, D), lambda j: (0, j))],
        out_specs=pl.BlockSpec((rows, D), lambda j: (0, j)),
        out_shape=jax.ShapeDtypeStruct((rows, n6), F32),
        compiler_params=_cparams(("arbitrary",)),
        name="modulation",
    )(c8, w_ada, b_ada.reshape(1, n6))
    return out[:B].reshape(B, 6, D)


def _norm_kernel(*refs, dils):
    nx = len(refs) - 2 - len(dils)
    x_refs = refs[:nx]
    mod_ref, g_ref = refs[nx:nx + 2]
    h_refs = refs[nx + 2:]
    tm = x_refs[0].shape[1]
    for h_ref, d in zip(h_refs, dils):
        rows = tm // d
        for g in range(d):
            src = pl.ds(g, rows, stride=d) if d > 1 else slice(None)
            x = jnp.concatenate([xr[0, src, :] for xr in x_refs], axis=1)
            y = x * lax.rsqrt(jnp.mean(x * x, axis=-1, keepdims=True) + NORM_EPS) * g_ref[...]
            h = y * (1.0 + mod_ref[0, 1:2, :]) + mod_ref[0, 0:1, :]
            h_ref[0, g] = h.astype(BF16)


def _norm_modulate(x, mod, g, dils, tm=1024):
    B, S, D = x.shape
    in_specs = [pl.BlockSpec((1, tm, LANES), functools.partial(lambda b, i, j: (b, i, j), j=j))
                for j in range(D // LANES)]
    in_specs += [pl.BlockSpec((1, 6, D), lambda b, i: (b, 0, 0)),
                 pl.BlockSpec((1, D), lambda b, i: (0, 0))]
    return pl.pallas_call(
        functools.partial(_norm_kernel, dils=dils),
        grid=(B, S // tm),
        in_specs=in_specs,
        out_specs=[pl.BlockSpec((1, d, tm // d, D), lambda b, i: (b, 0, i, 0)) for d in dils],
        out_shape=[jax.ShapeDtypeStruct((B, d, S // d, D), BF16) for d in dils],
        compiler_params=_cparams(("parallel", "parallel")),
        name="norm_modulate",
    )(*([x] * (D // LANES) + [mod, g]))


def _proj_kernel(h_ref, w_ref, o_ref, *, n_plain, n_tiles):
    acc = jnp.dot(h_ref[...], w_ref[...], preferred_element_type=F32)
    if n_plain == n_tiles:
        o_ref[...] = acc.astype(o_ref.dtype)
    else:
        n = pl.program_id(1)

        @pl.when(n < n_plain)
        def _():
            o_ref[...] = acc.astype(o_ref.dtype)

        @pl.when(n >= n_plain)
        def _():
            o_ref[...] = jax.nn.sigmoid(acc).astype(o_ref.dtype)


def _projection(h, w, name, n_plain=None, out_dtype=BF16, tm=2048, tn=1024):
    B, d, L, D = h.shape
    N = w.shape[1]
    tn = min(tn, N)
    tm = min(tm, B * d * L)
    n_tiles = N // tn
    n_plain = n_tiles if n_plain is None else n_plain
    out = pl.pallas_call(
        functools.partial(_proj_kernel, n_plain=n_plain, n_tiles=n_tiles),
        grid=(B * d * L // tm, n_tiles),
        in_specs=[pl.BlockSpec((tm, D), lambda i, n: (i, 0)),
                  pl.BlockSpec((D, tn), lambda i, n: (0, n))],
        out_specs=pl.BlockSpec((tm, tn), lambda i, n: (i, n)),
        out_shape=jax.ShapeDtypeStruct((B * d * L, N), out_dtype),
        compiler_params=_cparams(("parallel", "arbitrary")),
        name=name,
    )(h.reshape(B * d * L, D), w)
    return out.reshape(B, d, L, N)


def _mla_prep_kernel(lat_ref, pos_ref, gq_ref, gkv_ref, wq_ref, wqs_ref, wk_ref, wv_ref,
                     freq_ref, sgn_ref, sel_ref, sels_ref, one_ref, q_ref, k_ref, v_ref, *, scale):
    lat = lat_ref[0]
    cq = lat[:, :MLA_Q_RANK]
    ckv = lat[:, MLA_Q_RANK:MLA_Q_RANK + MLA_KV_RANK]
    kpe = lat[:, MLA_Q_RANK + MLA_KV_RANK:]
    cqn = (cq * lax.rsqrt(jnp.mean(cq * cq, axis=-1, keepdims=True) + NORM_EPS)
           * gq_ref[...]).astype(BF16)
    ckvn = (ckv * lax.rsqrt(jnp.mean(ckv * ckv, axis=-1, keepdims=True) + NORM_EPS)
            * gkv_ref[...]).astype(BF16)
    tm = lat.shape[0]
    pos = jnp.concatenate(
        [jnp.broadcast_to(pos_ref[0, :, c * LANES:(c + 1) * LANES], (LANES, LANES)).T
         for c in range(tm // LANES)], axis=0).astype(F32)
    ang = pos * freq_ref[...]
    cos = jnp.cos(ang)
    sin = jnp.sin(ang) * sgn_ref[...]
    qa = jnp.dot(cqn, wq_ref[...], preferred_element_type=F32)
    qs = jnp.dot(cqn, wqs_ref[...], preferred_element_type=F32)
    kn = jnp.dot(ckvn, wk_ref[...], preferred_element_type=F32)
    vv = jnp.dot(ckvn, wv_ref[...], preferred_element_type=F32)
    hp = lax.Precision.HIGHEST
    rk = (jnp.dot(kpe, sel_ref[...], preferred_element_type=F32, precision=hp) * cos
          + jnp.dot(kpe, sels_ref[...], preferred_element_type=F32, precision=hp) * sin).astype(BF16)
    cos_q = cos * scale
    sin_q = sin * scale
    ones = jnp.broadcast_to(one_ref[...], (tm, LANES)).astype(BF16)
    for h in range(MLA_HEADS):
        lo = slice(h * QK_PAD, h * QK_PAD + LANES)
        hi = slice(h * QK_PAD + LANES, (h + 1) * QK_PAD)
        hd = slice(h * LANES, (h + 1) * LANES)
        q_ref[0, :, lo] = (qa[:, lo] * scale).astype(BF16)
        q_ref[0, :, hi] = (qa[:, hi] * cos_q + qs[:, hd] * sin_q).astype(BF16)
        k_ref[0, :, lo] = kn[:, hd].astype(BF16)
        k_ref[0, :, hi] = rk
        v_ref[0, :, lo] = vv[:, hd].astype(BF16)
        v_ref[0, :, hi] = ones


def _mla_prep(lat, positions, g_q, g_kv, wq, wqs, wk, wv, tm=512):
    B, S, _ = lat.shape
    H = MLA_HEADS
    half = MLA_ROPE // 2
    assert MLA_NOPE == LANES and MLA_V == LANES and QK_PAD == 2 * LANES and tm % LANES == 0
    inv_freq = ROPE_THETA ** (-jnp.arange(0, MLA_ROPE, 2, dtype=F32) / MLA_ROPE)
    freq = jnp.zeros((1, LANES), F32).at[0, :MLA_ROPE].set(jnp.concatenate([inv_freq, inv_freq]))
    sgn = jnp.zeros((1, LANES), F32).at[0, :half].set(-1.0).at[0, half:MLA_ROPE].set(1.0)
    r = jnp.arange(MLA_ROPE)
    sel = jnp.zeros((LANES, LANES), F32).at[r, r].set(1.0)
    sels = jnp.zeros((LANES, LANES), F32).at[r, (r + half) % MLA_ROPE].set(1.0)
    scale = (MLA_NOPE + MLA_ROPE) ** -0.5 * math.log2(math.e)
    ones_col = jnp.zeros((1, LANES), F32).at[0, 0].set(1.0)
    const = lambda b, i: (0, 0)
    return pl.pallas_call(
        functools.partial(_mla_prep_kernel, scale=scale),
        grid=(B, S // tm),
        in_specs=[pl.BlockSpec((1, tm, LAT_PAD), lambda b, i: (b, i, 0)),
                  pl.BlockSpec((1, 1, tm), lambda b, i: (b, 0, i)),
                  pl.BlockSpec((1, MLA_Q_RANK), const),
                  pl.BlockSpec((1, MLA_KV_RANK), const),
                  pl.BlockSpec((MLA_Q_RANK, H * QK_PAD), const),
                  pl.BlockSpec((MLA_Q_RANK, H * LANES), const),
                  pl.BlockSpec((MLA_KV_RANK, H * LANES), const),
                  pl.BlockSpec((MLA_KV_RANK, H * LANES), const),
                  pl.BlockSpec((1, LANES), const),
                  pl.BlockSpec((1, LANES), const),
                  pl.BlockSpec((LANES, LANES), const),
                  pl.BlockSpec((LANES, LANES), const),
                  pl.BlockSpec((1, LANES), const)],
        out_specs=[pl.BlockSpec((1, tm, H * QK_PAD), lambda b, i: (b, i, 0)),
                   pl.BlockSpec((1, tm, H * QK_PAD), lambda b, i: (b, i, 0)),
                   pl.BlockSpec((1, tm, H * QK_PAD), lambda b, i: (b, i, 0))],
        out_shape=[jax.ShapeDtypeStruct((B, S, H * QK_PAD), BF16),
                   jax.ShapeDtypeStruct((B, S, H * QK_PAD), BF16),
                   jax.ShapeDtypeStruct((B, S, H * QK_PAD), BF16)],
        compiler_params=_cparams(("parallel", "parallel")),
        name="mla_prep",
    )(lat, positions.reshape(B, 1, S), g_q, g_kv, wq, wqs, wk, wv, freq, sgn, sel, sels, ones_col)


def _flash_kernel(q_ref, k_ref, v_ref, o_ref, m_ref, acc_ref, s_ref, *, tq, tkc):
    S = k_ref.shape[1]
    nk = S // tkc
    nq = S // tq
    n_items = nq * nk
    acc_ref[...] = jnp.zeros(acc_ref.shape, F32)
    m_ref[...] = jnp.full(m_ref.shape, -jnp.inf, F32)

    def scores(t, slot):
        t = jnp.minimum(t, n_items - 1)
        qoff = pl.multiple_of((t // nk) * tq, tq)
        koff = pl.multiple_of((t % nk) * tkc, tkc)
        s_ref[slot] = _nt_dot(q_ref[0, pl.ds(qoff, tq), :], k_ref[0, pl.ds(koff, tkc), :])

    def consume(t, slot, first):
        koff = pl.multiple_of((t % nk) * tkc, tkc)
        s = s_ref[slot]
        m_prev = m_ref[...]
        if first:
            m_prev = jnp.where(t % nk == 0, -jnp.inf, m_prev)
        m_new = jnp.maximum(m_prev, jnp.max(s, axis=-1, keepdims=True))
        alpha = jnp.exp2(m_prev - m_new)
        p = jnp.exp2(s - m_new).astype(BF16)
        acc_ref[...] = alpha * acc_ref[...] + jnp.dot(p, v_ref[0, pl.ds(koff, tkc), :],
                                                      preferred_element_type=F32)
        m_ref[...] = m_new

    scores(0, 0)

    def body(qi, carry):
        t0 = qi * nk
        for c in range(nk):
            scores(t0 + c + 1, (c + 1) % 2)
            consume(t0 + c, c % 2, c == 0)
        acc = acc_ref[...]
        qoff = pl.multiple_of(qi * tq, tq)
        o_ref[0, pl.ds(qoff, tq), :] = (acc[:, :MLA_V] / acc[:, MLA_V:MLA_V + 1]).astype(BF16)
        return carry

    lax.fori_loop(0, nq, body, 0)


def _flash_attention(q, k, v, tq=1024, tkc=1024):
    B, S, _ = q.shape
    H = MLA_HEADS
    assert (S // tkc) % 2 == 0 and S % tq == 0
    whole = pl.BlockSpec((1, S, QK_PAD), lambda b, h: (b, 0, h))
    return pl.pallas_call(
        functools.partial(_flash_kernel, tq=tq, tkc=tkc),
        grid=(B, H),
        in_specs=[whole, whole, whole],
        out_specs=pl.BlockSpec((1, S, MLA_V), lambda b, h: (b, 0, h)),
        out_shape=jax.ShapeDtypeStruct((B, S, H * MLA_V), BF16),
        scratch_shapes=[pltpu.VMEM((tq, 1), F32), pltpu.VMEM((tq, QK_PAD), F32),
                        pltpu.VMEM((2, tq, tkc), F32)],
        compiler_params=_cparams(("parallel", "parallel")),
        name="mla_flash",
    )(q, k, v)


def _t5_bucket(rel):
    nb = REL_BUCKETS // 2
    max_exact = nb // 2
    ret = jnp.where(rel > 0, nb, 0)
    n = jnp.abs(rel)
    nf = jnp.maximum(n, 1).astype(F32)
    large = max_exact + (jnp.log(nf / max_exact) / math.log(REL_MAX_DIST / max_exact)
                         * (nb - max_exact)).astype(I32)
    large = jnp.minimum(large, nb - 1)
    return ret + jnp.where(n < max_exact, n, large)


def _bias_lookup(tab_ref, h, bucket):
    rows, nk = bucket.shape
    tab = jnp.broadcast_to(tab_ref[h:h + 1, :], (rows, LANES))
    return jnp.concatenate(
        [jnp.take_along_axis(tab, bucket[:, c * LANES:(c + 1) * LANES], axis=1)
         for c in range(nk // LANES)], axis=1)


def _band_bias_kernel(tab_ref, band_ref, *, r, dil):
    _, ts, nk = band_ref.shape
    off = lax.broadcasted_iota(I32, (ts, nk), 1) - r - lax.broadcasted_iota(I32, (ts, nk), 0)
    bucket = jnp.where(jnp.abs(off) <= r, _t5_bucket(off * dil), REL_BUCKETS)
    for h in range(DIL_HEADS):
        band_ref[h] = _bias_lookup(tab_ref, h, bucket)


def _band_bias(tab, r, dil, ts):
    nk = ts + 2 * r
    return pl.pallas_call(
        functools.partial(_band_bias_kernel, r=r, dil=dil),
        grid=(1,),
        in_specs=[pl.BlockSpec((DIL_HEADS, LANES), lambda i: (0, 0))],
        out_specs=pl.BlockSpec((DIL_HEADS, ts, nk), lambda i: (0, 0, 0)),
        out_shape=jax.ShapeDtypeStruct((DIL_HEADS, ts, nk), F32),
        compiler_params=_cparams(("arbitrary",)),
        name=f"band_bias_d{dil}",
    )(tab)


def _dilated_kernel(q_ref, kp_ref, kc_ref, kn_ref, vp_ref, vc_ref, vn_ref, pq_ref, pk_ref, tab_ref,
                    band_ref, o_ref, l_ref, kcat_ref, vcat_ref, s_scr, *, r, tq, ts, n_rows, dil):
    n = pl.program_id(2)
    nsub = tq // ts
    nk = ts + 2 * r
    kcat_ref[0:r, :] = kp_ref[0, 0]
    kcat_ref[r:r + tq, :] = kc_ref[0, 0]
    kcat_ref[r + tq:tq + 2 * r, :] = kn_ref[0, 0]
    ones = jnp.ones((tq + 2 * r, DIL_HEAD_DIM), BF16)
    for h in range(DIL_HEADS):
        sl = slice(h * DIL_HEAD_DIM, (h + 1) * DIL_HEAD_DIM)
        dst = slice(2 * h * DIL_HEAD_DIM, (2 * h + 1) * DIL_HEAD_DIM)
        vcat_ref[0:r, dst] = vp_ref[0, 0, :, sl]
        vcat_ref[r:r + tq, dst] = vc_ref[0, 0, :, sl]
        vcat_ref[r + tq:tq + 2 * r, dst] = vn_ref[0, 0, :, sl]
        vcat_ref[:, (2 * h + 1) * DIL_HEAD_DIM:(2 * h + 2) * DIL_HEAD_DIM] = ones
    lane_grp = lax.broadcasted_iota(I32, (ts, LANES), 1) // (LANES // DIL_HEADS)
    scale = DIL_HEAD_DIM ** -0.5 * math.log2(math.e)
    for j in range(nsub):
        qi = n * tq + j * ts + lax.broadcasted_iota(I32, (ts, nk), 0)
        kj = n * tq + j * ts - r + lax.broadcasted_iota(I32, (ts, nk), 1)
        inside = (kj >= 0) & (kj < n_rows)
        valid = (jnp.abs(kj - qi) <= r) & inside
        pq = jnp.broadcast_to(pq_ref[0, 0, :, j * ts:(j + 1) * ts], (LANES, ts)).T
        pk = pk_ref[0, 0, j]
        rel = jnp.concatenate([pk[:, c * LANES:(c + 1) * LANES] - pq
                               for c in range(nk // LANES)], axis=1)
        regular = jnp.max(jnp.where(valid & (rel != (kj - qi) * dil), 1, 0)) == 0

        def logits(h):
            sl = slice(h * DIL_HEAD_DIM, (h + 1) * DIL_HEAD_DIM)
            return _nt_dot(q_ref[0, 0, j * ts:(j + 1) * ts, sl], kcat_ref[j * ts:j * ts + nk, sl]) * scale

        @pl.when(regular)
        def _():
            for h in range(DIL_HEADS):
                s_scr[j * DIL_HEADS + h] = jnp.where(inside, logits(h) + band_ref[h], NEG_INF)

        @pl.when(jnp.logical_not(regular))
        def _():
            bucket = jnp.where(valid, _t5_bucket(rel), REL_BUCKETS)
            for h in range(DIL_HEADS):
                s_scr[j * DIL_HEADS + h] = logits(h) + _bias_lookup(tab_ref, h, bucket)
    for j in range(nsub):
        lse_tile = jnp.zeros((ts, LANES), F32)
        for h in range(DIL_HEADS):
            sl = slice(h * DIL_HEAD_DIM, (h + 1) * DIL_HEAD_DIM)
            s = s_scr[j * DIL_HEADS + h]
            m = jnp.max(s, axis=-1, keepdims=True)
            p = jnp.exp2(s - m).astype(BF16)
            res = jnp.dot(p, vcat_ref[j * ts:j * ts + nk,
                                      2 * h * DIL_HEAD_DIM:(2 * h + 2) * DIL_HEAD_DIM],
                          preferred_element_type=F32)
            den = res[:, DIL_HEAD_DIM:]
            o_ref[0, 0, j * ts:(j + 1) * ts, sl] = (res[:, :DIL_HEAD_DIM] / den).astype(BF16)
            lse_tile = jnp.where(lane_grp == h, m * math.log(2.0) + jnp.log(den), lse_tile)
        l_ref[0, 0, j * ts:(j + 1) * ts, :] = lse_tile


def _dilated_group(qkv, positions, table, gi, window, d, tq=512, ts=128):
    B, _, L, _ = qkv.shape
    HD = DIL_HEADS * DIL_HEAD_DIM
    r = window // (2 * d)
    tq = min(tq, L)
    nt = L // tq
    nsub = tq // ts
    rb = tq // r
    nk = ts + 2 * r
    pcls = positions.reshape(B, L, d).transpose(0, 2, 1)
    pq = pcls.reshape(B, d, 1, L)
    ppad = jnp.pad(pcls, ((0, 0), (0, 0), (r, ts + r)))
    pk = jnp.concatenate([ppad[:, :, :L].reshape(B, d, L // ts, ts),
                          ppad[:, :, ts:ts + L].reshape(B, d, L // ts, ts)[..., :2 * r]], axis=-1)
    pk = pk.reshape(B, d, L // ts, 1, nk)
    tab = jnp.zeros((DIL_HEADS, LANES), F32).at[:, :REL_BUCKETS].set(
        table[:, gi * DIL_HEADS:(gi + 1) * DIL_HEADS].T * math.log2(math.e))
    tab = tab.at[:, REL_BUCKETS].set(NEG_INF)

    def cur(c):
        return pl.BlockSpec((1, 1, tq, HD), lambda b, g, n: (b, g, n, c))

    def prv(c):
        return pl.BlockSpec((1, 1, r, HD), lambda b, g, n: (b, g, jnp.maximum(n * rb - 1, 0), c))

    def nxt(c):
        return pl.BlockSpec((1, 1, r, HD),
                            lambda b, g, n: (b, g, jnp.minimum((n + 1) * rb, L // r - 1), c))

    return pl.pallas_call(
        functools.partial(_dilated_kernel, r=r, tq=tq, ts=ts, n_rows=L, dil=d),
        grid=(B, d, nt),
        in_specs=[cur(0), prv(1), cur(1), nxt(1), prv(2), cur(2), nxt(2),
                  pl.BlockSpec((1, 1, 1, tq), lambda b, g, n: (b, g, 0, n)),
                  pl.BlockSpec((1, 1, nsub, 1, nk), lambda b, g, n: (b, g, n, 0, 0)),
                  pl.BlockSpec((DIL_HEADS, LANES), lambda b, g, n: (0, 0)),
                  pl.BlockSpec((DIL_HEADS, ts, nk), lambda b, g, n: (0, 0, 0))],
        out_specs=[pl.BlockSpec((1, 1, tq, HD), lambda b, g, n: (b, g, n, 0)),
                   pl.BlockSpec((1, 1, tq, LANES), lambda b, g, n: (b, g, n, 0))],
        out_shape=[jax.ShapeDtypeStruct((B, d, L, HD), BF16),
                   jax.ShapeDtypeStruct((B, d, L, LANES), F32)],
        scratch_shapes=[pltpu.VMEM((tq + 2 * r, HD), BF16), pltpu.VMEM((tq + 2 * r, 2 * HD), BF16),
                        pltpu.VMEM((nsub * DIL_HEADS, ts, nk), F32)],
        compiler_params=_cparams(("parallel", "parallel", "parallel")),
        name=f"dilated_g{gi}",
    )(*([qkv] * 7 + [pq, pk, tab, _band_bias(tab, r, d, ts)]))


def _outproj_kernel(*refs, dils):
    ng = len(dils)
    oa_ref = refs[0]
    og_refs = refs[1:1 + ng]
    lg_refs = refs[1 + ng:1 + 2 * ng]
    (ga_ref, gb_ref, x_ref, mod_ref, g_ref, wo_ref, wrh_ref, wrl_ref,
     x1_ref, h2_ref, aff_ref, afft_ref, o_scr, l_scr) = refs[1 + 2 * ng:]
    tm = x_ref.shape[1]
    for i, d in enumerate(dils):
        rows = tm // d
        for g in range(d):
            dst = pl.ds(g, rows, stride=d) if d > 1 else slice(None)
            og = og_refs[i][0, g].astype(F32)
            for h in range(DIL_HEADS):
                o_scr[i, h, dst, :] = og[:, h * DIL_HEAD_DIM:(h + 1) * DIL_HEAD_DIM]
            l_scr[i, dst, :] = lg_refs[i][0, g]
    lses = [l_scr[i] for i in range(ng)]
    mx = functools.reduce(jnp.maximum, lses)
    es = [jnp.exp(l - mx) for l in lses]
    tot = functools.reduce(lambda a, b: a + b, es)
    wts = [e / tot for e in es]
    cols = []
    for h in range(DIL_HEADS):
        c = h * (LANES // DIL_HEADS)
        cols.append(functools.reduce(
            lambda a, b: a + b, [wts[i][:, c:c + 1] * o_scr[i, h] for i in range(ng)]))
    o_b = jnp.concatenate(cols, axis=1)
    comb = (ga_ref[0, 0].astype(F32) * oa_ref[0].astype(F32)
            + gb_ref[0, 0].astype(F32) * o_b).astype(BF16)
    x1 = x_ref[0] + mod_ref[0, 2:3, :] * jnp.dot(comb, wo_ref[...], preferred_element_type=F32)
    x1_ref[0] = x1
    y = x1 * lax.rsqrt(jnp.mean(x1 * x1, axis=-1, keepdims=True) + NORM_EPS) * g_ref[...]
    h2 = y * (1.0 + mod_ref[0, 4:5, :]) + mod_ref[0, 3:4, :]
    h2_hi = h2.astype(BF16)
    h2_ref[0] = h2_hi
    h2_lo = (h2 - h2_hi.astype(F32)).astype(BF16)
    logits = (jnp.dot(h2_hi, wrh_ref[...], preferred_element_type=F32)
              + jnp.dot(h2_lo, wrh_ref[...], preferred_element_type=F32)
              + jnp.dot(h2_hi, wrl_ref[...], preferred_element_type=F32))
    lane = lax.broadcasted_iota(I32, logits.shape, 1)
    logits = jnp.where(lane < N_EXPERTS, logits, -jnp.inf)
    e = jnp.exp(logits - jnp.max(logits, axis=-1, keepdims=True))
    aff = e / jnp.sum(e, axis=-1, keepdims=True)
    aff_ref[0] = aff
    afft_ref[0, 0] = aff.T[:N_EXPERTS, :]


def _out_projection(o_a, dil_outs, main, gate_col, x, mod, g, w_out, w_router, tm=512):
    B, S, D = x.shape
    const = lambda b, i: (0, 0)
    row = lambda b, i: (b, i, 0)
    wr = jnp.zeros((D, LANES), F32).at[:, :N_EXPERTS].set(w_router)
    wr_hi = wr.astype(BF16)
    wr_lo = (wr - wr_hi.astype(F32)).astype(BF16)
    dils = tuple(o.shape[1] for o, _ in dil_outs)
    og_specs = [pl.BlockSpec((1, d, tm // d, D), lambda b, i: (b, 0, i, 0)) for d in dils]
    lg_specs = [pl.BlockSpec((1, d, tm // d, LANES), lambda b, i: (b, 0, i, 0)) for d in dils]
    return pl.pallas_call(
        functools.partial(_outproj_kernel, dils=dils),
        grid=(B, S // tm),
        in_specs=[pl.BlockSpec((1, tm, D), row)] + og_specs + lg_specs + [
                  pl.BlockSpec((1, 1, tm, D), lambda b, i: (b, 0, i, gate_col)),
                  pl.BlockSpec((1, 1, tm, D), lambda b, i: (b, 0, i, gate_col + 1)),
                  pl.BlockSpec((1, tm, D), row),
                  pl.BlockSpec((1, 6, D), lambda b, i: (b, 0, 0)),
                  pl.BlockSpec((1, D), const),
                  pl.BlockSpec((D, D), const),
                  pl.BlockSpec((D, LANES), const),
                  pl.BlockSpec((D, LANES), const)],
        out_specs=[pl.BlockSpec((1, tm, D), row),
                   pl.BlockSpec((1, tm, D), row),
                   pl.BlockSpec((1, tm, LANES), row),
                   pl.BlockSpec((1, 1, N_EXPERTS, tm), lambda b, i: (b, i, 0, 0))],
        out_shape=[jax.ShapeDtypeStruct((B, S, D), F32),
                   jax.ShapeDtypeStruct((B, S, D), BF16),
                   jax.ShapeDtypeStruct((B, S, LANES), F32),
                   jax.ShapeDtypeStruct((B, S // tm, N_EXPERTS, tm), F32)],
        scratch_shapes=[pltpu.VMEM((len(dils), DIL_HEADS, tm, DIL_HEAD_DIM), F32),
                        pltpu.VMEM((len(dils), tm, LANES), F32)],
        compiler_params=_cparams(("parallel", "parallel")),
        name="out_projection",
    )(o_a, *[o for o, _ in dil_outs], *[l for _, l in dil_outs], main, main, x, mod, g, w_out, wr_hi, wr_lo)


def _select_kernel(afft_ref, thr_ref, tie_ref, slot_ref, boff_ref, *, cap, chunk):
    aff = jnp.concatenate([afft_ref[0, c] for c in range(afft_ref.shape[1])], axis=1)
    E, S = aff.shape
    bits = lax.bitcast_convert_type(aff, I32)

    def count(mask):
        return jnp.sum(mask.astype(I32), axis=1, keepdims=True)

    def thr_step(i, v):
        cand = v | jnp.left_shift(jnp.int32(1), 30 - i)
        return jnp.where(count(bits >= cand) >= cap, cand, v)

    thr = lax.fori_loop(0, 31, thr_step, jnp.zeros((E, 1), I32))
    need = cap - count(bits > thr)
    eq = bits == thr
    idx = lax.broadcasted_iota(I32, (E, S), 1)
    nbits = max(1, (S - 1).bit_length())

    def tie_step(i, j):
        cand = j | jnp.left_shift(jnp.int32(1), nbits - 1 - i)
        return jnp.where(count(eq & (idx < cand)) < need, cand, j)

    tie = lax.fori_loop(0, nbits, tie_step, jnp.zeros((E, 1), I32))
    sel = (bits > thr) | (eq & (idx <= tie))
    thr_ref[0] = jnp.broadcast_to(thr, (E, LANES))
    tie_ref[0] = jnp.broadcast_to(tie, (E, LANES))

    upper = (lax.broadcasted_iota(I32, (chunk, chunk), 0)
             <= lax.broadcasted_iota(I32, (chunk, chunk), 1)).astype(BF16)
    carry = jnp.zeros((E, 1), F32)
    self_ = sel.astype(F32)
    lane = lax.broadcasted_iota(I32, (E, LANES), 1)
    boff = jnp.zeros((E, LANES), I32)
    for c in range(S // chunk):
        boff = jnp.where(lane == c, carry.astype(I32), boff)
        scf = self_[:, c * chunk:(c + 1) * chunk]
        incl = jnp.dot(scf.astype(BF16), upper, preferred_element_type=F32)
        pos = (incl - scf + carry).astype(I32)
        slot_ref[0, c] = jnp.where(scf > 0.0, pos, -1)
        carry = carry + incl[:, chunk - 1:chunk]
    boff_ref[0] = jnp.where(lane == S // chunk, carry.astype(I32), boff)


def _select(aff_t, cap, chunk):
    B, nblk, E, _ = aff_t.shape
    assert aff_t.shape[3] == chunk and nblk < LANES
    rows = pl.BlockSpec((1, nblk, E, chunk), lambda b: (b, 0, 0, 0))
    return pl.pallas_call(
        functools.partial(_select_kernel, cap=cap, chunk=chunk),
        grid=(B,),
        in_specs=[rows],
        out_specs=[pl.BlockSpec((1, E, LANES), lambda b: (b, 0, 0)),
                   pl.BlockSpec((1, E, LANES), lambda b: (b, 0, 0)),
                   rows,
                   pl.BlockSpec((1, E, LANES), lambda b: (b, 0, 0))],
        out_shape=[jax.ShapeDtypeStruct((B, E, LANES), I32),
                   jax.ShapeDtypeStruct((B, E, LANES), I32),
                   jax.ShapeDtypeStruct((B, nblk, E, chunk), I32),
                   jax.ShapeDtypeStruct((B, E, LANES), I32)],
        compiler_params=_cparams(("parallel",)),
        name="ec_select",
    )(aff_t)


def _gather_kernel(boff_ref, slot_ref, afft_ref, h_ref, xe_ref, gate_ref, acc_ref, gacc_ref, *,
                   win, epg):
    _, nblk, _, tbk = slot_ref.shape
    cap = xe_ref.shape[2]
    n_e = pl.num_programs(1) * epg
    acc_ref[...] = jnp.zeros(acc_ref.shape, F32)
    gacc_ref[...] = jnp.zeros(gacc_ref.shape, F32)

    def chunk(tb, carry):
        off = pl.multiple_of(tb * tbk, tbk)
        first_x, slot_x, aff_x, trips = [], [], [], jnp.int32(0)
        for x in range(epg):
            ex = pl.program_id(1) * epg + x
            base = (pl.program_id(0) * n_e + ex) * (nblk + 1) + tb
            first = (boff_ref[base] // SUBLANES) * SUBLANES
            first_x.append(first)
            slot_x.append(slot_ref[0, tb, pl.ds(ex, 1), :])
            aff_x.append(afft_ref[0, tb, pl.ds(ex, 1), :])
            trips = jnp.maximum(trips, (boff_ref[base + 1] - first + win - 1) // win)

        def window(k, c):
            w0 = [pl.multiple_of(jnp.minimum(first_x[x] + k * win, cap), SUBLANES)
                  for x in range(epg)]
            hits = [lax.broadcasted_iota(I32, (win, tbk), 0) + w0[x] == slot_x[x] for x in range(epg)]
            rows = jnp.dot(jnp.concatenate(hits, axis=0).astype(BF16), h_ref[0, pl.ds(off, tbk), :],
                           preferred_element_type=F32)
            for x in range(epg):
                acc_ref[x, pl.ds(w0[x], win), :] += rows[x * win:(x + 1) * win]
                gacc_ref[x, pl.ds(w0[x], win), :] += jnp.sum(jnp.where(hits[x], aff_x[x], 0.0),
                                                             axis=1, keepdims=True)
            return c

        lax.fori_loop(0, trips, window, 0)
        return carry

    lax.fori_loop(0, nblk, chunk, 0)
    xe_ref[0] = acc_ref[:, 0:cap, :].astype(BF16)
    gate_ref[0] = gacc_ref[:, 0:cap, :]


def _gather(boff_flat, slot, aff_t, h2, cap, chunk, win=128, epg=2):
    B, nblk, E, _ = slot.shape
    S, D = h2.shape[1:]
    rowblk = pl.BlockSpec((1, nblk, E, chunk), lambda b, e, off: (b, 0, 0, 0))
    assert E % epg == 0 and cap % SUBLANES == 0
    return pl.pallas_call(
        functools.partial(_gather_kernel, win=win, epg=epg),
        grid_spec=pltpu.PrefetchScalarGridSpec(
            num_scalar_prefetch=1,
            grid=(B, E // epg),
            in_specs=[rowblk, rowblk,
                      pl.BlockSpec((1, S, D), lambda b, e, off: (b, 0, 0))],
            out_specs=[pl.BlockSpec((1, epg, cap, D), lambda b, e, off: (b, e, 0, 0)),
                       pl.BlockSpec((1, epg, cap, 1), lambda b, e, off: (b, e, 0, 0))],
            scratch_shapes=[pltpu.VMEM((epg, cap + win, D), F32),
                            pltpu.VMEM((epg, cap + win, 1), F32)]),
        out_shape=[jax.ShapeDtypeStruct((B, E, cap, D), BF16),
                   jax.ShapeDtypeStruct((B, E, cap, 1), F32)],
        compiler_params=_cparams(("parallel", "arbitrary")),
        name="ec_gather",
    )(boff_flat, slot, aff_t, h2)


def _ffn_kernel(xe_ref, gate_ref, wg_ref, wu_ref, wd_ref, ye_ref, acc_ref):
    f = pl.program_id(2)
    nb, _, cap, D = xe_ref.shape

    @pl.when((pl.program_id(0) == 0) & (pl.program_id(1) == 0) & (f == 0))
    def _():
        acc_ref[...] = jnp.zeros(acc_ref.shape, F32)

    wg = wg_ref[0].astype(BF16)
    wu = wu_ref[0].astype(BF16)
    wd = wd_ref[0].astype(BF16)
    for i in range(nb):
        x = xe_ref[i, 0]
        g = jnp.dot(x, wg, preferred_element_type=F32)
        u = jnp.dot(x, wu, preferred_element_type=F32)
        hid = (g * jax.nn.sigmoid(g) * u).astype(BF16)
        part = jnp.dot(hid, wd, preferred_element_type=F32)
        rows = slice(i * cap, (i + 1) * cap)
        acc_ref[rows, :] = jnp.where(f == 0, 0.0, acc_ref[rows, :]) + part

    @pl.when(f == pl.num_programs(2) - 1)
    def _():
        gate = gate_ref[...].reshape(nb * cap, 1)
        ye_ref[...] = (acc_ref[...] * gate).astype(BF16).reshape(ye_ref.shape)


def _expert_ffn(xe, gates, w_gate, w_up, w_down, nb=2, tf=512):
    B, E, cap, D = xe.shape
    F = w_gate.shape[-1]
    return pl.pallas_call(
        _ffn_kernel,
        grid=(E, B // nb, F // tf),
        in_specs=[pl.BlockSpec((nb, 1, cap, D), lambda e, b, f: (b, e, 0, 0)),
                  pl.BlockSpec((nb, 1, cap, 1), lambda e, b, f: (b, e, 0, 0)),
                  pl.BlockSpec((1, D, tf), lambda e, b, f: (e, 0, f)),
                  pl.BlockSpec((1, D, tf), lambda e, b, f: (e, 0, f)),
                  pl.BlockSpec((1, tf, D), lambda e, b, f: (e, f, 0))],
        out_specs=pl.BlockSpec((nb, 1, cap, D), lambda e, b, f: (b, e, 0, 0)),
        out_shape=jax.ShapeDtypeStruct((B, E, cap, D), BF16),
        scratch_shapes=[pltpu.VMEM((nb * cap, D), F32)],
        compiler_params=_cparams(("arbitrary", "arbitrary", "arbitrary")),
        name="ec_ffn",
    )(xe, gates, w_gate, w_up, w_down)


def _combine_kernel(boff_ref, aff_ref, thr_ref, tie_ref, ye_ref, x1_ref, mod_ref, g_ref, out_ref,
                    slot_ref, carry_ref, *, tb, wslot, nblk, epg):
    b = pl.program_id(0)
    t2 = pl.program_id(1)
    e = pl.program_id(2)
    nsub = aff_ref.shape[1] // tb
    cap = ye_ref.shape[2]
    lane = lax.broadcasted_iota(I32, (tb, LANES), 1)

    @pl.when(e == 0)
    def _():
        @pl.when(t2 == 0)
        def _():
            carry_ref[...] = jnp.zeros(carry_ref.shape, F32)

        lower = (lax.broadcasted_iota(I32, (tb, tb), 0)
                 >= lax.broadcasted_iota(I32, (tb, tb), 1)).astype(BF16)
        for u in range(nsub):
            rows = slice(u * tb, (u + 1) * tb)
            bits = lax.bitcast_convert_type(aff_ref[0, rows, :], I32)
            tok = (t2 * nsub + u) * tb + lax.broadcasted_iota(I32, (tb, LANES), 0)
            thr = thr_ref[0]
            sel = ((bits > thr) | ((bits == thr) & (tok <= tie_ref[0]))) & (lane < N_EXPERTS)
            self_ = sel.astype(F32)
            incl = jnp.dot(lower, self_.astype(BF16), preferred_element_type=F32)
            pos = (incl - self_ + carry_ref[...]).astype(I32)
            slot_ref[rows, :] = jnp.where(sel, pos, -1)
            carry_ref[...] = carry_ref[...] + incl[tb - 1:tb, :]
        out_ref[...] = jnp.zeros(out_ref.shape, F32)

    n_e = pl.num_programs(2) * epg
    for u in range(nsub):
        rows = slice(u * tb, (u + 1) * tb)
        slot_x, first_x, trips = [], [], jnp.int32(0)
        for x in range(epg):
            ex = e * epg + x
            base = (b * n_e + ex) * (nblk + 1) + t2 * nsub + u
            first = (boff_ref[base] // SUBLANES_BF16) * SUBLANES_BF16
            slot_x.append(jnp.sum(jnp.where(lane == ex, slot_ref[rows, :], 0),
                                  axis=1, keepdims=True))
            first_x.append(first)
            trips = jnp.maximum(trips, (boff_ref[base + 1] - first + wslot - 1) // wslot)

        def body(k, carry):
            hots, wins = [], []
            for x in range(epg):
                start = first_x[x] + k * wslot
                woff = pl.multiple_of(jnp.minimum(start, cap - wslot), SUBLANES_BF16)
                todo = jnp.where(slot_x[x] >= start, slot_x[x], -1)
                hots.append((lax.broadcasted_iota(I32, (tb, wslot), 1) + woff == todo).astype(BF16))
                wins.append(ye_ref[0, x, pl.ds(woff, wslot), :])
            out_ref[0, rows, :] += jnp.dot(jnp.concatenate(hots, axis=1),
                                           jnp.concatenate(wins, axis=0),
                                           preferred_element_type=F32)
            return carry

        lax.fori_loop(0, trips, body, 0)

    @pl.when(e == pl.num_programs(2) - 1)
    def _():
        for u in range(nsub):
            rows = slice(u * tb, (u + 1) * tb)
            x2 = x1_ref[0, rows, :] + mod_ref[0, 5:6, :] * out_ref[0, rows, :]
            out_ref[0, rows, :] = (x2 * lax.rsqrt(jnp.mean(x2 * x2, axis=-1, keepdims=True)
                                                  + NORM_EPS) * g_ref[...])


def _combine(boff_flat, aff, thr, tie, ye, x1, mod, g_final, tb, tsup=2048, wslot=128, epg=4):
    B, S, D = x1.shape
    E, cap = ye.shape[1], ye.shape[2]
    assert cap % wslot == 0 and E % epg == 0
    tsup = min(tsup, S)
    return pl.pallas_call(
        functools.partial(_combine_kernel, tb=tb, wslot=wslot, nblk=S // tb, epg=epg),
        grid_spec=pltpu.PrefetchScalarGridSpec(
            num_scalar_prefetch=1,
            grid=(B, S // tsup, E // epg),
            in_specs=[pl.BlockSpec((1, tsup, LANES), lambda b, t, e, off: (b, t, 0)),
                      pl.BlockSpec((1, 1, LANES), lambda b, t, e, off: (b, 0, 0)),
                      pl.BlockSpec((1, 1, LANES), lambda b, t, e, off: (b, 0, 0)),
                      pl.BlockSpec((1, epg, cap, D), lambda b, t, e, off: (b, e, 0, 0)),
                      pl.BlockSpec((1, tsup, D), lambda b, t, e, off: (b, t, 0)),
                      pl.BlockSpec((1, 6, D), lambda b, t, e, off: (b, 0, 0)),
                      pl.BlockSpec((1, D), lambda b, t, e, off: (0, 0))],
            out_specs=pl.BlockSpec((1, tsup, D), lambda b, t, e, off: (b, t, 0)),
            scratch_shapes=[pltpu.VMEM((tsup, LANES), I32), pltpu.VMEM((1, LANES), F32)]),
        out_shape=jax.ShapeDtypeStruct((B, S, D), F32),
        compiler_params=_cparams(("parallel", "arbitrary", "arbitrary")),
        name="ec_combine",
    )(boff_flat, aff, thr, tie, ye, x1, mod, g_final)


def _prep_weights(w_in, w_uq, w_ukv):
    D = w_in.shape[0]
    H = MLA_HEADS
    n_lat = MLA_Q_RANK + MLA_KV_RANK + MLA_ROPE
    n_grp = 3 * DIL_HEADS * DIL_HEAD_DIM
    n_dil = len(DIL_GROUPS) * n_grp
    w_lat = jnp.zeros((D, LAT_PAD), F32).at[:, :n_lat].set(w_in[:, :n_lat]).astype(BF16)
    w_grp = [w_in[:, n_lat + i * n_grp:n_lat + (i + 1) * n_grp].astype(BF16)
             for i in range(len(DIL_GROUPS))]
    w_gates = w_in[:, n_lat + n_dil:].astype(BF16)
    half = MLA_ROPE // 2
    uq = w_uq.reshape(MLA_Q_RANK, H, MLA_NOPE + MLA_ROPE)
    pe = uq[:, :, MLA_NOPE:]
    zq = jnp.zeros((MLA_Q_RANK, H, QK_PAD - MLA_NOPE - MLA_ROPE), F32)
    wq = jnp.concatenate([uq[:, :, :MLA_NOPE], pe, zq], axis=2)
    wqs = jnp.concatenate([pe[:, :, half:], pe[:, :, :half], zq], axis=2)
    ukv = w_ukv.reshape(MLA_KV_RANK, H, MLA_NOPE + MLA_V)
    wk = ukv[:, :, :MLA_NOPE]
    wv = ukv[:, :, MLA_NOPE:]
    return (w_lat, w_grp, w_gates, wq.reshape(MLA_Q_RANK, H * QK_PAD).astype(BF16),
            wqs.reshape(MLA_Q_RANK, H * LANES).astype(BF16),
            wk.reshape(MLA_KV_RANK, H * MLA_NOPE).astype(BF16),
            wv.reshape(MLA_KV_RANK, H * MLA_V).astype(BF16))


def kernel(x, c, positions, w_ada, b_ada, g_norm_mix, w_in, g_q_lat, g_kv_lat, w_uq, w_ukv,
           rel_bias, w_out, g_norm_ffn, w_router, w_gate, w_up, w_down, g_final):
    B, S, D = x.shape
    assert w_ada.shape[0] == 1, "the final norm is fused into the (single) layer's last kernel"
    assert DIL_GROUPS[0][1] == 1, "the gates ride along with the undilated group's projection"
    for l in range(w_ada.shape[0]):
        mod = _modulation(c, w_ada[l], b_ada[l])
        w_lat, w_grp, w_gates, wq, wqs, wk, wv = _prep_weights(w_in[l], w_uq[l], w_ukv[l])
        g_mix = g_norm_mix[l].reshape(1, D)
        n_qkv = w_grp[0].shape[1] // D
        hs = _norm_modulate(x, mod, g_mix, tuple(d for _, d in DIL_GROUPS))
        lat = _projection(hs[0], w_lat, "lat_projection", out_dtype=F32)
        main = _projection(hs[0], jnp.concatenate([w_grp[0], w_gates], axis=1), "in_projection_g0",
                           n_plain=n_qkv)
        qkvs = [main] + [_projection(hs[gi], w_grp[gi], f"in_projection_g{gi}")
                         for gi in range(1, len(DIL_GROUPS))]
        q, k, v = _mla_prep(lat.reshape(B, S, LAT_PAD), positions, g_q_lat[l].reshape(1, -1), g_kv_lat[l].reshape(1, -1),
                            wq, wqs, wk, wv)
        o_a = _flash_attention(q, k, v)
        dil_outs = [_dilated_group(qkvs[gi], positions, rel_bias, gi, window, d)
                    for gi, (window, d) in enumerate(DIL_GROUPS)]
        x1, h2, aff, aff_t = _out_projection(o_a, dil_outs, main, n_qkv, x, mod,
                                             g_norm_ffn[l].reshape(1, D),
                                             w_out[l].astype(BF16), w_router[l])
        cap = EC_CAPACITY_FACTOR * S // N_EXPERTS
        chunk = 512
        thr, tie, slot, boff = _select(aff_t, cap, chunk)
        boff_flat = boff[:, :, :S // chunk + 1].reshape(-1)
        xe, gates = _gather(boff_flat, slot, aff_t, h2, cap, chunk)
        ye = _expert_ffn(xe, gates, w_gate[l], w_up[l], w_down[l])
        pad = jnp.zeros((B, 1, LANES - N_EXPERTS), I32)
        thr_l = jnp.concatenate([thr[:, :, 0].reshape(B, 1, N_EXPERTS), pad], axis=2)
        tie_l = jnp.concatenate([tie[:, :, 0].reshape(B, 1, N_EXPERTS), pad], axis=2)
        x = _combine(boff_flat, aff, thr_l, tie_l, ye, x1, mod, g_final.reshape(1, D), tb=chunk)
    return x
```
